```python
import math
import jax
import jax.numpy as jnp
from jax import lax
import numpy as np

D_MODEL = 1024
BATCH = 2
SEQ = 8192
DEPTH = 1
DEC_BATCH = 8
DEC_SEQ = 32
PAST_LEN = 1024

CHUNK = 64
RET_HEADS = 4
RET_DK = 128
RET_DV = 256
RET_QK = RET_HEADS * RET_DK
RET_VW = RET_HEADS * RET_DV
ROPE_BASE = 10000.0
DN_QK_HEADS = 4
DN_V_HEADS = 8
DN_DK = 128
DN_DV = 128
DN_QK = DN_QK_HEADS * DN_DK
DN_VW = DN_V_HEADS * DN_DV
DN_CONV_W = 4
DN_CONV_CH = 2 * DN_QK + DN_VW
XA_HEADS = 4
XA_DH = 256
XA_W = XA_HEADS * XA_DH
N_MEM = 256
N_BRANCH = 3
BRANCH_W = 1024
MOE_GROUPS = 4
MOE_EXPERTS = 8
MOE_TOPK = 2
MOE_DFF = 256
ALPHA = (2.0 * DEPTH) ** 0.25
BETA = (8.0 * DEPTH) ** -0.25
IN_SIZES = (RET_QK, RET_QK, RET_VW, RET_VW, DN_CONV_CH, DN_VW, DN_V_HEADS, DN_V_HEADS, XA_W, N_BRANCH * D_MODEL)
N_IN = 2 * RET_QK + 2 * RET_VW + DN_CONV_CH + DN_VW + 2 * DN_V_HEADS + XA_W + N_BRANCH * D_MODEL

kernel_name = "hybrid_retention_gdn_hmoe_stream_step"

F32 = jnp.float32


def _split_cols(y, sizes):
    out = []
    start = 0
    for s in sizes:
        out.append(y[..., start:start + s])
        start += s
    return out


def _layernorm(x, g, b, eps=1e-5):
    xf = x.astype(F32)
    mu = jnp.mean(xf, -1, keepdims=True)
    var = jnp.mean(jnp.square(xf - mu), -1, keepdims=True)
    return ((xf - mu) * lax.rsqrt(var + eps) * g.astype(F32) + b.astype(F32)).astype(x.dtype)


def _head_layernorm(o, g, b, eps=1e-5):
    B, T = o.shape[:2]
    of = o.astype(F32)
    mu = jnp.mean(of, -1, keepdims=True)
    var = jnp.mean(jnp.square(of - mu), -1, keepdims=True)
    n = ((of - mu) * lax.rsqrt(var + eps)).reshape(B, T, -1)
    return (n * g.astype(F32) + b.astype(F32)).astype(o.dtype)


def _l2norm(x, eps=1e-6):
    xf = x.astype(F32)
    return (xf * lax.rsqrt(jnp.sum(xf * xf, -1, keepdims=True) + eps)).astype(x.dtype)


def _gated_rmsnorm(o, z, g, eps=1e-6):
    B, T, H, Dv = o.shape
    of = o.astype(F32)
    n = of * lax.rsqrt(jnp.mean(of * of, -1, keepdims=True) + eps) * g.astype(F32)
    n = n * jax.nn.silu(z.astype(F32).reshape(B, T, H, Dv))
    return n.reshape(B, T, H * Dv).astype(o.dtype)


def _rotary(x, pos):
    half = x.shape[-1] // 2
    inv = 1.0 / (ROPE_BASE ** (jnp.arange(half, dtype=F32) / half))
    ang = pos.astype(F32)[:, None] * inv[None, :]
    cos = jnp.cos(ang)[None, :, None, :]
    sin = jnp.sin(ang)[None, :, None, :]
    xf = x.astype(F32)
    x1, x2 = xf[..., :half], xf[..., half:]
    return jnp.concatenate([x1 * cos - x2 * sin, x2 * cos + x1 * sin], -1).astype(x.dtype)


def _to_chunks(a, C):
    B, T, H = a.shape[:3]
    a = a.reshape((B, T // C, C, H) + a.shape[3:])
    return jnp.moveaxis(a, (1, 3), (0, 2))


def _from_chunks(a):
    N, B, H, C = a.shape[:4]
    a = jnp.moveaxis(a, (0, 2), (1, 3))
    return a.reshape((B, N * C, H) + a.shape[4:])


def _retention(q, k, v, S0, log_gamma):
    T = q.shape[1]
    C = min(CHUNK, T)
    dt = q.dtype
    qc, kc, vc = _to_chunks(q, C), _to_chunks(k, C), _to_chunks(v, C)
    idx = jnp.arange(C, dtype=F32)
    diff = idx[:, None] - idx[None, :]
    causal = diff >= 0
    lg = log_gamma[:, None, None]
    decay = jnp.where(causal, jnp.exp(jnp.where(causal, diff, 0.0) * lg), 0.0).astype(dt)
    q_decay = jnp.exp((idx + 1.0)[None, :] * log_gamma[:, None]).astype(dt)
    k_decay = jnp.exp((C - 1.0 - idx)[None, :] * log_gamma[:, None]).astype(dt)
    chunk_decay = jnp.exp(C * log_gamma).astype(dt)

    def step(S, inp):
        qi, ki, vi = inp
        scores = jnp.einsum('bhid,bhjd->bhij', qi, ki) * decay
        o = (jnp.einsum('bhij,bhje->bhie', scores, vi)
             + jnp.einsum('bhid,bhde->bhie', qi, S) * q_decay[..., None])
        S = S * chunk_decay[:, None, None] + jnp.einsum('bhjd,bhje->bhde', ki * k_decay[..., None], vi)
        return S, o

    S, o = lax.scan(step, S0, (qc, kc, vc))
    return _from_chunks(o), S


def _gated_delta(q, k, v, g, beta, S0):
    T = q.shape[1]
    C = min(CHUNK, T)
    dt = q.dtype
    dv = v.shape[-1]
    qc, kc, vc = _to_chunks(q, C), _to_chunks(k, C), _to_chunks(v, C)
    gc, bc = _to_chunks(g, C), _to_chunks(beta, C)
    idx = jnp.arange(C)
    lower = idx[:, None] >= idx[None, :]
    strict = idx[:, None] > idx[None, :]
    eye = jnp.eye(C, dtype=F32)

    def step(S, inp):
        qi, ki, vi, gi, bi = inp
        G = jnp.cumsum(gi.astype(F32), axis=-1)
        rel = jnp.exp(jnp.where(lower, G[..., :, None] - G[..., None, :], -jnp.inf))
        kf, qf = ki.astype(F32), qi.astype(F32)
        bf = bi.astype(F32)[..., None]
        kb = kf * bf
        A = jnp.where(strict, jnp.einsum('bhid,bhjd->bhij', kb, kf) * rel, 0.0)
        rhs = jnp.concatenate([vi.astype(F32) * bf, kb * jnp.exp(G)[..., None]], -1)
        sol = lax.linalg.triangular_solve(eye + A, rhs, left_side=True, lower=True, unit_diagonal=True)
        u, w = sol[..., :dv], sol[..., dv:]
        Sf = S.astype(F32)
        v_new = u - jnp.einsum('bhik,bhkv->bhiv', w, Sf)
        attn = jnp.einsum('bhik,bhjk->bhij', qf, kf) * rel
        o = (jnp.einsum('bhik,bhkv->bhiv', qf * jnp.exp(G)[..., None], Sf)
             + jnp.einsum('bhij,bhjv->bhiv', attn, v_new))
        G_last = G[..., -1:]
        S_new = (Sf * jnp.exp(G_last)[..., None]
                 + jnp.einsum('bhjk,bhjv->bhkv', kf * jnp.exp(G_last - G)[..., None], v_new))
        return S_new.astype(S.dtype), o.astype(dt)

    S, o = lax.scan(step, S0, (qc, kc, vc, gc, bc))
    return _from_chunks(o), S


def _memory_kv(mem, w_mem_kv):
    B, M, _ = mem.shape
    mk, mv = _split_cols(mem @ w_mem_kv, (XA_W, XA_W))
    return mk.reshape(B, M, XA_HEADS, XA_DH), mv.reshape(B, M, XA_HEADS, XA_DH)


def _mem_attention(q, mem_k, mem_v):
    B, T = q.shape[:2]
    s = jnp.einsum('bthd,bmhd->bhtm', q, mem_k).astype(F32) * (XA_DH ** -0.5)
    p = jax.nn.softmax(s, axis=-1).astype(q.dtype)
    return jnp.einsum('bhtm,bmhd->bthd', p, mem_v).reshape(B, T, XA_W)


def _hier_moe(x, w_coarse, b_coarse, w_fine, b_fine, w_gate, w_up, w_down):
    B, T, _ = x.shape
    coarse = jax.nn.softmax((x @ w_coarse + b_coarse).astype(F32), axis=-1)
    p_grp, grp = lax.top_k(coarse, 1)
    grp_oh = jax.nn.one_hot(grp[..., 0], MOE_GROUPS, dtype=F32)
    fine = (x @ w_fine + b_fine).astype(F32).reshape(B, T, MOE_GROUPS, MOE_EXPERTS)
    fine_sel = jnp.sum(fine * grp_oh[..., None], axis=2)
    top_v, top_i = lax.top_k(fine_sel, MOE_TOPK)
    wk = jax.nn.softmax(top_v, axis=-1) * p_grp
    comb = jnp.einsum('btk,btke->bte', wk, jax.nn.one_hot(top_i, MOE_EXPERTS, dtype=F32))
    comb = (comb[:, :, None, :] * grp_oh[..., None]).astype(x.dtype)
    y = jnp.zeros_like(x)
    for gi in range(MOE_GROUPS):
        hg = (jax.nn.silu(jnp.einsum('btd,edf->btef', x, w_gate[gi]))
              * jnp.einsum('btd,edf->btef', x, w_up[gi]))
        y = y + jnp.einsum('btef,efd->btd', hg * comb[:, :, gi, :, None], w_down[gi])
    return y


def _layer(h, mem_k, mem_v, ret_S, dn_S, conv_buf, pos, lp):
    (w_in, ret_gn_g, ret_gn_b, dn_conv, dn_A_log, dn_dt_bias, dn_norm_g, w_branch, w_out,
     ln1_g, ln1_b, moe_w_coarse, moe_b_coarse, moe_w_fine, moe_b_fine,
     moe_w_gate, moe_w_up, moe_w_down, ln2_g, ln2_b) = lp
    B, T, D = h.shape
    rq, rk, rv, rg, dqkv, dz, db, da, xq, gate_logits = _split_cols(h @ w_in, IN_SIZES)

    log_gamma = jnp.log(1.0 - 2.0 ** (-5.0 - jnp.arange(RET_HEADS, dtype=F32)))
    rq = _rotary(rq.reshape(B, T, RET_HEADS, RET_DK), pos)
    rk = _rotary(rk.reshape(B, T, RET_HEADS, RET_DK), pos) * (RET_DK ** -0.5)
    ro, ret_S = _retention(rq, rk, rv.reshape(B, T, RET_HEADS, RET_DV), ret_S, log_gamma)
    ret_out = _head_layernorm(ro, ret_gn_g, ret_gn_b) * jax.nn.silu(rg)

    xpad = jnp.concatenate([conv_buf, dqkv], axis=1)
    new_buf = xpad[:, T:]
    conv = xpad[:, 0:T] * dn_conv[0]
    for w in range(1, DN_CONV_W):
        conv = conv + xpad[:, w:w + T] * dn_conv[w]
    conv = jax.nn.silu(conv)
    dq, dk, dv = _split_cols(conv, (DN_QK, DN_QK, DN_VW))
    rep = DN_V_HEADS // DN_QK_HEADS
    dq = jnp.repeat(_l2norm(dq.reshape(B, T, DN_QK_HEADS, DN_DK)) * (DN_DK ** -0.5), rep, axis=2)
    dk = jnp.repeat(_l2norm(dk.reshape(B, T, DN_QK_HEADS, DN_DK)), rep, axis=2)
    dv = dv.reshape(B, T, DN_V_HEADS, DN_DV)
    beta = jax.nn.sigmoid(db.astype(F32))
    g = -jnp.exp(dn_A_log.astype(F32)) * jax.nn.softplus((da + dn_dt_bias).astype(F32))
    do, dn_S = _gated_delta(dq, dk, dv, g, beta, dn_S)
    dn_out = _gated_rmsnorm(do, dz, dn_norm_g)

    xo = _mem_attention(xq.reshape(B, T, XA_HEADS, XA_DH), mem_k, mem_v)

    branches = jnp.einsum('btnc,ncd->btnd', jnp.stack([ret_out, dn_out, xo], axis=2), w_branch)
    gates = jax.nn.sigmoid(gate_logits).reshape(B, T, N_BRANCH, D)
    mixed = jnp.sum(gates * branches, axis=2) @ w_out
    h = _layernorm(ALPHA * h + mixed, ln1_g, ln1_b)

    moe = _hier_moe(h, moe_w_coarse, moe_b_coarse, moe_w_fine, moe_b_fine, moe_w_gate, moe_w_up, moe_w_down)
    h = _layernorm(ALPHA * h + moe, ln2_g, ln2_b)
    return h, ret_S, dn_S, new_buf


def setup_inputs(seed: int = 0) -> dict:
    key = jax.random.key(seed)
    ks = jax.random.split(key, 40)
    D = D_MODEL
    G, E, Fh = MOE_GROUPS, MOE_EXPERTS, MOE_DFF

    def nrm(k, shape, s):
        return jax.random.normal(k, shape, F32) * s

    dt0 = jnp.exp(jax.random.uniform(ks[13], (DEPTH, DN_V_HEADS), F32, math.log(1e-3), math.log(1e-1)))
    return {
        'x_prompt': nrm(ks[0], (BATCH, SEQ, D), 1.0),
        'x_sample': nrm(ks[1], (DEC_BATCH, DEC_SEQ, D), 1.0),
        'mem_prompt': nrm(ks[2], (BATCH, N_MEM, D), 1.0),
        'state_ret': nrm(ks[3], (DEPTH, DEC_BATCH, RET_HEADS, RET_DK, RET_DV), 0.5),
        'state_dn': nrm(ks[4], (DEPTH, DEC_BATCH, DN_V_HEADS, DN_DK, DN_DV), 0.1),
        'state_dn_conv': nrm(ks[5], (DEPTH, DEC_BATCH, DN_CONV_W - 1, DN_CONV_CH), 1.0),
        'cache_mem_k': nrm(ks[6], (DEPTH, DEC_BATCH, N_MEM, XA_HEADS, XA_DH), 1.0),
        'cache_mem_v': nrm(ks[7], (DEPTH, DEC_BATCH, N_MEM, XA_HEADS, XA_DH), 1.0),
        'ln_in_g': 1.0 + nrm(ks[8], (D,), 0.05),
        'ln_in_b': nrm(ks[9], (D,), 0.02),
        'w_in': nrm(ks[10], (DEPTH, D, N_IN), D ** -0.5),
        'ret_gn_g': 1.0 + nrm(ks[11], (DEPTH, RET_VW), 0.05),
        'ret_gn_b': nrm(ks[12], (DEPTH, RET_VW), 0.02),
        'dn_conv': nrm(ks[14], (DEPTH, DN_CONV_W, DN_CONV_CH), 0.5),
        'dn_A_log': jnp.log(jax.random.uniform(ks[15], (DEPTH, DN_V_HEADS), F32, 1.0, 16.0)),
        'dn_dt_bias': dt0 + jnp.log(-jnp.expm1(-dt0)),
        'dn_norm_g': 1.0 + nrm(ks[16], (DEPTH, DN_DV), 0.05),
        'w_mem_kv': nrm(ks[17], (DEPTH, D, 2 * XA_W), D ** -0.5),
        'w_branch': nrm(ks[18], (DEPTH, N_BRANCH, BRANCH_W, D), BRANCH_W ** -0.5),
        'w_out': nrm(ks[19], (DEPTH, D, D), BETA * D ** -0.5),
        'ln1_g': 1.0 + nrm(ks[20], (DEPTH, D), 0.05),
        'ln1_b': nrm(ks[21], (DEPTH, D), 0.02),
        'moe_w_coarse': nrm(ks[22], (DEPTH, D, G), D ** -0.5),
        'moe_b_coarse': nrm(ks[23], (DEPTH, G), 0.01),
        'moe_w_fine': nrm(ks[24], (DEPTH, D, G * E), D ** -0.5),
        'moe_b_fine': nrm(ks[25], (DEPTH, G * E), 0.01),
        'moe_w_gate': nrm(ks[26], (DEPTH, G, E, D, Fh), D ** -0.5),
        'moe_w_up': nrm(ks[27], (DEPTH, G, E, D, Fh), D ** -0.5),
        'moe_w_down': nrm(ks[28], (DEPTH, G, E, Fh, D), BETA * Fh ** -0.5),
        'ln2_g': 1.0 + nrm(ks[29], (DEPTH, D), 0.05),
        'ln2_b': nrm(ks[30], (DEPTH, D), 0.02),
    }


def reference(x_prompt, x_sample, mem_prompt, state_ret, state_dn, state_dn_conv, cache_mem_k, cache_mem_v,
              ln_in_g, ln_in_b, w_in, ret_gn_g, ret_gn_b, dn_conv, dn_A_log, dn_dt_bias, dn_norm_g,
              w_mem_kv, w_branch, w_out, ln1_g, ln1_b, moe_w_coarse, moe_b_coarse, moe_w_fine, moe_b_fine,
              moe_w_gate, moe_w_up, moe_w_down, ln2_g, ln2_b):
    Bp, Tp, _ = x_prompt.shape
    Bs, Ts, _ = x_sample.shape
    dt = x_prompt.dtype
    pos_p = jnp.arange(Tp, dtype=jnp.int32)
    pos_s = PAST_LEN + jnp.arange(Ts, dtype=jnp.int32)
    hp = _layernorm(x_prompt, ln_in_g, ln_in_b)
    hs = _layernorm(x_sample, ln_in_g, ln_in_b)
    ret_p, ret_s, dn_p, dn_s, conv_p, conv_s, mk_p, mv_p = [], [], [], [], [], [], [], []
    for l in range(DEPTH):
        lp = (w_in[l], ret_gn_g[l], ret_gn_b[l], dn_conv[l], dn_A_log[l], dn_dt_bias[l], dn_norm_g[l],
              w_branch[l], w_out[l], ln1_g[l], ln1_b[l], moe_w_coarse[l], moe_b_coarse[l],
              moe_w_fine[l], moe_b_fine[l], moe_w_gate[l], moe_w_up[l], moe_w_down[l], ln2_g[l], ln2_b[l])
        mk, mv = _memory_kv(mem_prompt, w_mem_kv[l])
        hp, rS, dS, cb = _layer(
            hp, mk, mv,
            jnp.zeros((Bp, RET_HEADS, RET_DK, RET_DV), dt),
            jnp.zeros((Bp, DN_V_HEADS, DN_DK, DN_DV), dt),
            jnp.zeros((Bp, DN_CONV_W - 1, DN_CONV_CH), dt),
            pos_p, lp)
        hs, rS2, dS2, cb2 = _layer(hs, cache_mem_k[l], cache_mem_v[l], state_ret[l], state_dn[l],
                                   state_dn_conv[l], pos_s, lp)
        ret_p.append(rS)
        ret_s.append(rS2)
        dn_p.append(dS)
        dn_s.append(dS2)
        conv_p.append(cb)
        conv_s.append(cb2)
        mk_p.append(mk)
        mv_p.append(mv)
    return (hp, hs, jnp.stack(ret_p), jnp.stack(ret_s), jnp.stack(dn_p), jnp.stack(dn_s),
            jnp.stack(conv_p), jnp.stack(conv_s), jnp.stack(mk_p), jnp.stack(mv_p))
```

```python
import functools
import math

import jax
import jax.numpy as jnp
from jax import lax
from jax.experimental import pallas as pl
from jax.experimental.pallas import tpu as pltpu

F32 = jnp.float32
BF16 = jnp.bfloat16

D_MODEL = 1024
PAST_LEN = 1024
RET_HEADS = 4
RET_DK = 128
RET_DV = 256
ROPE_BASE = 10000.0
DN_QK_HEADS = 4
DN_V_HEADS = 8
DN_DK = 128
DN_DV = 128
DN_QK = DN_QK_HEADS * DN_DK
DN_CONV_W = 4
DN_CONV_CH = 2048
DN_CHUNK = 64
DN_SUB = 16
XA_HEADS = 4
XA_DH = 256
N_MEM = 256
MOE_GROUPS = 4
MOE_EXPERTS = 8
MOE_NE = MOE_GROUPS * MOE_EXPERTS
MOE_DFF = 256
DEPTH = 1
ALPHA = (2.0 * DEPTH) ** 0.25
LANES = 128
PROJ_W = 6144
VMEM_LIMIT = 56 * 1024 * 1024


def _dot(a, b):
    return jnp.dot(a.astype(BF16), b.astype(BF16), preferred_element_type=F32)


def _dot_nt(a, b):
    return lax.dot_general(a.astype(BF16), b.astype(BF16), (((1,), (1,)), ((), ())),
                           preferred_element_type=F32)


def _dot_tn(a, b):
    return lax.dot_general(a.astype(BF16), b.astype(BF16), (((0,), (0,)), ((), ())),
                           preferred_element_type=F32)


def _dot_exact(a, b):
    return jnp.dot(a, b, preferred_element_type=F32, precision=lax.Precision.HIGHEST)


def _layernorm(x, g, b, eps=1e-5):
    mu = jnp.mean(x, -1, keepdims=True)
    xc = x - mu
    var = jnp.mean(xc * xc, -1, keepdims=True)
    return xc * lax.rsqrt(var + eps) * g + b


def _silu(x):
    return x * (1.0 / (1.0 + jnp.exp(-x)))


def _sigmoid(x):
    return 1.0 / (1.0 + jnp.exp(-x))


def _softplus(x):
    return jnp.maximum(x, 0.0) + jnp.log(1.0 + jnp.exp(-jnp.abs(x)))


def _idiv(x, n):
    return jnp.right_shift(x, int(math.log2(n)))


def _params(*sem):
    return pltpu.CompilerParams(dimension_semantics=sem, vmem_limit_bytes=VMEM_LIMIT)


def _in_proj_kernel(x_ref, g_ref, b_ref, w_ref, ws_ref, wst_ref,
                    h_ref, hb_ref, small_ref, smallt_ref, proj_ref, hs_ref):
    @pl.when(pl.program_id(1) == 0)
    def _():
        h = _layernorm(x_ref[...], g_ref[...], b_ref[...])
        h_ref[...] = h
        hb = h.astype(BF16)
        hb_ref[...] = hb
        hs_ref[...] = hb
        small_ref[...] = jnp.dot(hb, ws_ref[...], preferred_element_type=F32)
        smallt_ref[...] = lax.dot_general(wst_ref[...], hb, (((1,), (1,)), ((), ())),
                                          preferred_element_type=F32)

    proj_ref[...] = jnp.dot(hs_ref[...], w_ref[...], preferred_element_type=F32)


def _in_proj(x, ln_g, ln_b, w_main, w_small, w_small_t, tm, tn=1024):
    n_tok = x.shape[0]
    grid = (n_tok // tm, PROJ_W // tn)
    return pl.pallas_call(
        _in_proj_kernel,
        grid=grid,
        in_specs=[
            pl.BlockSpec((tm, D_MODEL), lambda i, n: (i, 0)),
            pl.BlockSpec((1, D_MODEL), lambda i, n: (0, 0)),
            pl.BlockSpec((1, D_MODEL), lambda i, n: (0, 0)),
            pl.BlockSpec((D_MODEL, tn), lambda i, n: (0, n)),
            pl.BlockSpec((D_MODEL, LANES), lambda i, n: (0, 0)),
            pl.BlockSpec((16, D_MODEL), lambda i, n: (0, 0)),
        ],
        out_specs=[
            pl.BlockSpec((tm, D_MODEL), lambda i, n: (i, 0)),
            pl.BlockSpec((tm, D_MODEL), lambda i, n: (i, 0)),
            pl.BlockSpec((tm, LANES), lambda i, n: (i, 0)),
            pl.BlockSpec((16, tm), lambda i, n: (0, i)),
            pl.BlockSpec((tm, tn), lambda i, n: (i, n)),
        ],
        out_shape=[
            jax.ShapeDtypeStruct((n_tok, D_MODEL), F32),
            jax.ShapeDtypeStruct((n_tok, D_MODEL), BF16),
            jax.ShapeDtypeStruct((n_tok, LANES), F32),
            jax.ShapeDtypeStruct((16, n_tok), F32),
            jax.ShapeDtypeStruct((n_tok, PROJ_W), F32),
        ],
        scratch_shapes=[pltpu.VMEM((tm, D_MODEL), BF16)],
        compiler_params=_params("parallel", "arbitrary"),
        name="in_proj",
    )(x, ln_g, ln_b, w_main, w_small, w_small_t)


def _ret_kernel(q_ref, k_ref, v_ref, rg_ref, cos_ref, sin_ref, s0_ref, gng_ref, gnb_ref,
                o_ref, sout_ref, s_scr, dec_scr, *, chunk):
    head = pl.program_id(1)
    c = pl.program_id(2)
    hv = jnp.zeros((1, 1), F32) + head.astype(F32)
    lg = jnp.log(1.0 - jnp.exp2(-5.0 - hv))

    @pl.when(c == 0)
    def _():
        s_scr[...] = s0_ref[0, 0]
        ri = lax.broadcasted_iota(jnp.int32, (chunk, chunk), 0)
        ci = lax.broadcasted_iota(jnp.int32, (chunk, chunk), 1)
        causal = ri >= ci
        diff = jnp.where(causal, (ri - ci).astype(F32), 0.0)
        dec_scr[...] = jnp.where(causal, jnp.exp(diff * lg), 0.0)

    cos = cos_ref[...]
    sin = sin_ref[...]
    q = q_ref[0]
    k = k_ref[0]
    qr = q * cos + pltpu.roll(q, RET_DK // 2, 1) * sin
    kr = (k * cos + pltpu.roll(k, RET_DK // 2, 1) * sin) * (RET_DK ** -0.5)
    v = v_ref[0]
    idx = lax.broadcasted_iota(jnp.int32, (chunk, 1), 0).astype(F32)
    q_dec = jnp.exp((idx + 1.0) * lg)
    k_dec = jnp.exp((chunk - 1.0 - idx) * lg)
    s = s_scr[...]
    scores = _dot_nt(qr, kr) * dec_scr[...]
    o = _dot(scores, v) + _dot(qr, s) * q_dec
    s_new = s * jnp.exp(chunk * lg) + _dot_tn(kr * k_dec, v)
    s_scr[...] = s_new
    sout_ref[0, 0] = s_new
    o_ref[0] = _layernorm(o, gng_ref[...], gnb_ref[...]) * _silu(rg_ref[0])


def _retention(proj, cos2, sin2, s0, gn_g, gn_b, chunk):
    nb, t = proj.shape[:2]
    grid = (nb, RET_HEADS, t // chunk)
    q_blk = 3072 // RET_DK
    k_blk = 3584 // RET_DK
    v_blk = 4096 // RET_DV
    g_blk = 5120 // RET_DV
    return pl.pallas_call(
        functools.partial(_ret_kernel, chunk=chunk),
        grid=grid,
        in_specs=[
            pl.BlockSpec((1, chunk, RET_DK), lambda b, h, c: (b, c, q_blk + h)),
            pl.BlockSpec((1, chunk, RET_DK), lambda b, h, c: (b, c, k_blk + h)),
            pl.BlockSpec((1, chunk, RET_DV), lambda b, h, c: (b, c, v_blk + h)),
            pl.BlockSpec((1, chunk, RET_DV), lambda b, h, c: (b, c, g_blk + h)),
            pl.BlockSpec((chunk, RET_DK), lambda b, h, c: (c, 0)),
            pl.BlockSpec((chunk, RET_DK), lambda b, h, c: (c, 0)),
            pl.BlockSpec((1, 1, RET_DK, RET_DV), lambda b, h, c: (b, h, 0, 0)),
            pl.BlockSpec((1, RET_DV), lambda b, h, c: (0, h)),
            pl.BlockSpec((1, RET_DV), lambda b, h, c: (0, h)),
        ],
        out_specs=[
            pl.BlockSpec((1, chunk, RET_DV), lambda b, h, c: (b, c, h)),
            pl.BlockSpec((1, 1, RET_DK, RET_DV), lambda b, h, c: (b, h, 0, 0)),
        ],
        out_shape=[
            jax.ShapeDtypeStruct((nb, t, RET_HEADS * RET_DV), F32),
            jax.ShapeDtypeStruct((nb, RET_HEADS, RET_DK, RET_DV), F32),
        ],
        scratch_shapes=[pltpu.VMEM((RET_DK, RET_DV), F32), pltpu.VMEM((chunk, chunk), F32)],
        compiler_params=_params("parallel", "parallel", "arbitrary"),
        name="retention",
    )(proj, proj, proj, proj, cos2, sin2, s0, gn_g, gn_b)


def _dn_kernel(x_ref, z_ref, sm_ref, smt_ref, cw_ref, cinit_ref, prow_ref, pcol_ref, ng_ref, s0_ref,
               o_ref, sout_ref, cout_ref, xext, s_scr, *, blk_len, chunk):
    L = blk_len
    blk = pl.program_id(1)

    @pl.when(blk == 0)
    def _():
        xext[0:8, :] = cinit_ref[0]
        s_scr[...] = s0_ref[0]

    x = x_ref[0]
    xext[8:8 + L, :] = x
    conv = x * cw_ref[3:4, :]
    for w in range(DN_CONV_W - 1):
        conv = conv + xext[5 + w:5 + w + L, :] * cw_ref[w:w + 1, :]
    conv = _silu(conv)
    xext[0:8, :] = x[L - 8:L, :]
    cout_ref[0] = x[L - (DN_CONV_W - 1):L, :]

    sm = sm_ref[...]
    prow = prow_ref[...]
    beta_tm = _sigmoid(sm)
    g_tm = prow[0:1, :] * _softplus(sm + prow[1:2, :])
    smt = smt_ref[0]
    pcol = pcol_ref[...]
    g_hm = pcol[8:16, 0:1] * _softplus(smt[8:16, :] + pcol[8:16, 1:2])

    ri = lax.broadcasted_iota(jnp.int32, (L, L), 0)
    ci = lax.broadcasted_iota(jnp.int32, (L, L), 1)
    same_chunk = _idiv(ri, chunk) == _idiv(ci, chunk)
    lower = jnp.logical_and(ri >= ci, same_chunk)
    strict = jnp.logical_and(ri > ci, same_chunk)
    same_sub = _idiv(ri, DN_SUB) == _idiv(ci, DN_SUB)
    tri = jnp.where(lower, 1.0, 0.0).astype(F32)
    tri_t = jnp.where(jnp.logical_and(ri <= ci, same_chunk), 1.0, 0.0).astype(F32)
    gcum_tm = _dot_exact(tri, g_tm)
    gcum_hm = _dot_exact(g_hm, tri_t)
    egc_tm = jnp.exp(gcum_tm)

    n_sub = L // chunk
    rep = DN_V_HEADS // DN_QK_HEADS
    for j in range(DN_QK_HEADS):
        qj = conv[:, j * DN_DK:(j + 1) * DN_DK]
        kj = conv[:, DN_QK + j * DN_DK:DN_QK + (j + 1) * DN_DK]
        qj = qj * lax.rsqrt(jnp.sum(qj * qj, -1, keepdims=True) + 1e-6) * (DN_DK ** -0.5)
        kj = kj * lax.rsqrt(jnp.sum(kj * kj, -1, keepdims=True) + 1e-6)
        kq = _dot_nt(jnp.concatenate([kj, qj], axis=0), kj)
        kk = kq[0:L]
        qk = kq[L:2 * L]
        for r in range(rep):
            hh = j * rep + r
            vh = conv[:, 2 * DN_QK + hh * DN_DV:2 * DN_QK + (hh + 1) * DN_DV]
            gc = gcum_tm[:, 8 + hh:9 + hh]
            gr = gcum_hm[hh:hh + 1, :]
            rel = jnp.where(lower, jnp.exp(jnp.where(lower, gc - gr, 0.0)), 0.0)
            beta = beta_tm[:, hh:hh + 1]
            eg = egc_tm[:, 8 + hh:9 + hh]
            a = jnp.where(strict, beta * kk * rel, 0.0)
            attn = qk * rel
            pd = jnp.where(same_sub, -a, 0.0)
            e = a + pd
            xacc = pd
            pw = pd
            for _ in range(3):
                pw = _dot(pw, pw)
                xacc = xacc + pw + _dot(xacc, pw)
            f = e + _dot(xacc, e)
            f2 = _dot(f, f)
            y = f2 - f - _dot(f, f2)
            rinv = xacc + y + _dot(y, xacc)
            rhs = jnp.concatenate([vh * beta, kj * (beta * eg)], axis=1)
            sol = rhs + _dot(rinv, rhs)
            u = sol[:, 0:DN_DV]
            wmat = sol[:, DN_DV:DN_DV + DN_DK]
            qe = qj * eg
            s = s_scr[hh]
            v_new = []
            q_s = []
            for sc in range(n_sub):
                r0 = sc * chunk
                ws = _dot(jnp.concatenate([wmat[r0:r0 + chunk], qe[r0:r0 + chunk]], axis=0), s)
                vn = u[r0:r0 + chunk] - ws[0:chunk]
                v_new.append(vn)
                q_s.append(ws[chunk:2 * chunk])
                g_last = gcum_tm[r0 + chunk - 1:r0 + chunk, 8 + hh:9 + hh]
                kd = kj[r0:r0 + chunk] * jnp.exp(g_last - gc[r0:r0 + chunk])
                s = s * jnp.exp(g_last) + _dot_tn(kd, vn)
            s_scr[hh] = s
            sout_ref[0, hh] = s
            v_new = v_new[0] if n_sub == 1 else jnp.concatenate(v_new, axis=0)
            q_s = q_s[0] if n_sub == 1 else jnp.concatenate(q_s, axis=0)
            o = q_s + _dot(attn, v_new)
            zh = z_ref[0, :, hh * DN_DV:(hh + 1) * DN_DV]
            n = o * lax.rsqrt(jnp.mean(o * o, -1, keepdims=True) + 1e-6) * ng_ref[...]
            o_ref[0, :, hh * DN_DV:(hh + 1) * DN_DV] = n * _silu(zh)


def _deltanet(proj, small, small_t, conv_w, conv_init, prow, pcol, norm_g, s0, blk_len, chunk):
    nb, t = proj.shape[:2]
    nblk = t // blk_len
    grid = (nb, nblk)
    return pl.pallas_call(
        functools.partial(_dn_kernel, blk_len=blk_len, chunk=chunk),
        grid=grid,
        in_specs=[
            pl.BlockSpec((1, blk_len, DN_CONV_CH), lambda b, i: (b, i, 0)),
            pl.BlockSpec((1, blk_len, 1024), lambda b, i: (b, i, 2)),
            pl.BlockSpec((blk_len, LANES), lambda b, i: (b * nblk + i, 0)),
            pl.BlockSpec((1, 16, blk_len), lambda b, i: (b, 0, i)),
            pl.BlockSpec((DN_CONV_W, DN_CONV_CH), lambda b, i: (0, 0)),
            pl.BlockSpec((1, 8, DN_CONV_CH), lambda b, i: (b, 0, 0)),
            pl.BlockSpec((8, LANES), lambda b, i: (0, 0)),
            pl.BlockSpec((16, LANES), lambda b, i: (0, 0)),
            pl.BlockSpec((1, DN_DV), lambda b, i: (0, 0)),
            pl.BlockSpec((1, DN_V_HEADS, DN_DK, DN_DV), lambda b, i: (b, 0, 0, 0)),
        ],
        out_specs=[
            pl.BlockSpec((1, blk_len, 1024), lambda b, i: (b, i, 0)),
            pl.BlockSpec((1, DN_V_HEADS, DN_DK, DN_DV), lambda b, i: (b, 0, 0, 0)),
            pl.BlockSpec((1, DN_CONV_W - 1, DN_CONV_CH), lambda b, i: (b, 0, 0)),
        ],
        out_shape=[
            jax.ShapeDtypeStruct((nb, t, 1024), F32),
            jax.ShapeDtypeStruct((nb, DN_V_HEADS, DN_DK, DN_DV), F32),
            jax.ShapeDtypeStruct((nb, DN_CONV_W - 1, DN_CONV_CH), F32),
        ],
        scratch_shapes=[pltpu.VMEM((blk_len + 8, DN_CONV_CH), F32),
                        pltpu.VMEM((DN_V_HEADS, DN_DK, DN_DV), F32)],
        compiler_params=_params("parallel", "arbitrary"),
        name="deltanet",
    )(proj, proj, small, small_t, conv_w, conv_init, prow, pcol, norm_g, s0)


def _memkv_kernel(m_ref, w_ref, o_ref):
    o_ref[...] = _dot(m_ref[...], w_ref[...])


def _memkv(mem, w_kv):
    n = mem.shape[0]
    tn = 1024
    return pl.pallas_call(
        _memkv_kernel,
        grid=(w_kv.shape[1] // tn,),
        in_specs=[pl.BlockSpec((n, D_MODEL), lambda j: (0, 0)),
                  pl.BlockSpec((D_MODEL, tn), lambda j: (0, j))],
        out_specs=pl.BlockSpec((n, tn), lambda j: (0, j)),
        out_shape=jax.ShapeDtypeStruct((n, w_kv.shape[1]), F32),
        compiler_params=_params("parallel"),
        name="memkv",
    )(mem, w_kv)


def _mix_kernel(h_ref, hb_ref, ret_ref, dn_ref, mk_ref, mv_ref, wxq_ref, wg_ref, wb_ref, wo_ref,
                g1_ref, b1_ref, h1_ref, h1b_ref):
    hb = hb_ref[...]
    xq = jnp.dot(hb, wxq_ref[...], preferred_element_type=F32)
    xo = []
    for hh in range(XA_HEADS):
        sl = slice(hh * XA_DH, (hh + 1) * XA_DH)
        s = _dot_nt(xq[:, sl], mk_ref[0, :, sl]) * (XA_DH ** -0.5)
        s = s - jnp.max(s, -1, keepdims=True)
        p = jnp.exp(s)
        p = p / jnp.sum(p, -1, keepdims=True)
        xo.append(_dot(p, mv_ref[0, :, sl]))
    xo = jnp.concatenate(xo, axis=1)
    mixed = None
    for n, br in enumerate((ret_ref[...], dn_ref[...], xo)):
        gate = _sigmoid(jnp.dot(hb, wg_ref[:, n * D_MODEL:(n + 1) * D_MODEL], preferred_element_type=F32))
        term = gate * _dot(br, wb_ref[n])
        mixed = term if mixed is None else mixed + term
    y = ALPHA * h_ref[...] + _dot(mixed, wo_ref[...])
    h1 = _layernorm(y, g1_ref[...], b1_ref[...])
    h1_ref[...] = h1
    h1b_ref[...] = h1.astype(BF16)


def _mix(h, hb, ret_o, dn_o, mem_k, mem_v, w_xq, w_gate, w_branch, w_out, ln_g, ln_b, tm, tiles_per_batch):
    n_tok = h.shape[0]
    const2 = lambda i: (0, 0)
    row = lambda i: (i, 0)
    single = pl.Buffered(1)
    return pl.pallas_call(
        _mix_kernel,
        grid=(n_tok // tm,),
        in_specs=[
            pl.BlockSpec((tm, D_MODEL), row),
            pl.BlockSpec((tm, D_MODEL), row),
            pl.BlockSpec((tm, D_MODEL), row),
            pl.BlockSpec((tm, D_MODEL), row),
            pl.BlockSpec((1, N_MEM, D_MODEL), lambda i: (i // tiles_per_batch, 0, 0)),
            pl.BlockSpec((1, N_MEM, D_MODEL), lambda i: (i // tiles_per_batch, 0, 0)),
            pl.BlockSpec((D_MODEL, D_MODEL), const2, pipeline_mode=single),
            pl.BlockSpec((D_MODEL, 3 * D_MODEL), const2, pipeline_mode=single),
            pl.BlockSpec((3, D_MODEL, D_MODEL), lambda i: (0, 0, 0), pipeline_mode=single),
            pl.BlockSpec((D_MODEL, D_MODEL), const2, pipeline_mode=single),
            pl.BlockSpec((1, D_MODEL), const2),
            pl.BlockSpec((1, D_MODEL), const2),
        ],
        out_specs=[pl.BlockSpec((tm, D_MODEL), row), pl.BlockSpec((tm, D_MODEL), row)],
        out_shape=[jax.ShapeDtypeStruct((n_tok, D_MODEL), F32),
                   jax.ShapeDtypeStruct((n_tok, D_MODEL), BF16)],
        compiler_params=_params("parallel"),
        name="mix",
    )(h, hb, ret_o, dn_o, mem_k, mem_v, w_xq, w_gate, w_branch, w_out, ln_g, ln_b)


def _route(x, wr, br):
    logits = _dot_exact(x, wr) + br
    lane_i = lax.broadcasted_iota(jnp.int32, logits.shape, 1)
    lane = lane_i.astype(F32)
    far = jnp.float32(LANES)
    neg = jnp.float32(-3.0e38)
    is_c = jnp.logical_and(lane_i >= MOE_NE, lane_i < MOE_NE + MOE_GROUPS)
    cl = jnp.where(is_c, logits, neg)
    cmax = jnp.max(cl, -1, keepdims=True)
    denom = jnp.sum(jnp.where(is_c, jnp.exp(jnp.where(is_c, logits - cmax, 0.0)), 0.0), -1, keepdims=True)
    p_grp = 1.0 / denom
    grp = jnp.min(jnp.where(jnp.logical_and(is_c, cl == cmax), lane - MOE_NE, far), -1, keepdims=True)
    in_grp = jnp.logical_and(lane_i < MOE_NE, _idiv(lane_i, MOE_EXPERTS).astype(F32) == grp)
    fl = jnp.where(in_grp, logits, neg)
    v1 = jnp.max(fl, -1, keepdims=True)
    i1 = jnp.min(jnp.where(jnp.logical_and(in_grp, fl == v1), lane, far), -1, keepdims=True)
    rest = jnp.logical_and(in_grp, lane != i1)
    fl2 = jnp.where(rest, logits, neg)
    v2 = jnp.max(fl2, -1, keepdims=True)
    i2 = jnp.min(jnp.where(jnp.logical_and(rest, fl2 == v2), lane, far), -1, keepdims=True)
    e2 = jnp.exp(v2 - v1)
    w1 = p_grp / (1.0 + e2)
    w2 = p_grp * e2 / (1.0 + e2)
    return jnp.where(lane == i1, w1, 0.0) + jnp.where(lane == i2, w2, 0.0)


def _moe_kernel(h1_ref, h1b_ref, wr_ref, br_ref, wg_ref, wu_ref, wd_ref, g2_ref, b2_ref,
                y_ref, comb_scr, acc_scr):
    e = pl.program_id(1)

    @pl.when(e == 0)
    def _():
        comb_scr[...] = _route(h1_ref[...], wr_ref[...], br_ref[...])
        acc_scr[...] = jnp.zeros_like(acc_scr)

    xb = h1b_ref[...]
    comb = comb_scr[...]
    lane = lax.broadcasted_iota(jnp.int32, comb.shape, 1)
    col = jnp.sum(jnp.where(lane == e, comb, 0.0), -1, keepdims=True)
    hg = _silu(jnp.dot(xb, wg_ref[0], preferred_element_type=F32)) * jnp.dot(
        xb, wu_ref[0], preferred_element_type=F32)
    acc_scr[...] += _dot(hg * col, wd_ref[0])

    @pl.when(e == MOE_NE - 1)
    def _():
        y_ref[...] = _layernorm(ALPHA * h1_ref[...] + acc_scr[...], g2_ref[...], b2_ref[...])


def _moe(h1, h1b, w_route, b_route, w_gate, w_up, w_down, ln_g, ln_b, tm):
    n_tok = h1.shape[0]
    return pl.pallas_call(
        _moe_kernel,
        grid=(n_tok // tm, MOE_NE),
        in_specs=[
            pl.BlockSpec((tm, D_MODEL), lambda i, e: (i, 0)),
            pl.BlockSpec((tm, D_MODEL), lambda i, e: (i, 0)),
            pl.BlockSpec((D_MODEL, LANES), lambda i, e: (0, 0)),
            pl.BlockSpec((1, LANES), lambda i, e: (0, 0)),
            pl.BlockSpec((1, D_MODEL, MOE_DFF), lambda i, e: (e, 0, 0)),
            pl.BlockSpec((1, D_MODEL, MOE_DFF), lambda i, e: (e, 0, 0)),
            pl.BlockSpec((1, MOE_DFF, D_MODEL), lambda i, e: (e, 0, 0)),
            pl.BlockSpec((1, D_MODEL), lambda i, e: (0, 0)),
            pl.BlockSpec((1, D_MODEL), lambda i, e: (0, 0)),
        ],
        out_specs=pl.BlockSpec((tm, D_MODEL), lambda i, e: (i, 0)),
        out_shape=jax.ShapeDtypeStruct((n_tok, D_MODEL), F32),
        scratch_shapes=[pltpu.VMEM((tm, LANES), F32), pltpu.VMEM((tm, D_MODEL), F32)],
        compiler_params=_params("parallel", "arbitrary"),
        name="moe",
    )(h1, h1b, w_route, b_route, w_gate, w_up, w_down, ln_g, ln_b)


def _rope_tables(pos):
    half = RET_DK // 2
    inv = 1.0 / (ROPE_BASE ** (jnp.arange(half, dtype=F32) / half))
    ang = pos.astype(F32)[:, None] * inv[None, :]
    cos = jnp.cos(ang)
    sin = jnp.sin(ang)
    return jnp.concatenate([cos, cos], -1), jnp.concatenate([-sin, sin], -1)


def _group(x, pos, mem_k, mem_v, ret_s0, dn_s0, conv_buf, wts, *, tm, tm_mix, ret_chunk, dn_blk, dn_chunk):
    nb, t, d = x.shape
    n_tok = nb * t
    h, hb, small, small_t, proj = _in_proj(x.reshape(n_tok, d), wts["ln_in_g"], wts["ln_in_b"],
                                           wts["w_main"], wts["w_small"], wts["w_small_t"], tm)
    proj3 = proj.reshape(nb, t, PROJ_W)
    cos2, sin2 = _rope_tables(pos)
    ret_o, ret_s = _retention(proj3, cos2, sin2, ret_s0, wts["ret_gn_g"], wts["ret_gn_b"], ret_chunk)
    small_t3 = small_t.reshape(16, nb, t).transpose(1, 0, 2)
    conv_init = jnp.concatenate([jnp.zeros((nb, 8 - (DN_CONV_W - 1), DN_CONV_CH), F32), conv_buf], axis=1)
    dn_o, dn_s, conv_new = _deltanet(proj3, small, small_t3, wts["dn_conv"], conv_init, wts["dn_prow"],
                                     wts["dn_pcol"], wts["dn_norm_g"], dn_s0, dn_blk, dn_chunk)
    h1, h1b = _mix(h, hb, ret_o.reshape(n_tok, d), dn_o.reshape(n_tok, d), mem_k, mem_v,
                   wts["w_xq"], wts["w_gate"], wts["w_branch"], wts["w_out"], wts["ln1_g"], wts["ln1_b"],
                   tm_mix, t // tm_mix)
    y = _moe(h1, h1b, wts["w_route"], wts["b_route"], wts["moe_w_gate"], wts["moe_w_up"], wts["moe_w_down"],
             wts["ln2_g"], wts["ln2_b"], tm)
    return y.reshape(nb, t, d), ret_s, dn_s, conv_new


def kernel(x_prompt, x_sample, mem_prompt, state_ret, state_dn, state_dn_conv, cache_mem_k, cache_mem_v,
           ln_in_g, ln_in_b, w_in, ret_gn_g, ret_gn_b, dn_conv, dn_A_log, dn_dt_bias, dn_norm_g,
           w_mem_kv, w_branch, w_out, ln1_g, ln1_b, moe_w_coarse, moe_b_coarse, moe_w_fine, moe_b_fine,
           moe_w_gate, moe_w_up, moe_w_down, ln2_g, ln2_b):
    bp, tp, d = x_prompt.shape
    bs, ts, _ = x_sample.shape
    l = 0
    wi = w_in[l]
    rq, rk, rv, rg = wi[:, 0:512], wi[:, 512:1024], wi[:, 1024:2048], wi[:, 2048:3072]
    dqkv, dz = wi[:, 3072:5120], wi[:, 5120:6144]
    dba = wi[:, 6144:6160]
    xq, gates = wi[:, 6160:7184], wi[:, 7184:10256]
    wts = {
        "ln_in_g": ln_in_g.reshape(1, d), "ln_in_b": ln_in_b.reshape(1, d),
        "w_main": jnp.concatenate([dqkv, dz, rq, rk, rv, rg], axis=1).astype(BF16),
        "w_small": jnp.pad(dba, ((0, 0), (0, LANES - 16))).astype(BF16),
        "w_small_t": dba.T.astype(BF16),
        "ret_gn_g": ret_gn_g[l].reshape(1, -1), "ret_gn_b": ret_gn_b[l].reshape(1, -1),
        "dn_conv": dn_conv[l],
        "dn_prow": jnp.zeros((8, LANES), F32).at[0, 8:16].set(-jnp.exp(dn_A_log[l])).at[1, 8:16].set(dn_dt_bias[l]),
        "dn_pcol": jnp.zeros((16, LANES), F32).at[8:16, 0].set(-jnp.exp(dn_A_log[l])).at[8:16, 1].set(dn_dt_bias[l]),
        "dn_norm_g": dn_norm_g[l].reshape(1, -1),
        "w_xq": xq.astype(BF16), "w_gate": gates.astype(BF16),
        "w_branch": w_branch[l].astype(BF16), "w_out": w_out[l].astype(BF16),
        "ln1_g": ln1_g[l].reshape(1, d), "ln1_b": ln1_b[l].reshape(1, d),
        "w_route": jnp.pad(jnp.concatenate([moe_w_fine[l], moe_w_coarse[l]], axis=1),
                           ((0, 0), (0, LANES - MOE_NE - MOE_GROUPS))),
        "b_route": jnp.pad(jnp.concatenate([moe_b_fine[l], moe_b_coarse[l]]),
                           (0, LANES - MOE_NE - MOE_GROUPS)).reshape(1, LANES),
        "moe_w_gate": moe_w_gate[l].reshape(MOE_NE, d, MOE_DFF).astype(BF16),
        "moe_w_up": moe_w_up[l].reshape(MOE_NE, d, MOE_DFF).astype(BF16),
        "moe_w_down": moe_w_down[l].reshape(MOE_NE, MOE_DFF, d).astype(BF16),
        "ln2_g": ln2_g[l].reshape(1, d), "ln2_b": ln2_b[l].reshape(1, d),
    }

    mkv = _memkv(mem_prompt.reshape(bp * N_MEM, d), w_mem_kv[l].astype(BF16))
    mk = mkv[:, :XA_HEADS * XA_DH].reshape(bp, N_MEM, XA_HEADS * XA_DH)
    mv = mkv[:, XA_HEADS * XA_DH:].reshape(bp, N_MEM, XA_HEADS * XA_DH)
    yp, rs_p, ds_p, cb_p = _group(
        x_prompt, jnp.arange(tp, dtype=jnp.int32), mk, mv,
        jnp.zeros((bp, RET_HEADS, RET_DK, RET_DV), F32),
        jnp.zeros((bp, DN_V_HEADS, DN_DK, DN_DV), F32),
        jnp.zeros((bp, DN_CONV_W - 1, DN_CONV_CH), F32),
        wts, tm=min(512, tp), tm_mix=min(512, tp), ret_chunk=min(256, tp), dn_blk=min(256, tp),
        dn_chunk=min(DN_CHUNK, tp))
    ys, rs_s, ds_s, cb_s = _group(
        x_sample, PAST_LEN + jnp.arange(ts, dtype=jnp.int32),
        cache_mem_k[l].reshape(bs, N_MEM, XA_HEADS * XA_DH), cache_mem_v[l].reshape(bs, N_MEM, XA_HEADS * XA_DH),
        state_ret[l], state_dn[l], state_dn_conv[l],
        wts, tm=bs * ts, tm_mix=ts, ret_chunk=ts, dn_blk=ts, dn_chunk=min(DN_CHUNK, ts))
    return (yp, ys, rs_p[None], rs_s[None], ds_p[None], ds_s[None], cb_p[None], cb_s[None],
            mk.reshape(1, bp, N_MEM, XA_HEADS, XA_DH), mv.reshape(1, bp, N_MEM, XA_HEADS, XA_DH))
```

```python
import functools
import math

import jax
import jax.numpy as jnp
from jax import lax
from jax.experimental import pallas as pl
from jax.experimental.pallas import tpu as pltpu

F32 = jnp.float32
BF16 = jnp.bfloat16

D_MODEL = 1024
PAST_LEN = 1024
RET_HEADS = 4
RET_DK = 128
RET_DV = 256
ROPE_BASE = 10000.0
DN_QK_HEADS = 4
DN_V_HEADS = 8
DN_DK = 128
DN_DV = 128
DN_QK = DN_QK_HEADS * DN_DK
DN_CONV_W = 4
DN_CONV_CH = 2048
DN_CHUNK = 64
DN_SUB = 16
XA_HEADS = 4
XA_DH = 256
N_MEM = 256
MOE_GROUPS = 4
MOE_EXPERTS = 8
MOE_NE = MOE_GROUPS * MOE_EXPERTS
MOE_DFF = 256
DEPTH = 1
ALPHA = (2.0 * DEPTH) ** 0.25
LANES = 128
PROJ_W = 6144
VMEM_LIMIT = 56 * 1024 * 1024


def _dot(a, b):
    return jnp.dot(a.astype(BF16), b.astype(BF16), preferred_element_type=F32)


def _dot_nt(a, b):
    return lax.dot_general(a.astype(BF16), b.astype(BF16), (((1,), (1,)), ((), ())),
                           preferred_element_type=F32)


def _dot_tn(a, b):
    return lax.dot_general(a.astype(BF16), b.astype(BF16), (((0,), (0,)), ((), ())),
                           preferred_element_type=F32)


def _dot_exact(a, b):
    return jnp.dot(a, b, preferred_element_type=F32, precision=lax.Precision.HIGHEST)


def _layernorm(x, g, b, eps=1e-5):
    mu = jnp.mean(x, -1, keepdims=True)
    xc = x - mu
    var = jnp.mean(xc * xc, -1, keepdims=True)
    return xc * lax.rsqrt(var + eps) * g + b


def _silu(x):
    return x * (1.0 / (1.0 + jnp.exp(-x)))


def _sigmoid(x):
    return 1.0 / (1.0 + jnp.exp(-x))


def _softplus(x):
    return jnp.maximum(x, 0.0) + jnp.log(1.0 + jnp.exp(-jnp.abs(x)))


def _idiv(x, n):
    return jnp.right_shift(x, int(math.log2(n)))


def _params(*sem):
    return pltpu.CompilerParams(dimension_semantics=sem, vmem_limit_bytes=VMEM_LIMIT)


def _in_proj_kernel(x_ref, g_ref, b_ref, w_ref, ws_ref, wst_ref,
                    h_ref, hb_ref, small_ref, smallt_ref, proj_ref, hs_ref):
    @pl.when(pl.program_id(1) == 0)
    def _():
        h = _layernorm(x_ref[...], g_ref[...], b_ref[...])
        h_ref[...] = h
        hb = h.astype(BF16)
        hb_ref[...] = hb
        hs_ref[...] = hb
        small_ref[...] = jnp.dot(hb, ws_ref[...], preferred_element_type=F32)
        smallt_ref[...] = lax.dot_general(wst_ref[...], hb, (((1,), (1,)), ((), ())),
                                          preferred_element_type=F32)

    proj_ref[...] = jnp.dot(hs_ref[...], w_ref[...], preferred_element_type=F32)


def _in_proj(x, ln_g, ln_b, w_main, w_small, w_small_t, tm, tn=1024):
    n_tok = x.shape[0]
    grid = (n_tok // tm, PROJ_W // tn)
    return pl.pallas_call(
        _in_proj_kernel,
        grid=grid,
        in_specs=[
            pl.BlockSpec((tm, D_MODEL), lambda i, n: (i, 0)),
            pl.BlockSpec((1, D_MODEL), lambda i, n: (0, 0)),
            pl.BlockSpec((1, D_MODEL), lambda i, n: (0, 0)),
            pl.BlockSpec((D_MODEL, tn), lambda i, n: (0, n)),
            pl.BlockSpec((D_MODEL, LANES), lambda i, n: (0, 0)),
            pl.BlockSpec((16, D_MODEL), lambda i, n: (0, 0)),
        ],
        out_specs=[
            pl.BlockSpec((tm, D_MODEL), lambda i, n: (i, 0)),
            pl.BlockSpec((tm, D_MODEL), lambda i, n: (i, 0)),
            pl.BlockSpec((tm, LANES), lambda i, n: (i, 0)),
            pl.BlockSpec((16, tm), lambda i, n: (0, i)),
            pl.BlockSpec((tm, tn), lambda i, n: (i, n)),
        ],
        out_shape=[
            jax.ShapeDtypeStruct((n_tok, D_MODEL), F32),
            jax.ShapeDtypeStruct((n_tok, D_MODEL), BF16),
            jax.ShapeDtypeStruct((n_tok, LANES), F32),
            jax.ShapeDtypeStruct((16, n_tok), F32),
            jax.ShapeDtypeStruct((n_tok, PROJ_W), F32),
        ],
        scratch_shapes=[pltpu.VMEM((tm, D_MODEL), BF16)],
        compiler_params=_params("parallel", "arbitrary"),
        name="in_proj",
    )(x, ln_g, ln_b, w_main, w_small, w_small_t)


def _ret_kernel(q_ref, k_ref, v_ref, rg_ref, cos_ref, sin_ref, s0_ref, gng_ref, gnb_ref,
                o_ref, sout_ref, s_scr, *, chunk):
    @pl.when(pl.program_id(1) == 0)
    def _():
        s_scr[...] = s0_ref[0]

    heads = range(RET_HEADS)
    lg = [math.log(1.0 - 2.0 ** (-5.0 - h)) for h in heads]
    cos = cos_ref[...]
    sin = sin_ref[...]
    ri = lax.broadcasted_iota(jnp.int32, (chunk, chunk), 0)
    ci = lax.broadcasted_iota(jnp.int32, (chunk, chunk), 1)
    causal = ri >= ci
    diff = jnp.where(causal, (ri - ci).astype(F32), 0.0)
    idx = lax.broadcasted_iota(jnp.int32, (chunk, 1), 0).astype(F32)
    q = [q_ref[0, :, h * RET_DK:(h + 1) * RET_DK] for h in heads]
    k = [k_ref[0, :, h * RET_DK:(h + 1) * RET_DK] for h in heads]
    v = [v_ref[0, :, h * RET_DV:(h + 1) * RET_DV] for h in heads]
    qr = [q[h] * cos + pltpu.roll(q[h], RET_DK // 2, 1) * sin for h in heads]
    kr = [(k[h] * cos + pltpu.roll(k[h], RET_DK // 2, 1) * sin) * (RET_DK ** -0.5) for h in heads]
    scores = [_dot_nt(qr[h], kr[h]) * jnp.where(causal, jnp.exp(diff * lg[h]), 0.0) for h in heads]
    s = [s_scr[h] for h in heads]
    cross = [_dot(qr[h], s[h]) * jnp.exp((idx + 1.0) * lg[h]) for h in heads]
    o = [_dot(scores[h], v[h]) + cross[h] for h in heads]
    kv = [_dot_tn(kr[h] * jnp.exp((chunk - 1.0 - idx) * lg[h]), v[h]) for h in heads]
    for h in heads:
        s_new = s[h] * math.exp(chunk * lg[h]) + kv[h]
        s_scr[h] = s_new
        sout_ref[0, h] = s_new
        sl = slice(h * RET_DV, (h + 1) * RET_DV)
        o_ref[0, :, sl] = _layernorm(o[h], gng_ref[:, sl], gnb_ref[:, sl]) * _silu(rg_ref[0, :, sl])


def _retention(proj, cos2, sin2, s0, gn_g, gn_b, chunk):
    nb, t = proj.shape[:2]
    grid = (nb, t // chunk)
    qk_w = RET_HEADS * RET_DK
    v_w = RET_HEADS * RET_DV
    return pl.pallas_call(
        functools.partial(_ret_kernel, chunk=chunk),
        grid=grid,
        in_specs=[
            pl.BlockSpec((1, chunk, qk_w), lambda b, c: (b, c, 3072 // qk_w)),
            pl.BlockSpec((1, chunk, qk_w), lambda b, c: (b, c, 3584 // qk_w)),
            pl.BlockSpec((1, chunk, v_w), lambda b, c: (b, c, 4096 // v_w)),
            pl.BlockSpec((1, chunk, v_w), lambda b, c: (b, c, 5120 // v_w)),
            pl.BlockSpec((chunk, RET_DK), lambda b, c: (c, 0)),
            pl.BlockSpec((chunk, RET_DK), lambda b, c: (c, 0)),
            pl.BlockSpec((1, RET_HEADS, RET_DK, RET_DV), lambda b, c: (b, 0, 0, 0)),
            pl.BlockSpec((1, v_w), lambda b, c: (0, 0)),
            pl.BlockSpec((1, v_w), lambda b, c: (0, 0)),
        ],
        out_specs=[
            pl.BlockSpec((1, chunk, v_w), lambda b, c: (b, c, 0)),
            pl.BlockSpec((1, RET_HEADS, RET_DK, RET_DV), lambda b, c: (b, 0, 0, 0)),
        ],
        out_shape=[
            jax.ShapeDtypeStruct((nb, t, v_w), F32),
            jax.ShapeDtypeStruct((nb, RET_HEADS, RET_DK, RET_DV), F32),
        ],
        scratch_shapes=[pltpu.VMEM((RET_HEADS, RET_DK, RET_DV), F32)],
        compiler_params=_params("parallel", "arbitrary"),
        name="retention",
    )(proj, proj, proj, proj, cos2, sin2, s0, gn_g, gn_b)


def _dn_kernel(x_ref, z_ref, sm_ref, smt_ref, cw_ref, cinit_ref, prow_ref, pcol_ref, ng_ref, s0_ref,
               o_ref, sout_ref, cout_ref, xext, s_scr, *, blk_len, chunk):
    L = blk_len
    blk = pl.program_id(1)

    @pl.when(blk == 0)
    def _():
        xext[0:8, :] = cinit_ref[0]
        s_scr[...] = s0_ref[0]

    x = x_ref[0]
    xext[8:8 + L, :] = x
    conv = x * cw_ref[3:4, :]
    for w in range(DN_CONV_W - 1):
        conv = conv + xext[5 + w:5 + w + L, :] * cw_ref[w:w + 1, :]
    conv = _silu(conv)
    xext[0:8, :] = x[L - 8:L, :]
    cout_ref[0] = x[L - (DN_CONV_W - 1):L, :]

    sm = sm_ref[...]
    prow = prow_ref[...]
    beta_tm = _sigmoid(sm)
    g_tm = prow[0:1, :] * _softplus(sm + prow[1:2, :])
    smt = smt_ref[0]
    pcol = pcol_ref[...]
    g_hm = pcol[8:16, 0:1] * _softplus(smt[8:16, :] + pcol[8:16, 1:2])

    ri = lax.broadcasted_iota(jnp.int32, (L, L), 0)
    ci = lax.broadcasted_iota(jnp.int32, (L, L), 1)
    same_chunk = _idiv(ri, chunk) == _idiv(ci, chunk)
    lower = jnp.logical_and(ri >= ci, same_chunk)
    strict = jnp.logical_and(ri > ci, same_chunk)
    same_sub = _idiv(ri, DN_SUB) == _idiv(ci, DN_SUB)
    tri = jnp.where(lower, 1.0, 0.0).astype(F32)
    tri_t = jnp.where(jnp.logical_and(ri <= ci, same_chunk), 1.0, 0.0).astype(F32)
    gcum_tm = _dot_exact(tri, g_tm)
    gcum_hm = _dot_exact(g_hm, tri_t)
    egc_tm = jnp.exp(gcum_tm)

    n_sub = L // chunk
    rep = DN_V_HEADS // DN_QK_HEADS
    heads = range(DN_V_HEADS)
    q, k, kk, qk = [], [], [], []
    for j in range(DN_QK_HEADS):
        qj = conv[:, j * DN_DK:(j + 1) * DN_DK]
        kj = conv[:, DN_QK + j * DN_DK:DN_QK + (j + 1) * DN_DK]
        qj = qj * lax.rsqrt(jnp.sum(qj * qj, -1, keepdims=True) + 1e-6) * (DN_DK ** -0.5)
        kj = kj * lax.rsqrt(jnp.sum(kj * kj, -1, keepdims=True) + 1e-6)
        kq = _dot_nt(jnp.concatenate([kj, qj], axis=0), kj)
        q.append(qj)
        k.append(kj)
        kk.append(kq[0:L])
        qk.append(kq[L:2 * L])
    gc = [gcum_tm[:, 8 + hh:9 + hh] for hh in heads]
    beta = [beta_tm[:, hh:hh + 1] for hh in heads]
    eg = [egc_tm[:, 8 + hh:9 + hh] for hh in heads]
    rel = [jnp.where(lower, jnp.exp(jnp.where(lower, gc[hh] - gcum_hm[hh:hh + 1, :], 0.0)), 0.0) for hh in heads]
    a = [jnp.where(strict, beta[hh] * kk[hh // rep] * rel[hh], 0.0) for hh in heads]
    attn = [qk[hh // rep] * rel[hh] for hh in heads]
    pd = [jnp.where(same_sub, -a[hh], 0.0) for hh in heads]
    e = [a[hh] + pd[hh] for hh in heads]
    xacc = pd
    pw = pd
    for _ in range(3):
        pw = [_dot(pw[hh], pw[hh]) for hh in heads]
        xacc = [xacc[hh] + pw[hh] + _dot(xacc[hh], pw[hh]) for hh in heads]
    f = [e[hh] + _dot(xacc[hh], e[hh]) for hh in heads]
    f2 = [_dot(f[hh], f[hh]) for hh in heads]
    y = [f2[hh] - f[hh] - _dot(f[hh], f2[hh]) for hh in heads]
    rinv = [xacc[hh] + y[hh] + _dot(y[hh], xacc[hh]) for hh in heads]
    sol = []
    for hh in heads:
        vh = conv[:, 2 * DN_QK + hh * DN_DV:2 * DN_QK + (hh + 1) * DN_DV]
        rhs = jnp.concatenate([vh * beta[hh], k[hh // rep] * (beta[hh] * eg[hh])], axis=1)
        sol.append(rhs + _dot(rinv[hh], rhs))
    u = [sol[hh][:, 0:DN_DV] for hh in heads]
    wmat = [sol[hh][:, DN_DV:DN_DV + DN_DK] for hh in heads]
    qe = [q[hh // rep] * eg[hh] for hh in heads]
    s = [s_scr[hh] for hh in heads]
    v_new = [[] for _ in heads]
    q_s = [[] for _ in heads]
    for sc in range(n_sub):
        r0 = sc * chunk
        for hh in heads:
            ws = _dot(jnp.concatenate([wmat[hh][r0:r0 + chunk], qe[hh][r0:r0 + chunk]], axis=0), s[hh])
            vn = u[hh][r0:r0 + chunk] - ws[0:chunk]
            v_new[hh].append(vn)
            q_s[hh].append(ws[chunk:2 * chunk])
            g_last = gcum_tm[r0 + chunk - 1:r0 + chunk, 8 + hh:9 + hh]
            kd = k[hh // rep][r0:r0 + chunk] * jnp.exp(g_last - gc[hh][r0:r0 + chunk])
            s[hh] = s[hh] * jnp.exp(g_last) + _dot_tn(kd, vn)
    for hh in heads:
        s_scr[hh] = s[hh]
        sout_ref[0, hh] = s[hh]
        vn = v_new[hh][0] if n_sub == 1 else jnp.concatenate(v_new[hh], axis=0)
        qs = q_s[hh][0] if n_sub == 1 else jnp.concatenate(q_s[hh], axis=0)
        o = qs + _dot(attn[hh], vn)
        zh = z_ref[0, :, hh * DN_DV:(hh + 1) * DN_DV]
        n = o * lax.rsqrt(jnp.mean(o * o, -1, keepdims=True) + 1e-6) * ng_ref[...]
        o_ref[0, :, hh * DN_DV:(hh + 1) * DN_DV] = n * _silu(zh)


def _deltanet(proj, small, small_t, conv_w, conv_init, prow, pcol, norm_g, s0, blk_len, chunk):
    nb, t = proj.shape[:2]
    nblk = t // blk_len
    grid = (nb, nblk)
    return pl.pallas_call(
        functools.partial(_dn_kernel, blk_len=blk_len, chunk=chunk),
        grid=grid,
        in_specs=[
            pl.BlockSpec((1, blk_len, DN_CONV_CH), lambda b, i: (b, i, 0)),
            pl.BlockSpec((1, blk_len, 1024), lambda b, i: (b, i, 2)),
            pl.BlockSpec((blk_len, LANES), lambda b, i: (b * nblk + i, 0)),
            pl.BlockSpec((1, 16, blk_len), lambda b, i: (b, 0, i)),
            pl.BlockSpec((DN_CONV_W, DN_CONV_CH), lambda b, i: (0, 0)),
            pl.BlockSpec((1, 8, DN_CONV_CH), lambda b, i: (b, 0, 0)),
            pl.BlockSpec((8, LANES), lambda b, i: (0, 0)),
            pl.BlockSpec((16, LANES), lambda b, i: (0, 0)),
            pl.BlockSpec((1, DN_DV), lambda b, i: (0, 0)),
            pl.BlockSpec((1, DN_V_HEADS, DN_DK, DN_DV), lambda b, i: (b, 0, 0, 0)),
        ],
        out_specs=[
            pl.BlockSpec((1, blk_len, 1024), lambda b, i: (b, i, 0)),
            pl.BlockSpec((1, DN_V_HEADS, DN_DK, DN_DV), lambda b, i: (b, 0, 0, 0)),
            pl.BlockSpec((1, DN_CONV_W - 1, DN_CONV_CH), lambda b, i: (b, 0, 0)),
        ],
        out_shape=[
            jax.ShapeDtypeStruct((nb, t, 1024), F32),
            jax.ShapeDtypeStruct((nb, DN_V_HEADS, DN_DK, DN_DV), F32),
            jax.ShapeDtypeStruct((nb, DN_CONV_W - 1, DN_CONV_CH), F32),
        ],
        scratch_shapes=[pltpu.VMEM((blk_len + 8, DN_CONV_CH), F32),
                        pltpu.VMEM((DN_V_HEADS, DN_DK, DN_DV), F32)],
        compiler_params=_params("parallel", "arbitrary"),
        name="deltanet",
    )(proj, proj, small, small_t, conv_w, conv_init, prow, pcol, norm_g, s0)


def _memkv_kernel(m_ref, w_ref, o_ref):
    o_ref[...] = _dot(m_ref[...], w_ref[...])


def _memkv(mem, w_kv):
    n = mem.shape[0]
    tn = 1024
    return pl.pallas_call(
        _memkv_kernel,
        grid=(w_kv.shape[1] // tn,),
        in_specs=[pl.BlockSpec((n, D_MODEL), lambda j: (0, 0)),
                  pl.BlockSpec((D_MODEL, tn), lambda j: (0, j))],
        out_specs=pl.BlockSpec((n, tn), lambda j: (0, j)),
        out_shape=jax.ShapeDtypeStruct((n, w_kv.shape[1]), F32),
        compiler_params=_params("parallel"),
        name="memkv",
    )(mem, w_kv)


def _mix_kernel(h_ref, hb_ref, ret_ref, dn_ref, mk_ref, mv_ref, wxq_ref, wg_ref, wb_ref, wo_ref,
                g1_ref, b1_ref, h1_ref, h1b_ref):
    hb = hb_ref[...]
    xq = jnp.dot(hb, wxq_ref[...], preferred_element_type=F32)
    xo = []
    for hh in range(XA_HEADS):
        sl = slice(hh * XA_DH, (hh + 1) * XA_DH)
        s = _dot_nt(xq[:, sl], mk_ref[0, :, sl]) * (XA_DH ** -0.5)
        s = s - jnp.max(s, -1, keepdims=True)
        p = jnp.exp(s)
        p = p / jnp.sum(p, -1, keepdims=True)
        xo.append(_dot(p, mv_ref[0, :, sl]))
    xo = jnp.concatenate(xo, axis=1)
    mixed = None
    for n, br in enumerate((ret_ref[...], dn_ref[...], xo)):
        gate = _sigmoid(jnp.dot(hb, wg_ref[:, n * D_MODEL:(n + 1) * D_MODEL], preferred_element_type=F32))
        term = gate * _dot(br, wb_ref[n])
        mixed = term if mixed is None else mixed + term
    y = ALPHA * h_ref[...] + _dot(mixed, wo_ref[...])
    h1 = _layernorm(y, g1_ref[...], b1_ref[...])
    h1_ref[...] = h1
    h1b_ref[...] = h1.astype(BF16)


def _mix(h, hb, ret_o, dn_o, mem_k, mem_v, w_xq, w_gate, w_branch, w_out, ln_g, ln_b, tm, tiles_per_batch):
    n_tok = h.shape[0]
    const2 = lambda i: (0, 0)
    row = lambda i: (i, 0)
    single = pl.Buffered(1)
    return pl.pallas_call(
        _mix_kernel,
        grid=(n_tok // tm,),
        in_specs=[
            pl.BlockSpec((tm, D_MODEL), row),
            pl.BlockSpec((tm, D_MODEL), row),
            pl.BlockSpec((tm, D_MODEL), row),
            pl.BlockSpec((tm, D_MODEL), row),
            pl.BlockSpec((1, N_MEM, D_MODEL), lambda i: (i // tiles_per_batch, 0, 0)),
            pl.BlockSpec((1, N_MEM, D_MODEL), lambda i: (i // tiles_per_batch, 0, 0)),
            pl.BlockSpec((D_MODEL, D_MODEL), const2, pipeline_mode=single),
            pl.BlockSpec((D_MODEL, 3 * D_MODEL), const2, pipeline_mode=single),
            pl.BlockSpec((3, D_MODEL, D_MODEL), lambda i: (0, 0, 0), pipeline_mode=single),
            pl.BlockSpec((D_MODEL, D_MODEL), const2, pipeline_mode=single),
            pl.BlockSpec((1, D_MODEL), const2),
            pl.BlockSpec((1, D_MODEL), const2),
        ],
        out_specs=[pl.BlockSpec((tm, D_MODEL), row), pl.BlockSpec((tm, D_MODEL), row)],
        out_shape=[jax.ShapeDtypeStruct((n_tok, D_MODEL), F32),
                   jax.ShapeDtypeStruct((n_tok, D_MODEL), BF16)],
        compiler_params=_params("parallel"),
        name="mix",
    )(h, hb, ret_o, dn_o, mem_k, mem_v, w_xq, w_gate, w_branch, w_out, ln_g, ln_b)


def _route(x, wr, br):
    logits = _dot_exact(x, wr) + br
    lane_i = lax.broadcasted_iota(jnp.int32, logits.shape, 1)
    lane = lane_i.astype(F32)
    far = jnp.float32(LANES)
    neg = jnp.float32(-3.0e38)
    is_c = jnp.logical_and(lane_i >= MOE_NE, lane_i < MOE_NE + MOE_GROUPS)
    cl = jnp.where(is_c, logits, neg)
    cmax = jnp.max(cl, -1, keepdims=True)
    denom = jnp.sum(jnp.where(is_c, jnp.exp(jnp.where(is_c, logits - cmax, 0.0)), 0.0), -1, keepdims=True)
    p_grp = 1.0 / denom
    grp = jnp.min(jnp.where(jnp.logical_and(is_c, cl == cmax), lane - MOE_NE, far), -1, keepdims=True)
    in_grp = jnp.logical_and(lane_i < MOE_NE, _idiv(lane_i, MOE_EXPERTS).astype(F32) == grp)
    fl = jnp.where(in_grp, logits, neg)
    v1 = jnp.max(fl, -1, keepdims=True)
    i1 = jnp.min(jnp.where(jnp.logical_and(in_grp, fl == v1), lane, far), -1, keepdims=True)
    rest = jnp.logical_and(in_grp, lane != i1)
    fl2 = jnp.where(rest, logits, neg)
    v2 = jnp.max(fl2, -1, keepdims=True)
    i2 = jnp.min(jnp.where(jnp.logical_and(rest, fl2 == v2), lane, far), -1, keepdims=True)
    e2 = jnp.exp(v2 - v1)
    w1 = p_grp / (1.0 + e2)
    w2 = p_grp * e2 / (1.0 + e2)
    return jnp.where(lane == i1, w1, 0.0) + jnp.where(lane == i2, w2, 0.0)


def _moe_kernel(h1_ref, h1b_ref, wr_ref, br_ref, wg_ref, wu_ref, wd_ref, g2_ref, b2_ref,
                y_ref, comb_scr, acc_scr):
    e = pl.program_id(1)

    @pl.when(e == 0)
    def _():
        comb_scr[...] = _route(h1_ref[...], wr_ref[...], br_ref[...])
        acc_scr[...] = jnp.zeros_like(acc_scr)

    xb = h1b_ref[...]
    comb = comb_scr[...]
    lane = lax.broadcasted_iota(jnp.int32, comb.shape, 1)
    col = jnp.sum(jnp.where(lane == e, comb, 0.0), -1, keepdims=True)
    hg = _silu(jnp.dot(xb, wg_ref[0], preferred_element_type=F32)) * jnp.dot(
        xb, wu_ref[0], preferred_element_type=F32)
    acc_scr[...] += _dot(hg * col, wd_ref[0])

    @pl.when(e == MOE_NE - 1)
    def _():
        y_ref[...] = _layernorm(ALPHA * h1_ref[...] + acc_scr[...], g2_ref[...], b2_ref[...])


def _moe(h1, h1b, w_route, b_route, w_gate, w_up, w_down, ln_g, ln_b, tm):
    n_tok = h1.shape[0]
    return pl.pallas_call(
        _moe_kernel,
        grid=(n_tok // tm, MOE_NE),
        in_specs=[
            pl.BlockSpec((tm, D_MODEL), lambda i, e: (i, 0)),
            pl.BlockSpec((tm, D_MODEL), lambda i, e: (i, 0)),
            pl.BlockSpec((D_MODEL, LANES), lambda i, e: (0, 0)),
            pl.BlockSpec((1, LANES), lambda i, e: (0, 0)),
            pl.BlockSpec((1, D_MODEL, MOE_DFF), lambda i, e: (e, 0, 0)),
            pl.BlockSpec((1, D_MODEL, MOE_DFF), lambda i, e: (e, 0, 0)),
            pl.BlockSpec((1, MOE_DFF, D_MODEL), lambda i, e: (e, 0, 0)),
            pl.BlockSpec((1, D_MODEL), lambda i, e: (0, 0)),
            pl.BlockSpec((1, D_MODEL), lambda i, e: (0, 0)),
        ],
        out_specs=pl.BlockSpec((tm, D_MODEL), lambda i, e: (i, 0)),
        out_shape=jax.ShapeDtypeStruct((n_tok, D_MODEL), F32),
        scratch_shapes=[pltpu.VMEM((tm, LANES), F32), pltpu.VMEM((tm, D_MODEL), F32)],
        compiler_params=_params("parallel", "arbitrary"),
        name="moe",
    )(h1, h1b, w_route, b_route, w_gate, w_up, w_down, ln_g, ln_b)


def _rope_tables(pos):
    half = RET_DK // 2
    inv = 1.0 / (ROPE_BASE ** (jnp.arange(half, dtype=F32) / half))
    ang = pos.astype(F32)[:, None] * inv[None, :]
    cos = jnp.cos(ang)
    sin = jnp.sin(ang)
    return jnp.concatenate([cos, cos], -1), jnp.concatenate([-sin, sin], -1)


def _group(x, pos, mem_k, mem_v, ret_s0, dn_s0, conv_buf, wts, *, tm, tm_mix, ret_chunk, dn_blk, dn_chunk):
    nb, t, d = x.shape
    n_tok = nb * t
    h, hb, small, small_t, proj = _in_proj(x.reshape(n_tok, d), wts["ln_in_g"], wts["ln_in_b"],
                                           wts["w_main"], wts["w_small"], wts["w_small_t"], tm)
    proj3 = proj.reshape(nb, t, PROJ_W)
    cos2, sin2 = _rope_tables(pos)
    ret_o, ret_s = _retention(proj3, cos2, sin2, ret_s0, wts["ret_gn_g"], wts["ret_gn_b"], ret_chunk)
    small_t3 = small_t.reshape(16, nb, t).transpose(1, 0, 2)
    conv_init = jnp.concatenate([jnp.zeros((nb, 8 - (DN_CONV_W - 1), DN_CONV_CH), F32), conv_buf], axis=1)
    dn_o, dn_s, conv_new = _deltanet(proj3, small, small_t3, wts["dn_conv"], conv_init, wts["dn_prow"],
                                     wts["dn_pcol"], wts["dn_norm_g"], dn_s0, dn_blk, dn_chunk)
    h1, h1b = _mix(h, hb, ret_o.reshape(n_tok, d), dn_o.reshape(n_tok, d), mem_k, mem_v,
                   wts["w_xq"], wts["w_gate"], wts["w_branch"], wts["w_out"], wts["ln1_g"], wts["ln1_b"],
                   tm_mix, t // tm_mix)
    y = _moe(h1, h1b, wts["w_route"], wts["b_route"], wts["moe_w_gate"], wts["moe_w_up"], wts["moe_w_down"],
             wts["ln2_g"], wts["ln2_b"], tm)
    return y.reshape(nb, t, d), ret_s, dn_s, conv_new


def kernel(x_prompt, x_sample, mem_prompt, state_ret, state_dn, state_dn_conv, cache_mem_k, cache_mem_v,
           ln_in_g, ln_in_b, w_in, ret_gn_g, ret_gn_b, dn_conv, dn_A_log, dn_dt_bias, dn_norm_g,
           w_mem_kv, w_branch, w_out, ln1_g, ln1_b, moe_w_coarse, moe_b_coarse, moe_w_fine, moe_b_fine,
           moe_w_gate, moe_w_up, moe_w_down, ln2_g, ln2_b):
    bp, tp, d = x_prompt.shape
    bs, ts, _ = x_sample.shape
    l = 0
    wi = w_in[l]
    rq, rk, rv, rg = wi[:, 0:512], wi[:, 512:1024], wi[:, 1024:2048], wi[:, 2048:3072]
    dqkv, dz = wi[:, 3072:5120], wi[:, 5120:6144]
    dba = wi[:, 6144:6160]
    xq, gates = wi[:, 6160:7184], wi[:, 7184:10256]
    wts = {
        "ln_in_g": ln_in_g.reshape(1, d), "ln_in_b": ln_in_b.reshape(1, d),
        "w_main": jnp.concatenate([dqkv, dz, rq, rk, rv, rg], axis=1).astype(BF16),
        "w_small": jnp.pad(dba, ((0, 0), (0, LANES - 16))).astype(BF16),
        "w_small_t": dba.T.astype(BF16),
        "ret_gn_g": ret_gn_g[l].reshape(1, -1), "ret_gn_b": ret_gn_b[l].reshape(1, -1),
        "dn_conv": dn_conv[l],
        "dn_prow": jnp.zeros((8, LANES), F32).at[0, 8:16].set(-jnp.exp(dn_A_log[l])).at[1, 8:16].set(dn_dt_bias[l]),
        "dn_pcol": jnp.zeros((16, LANES), F32).at[8:16, 0].set(-jnp.exp(dn_A_log[l])).at[8:16, 1].set(dn_dt_bias[l]),
        "dn_norm_g": dn_norm_g[l].reshape(1, -1),
        "w_xq": xq.astype(BF16), "w_gate": gates.astype(BF16),
        "w_branch": w_branch[l].astype(BF16), "w_out": w_out[l].astype(BF16),
        "ln1_g": ln1_g[l].reshape(1, d), "ln1_b": ln1_b[l].reshape(1, d),
        "w_route": jnp.pad(jnp.concatenate([moe_w_fine[l], moe_w_coarse[l]], axis=1),
                           ((0, 0), (0, LANES - MOE_NE - MOE_GROUPS))),
        "b_route": jnp.pad(jnp.concatenate([moe_b_fine[l], moe_b_coarse[l]]),
                           (0, LANES - MOE_NE - MOE_GROUPS)).reshape(1, LANES),
        "moe_w_gate": moe_w_gate[l].reshape(MOE_NE, d, MOE_DFF).astype(BF16),
        "moe_w_up": moe_w_up[l].reshape(MOE_NE, d, MOE_DFF).astype(BF16),
        "moe_w_down": moe_w_down[l].reshape(MOE_NE, MOE_DFF, d).astype(BF16),
        "ln2_g": ln2_g[l].reshape(1, d), "ln2_b": ln2_b[l].reshape(1, d),
    }

    mkv = _memkv(mem_prompt.reshape(bp * N_MEM, d), w_mem_kv[l].astype(BF16))
    mk = mkv[:, :XA_HEADS * XA_DH].reshape(bp, N_MEM, XA_HEADS * XA_DH)
    mv = mkv[:, XA_HEADS * XA_DH:].reshape(bp, N_MEM, XA_HEADS * XA_DH)
    yp, rs_p, ds_p, cb_p = _group(
        x_prompt, jnp.arange(tp, dtype=jnp.int32), mk, mv,
        jnp.zeros((bp, RET_HEADS, RET_DK, RET_DV), F32),
        jnp.zeros((bp, DN_V_HEADS, DN_DK, DN_DV), F32),
        jnp.zeros((bp, DN_CONV_W - 1, DN_CONV_CH), F32),
        wts, tm=min(512, tp), tm_mix=min(512, tp), ret_chunk=min(256, tp), dn_blk=min(256, tp),
        dn_chunk=min(DN_CHUNK, tp))
    ys, rs_s, ds_s, cb_s = _group(
        x_sample, PAST_LEN + jnp.arange(ts, dtype=jnp.int32),
        cache_mem_k[l].reshape(bs, N_MEM, XA_HEADS * XA_DH), cache_mem_v[l].reshape(bs, N_MEM, XA_HEADS * XA_DH),
        state_ret[l], state_dn[l], state_dn_conv[l],
        wts, tm=bs * ts, tm_mix=ts, ret_chunk=ts, dn_blk=ts, dn_chunk=min(DN_CHUNK, ts))
    return (yp, ys, rs_p[None], rs_s[None], ds_p[None], ds_s[None], cb_p[None], cb_s[None],
            mk.reshape(1, bp, N_MEM, XA_HEADS, XA_DH), mv.reshape(1, bp, N_MEM, XA_HEADS, XA_DH))
```

```python
import functools
import math

import jax
import jax.numpy as jnp
from jax import lax
from jax.experimental import pallas as pl
from jax.experimental.pallas import tpu as pltpu

F32 = jnp.float32
BF16 = jnp.bfloat16

D_MODEL = 1024
PAST_LEN = 1024
RET_HEADS = 4
RET_DK = 128
RET_DV = 256
ROPE_BASE = 10000.0
DN_QK_HEADS = 4
DN_V_HEADS = 8
DN_DK = 128
DN_DV = 128
DN_QK = DN_QK_HEADS * DN_DK
DN_CONV_W = 4
DN_CONV_CH = 2048
DN_CHUNK = 64
DN_SUB = 16
XA_HEADS = 4
XA_DH = 256
N_MEM = 256
MOE_GROUPS = 4
MOE_EXPERTS = 8
MOE_NE = MOE_GROUPS * MOE_EXPERTS
MOE_DFF = 256
DEPTH = 1
ALPHA = (2.0 * DEPTH) ** 0.25
LANES = 128
PROJ_W = 6144
VMEM_LIMIT = 56 * 1024 * 1024


def _dot(a, b):
    return jnp.dot(a.astype(BF16), b.astype(BF16), preferred_element_type=F32)


def _dot_nt(a, b):
    return lax.dot_general(a.astype(BF16), b.astype(BF16), (((1,), (1,)), ((), ())),
                           preferred_element_type=F32)


def _dot_tn(a, b):
    return lax.dot_general(a.astype(BF16), b.astype(BF16), (((0,), (0,)), ((), ())),
                           preferred_element_type=F32)


def _dot_exact(a, b):
    return jnp.dot(a, b, preferred_element_type=F32, precision=lax.Precision.HIGHEST)


def _layernorm(x, g, b, eps=1e-5):
    mu = jnp.mean(x, -1, keepdims=True)
    xc = x - mu
    var = jnp.mean(xc * xc, -1, keepdims=True)
    return xc * lax.rsqrt(var + eps) * g + b


def _silu(x):
    return x * (1.0 / (1.0 + jnp.exp(-x)))


def _sigmoid(x):
    return 1.0 / (1.0 + jnp.exp(-x))


def _softplus(x):
    return jnp.maximum(x, 0.0) + jnp.log(1.0 + jnp.exp(-jnp.abs(x)))


def _idiv(x, n):
    return jnp.right_shift(x, int(math.log2(n)))


def _params(*sem):
    return pltpu.CompilerParams(dimension_semantics=sem, vmem_limit_bytes=VMEM_LIMIT)


def _in_proj_kernel(x_ref, g_ref, b_ref, w_ref, ws_ref, wst_ref,
                    h_ref, hb_ref, small_ref, smallt_ref, proj_ref, hs_ref):
    @pl.when(pl.program_id(1) == 0)
    def _():
        h = _layernorm(x_ref[...], g_ref[...], b_ref[...])
        h_ref[...] = h
        hb = h.astype(BF16)
        hb_ref[...] = hb
        hs_ref[...] = hb
        small_ref[...] = jnp.dot(hb, ws_ref[...], preferred_element_type=F32)
        smallt_ref[...] = lax.dot_general(wst_ref[...], hb, (((1,), (1,)), ((), ())),
                                          preferred_element_type=F32)

    proj_ref[...] = jnp.dot(hs_ref[...], w_ref[...], preferred_element_type=F32)


def _in_proj(x, ln_g, ln_b, w_main, w_small, w_small_t, tm, tn=1024):
    n_tok = x.shape[0]
    grid = (n_tok // tm, PROJ_W // tn)
    return pl.pallas_call(
        _in_proj_kernel,
        grid=grid,
        in_specs=[
            pl.BlockSpec((tm, D_MODEL), lambda i, n: (i, 0)),
            pl.BlockSpec((1, D_MODEL), lambda i, n: (0, 0)),
            pl.BlockSpec((1, D_MODEL), lambda i, n: (0, 0)),
            pl.BlockSpec((D_MODEL, tn), lambda i, n: (0, n)),
            pl.BlockSpec((D_MODEL, LANES), lambda i, n: (0, 0)),
            pl.BlockSpec((16, D_MODEL), lambda i, n: (0, 0)),
        ],
        out_specs=[
            pl.BlockSpec((tm, D_MODEL), lambda i, n: (i, 0)),
            pl.BlockSpec((tm, D_MODEL), lambda i, n: (i, 0)),
            pl.BlockSpec((tm, LANES), lambda i, n: (i, 0)),
            pl.BlockSpec((16, tm), lambda i, n: (0, i)),
            pl.BlockSpec((tm, tn), lambda i, n: (i, n)),
        ],
        out_shape=[
            jax.ShapeDtypeStruct((n_tok, D_MODEL), F32),
            jax.ShapeDtypeStruct((n_tok, D_MODEL), BF16),
            jax.ShapeDtypeStruct((n_tok, LANES), F32),
            jax.ShapeDtypeStruct((16, n_tok), F32),
            jax.ShapeDtypeStruct((n_tok, PROJ_W), F32),
        ],
        scratch_shapes=[pltpu.VMEM((tm, D_MODEL), BF16)],
        compiler_params=_params("parallel", "arbitrary"),
        name="in_proj",
    )(x, ln_g, ln_b, w_main, w_small, w_small_t)


def _ret_kernel(q_ref, k_ref, v_ref, rg_ref, cos_ref, sin_ref, s0_ref, gng_ref, gnb_ref,
                o_ref, sout_ref, s_scr, *, chunk):
    @pl.when(pl.program_id(1) == 0)
    def _():
        s_scr[...] = s0_ref[0]

    heads = range(RET_HEADS)
    lg = [math.log(1.0 - 2.0 ** (-5.0 - h)) for h in heads]
    cos = cos_ref[...]
    sin = sin_ref[...]
    ri = lax.broadcasted_iota(jnp.int32, (chunk, chunk), 0)
    ci = lax.broadcasted_iota(jnp.int32, (chunk, chunk), 1)
    causal = ri >= ci
    diff = jnp.where(causal, (ri - ci).astype(F32), 0.0)
    idx = lax.broadcasted_iota(jnp.int32, (chunk, 1), 0).astype(F32)
    q = [q_ref[0, :, h * RET_DK:(h + 1) * RET_DK] for h in heads]
    k = [k_ref[0, :, h * RET_DK:(h + 1) * RET_DK] for h in heads]
    v = [v_ref[0, :, h * RET_DV:(h + 1) * RET_DV] for h in heads]
    qr = [q[h] * cos + pltpu.roll(q[h], RET_DK // 2, 1) * sin for h in heads]
    kr = [(k[h] * cos + pltpu.roll(k[h], RET_DK // 2, 1) * sin) * (RET_DK ** -0.5) for h in heads]
    scores = [_dot_nt(qr[h], kr[h]) * jnp.where(causal, jnp.exp(diff * lg[h]), 0.0) for h in heads]
    s = [s_scr[h] for h in heads]
    cross = [_dot(qr[h], s[h]) * jnp.exp((idx + 1.0) * lg[h]) for h in heads]
    o = [_dot(scores[h], v[h]) + cross[h] for h in heads]
    kv = [_dot_tn(kr[h] * jnp.exp((chunk - 1.0 - idx) * lg[h]), v[h]) for h in heads]
    for h in heads:
        s_new = s[h] * math.exp(chunk * lg[h]) + kv[h]
        s_scr[h] = s_new
        sout_ref[0, h] = s_new
        sl = slice(h * RET_DV, (h + 1) * RET_DV)
        o_ref[0, :, sl] = _layernorm(o[h], gng_ref[:, sl], gnb_ref[:, sl]) * _silu(rg_ref[0, :, sl])


def _retention(proj, cos2, sin2, s0, gn_g, gn_b, chunk):
    nb, t = proj.shape[:2]
    grid = (nb, t // chunk)
    qk_w = RET_HEADS * RET_DK
    v_w = RET_HEADS * RET_DV
    return pl.pallas_call(
        functools.partial(_ret_kernel, chunk=chunk),
        grid=grid,
        in_specs=[
            pl.BlockSpec((1, chunk, qk_w), lambda b, c: (b, c, 3072 // qk_w)),
            pl.BlockSpec((1, chunk, qk_w), lambda b, c: (b, c, 3584 // qk_w)),
            pl.BlockSpec((1, chunk, v_w), lambda b, c: (b, c, 4096 // v_w)),
            pl.BlockSpec((1, chunk, v_w), lambda b, c: (b, c, 5120 // v_w)),
            pl.BlockSpec((chunk, RET_DK), lambda b, c: (c, 0)),
            pl.BlockSpec((chunk, RET_DK), lambda b, c: (c, 0)),
            pl.BlockSpec((1, RET_HEADS, RET_DK, RET_DV), lambda b, c: (b, 0, 0, 0)),
            pl.BlockSpec((1, v_w), lambda b, c: (0, 0)),
            pl.BlockSpec((1, v_w), lambda b, c: (0, 0)),
        ],
        out_specs=[
            pl.BlockSpec((1, chunk, v_w), lambda b, c: (b, c, 0)),
            pl.BlockSpec((1, RET_HEADS, RET_DK, RET_DV), lambda b, c: (b, 0, 0, 0)),
        ],
        out_shape=[
            jax.ShapeDtypeStruct((nb, t, v_w), F32),
            jax.ShapeDtypeStruct((nb, RET_HEADS, RET_DK, RET_DV), F32),
        ],
        scratch_shapes=[pltpu.VMEM((RET_HEADS, RET_DK, RET_DV), F32)],
        compiler_params=_params("parallel", "arbitrary"),
        name="retention",
    )(proj, proj, proj, proj, cos2, sin2, s0, gn_g, gn_b)


def _dn_kernel(x_ref, z_ref, sm_ref, smt_ref, cw_ref, cinit_ref, prow_ref, pcol_ref, ng_ref, s0_ref,
               o_ref, sout_ref, cout_ref, xext, s_scr, *, blk_len, chunk):
    L = blk_len
    blk = pl.program_id(1)

    @pl.when(blk == 0)
    def _():
        xext[0:8, :] = cinit_ref[0]
        s_scr[...] = s0_ref[0]

    x = x_ref[0]
    xext[8:8 + L, :] = x
    conv = x * cw_ref[3:4, :]
    for w in range(DN_CONV_W - 1):
        conv = conv + xext[5 + w:5 + w + L, :] * cw_ref[w:w + 1, :]
    conv = _silu(conv)
    xext[0:8, :] = x[L - 8:L, :]
    cout_ref[0] = x[L - (DN_CONV_W - 1):L, :]

    sm = sm_ref[...]
    prow = prow_ref[...]
    beta_tm = _sigmoid(sm)
    g_tm = prow[0:1, :] * _softplus(sm + prow[1:2, :])
    smt = smt_ref[0]
    pcol = pcol_ref[...]
    g_hm = pcol[8:16, 0:1] * _softplus(smt[8:16, :] + pcol[8:16, 1:2])

    ri = lax.broadcasted_iota(jnp.int32, (L, L), 0)
    ci = lax.broadcasted_iota(jnp.int32, (L, L), 1)
    same_chunk = _idiv(ri, chunk) == _idiv(ci, chunk)
    lower = jnp.logical_and(ri >= ci, same_chunk)
    strict = jnp.logical_and(ri > ci, same_chunk)
    same_sub = _idiv(ri, DN_SUB) == _idiv(ci, DN_SUB)
    tri = jnp.where(lower, 1.0, 0.0).astype(F32)
    tri_t = jnp.where(jnp.logical_and(ri <= ci, same_chunk), 1.0, 0.0).astype(F32)
    gcum_tm = _dot_exact(tri, g_tm)
    gcum_hm = _dot_exact(g_hm, tri_t)
    egc_tm = jnp.exp(gcum_tm)

    n_sub = L // chunk
    rep = DN_V_HEADS // DN_QK_HEADS
    heads = range(DN_V_HEADS)
    q, k, kk, qk = [], [], [], []
    for j in range(DN_QK_HEADS):
        qj = conv[:, j * DN_DK:(j + 1) * DN_DK]
        kj = conv[:, DN_QK + j * DN_DK:DN_QK + (j + 1) * DN_DK]
        qj = qj * lax.rsqrt(jnp.sum(qj * qj, -1, keepdims=True) + 1e-6) * (DN_DK ** -0.5)
        kj = kj * lax.rsqrt(jnp.sum(kj * kj, -1, keepdims=True) + 1e-6)
        kq = _dot_nt(jnp.concatenate([kj, qj], axis=0), kj)
        q.append(qj)
        k.append(kj)
        kk.append(kq[0:L])
        qk.append(kq[L:2 * L])
    gc = [gcum_tm[:, 8 + hh:9 + hh] for hh in heads]
    beta = [beta_tm[:, hh:hh + 1] for hh in heads]
    eg = [egc_tm[:, 8 + hh:9 + hh] for hh in heads]
    rel = [jnp.where(lower, jnp.exp(jnp.where(lower, gc[hh] - gcum_hm[hh:hh + 1, :], 0.0)), 0.0) for hh in heads]
    a = [jnp.where(strict, beta[hh] * kk[hh // rep] * rel[hh], 0.0) for hh in heads]
    attn = [qk[hh // rep] * rel[hh] for hh in heads]
    pd = [jnp.where(same_sub, -a[hh], 0.0) for hh in heads]
    e = [a[hh] + pd[hh] for hh in heads]
    xacc = pd
    pw = pd
    for _ in range(3):
        pw = [_dot(pw[hh], pw[hh]) for hh in heads]
        xacc = [xacc[hh] + pw[hh] + _dot(xacc[hh], pw[hh]) for hh in heads]
    f = [e[hh] + _dot(xacc[hh], e[hh]) for hh in heads]
    f2 = [_dot(f[hh], f[hh]) for hh in heads]
    y = [f2[hh] - f[hh] - _dot(f[hh], f2[hh]) for hh in heads]
    rinv = [xacc[hh] + y[hh] + _dot(y[hh], xacc[hh]) for hh in heads]
    sol = []
    for hh in heads:
        vh = conv[:, 2 * DN_QK + hh * DN_DV:2 * DN_QK + (hh + 1) * DN_DV]
        rhs = jnp.concatenate([vh * beta[hh], k[hh // rep] * (beta[hh] * eg[hh])], axis=1)
        sol.append(rhs + _dot(rinv[hh], rhs))
    u = [sol[hh][:, 0:DN_DV] for hh in heads]
    wmat = [sol[hh][:, DN_DV:DN_DV + DN_DK] for hh in heads]
    qe = [q[hh // rep] * eg[hh] for hh in heads]
    s = [s_scr[hh] for hh in heads]
    v_new = [[] for _ in heads]
    q_s = [[] for _ in heads]
    for sc in range(n_sub):
        r0 = sc * chunk
        for hh in heads:
            ws = _dot(jnp.concatenate([wmat[hh][r0:r0 + chunk], qe[hh][r0:r0 + chunk]], axis=0), s[hh])
            vn = u[hh][r0:r0 + chunk] - ws[0:chunk]
            v_new[hh].append(vn)
            q_s[hh].append(ws[chunk:2 * chunk])
            g_last = gcum_tm[r0 + chunk - 1:r0 + chunk, 8 + hh:9 + hh]
            kd = k[hh // rep][r0:r0 + chunk] * jnp.exp(g_last - gc[hh][r0:r0 + chunk])
            s[hh] = s[hh] * jnp.exp(g_last) + _dot_tn(kd, vn)
    for hh in heads:
        s_scr[hh] = s[hh]
        sout_ref[0, hh] = s[hh]
        vn = v_new[hh][0] if n_sub == 1 else jnp.concatenate(v_new[hh], axis=0)
        qs = q_s[hh][0] if n_sub == 1 else jnp.concatenate(q_s[hh], axis=0)
        o = qs + _dot(attn[hh], vn)
        zh = z_ref[0, :, hh * DN_DV:(hh + 1) * DN_DV]
        n = o * lax.rsqrt(jnp.mean(o * o, -1, keepdims=True) + 1e-6) * ng_ref[...]
        o_ref[0, :, hh * DN_DV:(hh + 1) * DN_DV] = n * _silu(zh)


def _deltanet(proj, small, small_t, conv_w, conv_init, prow, pcol, norm_g, s0, blk_len, chunk):
    nb, t = proj.shape[:2]
    nblk = t // blk_len
    grid = (nb, nblk)
    return pl.pallas_call(
        functools.partial(_dn_kernel, blk_len=blk_len, chunk=chunk),
        grid=grid,
        in_specs=[
            pl.BlockSpec((1, blk_len, DN_CONV_CH), lambda b, i: (b, i, 0)),
            pl.BlockSpec((1, blk_len, 1024), lambda b, i: (b, i, 2)),
            pl.BlockSpec((blk_len, LANES), lambda b, i: (b * nblk + i, 0)),
            pl.BlockSpec((1, 16, blk_len), lambda b, i: (b, 0, i)),
            pl.BlockSpec((DN_CONV_W, DN_CONV_CH), lambda b, i: (0, 0)),
            pl.BlockSpec((1, 8, DN_CONV_CH), lambda b, i: (b, 0, 0)),
            pl.BlockSpec((8, LANES), lambda b, i: (0, 0)),
            pl.BlockSpec((16, LANES), lambda b, i: (0, 0)),
            pl.BlockSpec((1, DN_DV), lambda b, i: (0, 0)),
            pl.BlockSpec((1, DN_V_HEADS, DN_DK, DN_DV), lambda b, i: (b, 0, 0, 0)),
        ],
        out_specs=[
            pl.BlockSpec((1, blk_len, 1024), lambda b, i: (b, i, 0)),
            pl.BlockSpec((1, DN_V_HEADS, DN_DK, DN_DV), lambda b, i: (b, 0, 0, 0)),
            pl.BlockSpec((1, DN_CONV_W - 1, DN_CONV_CH), lambda b, i: (b, 0, 0)),
        ],
        out_shape=[
            jax.ShapeDtypeStruct((nb, t, 1024), F32),
            jax.ShapeDtypeStruct((nb, DN_V_HEADS, DN_DK, DN_DV), F32),
            jax.ShapeDtypeStruct((nb, DN_CONV_W - 1, DN_CONV_CH), F32),
        ],
        scratch_shapes=[pltpu.VMEM((blk_len + 8, DN_CONV_CH), F32),
                        pltpu.VMEM((DN_V_HEADS, DN_DK, DN_DV), F32)],
        compiler_params=_params("parallel", "arbitrary"),
        name="deltanet",
    )(proj, proj, small, small_t, conv_w, conv_init, prow, pcol, norm_g, s0)


def _memkv_kernel(m_ref, w_ref, o_ref):
    o_ref[...] = _dot(m_ref[...], w_ref[...])


def _memkv(mem, w_kv):
    n = mem.shape[0]
    tn = 1024
    return pl.pallas_call(
        _memkv_kernel,
        grid=(w_kv.shape[1] // tn,),
        in_specs=[pl.BlockSpec((n, D_MODEL), lambda j: (0, 0)),
                  pl.BlockSpec((D_MODEL, tn), lambda j: (0, j))],
        out_specs=pl.BlockSpec((n, tn), lambda j: (0, j)),
        out_shape=jax.ShapeDtypeStruct((n, w_kv.shape[1]), F32),
        compiler_params=_params("parallel"),
        name="memkv",
    )(mem, w_kv)


def _mix_kernel(h_ref, hb_ref, ret_ref, dn_ref, mk_ref, mv_ref, wxq_ref, wg_ref, wb_ref, wo_ref,
                g1_ref, b1_ref, h1_ref, h1b_ref):
    hb = hb_ref[...]
    xq = jnp.dot(hb, wxq_ref[...], preferred_element_type=F32)
    xo = []
    for hh in range(XA_HEADS):
        sl = slice(hh * XA_DH, (hh + 1) * XA_DH)
        s = _dot_nt(xq[:, sl], mk_ref[0, :, sl]) * (XA_DH ** -0.5)
        s = s - jnp.max(s, -1, keepdims=True)
        p = jnp.exp(s)
        p = p / jnp.sum(p, -1, keepdims=True)
        xo.append(_dot(p, mv_ref[0, :, sl]))
    xo = jnp.concatenate(xo, axis=1)
    mixed = None
    for n, br in enumerate((ret_ref[...], dn_ref[...], xo)):
        gate = _sigmoid(jnp.dot(hb, wg_ref[:, n * D_MODEL:(n + 1) * D_MODEL], preferred_element_type=F32))
        term = gate * _dot(br, wb_ref[n])
        mixed = term if mixed is None else mixed + term
    y = ALPHA * h_ref[...] + _dot(mixed, wo_ref[...])
    h1 = _layernorm(y, g1_ref[...], b1_ref[...])
    h1_ref[...] = h1
    h1b_ref[...] = h1.astype(BF16)


def _mix(h, hb, ret_o, dn_o, mem_k, mem_v, w_xq, w_gate, w_branch, w_out, ln_g, ln_b, tm, tiles_per_batch):
    n_tok = h.shape[0]
    const2 = lambda i: (0, 0)
    row = lambda i: (i, 0)
    single = pl.Buffered(1)
    return pl.pallas_call(
        _mix_kernel,
        grid=(n_tok // tm,),
        in_specs=[
            pl.BlockSpec((tm, D_MODEL), row),
            pl.BlockSpec((tm, D_MODEL), row),
            pl.BlockSpec((tm, D_MODEL), row),
            pl.BlockSpec((tm, D_MODEL), row),
            pl.BlockSpec((1, N_MEM, D_MODEL), lambda i: (i // tiles_per_batch, 0, 0)),
            pl.BlockSpec((1, N_MEM, D_MODEL), lambda i: (i // tiles_per_batch, 0, 0)),
            pl.BlockSpec((D_MODEL, D_MODEL), const2, pipeline_mode=single),
            pl.BlockSpec((D_MODEL, 3 * D_MODEL), const2, pipeline_mode=single),
            pl.BlockSpec((3, D_MODEL, D_MODEL), lambda i: (0, 0, 0), pipeline_mode=single),
            pl.BlockSpec((D_MODEL, D_MODEL), const2, pipeline_mode=single),
            pl.BlockSpec((1, D_MODEL), const2),
            pl.BlockSpec((1, D_MODEL), const2),
        ],
        out_specs=[pl.BlockSpec((tm, D_MODEL), row), pl.BlockSpec((tm, D_MODEL), row)],
        out_shape=[jax.ShapeDtypeStruct((n_tok, D_MODEL), F32),
                   jax.ShapeDtypeStruct((n_tok, D_MODEL), BF16)],
        compiler_params=_params("parallel"),
        name="mix",
    )(h, hb, ret_o, dn_o, mem_k, mem_v, w_xq, w_gate, w_branch, w_out, ln_g, ln_b)


def _route(x, wr, br):
    logits = _dot_exact(x, wr) + br
    lane_i = lax.broadcasted_iota(jnp.int32, logits.shape, 1)
    lane = lane_i.astype(F32)
    far = jnp.float32(LANES)
    neg = jnp.float32(-3.0e38)
    is_c = jnp.logical_and(lane_i >= MOE_NE, lane_i < MOE_NE + MOE_GROUPS)
    cl = jnp.where(is_c, logits, neg)
    cmax = jnp.max(cl, -1, keepdims=True)
    denom = jnp.sum(jnp.where(is_c, jnp.exp(jnp.where(is_c, logits - cmax, 0.0)), 0.0), -1, keepdims=True)
    p_grp = 1.0 / denom
    grp = jnp.min(jnp.where(jnp.logical_and(is_c, cl == cmax), lane - MOE_NE, far), -1, keepdims=True)
    in_grp = jnp.logical_and(lane_i < MOE_NE, _idiv(lane_i, MOE_EXPERTS).astype(F32) == grp)
    fl = jnp.where(in_grp, logits, neg)
    v1 = jnp.max(fl, -1, keepdims=True)
    i1 = jnp.min(jnp.where(jnp.logical_and(in_grp, fl == v1), lane, far), -1, keepdims=True)
    rest = jnp.logical_and(in_grp, lane != i1)
    fl2 = jnp.where(rest, logits, neg)
    v2 = jnp.max(fl2, -1, keepdims=True)
    i2 = jnp.min(jnp.where(jnp.logical_and(rest, fl2 == v2), lane, far), -1, keepdims=True)
    e2 = jnp.exp(v2 - v1)
    w1 = p_grp / (1.0 + e2)
    w2 = p_grp * e2 / (1.0 + e2)
    return jnp.where(lane == i1, w1, 0.0) + jnp.where(lane == i2, w2, 0.0), grp


def _route_kernel(h1_ref, wr_ref, br_ref, comb_ref, cnt_ref):
    comb, grp = _route(h1_ref[...], wr_ref[...], br_ref[...])
    lane_i = lax.broadcasted_iota(jnp.int32, comb.shape, 1)
    comb_ref[...] = jnp.where(lane_i == LANES - 1, grp, comb)
    onehot = jnp.where(lane_i.astype(F32) == grp, 1.0, 0.0)
    cnt_ref[0] = jnp.broadcast_to(jnp.sum(onehot, 0, keepdims=True), (8, LANES))


def _route_call(h1, w_route, b_route, tm):
    n_tok = h1.shape[0]
    return pl.pallas_call(
        _route_kernel,
        grid=(n_tok // tm,),
        in_specs=[pl.BlockSpec((tm, D_MODEL), lambda i: (i, 0)),
                  pl.BlockSpec((D_MODEL, LANES), lambda i: (0, 0)),
                  pl.BlockSpec((1, LANES), lambda i: (0, 0))],
        out_specs=[pl.BlockSpec((tm, LANES), lambda i: (i, 0)),
                   pl.BlockSpec((1, 8, LANES), lambda i: (i, 0, 0))],
        out_shape=[jax.ShapeDtypeStruct((n_tok, LANES), F32),
                   jax.ShapeDtypeStruct((n_tok // tm, 8, LANES), F32)],
        compiler_params=_params("parallel"),
        name="route",
    )(h1, w_route, b_route)


MOE_ALIGN = 16


def _moe_sizes(tm):
    s0 = min(tm, tm // MOE_GROUPS + 64)
    return tuple(sorted({s0, max(s0, tm // 2), tm}))


def _moe_kernel(offs_ref, cls_ref, h1_ref, h1b_ref, comb_ref, wg_ref, wu_ref, wd_ref, g2_ref, b2_ref,
                y_ref, pt_scr, xs_scr, cs_scr, ys_scr, *, tm, nr1, sizes):
    i = pl.program_id(0)
    e = pl.program_id(1)
    nrt = xs_scr.shape[0]

    @pl.when(e == 0)
    def _():
        comb = comb_ref[...]
        lane_i = lax.broadcasted_iota(jnp.int32, (tm, LANES), 1)
        grp = comb[:, LANES - 1:LANES]
        is_g = lane_i.astype(F32) == grp
        ri = lax.broadcasted_iota(jnp.int32, (tm, tm), 0)
        ci = lax.broadcasted_iota(jnp.int32, (tm, tm), 1)
        tri = jnp.where(ri > ci, 1.0, 0.0).astype(BF16)
        prefix = jnp.dot(tri, jnp.where(is_g, 1.0, 0.0).astype(BF16), preferred_element_type=F32)
        rank = jnp.sum(jnp.where(is_g, prefix, 0.0), -1, keepdims=True)
        offv = jnp.zeros((tm, 1), F32)
        for g in range(MOE_GROUPS):
            offv = jnp.where(grp == g, offs_ref[i * MOE_GROUPS + g].astype(F32), offv)
        pos = offv + rank
        col = lax.broadcasted_iota(jnp.int32, (tm, nr1), 1).astype(F32)
        pt = jnp.where(col == pos, 1.0, 0.0).astype(BF16)
        pt_scr[...] = pt
        combw = jnp.where(lane_i < MOE_NE, comb, 0.0)
        c_hi = combw.astype(BF16)
        c_lo = (combw - c_hi.astype(F32)).astype(BF16)
        xs_scr[0:nr1, :] = _dot_tn(pt, h1b_ref[...]).astype(BF16)
        cs_scr[0:nr1, :] = _dot_tn(pt, c_hi) + _dot_tn(pt, c_lo)
        xs_scr[nr1:nrt, :] = jnp.zeros((nrt - nr1, D_MODEL), BF16)
        cs_scr[nr1:nrt, :] = jnp.zeros((nrt - nr1, LANES), F32)
        ys_scr[...] = jnp.zeros_like(ys_scr)

    g = jnp.right_shift(e, int(math.log2(MOE_EXPERTS)))
    off = pl.multiple_of(offs_ref[i * MOE_GROUPS + g], MOE_ALIGN)
    cls = cls_ref[i * MOE_GROUPS + g]
    for ci, m in enumerate(sizes):
        @pl.when(cls == ci)
        def _(m=m):
            rows = pl.ds(off, m)
            xb = xs_scr[rows, :]
            lane_i = lax.broadcasted_iota(jnp.int32, (m, LANES), 1)
            colw = jnp.sum(jnp.where(lane_i == e, cs_scr[rows, :], 0.0), -1, keepdims=True)
            hg = _silu(jnp.dot(xb, wg_ref[0], preferred_element_type=F32)) * jnp.dot(
                xb, wu_ref[0], preferred_element_type=F32)
            ys_scr[rows, :] += _dot(hg * colw, wd_ref[0])

    @pl.when(e == MOE_NE - 1)
    def _():
        moe = jnp.dot(pt_scr[...], ys_scr[0:nr1, :].astype(BF16), preferred_element_type=F32)
        y_ref[...] = _layernorm(ALPHA * h1_ref[...] + moe, g2_ref[...], b2_ref[...])


def _moe(h1, h1b, comb, cnt, w_gate, w_up, w_down, ln_g, ln_b, tm):
    n_tok = h1.shape[0]
    sizes = _moe_sizes(tm)
    nr1 = -(-(tm + MOE_GROUPS * MOE_ALIGN) // LANES) * LANES
    nrt = nr1 + tm
    c = cnt[:, 0, :MOE_GROUPS].astype(jnp.int32)
    seg = (c + MOE_ALIGN - 1) // MOE_ALIGN * MOE_ALIGN
    offs = (jnp.cumsum(seg, axis=1) - seg).reshape(-1)
    cls = sum((c > s).astype(jnp.int32) for s in sizes[:-1]).reshape(-1)
    grid_spec = pltpu.PrefetchScalarGridSpec(
        num_scalar_prefetch=2,
        grid=(n_tok // tm, MOE_NE),
        in_specs=[
            pl.BlockSpec((tm, D_MODEL), lambda i, e, o, k: (i, 0)),
            pl.BlockSpec((tm, D_MODEL), lambda i, e, o, k: (i, 0)),
            pl.BlockSpec((tm, LANES), lambda i, e, o, k: (i, 0)),
            pl.BlockSpec((1, D_MODEL, MOE_DFF), lambda i, e, o, k: (e, 0, 0)),
            pl.BlockSpec((1, D_MODEL, MOE_DFF), lambda i, e, o, k: (e, 0, 0)),
            pl.BlockSpec((1, MOE_DFF, D_MODEL), lambda i, e, o, k: (e, 0, 0)),
            pl.BlockSpec((1, D_MODEL), lambda i, e, o, k: (0, 0)),
            pl.BlockSpec((1, D_MODEL), lambda i, e, o, k: (0, 0)),
        ],
        out_specs=pl.BlockSpec((tm, D_MODEL), lambda i, e, o, k: (i, 0)),
        scratch_shapes=[pltpu.VMEM((tm, nr1), BF16), pltpu.VMEM((nrt, D_MODEL), BF16),
                        pltpu.VMEM((nrt, LANES), F32), pltpu.VMEM((nrt, D_MODEL), F32)],
    )
    return pl.pallas_call(
        functools.partial(_moe_kernel, tm=tm, nr1=nr1, sizes=sizes),
        grid_spec=grid_spec,
        out_shape=jax.ShapeDtypeStruct((n_tok, D_MODEL), F32),
        compiler_params=_params("parallel", "arbitrary"),
        name="moe",
    )(offs, cls, h1, h1b, comb, w_gate, w_up, w_down, ln_g, ln_b)


def _rope_tables(pos):
    half = RET_DK // 2
    inv = 1.0 / (ROPE_BASE ** (jnp.arange(half, dtype=F32) / half))
    ang = pos.astype(F32)[:, None] * inv[None, :]
    cos = jnp.cos(ang)
    sin = jnp.sin(ang)
    return jnp.concatenate([cos, cos], -1), jnp.concatenate([-sin, sin], -1)


def _group(x, pos, mem_k, mem_v, ret_s0, dn_s0, conv_buf, wts, *, tm, tm_mix, tm_moe, ret_chunk, dn_blk,
           dn_chunk):
    nb, t, d = x.shape
    n_tok = nb * t
    h, hb, small, small_t, proj = _in_proj(x.reshape(n_tok, d), wts["ln_in_g"], wts["ln_in_b"],
                                           wts["w_main"], wts["w_small"], wts["w_small_t"], tm)
    proj3 = proj.reshape(nb, t, PROJ_W)
    cos2, sin2 = _rope_tables(pos)
    ret_o, ret_s = _retention(proj3, cos2, sin2, ret_s0, wts["ret_gn_g"], wts["ret_gn_b"], ret_chunk)
    small_t3 = small_t.reshape(16, nb, t).transpose(1, 0, 2)
    conv_init = jnp.concatenate([jnp.zeros((nb, 8 - (DN_CONV_W - 1), DN_CONV_CH), F32), conv_buf], axis=1)
    dn_o, dn_s, conv_new = _deltanet(proj3, small, small_t3, wts["dn_conv"], conv_init, wts["dn_prow"],
                                     wts["dn_pcol"], wts["dn_norm_g"], dn_s0, dn_blk, dn_chunk)
    h1, h1b = _mix(h, hb, ret_o.reshape(n_tok, d), dn_o.reshape(n_tok, d), mem_k, mem_v,
                   wts["w_xq"], wts["w_gate"], wts["w_branch"], wts["w_out"], wts["ln1_g"], wts["ln1_b"],
                   tm_mix, t // tm_mix)
    comb, cnt = _route_call(h1, wts["w_route"], wts["b_route"], tm_moe)
    y = _moe(h1, h1b, comb, cnt, wts["moe_w_gate"], wts["moe_w_up"], wts["moe_w_down"],
             wts["ln2_g"], wts["ln2_b"], tm_moe)
    return y.reshape(nb, t, d), ret_s, dn_s, conv_new


def kernel(x_prompt, x_sample, mem_prompt, state_ret, state_dn, state_dn_conv, cache_mem_k, cache_mem_v,
           ln_in_g, ln_in_b, w_in, ret_gn_g, ret_gn_b, dn_conv, dn_A_log, dn_dt_bias, dn_norm_g,
           w_mem_kv, w_branch, w_out, ln1_g, ln1_b, moe_w_coarse, moe_b_coarse, moe_w_fine, moe_b_fine,
           moe_w_gate, moe_w_up, moe_w_down, ln2_g, ln2_b):
    bp, tp, d = x_prompt.shape
    bs, ts, _ = x_sample.shape
    l = 0
    wi = w_in[l]
    rq, rk, rv, rg = wi[:, 0:512], wi[:, 512:1024], wi[:, 1024:2048], wi[:, 2048:3072]
    dqkv, dz = wi[:, 3072:5120], wi[:, 5120:6144]
    dba = wi[:, 6144:6160]
    xq, gates = wi[:, 6160:7184], wi[:, 7184:10256]
    wts = {
        "ln_in_g": ln_in_g.reshape(1, d), "ln_in_b": ln_in_b.reshape(1, d),
        "w_main": jnp.concatenate([dqkv, dz, rq, rk, rv, rg], axis=1).astype(BF16),
        "w_small": jnp.pad(dba, ((0, 0), (0, LANES - 16))).astype(BF16),
        "w_small_t": dba.T.astype(BF16),
        "ret_gn_g": ret_gn_g[l].reshape(1, -1), "ret_gn_b": ret_gn_b[l].reshape(1, -1),
        "dn_conv": dn_conv[l],
        "dn_prow": jnp.zeros((8, LANES), F32).at[0, 8:16].set(-jnp.exp(dn_A_log[l])).at[1, 8:16].set(dn_dt_bias[l]),
        "dn_pcol": jnp.zeros((16, LANES), F32).at[8:16, 0].set(-jnp.exp(dn_A_log[l])).at[8:16, 1].set(dn_dt_bias[l]),
        "dn_norm_g": dn_norm_g[l].reshape(1, -1),
        "w_xq": xq.astype(BF16), "w_gate": gates.astype(BF16),
        "w_branch": w_branch[l].astype(BF16), "w_out": w_out[l].astype(BF16),
        "ln1_g": ln1_g[l].reshape(1, d), "ln1_b": ln1_b[l].reshape(1, d),
        "w_route": jnp.pad(jnp.concatenate([moe_w_fine[l], moe_w_coarse[l]], axis=1),
                           ((0, 0), (0, LANES - MOE_NE - MOE_GROUPS))),
        "b_route": jnp.pad(jnp.concatenate([moe_b_fine[l], moe_b_coarse[l]]),
                           (0, LANES - MOE_NE - MOE_GROUPS)).reshape(1, LANES),
        "moe_w_gate": moe_w_gate[l].reshape(MOE_NE, d, MOE_DFF).astype(BF16),
        "moe_w_up": moe_w_up[l].reshape(MOE_NE, d, MOE_DFF).astype(BF16),
        "moe_w_down": moe_w_down[l].reshape(MOE_NE, MOE_DFF, d).astype(BF16),
        "ln2_g": ln2_g[l].reshape(1, d), "ln2_b": ln2_b[l].reshape(1, d),
    }

    mkv = _memkv(mem_prompt.reshape(bp * N_MEM, d), w_mem_kv[l].astype(BF16))
    mk = mkv[:, :XA_HEADS * XA_DH].reshape(bp, N_MEM, XA_HEADS * XA_DH)
    mv = mkv[:, XA_HEADS * XA_DH:].reshape(bp, N_MEM, XA_HEADS * XA_DH)
    yp, rs_p, ds_p, cb_p = _group(
        x_prompt, jnp.arange(tp, dtype=jnp.int32), mk, mv,
        jnp.zeros((bp, RET_HEADS, RET_DK, RET_DV), F32),
        jnp.zeros((bp, DN_V_HEADS, DN_DK, DN_DV), F32),
        jnp.zeros((bp, DN_CONV_W - 1, DN_CONV_CH), F32),
        wts, tm=min(512, tp), tm_mix=min(512, tp), tm_moe=min(1024, tp), ret_chunk=min(256, tp),
        dn_blk=min(256, tp),
        dn_chunk=min(DN_CHUNK, tp))
    ys, rs_s, ds_s, cb_s = _group(
        x_sample, PAST_LEN + jnp.arange(ts, dtype=jnp.int32),
        cache_mem_k[l].reshape(bs, N_MEM, XA_HEADS * XA_DH), cache_mem_v[l].reshape(bs, N_MEM, XA_HEADS * XA_DH),
        state_ret[l], state_dn[l], state_dn_conv[l],
        wts, tm=bs * ts, tm_mix=ts, tm_moe=bs * ts, ret_chunk=ts, dn_blk=ts, dn_chunk=min(DN_CHUNK, ts))
    return (yp, ys, rs_p[None], rs_s[None], ds_p[None], ds_s[None], cb_p[None], cb_s[None],
            mk.reshape(1, bp, N_MEM, XA_HEADS, XA_DH), mv.reshape(1, bp, N_MEM, XA_HEADS, XA_DH))
```

```python
import functools
import math

import jax
import jax.numpy as jnp
from jax import lax
from jax.experimental import pallas as pl
from jax.experimental.pallas import tpu as pltpu

F32 = jnp.float32
BF16 = jnp.bfloat16

D_MODEL = 1024
PAST_LEN = 1024
RET_HEADS = 4
RET_DK = 128
RET_DV = 256
ROPE_BASE = 10000.0
DN_QK_HEADS = 4
DN_V_HEADS = 8
DN_DK = 128
DN_DV = 128
DN_QK = DN_QK_HEADS * DN_DK
DN_CONV_W = 4
DN_CONV_CH = 2048
DN_CHUNK = 64
DN_SUB = 16
XA_HEADS = 4
XA_DH = 256
N_MEM = 256
MOE_GROUPS = 4
MOE_EXPERTS = 8
MOE_NE = MOE_GROUPS * MOE_EXPERTS
MOE_DFF = 256
DEPTH = 1
ALPHA = (2.0 * DEPTH) ** 0.25
LANES = 128
PROJ_W = 6144
VMEM_LIMIT = 56 * 1024 * 1024


def _dot(a, b):
    return jnp.dot(a.astype(BF16), b.astype(BF16), preferred_element_type=F32)


def _dot_nt(a, b):
    return lax.dot_general(a.astype(BF16), b.astype(BF16), (((1,), (1,)), ((), ())),
                           preferred_element_type=F32)


def _dot_tn(a, b):
    return lax.dot_general(a.astype(BF16), b.astype(BF16), (((0,), (0,)), ((), ())),
                           preferred_element_type=F32)


def _dot_exact(a, b):
    return jnp.dot(a, b, preferred_element_type=F32, precision=lax.Precision.HIGHEST)


def _layernorm(x, g, b, eps=1e-5):
    mu = jnp.mean(x, -1, keepdims=True)
    xc = x - mu
    var = jnp.mean(xc * xc, -1, keepdims=True)
    return xc * lax.rsqrt(var + eps) * g + b


def _silu(x):
    return x * (1.0 / (1.0 + jnp.exp(-x)))


def _sigmoid(x):
    return 1.0 / (1.0 + jnp.exp(-x))


def _softplus(x):
    return jnp.maximum(x, 0.0) + jnp.log(1.0 + jnp.exp(-jnp.abs(x)))


def _idiv(x, n):
    return jnp.right_shift(x, int(math.log2(n)))


def _params(*sem):
    return pltpu.CompilerParams(dimension_semantics=sem, vmem_limit_bytes=VMEM_LIMIT)


def _in_proj_kernel(x_ref, g_ref, b_ref, w_ref, ws_ref, wst_ref,
                    h_ref, hb_ref, small_ref, smallt_ref, proj_ref, hs_ref):
    @pl.when(pl.program_id(1) == 0)
    def _():
        h = _layernorm(x_ref[...], g_ref[...], b_ref[...])
        h_ref[...] = h
        hb = h.astype(BF16)
        hb_ref[...] = hb
        hs_ref[...] = hb
        small_ref[...] = jnp.dot(hb, ws_ref[...], preferred_element_type=F32)
        smallt_ref[...] = lax.dot_general(wst_ref[...], hb, (((1,), (1,)), ((), ())),
                                          preferred_element_type=F32)

    proj_ref[...] = jnp.dot(hs_ref[...], w_ref[...], preferred_element_type=F32).astype(BF16)


def _in_proj(x, ln_g, ln_b, w_main, w_small, w_small_t, tm, tn=1024):
    n_tok = x.shape[0]
    grid = (n_tok // tm, PROJ_W // tn)
    return pl.pallas_call(
        _in_proj_kernel,
        grid=grid,
        in_specs=[
            pl.BlockSpec((tm, D_MODEL), lambda i, n: (i, 0)),
            pl.BlockSpec((1, D_MODEL), lambda i, n: (0, 0)),
            pl.BlockSpec((1, D_MODEL), lambda i, n: (0, 0)),
            pl.BlockSpec((D_MODEL, tn), lambda i, n: (0, n)),
            pl.BlockSpec((D_MODEL, LANES), lambda i, n: (0, 0)),
            pl.BlockSpec((16, D_MODEL), lambda i, n: (0, 0)),
        ],
        out_specs=[
            pl.BlockSpec((tm, D_MODEL), lambda i, n: (i, 0)),
            pl.BlockSpec((tm, D_MODEL), lambda i, n: (i, 0)),
            pl.BlockSpec((tm, LANES), lambda i, n: (i, 0)),
            pl.BlockSpec((16, tm), lambda i, n: (0, i)),
            pl.BlockSpec((tm, tn), lambda i, n: (i, n)),
        ],
        out_shape=[
            jax.ShapeDtypeStruct((n_tok, D_MODEL), F32),
            jax.ShapeDtypeStruct((n_tok, D_MODEL), BF16),
            jax.ShapeDtypeStruct((n_tok, LANES), F32),
            jax.ShapeDtypeStruct((16, n_tok), F32),
            jax.ShapeDtypeStruct((n_tok, PROJ_W), BF16),
        ],
        scratch_shapes=[pltpu.VMEM((tm, D_MODEL), BF16)],
        compiler_params=_params("parallel", "arbitrary"),
        name="in_proj",
    )(x, ln_g, ln_b, w_main, w_small, w_small_t)


def _ret_kernel(q_ref, k_ref, v_ref, rg_ref, cos_ref, sin_ref, s0_ref, gng_ref, gnb_ref,
                o_ref, sout_ref, s_scr, *, chunk):
    @pl.when(pl.program_id(1) == 0)
    def _():
        s_scr[...] = s0_ref[0]

    heads = range(RET_HEADS)
    lg = [math.log(1.0 - 2.0 ** (-5.0 - h)) for h in heads]
    cos = cos_ref[...]
    sin = sin_ref[...]
    ri = lax.broadcasted_iota(jnp.int32, (chunk, chunk), 0)
    ci = lax.broadcasted_iota(jnp.int32, (chunk, chunk), 1)
    causal = ri >= ci
    diff = jnp.where(causal, (ri - ci).astype(F32), 0.0)
    idx = lax.broadcasted_iota(jnp.int32, (chunk, 1), 0).astype(F32)
    q = [q_ref[0, :, h * RET_DK:(h + 1) * RET_DK].astype(F32) for h in heads]
    k = [k_ref[0, :, h * RET_DK:(h + 1) * RET_DK].astype(F32) for h in heads]
    v = [v_ref[0, :, h * RET_DV:(h + 1) * RET_DV] for h in heads]
    qr = [q[h] * cos + pltpu.roll(q[h], RET_DK // 2, 1) * sin for h in heads]
    kr = [(k[h] * cos + pltpu.roll(k[h], RET_DK // 2, 1) * sin) * (RET_DK ** -0.5) for h in heads]
    scores = [_dot_nt(qr[h], kr[h]) * jnp.where(causal, jnp.exp(diff * lg[h]), 0.0) for h in heads]
    s = [s_scr[h] for h in heads]
    cross = [_dot(qr[h], s[h]) * jnp.exp((idx + 1.0) * lg[h]) for h in heads]
    o = [_dot(scores[h], v[h]) + cross[h] for h in heads]
    kv = [_dot_tn(kr[h] * jnp.exp((chunk - 1.0 - idx) * lg[h]), v[h]) for h in heads]
    for h in heads:
        s_new = s[h] * math.exp(chunk * lg[h]) + kv[h]
        s_scr[h] = s_new
        sout_ref[0, h] = s_new
        sl = slice(h * RET_DV, (h + 1) * RET_DV)
        gated = _layernorm(o[h], gng_ref[:, sl], gnb_ref[:, sl]) * _silu(rg_ref[0, :, sl].astype(F32))
        o_ref[0, :, sl] = gated.astype(BF16)


def _retention(proj, cos2, sin2, s0, gn_g, gn_b, chunk):
    nb, t = proj.shape[:2]
    grid = (nb, t // chunk)
    qk_w = RET_HEADS * RET_DK
    v_w = RET_HEADS * RET_DV
    return pl.pallas_call(
        functools.partial(_ret_kernel, chunk=chunk),
        grid=grid,
        in_specs=[
            pl.BlockSpec((1, chunk, qk_w), lambda b, c: (b, c, 3072 // qk_w)),
            pl.BlockSpec((1, chunk, qk_w), lambda b, c: (b, c, 3584 // qk_w)),
            pl.BlockSpec((1, chunk, v_w), lambda b, c: (b, c, 4096 // v_w)),
            pl.BlockSpec((1, chunk, v_w), lambda b, c: (b, c, 5120 // v_w)),
            pl.BlockSpec((chunk, RET_DK), lambda b, c: (c, 0)),
            pl.BlockSpec((chunk, RET_DK), lambda b, c: (c, 0)),
            pl.BlockSpec((1, RET_HEADS, RET_DK, RET_DV), lambda b, c: (b, 0, 0, 0)),
            pl.BlockSpec((1, v_w), lambda b, c: (0, 0)),
            pl.BlockSpec((1, v_w), lambda b, c: (0, 0)),
        ],
        out_specs=[
            pl.BlockSpec((1, chunk, v_w), lambda b, c: (b, c, 0)),
            pl.BlockSpec((1, RET_HEADS, RET_DK, RET_DV), lambda b, c: (b, 0, 0, 0)),
        ],
        out_shape=[
            jax.ShapeDtypeStruct((nb, t, v_w), BF16),
            jax.ShapeDtypeStruct((nb, RET_HEADS, RET_DK, RET_DV), F32),
        ],
        scratch_shapes=[pltpu.VMEM((RET_HEADS, RET_DK, RET_DV), F32)],
        compiler_params=_params("parallel", "arbitrary"),
        name="retention",
    )(proj, proj, proj, proj, cos2, sin2, s0, gn_g, gn_b)


def _dn_kernel(x_ref, z_ref, sm_ref, smt_ref, cw_ref, cinit_ref, prow_ref, pcol_ref, ng_ref, s0_ref,
               o_ref, sout_ref, cout_ref, xext, s_scr, *, blk_len, chunk):
    L = blk_len
    blk = pl.program_id(1)

    @pl.when(blk == 0)
    def _():
        xext[0:8, :] = cinit_ref[0]
        s_scr[...] = s0_ref[0]

    x = x_ref[0].astype(F32)
    xext[8:8 + L, :] = x
    conv = x * cw_ref[3:4, :]
    for w in range(DN_CONV_W - 1):
        conv = conv + xext[5 + w:5 + w + L, :] * cw_ref[w:w + 1, :]
    conv = _silu(conv)
    xext[0:8, :] = x[L - 8:L, :]
    cout_ref[0] = x[L - (DN_CONV_W - 1):L, :]

    sm = sm_ref[...]
    prow = prow_ref[...]
    beta_tm = _sigmoid(sm)
    g_tm = prow[0:1, :] * _softplus(sm + prow[1:2, :])
    smt = smt_ref[0]
    pcol = pcol_ref[...]
    g_hm = pcol[8:16, 0:1] * _softplus(smt[8:16, :] + pcol[8:16, 1:2])

    ri = lax.broadcasted_iota(jnp.int32, (L, L), 0)
    ci = lax.broadcasted_iota(jnp.int32, (L, L), 1)
    same_chunk = _idiv(ri, chunk) == _idiv(ci, chunk)
    lower = jnp.logical_and(ri >= ci, same_chunk)
    strict = jnp.logical_and(ri > ci, same_chunk)
    same_sub = _idiv(ri, DN_SUB) == _idiv(ci, DN_SUB)
    tri = jnp.where(lower, 1.0, 0.0).astype(F32)
    tri_t = jnp.where(jnp.logical_and(ri <= ci, same_chunk), 1.0, 0.0).astype(F32)
    gcum_tm = _dot_exact(tri, g_tm)
    gcum_hm = _dot_exact(g_hm, tri_t)
    egc_tm = jnp.exp(gcum_tm)

    n_sub = L // chunk
    rep = DN_V_HEADS // DN_QK_HEADS
    heads = range(DN_V_HEADS)
    q, k, kk, qk = [], [], [], []
    for j in range(DN_QK_HEADS):
        qj = conv[:, j * DN_DK:(j + 1) * DN_DK]
        kj = conv[:, DN_QK + j * DN_DK:DN_QK + (j + 1) * DN_DK]
        qj = qj * lax.rsqrt(jnp.sum(qj * qj, -1, keepdims=True) + 1e-6) * (DN_DK ** -0.5)
        kj = kj * lax.rsqrt(jnp.sum(kj * kj, -1, keepdims=True) + 1e-6)
        kq = _dot_nt(jnp.concatenate([kj, qj], axis=0), kj)
        q.append(qj)
        k.append(kj)
        kk.append(kq[0:L])
        qk.append(kq[L:2 * L])
    gc = [gcum_tm[:, 8 + hh:9 + hh] for hh in heads]
    beta = [beta_tm[:, hh:hh + 1] for hh in heads]
    eg = [egc_tm[:, 8 + hh:9 + hh] for hh in heads]
    rel = [jnp.where(lower, jnp.exp(jnp.where(lower, gc[hh] - gcum_hm[hh:hh + 1, :], 0.0)), 0.0) for hh in heads]
    a = [jnp.where(strict, beta[hh] * kk[hh // rep] * rel[hh], 0.0) for hh in heads]
    attn = [qk[hh // rep] * rel[hh] for hh in heads]
    pd = [jnp.where(same_sub, -a[hh], 0.0) for hh in heads]
    e = [a[hh] + pd[hh] for hh in heads]
    xacc = pd
    pw = pd
    for _ in range(3):
        pw = [_dot(pw[hh], pw[hh]) for hh in heads]
        xacc = [xacc[hh] + pw[hh] + _dot(xacc[hh], pw[hh]) for hh in heads]
    f = [e[hh] + _dot(xacc[hh], e[hh]) for hh in heads]
    f2 = [_dot(f[hh], f[hh]) for hh in heads]
    y = [f2[hh] - f[hh] - _dot(f[hh], f2[hh]) for hh in heads]
    rinv = [xacc[hh] + y[hh] + _dot(y[hh], xacc[hh]) for hh in heads]
    sol = []
    for hh in heads:
        vh = conv[:, 2 * DN_QK + hh * DN_DV:2 * DN_QK + (hh + 1) * DN_DV]
        rhs = jnp.concatenate([vh * beta[hh], k[hh // rep] * (beta[hh] * eg[hh])], axis=1)
        sol.append(rhs + _dot(rinv[hh], rhs))
    u = [sol[hh][:, 0:DN_DV] for hh in heads]
    wmat = [sol[hh][:, DN_DV:DN_DV + DN_DK] for hh in heads]
    qe = [q[hh // rep] * eg[hh] for hh in heads]
    s = [s_scr[hh] for hh in heads]
    v_new = [[] for _ in heads]
    q_s = [[] for _ in heads]
    for sc in range(n_sub):
        r0 = sc * chunk
        for hh in heads:
            ws = _dot(jnp.concatenate([wmat[hh][r0:r0 + chunk], qe[hh][r0:r0 + chunk]], axis=0), s[hh])
            vn = u[hh][r0:r0 + chunk] - ws[0:chunk]
            v_new[hh].append(vn)
            q_s[hh].append(ws[chunk:2 * chunk])
            g_last = gcum_tm[r0 + chunk - 1:r0 + chunk, 8 + hh:9 + hh]
            kd = k[hh // rep][r0:r0 + chunk] * jnp.exp(g_last - gc[hh][r0:r0 + chunk])
            s[hh] = s[hh] * jnp.exp(g_last) + _dot_tn(kd, vn)
    for hh in heads:
        s_scr[hh] = s[hh]
        sout_ref[0, hh] = s[hh]
        vn = v_new[hh][0] if n_sub == 1 else jnp.concatenate(v_new[hh], axis=0)
        qs = q_s[hh][0] if n_sub == 1 else jnp.concatenate(q_s[hh], axis=0)
        o = qs + _dot(attn[hh], vn)
        zh = z_ref[0, :, hh * DN_DV:(hh + 1) * DN_DV].astype(F32)
        n = o * lax.rsqrt(jnp.mean(o * o, -1, keepdims=True) + 1e-6) * ng_ref[...]
        o_ref[0, :, hh * DN_DV:(hh + 1) * DN_DV] = (n * _silu(zh)).astype(BF16)


def _deltanet(proj, small, small_t, conv_w, conv_init, prow, pcol, norm_g, s0, blk_len, chunk):
    nb, t = proj.shape[:2]
    nblk = t // blk_len
    grid = (nb, nblk)
    return pl.pallas_call(
        functools.partial(_dn_kernel, blk_len=blk_len, chunk=chunk),
        grid=grid,
        in_specs=[
            pl.BlockSpec((1, blk_len, DN_CONV_CH), lambda b, i: (b, i, 0)),
            pl.BlockSpec((1, blk_len, 1024), lambda b, i: (b, i, 2)),
            pl.BlockSpec((blk_len, LANES), lambda b, i: (b * nblk + i, 0)),
            pl.BlockSpec((1, 16, blk_len), lambda b, i: (b, 0, i)),
            pl.BlockSpec((DN_CONV_W, DN_CONV_CH), lambda b, i: (0, 0)),
            pl.BlockSpec((1, 8, DN_CONV_CH), lambda b, i: (b, 0, 0)),
            pl.BlockSpec((8, LANES), lambda b, i: (0, 0)),
            pl.BlockSpec((16, LANES), lambda b, i: (0, 0)),
            pl.BlockSpec((1, DN_DV), lambda b, i: (0, 0)),
            pl.BlockSpec((1, DN_V_HEADS, DN_DK, DN_DV), lambda b, i: (b, 0, 0, 0)),
        ],
        out_specs=[
            pl.BlockSpec((1, blk_len, 1024), lambda b, i: (b, i, 0)),
            pl.BlockSpec((1, DN_V_HEADS, DN_DK, DN_DV), lambda b, i: (b, 0, 0, 0)),
            pl.BlockSpec((1, DN_CONV_W - 1, DN_CONV_CH), lambda b, i: (b, 0, 0)),
        ],
        out_shape=[
            jax.ShapeDtypeStruct((nb, t, 1024), BF16),
            jax.ShapeDtypeStruct((nb, DN_V_HEADS, DN_DK, DN_DV), F32),
            jax.ShapeDtypeStruct((nb, DN_CONV_W - 1, DN_CONV_CH), F32),
        ],
        scratch_shapes=[pltpu.VMEM((blk_len + 8, DN_CONV_CH), F32),
                        pltpu.VMEM((DN_V_HEADS, DN_DK, DN_DV), F32)],
        compiler_params=_params("parallel", "arbitrary"),
        name="deltanet",
    )(proj, proj, small, small_t, conv_w, conv_init, prow, pcol, norm_g, s0)


def _memkv_kernel(m_ref, w_ref, o_ref):
    o_ref[...] = _dot(m_ref[...], w_ref[...])


def _memkv(mem, w_kv):
    n = mem.shape[0]
    tn = 1024
    return pl.pallas_call(
        _memkv_kernel,
        grid=(w_kv.shape[1] // tn,),
        in_specs=[pl.BlockSpec((n, D_MODEL), lambda j: (0, 0)),
                  pl.BlockSpec((D_MODEL, tn), lambda j: (0, j))],
        out_specs=pl.BlockSpec((n, tn), lambda j: (0, j)),
        out_shape=jax.ShapeDtypeStruct((n, w_kv.shape[1]), F32),
        compiler_params=_params("parallel"),
        name="memkv",
    )(mem, w_kv)


def _mix_kernel(h_ref, hb_ref, ret_ref, dn_ref, mk_ref, mv_ref, wxq_ref, wg_ref, wb_ref, wo_ref,
                g1_ref, b1_ref, h1_ref, h1b_ref):
    hb = hb_ref[...]
    xq = jnp.dot(hb, wxq_ref[...], preferred_element_type=F32)
    xo = []
    for hh in range(XA_HEADS):
        sl = slice(hh * XA_DH, (hh + 1) * XA_DH)
        s = _dot_nt(xq[:, sl], mk_ref[0, :, sl]) * (XA_DH ** -0.5)
        s = s - jnp.max(s, -1, keepdims=True)
        p = jnp.exp(s)
        p = p / jnp.sum(p, -1, keepdims=True)
        xo.append(_dot(p, mv_ref[0, :, sl]))
    xo = jnp.concatenate(xo, axis=1)
    mixed = None
    for n, br in enumerate((ret_ref[...], dn_ref[...], xo)):
        gate = _sigmoid(jnp.dot(hb, wg_ref[:, n * D_MODEL:(n + 1) * D_MODEL], preferred_element_type=F32))
        term = gate * _dot(br, wb_ref[n])
        mixed = term if mixed is None else mixed + term
    y = ALPHA * h_ref[...] + _dot(mixed, wo_ref[...])
    h1 = _layernorm(y, g1_ref[...], b1_ref[...])
    h1_ref[...] = h1
    h1b_ref[...] = h1.astype(BF16)


def _mix(h, hb, ret_o, dn_o, mem_k, mem_v, w_xq, w_gate, w_branch, w_out, ln_g, ln_b, tm, tiles_per_batch):
    n_tok = h.shape[0]
    const2 = lambda i: (0, 0)
    row = lambda i: (i, 0)
    single = pl.Buffered(1)
    return pl.pallas_call(
        _mix_kernel,
        grid=(n_tok // tm,),
        in_specs=[
            pl.BlockSpec((tm, D_MODEL), row),
            pl.BlockSpec((tm, D_MODEL), row),
            pl.BlockSpec((tm, D_MODEL), row),
            pl.BlockSpec((tm, D_MODEL), row),
            pl.BlockSpec((1, N_MEM, D_MODEL), lambda i: (i // tiles_per_batch, 0, 0)),
            pl.BlockSpec((1, N_MEM, D_MODEL), lambda i: (i // tiles_per_batch, 0, 0)),
            pl.BlockSpec((D_MODEL, D_MODEL), const2, pipeline_mode=single),
            pl.BlockSpec((D_MODEL, 3 * D_MODEL), const2, pipeline_mode=single),
            pl.BlockSpec((3, D_MODEL, D_MODEL), lambda i: (0, 0, 0), pipeline_mode=single),
            pl.BlockSpec((D_MODEL, D_MODEL), const2, pipeline_mode=single),
            pl.BlockSpec((1, D_MODEL), const2),
            pl.BlockSpec((1, D_MODEL), const2),
        ],
        out_specs=[pl.BlockSpec((tm, D_MODEL), row), pl.BlockSpec((tm, D_MODEL), row)],
        out_shape=[jax.ShapeDtypeStruct((n_tok, D_MODEL), F32),
                   jax.ShapeDtypeStruct((n_tok, D_MODEL), BF16)],
        compiler_params=_params("parallel"),
        name="mix",
    )(h, hb, ret_o, dn_o, mem_k, mem_v, w_xq, w_gate, w_branch, w_out, ln_g, ln_b)


def _route(x, wr, br):
    logits = _dot_exact(x, wr) + br
    lane_i = lax.broadcasted_iota(jnp.int32, logits.shape, 1)
    lane = lane_i.astype(F32)
    far = jnp.float32(LANES)
    neg = jnp.float32(-3.0e38)
    is_c = jnp.logical_and(lane_i >= MOE_NE, lane_i < MOE_NE + MOE_GROUPS)
    cl = jnp.where(is_c, logits, neg)
    cmax = jnp.max(cl, -1, keepdims=True)
    denom = jnp.sum(jnp.where(is_c, jnp.exp(jnp.where(is_c, logits - cmax, 0.0)), 0.0), -1, keepdims=True)
    p_grp = 1.0 / denom
    grp = jnp.min(jnp.where(jnp.logical_and(is_c, cl == cmax), lane - MOE_NE, far), -1, keepdims=True)
    in_grp = jnp.logical_and(lane_i < MOE_NE, _idiv(lane_i, MOE_EXPERTS).astype(F32) == grp)
    fl = jnp.where(in_grp, logits, neg)
    v1 = jnp.max(fl, -1, keepdims=True)
    i1 = jnp.min(jnp.where(jnp.logical_and(in_grp, fl == v1), lane, far), -1, keepdims=True)
    rest = jnp.logical_and(in_grp, lane != i1)
    fl2 = jnp.where(rest, logits, neg)
    v2 = jnp.max(fl2, -1, keepdims=True)
    i2 = jnp.min(jnp.where(jnp.logical_and(rest, fl2 == v2), lane, far), -1, keepdims=True)
    e2 = jnp.exp(v2 - v1)
    w1 = p_grp / (1.0 + e2)
    w2 = p_grp * e2 / (1.0 + e2)
    return jnp.where(lane == i1, w1, 0.0) + jnp.where(lane == i2, w2, 0.0), grp


def _route_kernel(h1_ref, wr_ref, br_ref, comb_ref, cnt_ref):
    comb, grp = _route(h1_ref[...], wr_ref[...], br_ref[...])
    lane_i = lax.broadcasted_iota(jnp.int32, comb.shape, 1)
    comb_ref[...] = jnp.where(lane_i == LANES - 1, grp, comb)
    onehot = jnp.where(lane_i.astype(F32) == grp, 1.0, 0.0)
    cnt_ref[0] = jnp.broadcast_to(jnp.sum(onehot, 0, keepdims=True), (8, LANES))


def _route_call(h1, w_route, b_route, tm):
    n_tok = h1.shape[0]
    return pl.pallas_call(
        _route_kernel,
        grid=(n_tok // tm,),
        in_specs=[pl.BlockSpec((tm, D_MODEL), lambda i: (i, 0)),
                  pl.BlockSpec((D_MODEL, LANES), lambda i: (0, 0)),
                  pl.BlockSpec((1, LANES), lambda i: (0, 0))],
        out_specs=[pl.BlockSpec((tm, LANES), lambda i: (i, 0)),
                   pl.BlockSpec((1, 8, LANES), lambda i: (i, 0, 0))],
        out_shape=[jax.ShapeDtypeStruct((n_tok, LANES), F32),
                   jax.ShapeDtypeStruct((n_tok // tm, 8, LANES), F32)],
        compiler_params=_params("parallel"),
        name="route",
    )(h1, w_route, b_route)


MOE_ALIGN = 16
MOE_EPS = 4


def _moe_sizes(tm):
    s0 = min(tm, tm // MOE_GROUPS + 64)
    return tuple(sorted({s0, max(s0, tm // 2), tm}))


def _moe_kernel(offs_ref, cls_ref, h1_ref, h1b_ref, comb_ref, wg_ref, wu_ref, wd_ref, g2_ref, b2_ref,
                y_ref, pt_scr, xs_scr, cs_scr, ys_scr, *, tm, nr1, sizes):
    i = pl.program_id(0)
    e = pl.program_id(1)
    nrt = xs_scr.shape[0]

    @pl.when(e == 0)
    def _():
        comb = comb_ref[...]
        lane_i = lax.broadcasted_iota(jnp.int32, (tm, LANES), 1)
        grp = comb[:, LANES - 1:LANES]
        is_g = lane_i.astype(F32) == grp
        ri = lax.broadcasted_iota(jnp.int32, (tm, tm), 0)
        ci = lax.broadcasted_iota(jnp.int32, (tm, tm), 1)
        tri = jnp.where(ri > ci, 1.0, 0.0).astype(BF16)
        prefix = jnp.dot(tri, jnp.where(is_g, 1.0, 0.0).astype(BF16), preferred_element_type=F32)
        rank = jnp.sum(jnp.where(is_g, prefix, 0.0), -1, keepdims=True)
        offv = jnp.zeros((tm, 1), F32)
        for g in range(MOE_GROUPS):
            offv = jnp.where(grp == g, offs_ref[i * MOE_GROUPS + g].astype(F32), offv)
        pos = offv + rank
        col = lax.broadcasted_iota(jnp.int32, (tm, nr1), 1).astype(F32)
        pt = jnp.where(col == pos, 1.0, 0.0).astype(BF16)
        pt_scr[...] = pt
        combw = jnp.where(lane_i < MOE_NE, comb, 0.0)
        c_hi = combw.astype(BF16)
        c_lo = (combw - c_hi.astype(F32)).astype(BF16)
        xs_scr[0:nr1, :] = _dot_tn(pt, h1b_ref[...]).astype(BF16)
        cs_scr[0:nr1, :] = _dot_tn(pt, c_hi) + _dot_tn(pt, c_lo)
        xs_scr[nr1:nrt, :] = jnp.zeros((nrt - nr1, D_MODEL), BF16)
        cs_scr[nr1:nrt, :] = jnp.zeros((nrt - nr1, LANES), F32)
        ys_scr[...] = jnp.zeros_like(ys_scr)

    g = e // (MOE_EXPERTS // MOE_EPS)
    off = pl.multiple_of(offs_ref[i * MOE_GROUPS + g], MOE_ALIGN)
    cls = cls_ref[i * MOE_GROUPS + g]
    for ci, m in enumerate(sizes):
        @pl.when(cls == ci)
        def _(m=m):
            rows = pl.ds(off, m)
            xb = xs_scr[rows, :]
            cb = cs_scr[rows, :]
            lane_i = lax.broadcasted_iota(jnp.int32, (m, LANES), 1)
            sub = range(MOE_EPS)
            colw = [jnp.sum(jnp.where(lane_i == e * MOE_EPS + j, cb, 0.0), -1, keepdims=True) for j in sub]
            gate = [jnp.dot(xb, wg_ref[j], preferred_element_type=F32) for j in sub]
            up = [jnp.dot(xb, wu_ref[j], preferred_element_type=F32) for j in sub]
            hg = [(_silu(gate[j]) * up[j] * colw[j]).astype(BF16) for j in sub]
            wd = wd_ref[...].reshape(MOE_EPS * MOE_DFF, D_MODEL)
            ys_scr[rows, :] += jnp.dot(jnp.concatenate(hg, axis=1), wd, preferred_element_type=F32)

    @pl.when(e == MOE_NE // MOE_EPS - 1)
    def _():
        moe = jnp.dot(pt_scr[...], ys_scr[0:nr1, :].astype(BF16), preferred_element_type=F32)
        y_ref[...] = _layernorm(ALPHA * h1_ref[...] + moe, g2_ref[...], b2_ref[...])


def _moe(h1, h1b, comb, cnt, w_gate, w_up, w_down, ln_g, ln_b, tm):
    n_tok = h1.shape[0]
    sizes = _moe_sizes(tm)
    nr1 = -(-(tm + MOE_GROUPS * MOE_ALIGN) // LANES) * LANES
    nrt = nr1 + tm
    c = cnt[:, 0, :MOE_GROUPS].astype(jnp.int32)
    seg = (c + MOE_ALIGN - 1) // MOE_ALIGN * MOE_ALIGN
    offs = (jnp.cumsum(seg, axis=1) - seg).reshape(-1)
    cls = sum((c > s).astype(jnp.int32) for s in sizes[:-1]).reshape(-1)
    grid_spec = pltpu.PrefetchScalarGridSpec(
        num_scalar_prefetch=2,
        grid=(n_tok // tm, MOE_NE // MOE_EPS),
        in_specs=[
            pl.BlockSpec((tm, D_MODEL), lambda i, e, o, k: (i, 0), pipeline_mode=pl.Buffered(1)),
            pl.BlockSpec((tm, D_MODEL), lambda i, e, o, k: (i, 0)),
            pl.BlockSpec((tm, LANES), lambda i, e, o, k: (i, 0)),
            pl.BlockSpec((MOE_EPS, D_MODEL, MOE_DFF), lambda i, e, o, k: (e, 0, 0)),
            pl.BlockSpec((MOE_EPS, D_MODEL, MOE_DFF), lambda i, e, o, k: (e, 0, 0)),
            pl.BlockSpec((MOE_EPS, MOE_DFF, D_MODEL), lambda i, e, o, k: (e, 0, 0)),
            pl.BlockSpec((1, D_MODEL), lambda i, e, o, k: (0, 0)),
            pl.BlockSpec((1, D_MODEL), lambda i, e, o, k: (0, 0)),
        ],
        out_specs=pl.BlockSpec((tm, D_MODEL), lambda i, e, o, k: (i, 0)),
        scratch_shapes=[pltpu.VMEM((tm, nr1), BF16), pltpu.VMEM((nrt, D_MODEL), BF16),
                        pltpu.VMEM((nrt, LANES), F32), pltpu.VMEM((nrt, D_MODEL), F32)],
    )
    return pl.pallas_call(
        functools.partial(_moe_kernel, tm=tm, nr1=nr1, sizes=sizes),
        grid_spec=grid_spec,
        out_shape=jax.ShapeDtypeStruct((n_tok, D_MODEL), F32),
        compiler_params=_params("parallel", "arbitrary"),
        name="moe",
    )(offs, cls, h1, h1b, comb, w_gate, w_up, w_down, ln_g, ln_b)


def _rope_tables(pos):
    half = RET_DK // 2
    inv = 1.0 / (ROPE_BASE ** (jnp.arange(half, dtype=F32) / half))
    ang = pos.astype(F32)[:, None] * inv[None, :]
    cos = jnp.cos(ang)
    sin = jnp.sin(ang)
    return jnp.concatenate([cos, cos], -1), jnp.concatenate([-sin, sin], -1)


def _group(x, pos, mem_k, mem_v, ret_s0, dn_s0, conv_buf, wts, *, tm, tm_mix, tm_moe, ret_chunk, dn_blk,
           dn_chunk):
    nb, t, d = x.shape
    n_tok = nb * t
    h, hb, small, small_t, proj = _in_proj(x.reshape(n_tok, d), wts["ln_in_g"], wts["ln_in_b"],
                                           wts["w_main"], wts["w_small"], wts["w_small_t"], tm)
    proj3 = proj.reshape(nb, t, PROJ_W)
    cos2, sin2 = _rope_tables(pos)
    ret_o, ret_s = _retention(proj3, cos2, sin2, ret_s0, wts["ret_gn_g"], wts["ret_gn_b"], ret_chunk)
    small_t3 = small_t.reshape(16, nb, t).transpose(1, 0, 2)
    conv_init = jnp.concatenate([jnp.zeros((nb, 8 - (DN_CONV_W - 1), DN_CONV_CH), F32), conv_buf], axis=1)
    dn_o, dn_s, conv_new = _deltanet(proj3, small, small_t3, wts["dn_conv"], conv_init, wts["dn_prow"],
                                     wts["dn_pcol"], wts["dn_norm_g"], dn_s0, dn_blk, dn_chunk)
    h1, h1b = _mix(h, hb, ret_o.reshape(n_tok, d), dn_o.reshape(n_tok, d), mem_k, mem_v,
                   wts["w_xq"], wts["w_gate"], wts["w_branch"], wts["w_out"], wts["ln1_g"], wts["ln1_b"],
                   tm_mix, t // tm_mix)
    comb, cnt = _route_call(h1, wts["w_route"], wts["b_route"], tm_moe)
    y = _moe(h1, h1b, comb, cnt, wts["moe_w_gate"], wts["moe_w_up"], wts["moe_w_down"],
             wts["ln2_g"], wts["ln2_b"], tm_moe)
    return y.reshape(nb, t, d), ret_s, dn_s, conv_new


def kernel(x_prompt, x_sample, mem_prompt, state_ret, state_dn, state_dn_conv, cache_mem_k, cache_mem_v,
           ln_in_g, ln_in_b, w_in, ret_gn_g, ret_gn_b, dn_conv, dn_A_log, dn_dt_bias, dn_norm_g,
           w_mem_kv, w_branch, w_out, ln1_g, ln1_b, moe_w_coarse, moe_b_coarse, moe_w_fine, moe_b_fine,
           moe_w_gate, moe_w_up, moe_w_down, ln2_g, ln2_b):
    bp, tp, d = x_prompt.shape
    bs, ts, _ = x_sample.shape
    l = 0
    wi = w_in[l]
    rq, rk, rv, rg = wi[:, 0:512], wi[:, 512:1024], wi[:, 1024:2048], wi[:, 2048:3072]
    dqkv, dz = wi[:, 3072:5120], wi[:, 5120:6144]
    dba = wi[:, 6144:6160]
    xq, gates = wi[:, 6160:7184], wi[:, 7184:10256]
    wts = {
        "ln_in_g": ln_in_g.reshape(1, d), "ln_in_b": ln_in_b.reshape(1, d),
        "w_main": jnp.concatenate([dqkv, dz, rq, rk, rv, rg], axis=1).astype(BF16),
        "w_small": jnp.pad(dba, ((0, 0), (0, LANES - 16))).astype(BF16),
        "w_small_t": dba.T.astype(BF16),
        "ret_gn_g": ret_gn_g[l].reshape(1, -1), "ret_gn_b": ret_gn_b[l].reshape(1, -1),
        "dn_conv": dn_conv[l],
        "dn_prow": jnp.zeros((8, LANES), F32).at[0, 8:16].set(-jnp.exp(dn_A_log[l])).at[1, 8:16].set(dn_dt_bias[l]),
        "dn_pcol": jnp.zeros((16, LANES), F32).at[8:16, 0].set(-jnp.exp(dn_A_log[l])).at[8:16, 1].set(dn_dt_bias[l]),
        "dn_norm_g": dn_norm_g[l].reshape(1, -1),
        "w_xq": xq.astype(BF16), "w_gate": gates.astype(BF16),
        "w_branch": w_branch[l].astype(BF16), "w_out": w_out[l].astype(BF16),
        "ln1_g": ln1_g[l].reshape(1, d), "ln1_b": ln1_b[l].reshape(1, d),
        "w_route": jnp.pad(jnp.concatenate([moe_w_fine[l], moe_w_coarse[l]], axis=1),
                           ((0, 0), (0, LANES - MOE_NE - MOE_GROUPS))),
        "b_route": jnp.pad(jnp.concatenate([moe_b_fine[l], moe_b_coarse[l]]),
                           (0, LANES - MOE_NE - MOE_GROUPS)).reshape(1, LANES),
        "moe_w_gate": moe_w_gate[l].reshape(MOE_NE, d, MOE_DFF).astype(BF16),
        "moe_w_up": moe_w_up[l].reshape(MOE_NE, d, MOE_DFF).astype(BF16),
        "moe_w_down": moe_w_down[l].reshape(MOE_NE, MOE_DFF, d).astype(BF16),
        "ln2_g": ln2_g[l].reshape(1, d), "ln2_b": ln2_b[l].reshape(1, d),
    }

    mkv = _memkv(mem_prompt.reshape(bp * N_MEM, d), w_mem_kv[l].astype(BF16))
    mk = mkv[:, :XA_HEADS * XA_DH].reshape(bp, N_MEM, XA_HEADS * XA_DH)
    mv = mkv[:, XA_HEADS * XA_DH:].reshape(bp, N_MEM, XA_HEADS * XA_DH)
    yp, rs_p, ds_p, cb_p = _group(
        x_prompt, jnp.arange(tp, dtype=jnp.int32), mk, mv,
        jnp.zeros((bp, RET_HEADS, RET_DK, RET_DV), F32),
        jnp.zeros((bp, DN_V_HEADS, DN_DK, DN_DV), F32),
        jnp.zeros((bp, DN_CONV_W - 1, DN_CONV_CH), F32),
        wts, tm=min(1024, tp), tm_mix=min(512, tp), tm_moe=min(1024, tp), ret_chunk=min(256, tp),
        dn_blk=min(128, tp),
        dn_chunk=min(DN_CHUNK, tp))
    ys, rs_s, ds_s, cb_s = _group(
        x_sample, PAST_LEN + jnp.arange(ts, dtype=jnp.int32),
        cache_mem_k[l].reshape(bs, N_MEM, XA_HEADS * XA_DH), cache_mem_v[l].reshape(bs, N_MEM, XA_HEADS * XA_DH),
        state_ret[l], state_dn[l], state_dn_conv[l],
        wts, tm=bs * ts, tm_mix=ts, tm_moe=bs * ts, ret_chunk=ts, dn_blk=ts, dn_chunk=min(DN_CHUNK, ts))
    return (yp, ys, rs_p[None], rs_s[None], ds_p[None], ds_s[None], cb_p[None], cb_s[None],
            mk.reshape(1, bp, N_MEM, XA_HEADS, XA_DH), mv.reshape(1, bp, N_MEM, XA_HEADS, XA_DH))
```

```python
import functools
import math

import numpy as np

import jax
import jax.numpy as jnp
from jax import lax
from jax.experimental import pallas as pl
from jax.experimental.pallas import tpu as pltpu

F32 = jnp.float32
BF16 = jnp.bfloat16

D_MODEL = 1024
PAST_LEN = 1024
RET_HEADS = 4
RET_DK = 128
RET_DV = 256
ROPE_BASE = 10000.0
DN_QK_HEADS = 4
DN_V_HEADS = 8
DN_DK = 128
DN_DV = 128
DN_QK = DN_QK_HEADS * DN_DK
DN_CONV_W = 4
DN_CONV_CH = 2048
DN_CHUNK = 64
DN_SUB = 16
XA_HEADS = 4
XA_DH = 256
N_MEM = 256
MOE_GROUPS = 4
MOE_EXPERTS = 8
MOE_NE = MOE_GROUPS * MOE_EXPERTS
MOE_DFF = 256
DEPTH = 1
ALPHA = (2.0 * DEPTH) ** 0.25
LANES = 128
PROJ_W = 6144
VMEM_LIMIT = 56 * 1024 * 1024


def _dot(a, b):
    return jnp.dot(a.astype(BF16), b.astype(BF16), preferred_element_type=F32)


def _dot_nt(a, b):
    return lax.dot_general(a.astype(BF16), b.astype(BF16), (((1,), (1,)), ((), ())),
                           preferred_element_type=F32)


def _dot_tn(a, b):
    return lax.dot_general(a.astype(BF16), b.astype(BF16), (((0,), (0,)), ((), ())),
                           preferred_element_type=F32)


def _dot_exact(a, b):
    return jnp.dot(a, b, preferred_element_type=F32, precision=lax.Precision.HIGHEST)


def _dot_split(a, b):
    ah = a.astype(BF16)
    al = (a - ah.astype(F32)).astype(BF16)
    bh = b.astype(BF16)
    bl = (b - bh.astype(F32)).astype(BF16)
    hi = jnp.dot(ah, bh, preferred_element_type=F32)
    return hi + (jnp.dot(ah, bl, preferred_element_type=F32) + jnp.dot(al, bh, preferred_element_type=F32))


def _layernorm(x, g, b, eps=1e-5):
    mu = jnp.mean(x, -1, keepdims=True)
    xc = x - mu
    var = jnp.mean(xc * xc, -1, keepdims=True)
    return xc * lax.rsqrt(var + eps) * g + b


def _silu(x):
    return x * (1.0 / (1.0 + jnp.exp(-x)))


def _sigmoid(x):
    return 1.0 / (1.0 + jnp.exp(-x))


def _softplus(x):
    return jnp.maximum(x, 0.0) + jnp.log(1.0 + jnp.exp(-jnp.abs(x)))


def _idiv(x, n):
    return jnp.right_shift(x, int(math.log2(n)))


def _params(*sem):
    return pltpu.CompilerParams(dimension_semantics=sem, vmem_limit_bytes=VMEM_LIMIT)


def _in_proj_kernel(x_ref, g_ref, b_ref, w_ref, ws_ref, wst_ref,
                    h_ref, hb_ref, small_ref, smallt_ref, proj_ref, hs_ref):
    @pl.when(pl.program_id(1) == 0)
    def _():
        h = _layernorm(x_ref[...], g_ref[...], b_ref[...])
        h_ref[...] = h
        hb = h.astype(BF16)
        hb_ref[...] = hb
        hs_ref[...] = hb
        small_ref[...] = jnp.dot(hb, ws_ref[...], preferred_element_type=F32)
        smallt_ref[...] = lax.dot_general(wst_ref[...], hb, (((1,), (1,)), ((), ())),
                                          preferred_element_type=F32)

    proj_ref[...] = jnp.dot(hs_ref[...], w_ref[...], preferred_element_type=F32).astype(BF16)


def _in_proj(x, ln_g, ln_b, w_main, w_small, w_small_t, tm, tn=1024):
    n_tok = x.shape[0]
    grid = (n_tok // tm, PROJ_W // tn)
    return pl.pallas_call(
        _in_proj_kernel,
        grid=grid,
        in_specs=[
            pl.BlockSpec((tm, D_MODEL), lambda i, n: (i, 0)),
            pl.BlockSpec((1, D_MODEL), lambda i, n: (0, 0)),
            pl.BlockSpec((1, D_MODEL), lambda i, n: (0, 0)),
            pl.BlockSpec((D_MODEL, tn), lambda i, n: (0, n)),
            pl.BlockSpec((D_MODEL, LANES), lambda i, n: (0, 0)),
            pl.BlockSpec((16, D_MODEL), lambda i, n: (0, 0)),
        ],
        out_specs=[
            pl.BlockSpec((tm, D_MODEL), lambda i, n: (i, 0)),
            pl.BlockSpec((tm, D_MODEL), lambda i, n: (i, 0)),
            pl.BlockSpec((tm, LANES), lambda i, n: (i, 0)),
            pl.BlockSpec((16, tm), lambda i, n: (0, i)),
            pl.BlockSpec((tm, tn), lambda i, n: (i, n)),
        ],
        out_shape=[
            jax.ShapeDtypeStruct((n_tok, D_MODEL), F32),
            jax.ShapeDtypeStruct((n_tok, D_MODEL), BF16),
            jax.ShapeDtypeStruct((n_tok, LANES), F32),
            jax.ShapeDtypeStruct((16, n_tok), F32),
            jax.ShapeDtypeStruct((n_tok, PROJ_W), BF16),
        ],
        scratch_shapes=[pltpu.VMEM((tm, D_MODEL), BF16)],
        compiler_params=_params("parallel", "arbitrary"),
        name="in_proj",
    )(x, ln_g, ln_b, w_main, w_small, w_small_t)


def _ret_kernel(q_ref, k_ref, v_ref, rg_ref, cos_ref, sin_ref, s0_ref, gng_ref, gnb_ref,
                o_ref, sout_ref, s_scr, *, chunk):
    @pl.when(pl.program_id(1) == 0)
    def _():
        s_scr[...] = s0_ref[0]

    heads = range(RET_HEADS)
    lg = [math.log(1.0 - 2.0 ** (-5.0 - h)) for h in heads]
    cos = cos_ref[...]
    sin = sin_ref[...]
    ri = lax.broadcasted_iota(jnp.int32, (chunk, chunk), 0)
    ci = lax.broadcasted_iota(jnp.int32, (chunk, chunk), 1)
    causal = ri >= ci
    diff = jnp.where(causal, (ri - ci).astype(F32), 0.0)
    idx = lax.broadcasted_iota(jnp.int32, (chunk, 1), 0).astype(F32)
    q = [q_ref[0, :, h * RET_DK:(h + 1) * RET_DK].astype(F32) for h in heads]
    k = [k_ref[0, :, h * RET_DK:(h + 1) * RET_DK].astype(F32) for h in heads]
    v = [v_ref[0, :, h * RET_DV:(h + 1) * RET_DV] for h in heads]
    qr = [q[h] * cos + pltpu.roll(q[h], RET_DK // 2, 1) * sin for h in heads]
    kr = [(k[h] * cos + pltpu.roll(k[h], RET_DK // 2, 1) * sin) * (RET_DK ** -0.5) for h in heads]
    scores = [_dot_nt(qr[h], kr[h]) * jnp.where(causal, jnp.exp(diff * lg[h]), 0.0) for h in heads]
    s = [s_scr[h] for h in heads]
    cross = [_dot(qr[h], s[h]) * jnp.exp((idx + 1.0) * lg[h]) for h in heads]
    o = [_dot(scores[h], v[h]) + cross[h] for h in heads]
    kv = [_dot_tn(kr[h] * jnp.exp((chunk - 1.0 - idx) * lg[h]), v[h]) for h in heads]
    for h in heads:
        s_new = s[h] * math.exp(chunk * lg[h]) + kv[h]
        s_scr[h] = s_new
        sout_ref[0, h] = s_new
        sl = slice(h * RET_DV, (h + 1) * RET_DV)
        gated = _layernorm(o[h], gng_ref[:, sl], gnb_ref[:, sl]) * _silu(rg_ref[0, :, sl].astype(F32))
        o_ref[0, :, sl] = gated.astype(BF16)


def _retention(proj, cos2, sin2, s0, gn_g, gn_b, chunk):
    nb, t = proj.shape[:2]
    grid = (nb, t // chunk)
    qk_w = RET_HEADS * RET_DK
    v_w = RET_HEADS * RET_DV
    return pl.pallas_call(
        functools.partial(_ret_kernel, chunk=chunk),
        grid=grid,
        in_specs=[
            pl.BlockSpec((1, chunk, qk_w), lambda b, c: (b, c, 0)),
            pl.BlockSpec((1, chunk, qk_w), lambda b, c: (b, c, 512 // qk_w)),
            pl.BlockSpec((1, chunk, v_w), lambda b, c: (b, c, 1024 // v_w)),
            pl.BlockSpec((1, chunk, v_w), lambda b, c: (b, c, 2048 // v_w)),
            pl.BlockSpec((chunk, RET_DK), lambda b, c: (c, 0)),
            pl.BlockSpec((chunk, RET_DK), lambda b, c: (c, 0)),
            pl.BlockSpec((1, RET_HEADS, RET_DK, RET_DV), lambda b, c: (b, 0, 0, 0)),
            pl.BlockSpec((1, v_w), lambda b, c: (0, 0)),
            pl.BlockSpec((1, v_w), lambda b, c: (0, 0)),
        ],
        out_specs=[
            pl.BlockSpec((1, chunk, v_w), lambda b, c: (b, c, 0)),
            pl.BlockSpec((1, RET_HEADS, RET_DK, RET_DV), lambda b, c: (b, 0, 0, 0)),
        ],
        out_shape=[
            jax.ShapeDtypeStruct((nb, t, v_w), BF16),
            jax.ShapeDtypeStruct((nb, RET_HEADS, RET_DK, RET_DV), F32),
        ],
        scratch_shapes=[pltpu.VMEM((RET_HEADS, RET_DK, RET_DV), F32)],
        compiler_params=_params("parallel", "arbitrary"),
        name="retention",
    )(proj, proj, proj, proj, cos2, sin2, s0, gn_g, gn_b)


def _dn_kernel(xqk_ref, xv_ref, z_ref, sm_ref, smt_ref, cw_ref, cinit_ref, prow_ref, pcol_ref, ng_ref, s0_ref,
               o_ref, sout_ref, cout_ref, xext, s_scr, *, blk_len, chunk):
    L = blk_len
    blk = pl.program_id(1)

    @pl.when(blk == 0)
    def _():
        xext[0:8, :] = cinit_ref[0]
        s_scr[...] = s0_ref[0]

    x = jnp.concatenate([xqk_ref[0], xv_ref[0]], axis=1).astype(F32)
    xext[8:8 + L, :] = x
    conv = x * cw_ref[3:4, :]
    for w in range(DN_CONV_W - 1):
        conv = conv + xext[5 + w:5 + w + L, :] * cw_ref[w:w + 1, :]
    conv = _silu(conv)
    xext[0:8, :] = x[L - 8:L, :]
    cout_ref[0] = x[L - (DN_CONV_W - 1):L, :]

    sm = sm_ref[...]
    prow = prow_ref[...]
    beta_tm = _sigmoid(sm)
    g_tm = prow[0:1, :] * _softplus(sm + prow[1:2, :])
    smt = smt_ref[0]
    pcol = pcol_ref[...]
    g_hm = pcol[8:16, 0:1] * _softplus(smt[8:16, :] + pcol[8:16, 1:2])

    ri = lax.broadcasted_iota(jnp.int32, (L, L), 0)
    ci = lax.broadcasted_iota(jnp.int32, (L, L), 1)
    same_chunk = _idiv(ri, chunk) == _idiv(ci, chunk)
    lower = jnp.logical_and(ri >= ci, same_chunk)
    strict = jnp.logical_and(ri > ci, same_chunk)
    same_sub = _idiv(ri, DN_SUB) == _idiv(ci, DN_SUB)
    tri = jnp.where(lower, 1.0, 0.0).astype(F32)
    tri_t = jnp.where(jnp.logical_and(ri <= ci, same_chunk), 1.0, 0.0).astype(F32)
    gcum_tm = _dot_exact(tri, g_tm)
    gcum_hm = _dot_exact(g_hm, tri_t)
    egc_tm = jnp.exp(gcum_tm)

    n_sub = L // chunk
    rep = DN_V_HEADS // DN_QK_HEADS
    heads = range(DN_V_HEADS)
    q, k, kk, qk = [], [], [], []
    for j in range(DN_QK_HEADS):
        qj = conv[:, j * DN_DK:(j + 1) * DN_DK]
        kj = conv[:, DN_QK + j * DN_DK:DN_QK + (j + 1) * DN_DK]
        qj = qj * lax.rsqrt(jnp.sum(qj * qj, -1, keepdims=True) + 1e-6) * (DN_DK ** -0.5)
        kj = kj * lax.rsqrt(jnp.sum(kj * kj, -1, keepdims=True) + 1e-6)
        kq = _dot_nt(jnp.concatenate([kj, qj], axis=0), kj)
        q.append(qj)
        k.append(kj)
        kk.append(kq[0:L])
        qk.append(kq[L:2 * L])
    gc = [gcum_tm[:, 8 + hh:9 + hh] for hh in heads]
    beta = [beta_tm[:, hh:hh + 1] for hh in heads]
    eg = [egc_tm[:, 8 + hh:9 + hh] for hh in heads]
    rel = [jnp.where(lower, jnp.exp(jnp.where(lower, gc[hh] - gcum_hm[hh:hh + 1, :], 0.0)), 0.0) for hh in heads]
    a = [jnp.where(strict, beta[hh] * kk[hh // rep] * rel[hh], 0.0) for hh in heads]
    attn = [qk[hh // rep] * rel[hh] for hh in heads]
    pd = [jnp.where(same_sub, -a[hh], 0.0) for hh in heads]
    e = [a[hh] + pd[hh] for hh in heads]
    xacc = pd
    pw = pd
    for _ in range(3):
        pw = [_dot(pw[hh], pw[hh]) for hh in heads]
        xacc = [xacc[hh] + pw[hh] + _dot(xacc[hh], pw[hh]) for hh in heads]
    f = [e[hh] + _dot(xacc[hh], e[hh]) for hh in heads]
    f2 = [_dot(f[hh], f[hh]) for hh in heads]
    y = [f2[hh] - f[hh] - _dot(f[hh], f2[hh]) for hh in heads]
    rinv = [xacc[hh] + y[hh] + _dot(y[hh], xacc[hh]) for hh in heads]
    sol = []
    for hh in heads:
        vh = conv[:, 2 * DN_QK + hh * DN_DV:2 * DN_QK + (hh + 1) * DN_DV]
        rhs = jnp.concatenate([vh * beta[hh], k[hh // rep] * (beta[hh] * eg[hh])], axis=1)
        sol.append(rhs + _dot(rinv[hh], rhs))
    u = [sol[hh][:, 0:DN_DV] for hh in heads]
    wmat = [sol[hh][:, DN_DV:DN_DV + DN_DK] for hh in heads]
    qe = [q[hh // rep] * eg[hh] for hh in heads]
    s = [s_scr[hh] for hh in heads]
    v_new = [[] for _ in heads]
    q_s = [[] for _ in heads]
    for sc in range(n_sub):
        r0 = sc * chunk
        for hh in heads:
            ws = _dot(jnp.concatenate([wmat[hh][r0:r0 + chunk], qe[hh][r0:r0 + chunk]], axis=0), s[hh])
            vn = u[hh][r0:r0 + chunk] - ws[0:chunk]
            v_new[hh].append(vn)
            q_s[hh].append(ws[chunk:2 * chunk])
            g_last = gcum_tm[r0 + chunk - 1:r0 + chunk, 8 + hh:9 + hh]
            kd = k[hh // rep][r0:r0 + chunk] * jnp.exp(g_last - gc[hh][r0:r0 + chunk])
            s[hh] = s[hh] * jnp.exp(g_last) + _dot_tn(kd, vn)
    for hh in heads:
        s_scr[hh] = s[hh]
        sout_ref[0, hh] = s[hh]
        vn = v_new[hh][0] if n_sub == 1 else jnp.concatenate(v_new[hh], axis=0)
        qs = q_s[hh][0] if n_sub == 1 else jnp.concatenate(q_s[hh], axis=0)
        o = qs + _dot(attn[hh], vn)
        zh = z_ref[0, :, hh * DN_DV:(hh + 1) * DN_DV].astype(F32)
        n = o * lax.rsqrt(jnp.mean(o * o, -1, keepdims=True) + 1e-6) * ng_ref[...]
        o_ref[0, :, hh * DN_DV:(hh + 1) * DN_DV] = (n * _silu(zh)).astype(BF16)


def _deltanet(proj, small, small_t, conv_w, conv_init, prow, pcol, norm_g, s0, blk_len, chunk):
    nb, t = proj.shape[:2]
    nblk = t // blk_len
    grid = (nb, nblk)
    return pl.pallas_call(
        functools.partial(_dn_kernel, blk_len=blk_len, chunk=chunk),
        grid=grid,
        in_specs=[
            pl.BlockSpec((1, blk_len, 1024), lambda b, i: (b, i, 3072 // 1024)),
            pl.BlockSpec((1, blk_len, 1024), lambda b, i: (b, i, 4096 // 1024)),
            pl.BlockSpec((1, blk_len, 1024), lambda b, i: (b, i, 5120 // 1024)),
            pl.BlockSpec((blk_len, LANES), lambda b, i: (b * nblk + i, 0)),
            pl.BlockSpec((1, 16, blk_len), lambda b, i: (b, 0, i)),
            pl.BlockSpec((DN_CONV_W, DN_CONV_CH), lambda b, i: (0, 0)),
            pl.BlockSpec((1, 8, DN_CONV_CH), lambda b, i: (b, 0, 0)),
            pl.BlockSpec((8, LANES), lambda b, i: (0, 0)),
            pl.BlockSpec((16, LANES), lambda b, i: (0, 0)),
            pl.BlockSpec((1, DN_DV), lambda b, i: (0, 0)),
            pl.BlockSpec((1, DN_V_HEADS, DN_DK, DN_DV), lambda b, i: (b, 0, 0, 0)),
        ],
        out_specs=[
            pl.BlockSpec((1, blk_len, 1024), lambda b, i: (b, i, 0)),
            pl.BlockSpec((1, DN_V_HEADS, DN_DK, DN_DV), lambda b, i: (b, 0, 0, 0)),
            pl.BlockSpec((1, DN_CONV_W - 1, DN_CONV_CH), lambda b, i: (b, 0, 0)),
        ],
        out_shape=[
            jax.ShapeDtypeStruct((nb, t, 1024), BF16),
            jax.ShapeDtypeStruct((nb, DN_V_HEADS, DN_DK, DN_DV), F32),
            jax.ShapeDtypeStruct((nb, DN_CONV_W - 1, DN_CONV_CH), F32),
        ],
        scratch_shapes=[pltpu.VMEM((blk_len + 8, DN_CONV_CH), F32),
                        pltpu.VMEM((DN_V_HEADS, DN_DK, DN_DV), F32)],
        compiler_params=_params("parallel", "arbitrary"),
        name="deltanet",
    )(proj, proj, proj, small, small_t, conv_w, conv_init, prow, pcol, norm_g, s0)


def _memkv_kernel(m_ref, w_ref, o_ref):
    o_ref[...] = _dot(m_ref[...], w_ref[...])


def _memkv(mem, w_kv):
    n = mem.shape[0]
    tn = 1024
    return pl.pallas_call(
        _memkv_kernel,
        grid=(w_kv.shape[1] // tn,),
        in_specs=[pl.BlockSpec((n, D_MODEL), lambda j: (0, 0)),
                  pl.BlockSpec((D_MODEL, tn), lambda j: (0, j))],
        out_specs=pl.BlockSpec((n, tn), lambda j: (0, j)),
        out_shape=jax.ShapeDtypeStruct((n, w_kv.shape[1]), F32),
        compiler_params=_params("parallel"),
        name="memkv",
    )(mem, w_kv)


def _mix_kernel(h_ref, hb_ref, ret_ref, dn_ref, mk_ref, mv_ref, wxq_ref, wg_ref, wb_ref, wo_ref,
                g1_ref, b1_ref, h1_ref, h1b_ref, mem_scr, *, n_batch, tiles_per_batch):
    @pl.when(pl.program_id(0) % tiles_per_batch == 0)
    def _():
        for b in range(n_batch):
            for hh in range(XA_HEADS):
                mem_scr[0, hh, b] = mk_ref[0, b, :, hh, :].astype(BF16)
                mem_scr[1, hh, b] = mv_ref[0, b, :, hh, :].astype(BF16)

    hb = hb_ref[...]
    xq = jnp.dot(hb, wxq_ref[...], preferred_element_type=F32)
    tb = xq.shape[0] // n_batch
    pairs = [(b, hh) for b in range(n_batch) for hh in range(XA_HEADS)]
    s = [_dot_nt(xq[b * tb:(b + 1) * tb, hh * XA_DH:(hh + 1) * XA_DH], mem_scr[0, hh, b]) * (XA_DH ** -0.5)
         for b, hh in pairs]
    p = [jnp.exp(si - jnp.max(si, -1, keepdims=True)) for si in s]
    p = [pi / jnp.sum(pi, -1, keepdims=True) for pi in p]
    xo = [_dot(pi, mem_scr[1, hh, b]) for pi, (b, hh) in zip(p, pairs)]
    xo = [jnp.concatenate(xo[b * XA_HEADS:(b + 1) * XA_HEADS], axis=1) for b in range(n_batch)]
    xo = xo[0] if n_batch == 1 else jnp.concatenate(xo, axis=0)
    mixed = None
    for n, br in enumerate((ret_ref[...], dn_ref[...], xo)):
        gate = _sigmoid(jnp.dot(hb, wg_ref[:, n * D_MODEL:(n + 1) * D_MODEL], preferred_element_type=F32))
        term = gate * _dot(br, wb_ref[n])
        mixed = term if mixed is None else mixed + term
    y = ALPHA * h_ref[...] + _dot(mixed, wo_ref[...])
    h1 = _layernorm(y, g1_ref[...], b1_ref[...])
    h1_ref[...] = h1
    h1b_ref[...] = h1.astype(BF16)


def _mix(h, hb, ret_o, dn_o, mem_k, mem_v, w_xq, w_gate, w_branch, w_out, ln_g, ln_b, tm, t):
    n_tok = h.shape[0]
    const2 = lambda i: (0, 0)
    row = lambda i: (i, 0)
    single = pl.Buffered(1)
    if tm <= t:
        n_batch = 1
        tiles_per_batch = t // tm
    else:
        n_batch = tm // t
        tiles_per_batch = 1
    mem_spec = pl.BlockSpec((1, n_batch, N_MEM, XA_HEADS, XA_DH), lambda i: (0, i // tiles_per_batch, 0, 0, 0))
    return pl.pallas_call(
        functools.partial(_mix_kernel, n_batch=n_batch, tiles_per_batch=tiles_per_batch),
        grid=(n_tok // tm,),
        in_specs=[
            pl.BlockSpec((tm, D_MODEL), row),
            pl.BlockSpec((tm, D_MODEL), row),
            pl.BlockSpec((tm, D_MODEL), row),
            pl.BlockSpec((tm, D_MODEL), row),
            mem_spec,
            mem_spec,
            pl.BlockSpec((D_MODEL, D_MODEL), const2, pipeline_mode=single),
            pl.BlockSpec((D_MODEL, 3 * D_MODEL), const2, pipeline_mode=single),
            pl.BlockSpec((3, D_MODEL, D_MODEL), lambda i: (0, 0, 0), pipeline_mode=single),
            pl.BlockSpec((D_MODEL, D_MODEL), const2, pipeline_mode=single),
            pl.BlockSpec((1, D_MODEL), const2),
            pl.BlockSpec((1, D_MODEL), const2),
        ],
        out_specs=[pl.BlockSpec((tm, D_MODEL), row), pl.BlockSpec((tm, D_MODEL), row)],
        out_shape=[jax.ShapeDtypeStruct((n_tok, D_MODEL), F32),
                   jax.ShapeDtypeStruct((n_tok, D_MODEL), BF16)],
        scratch_shapes=[pltpu.VMEM((2, XA_HEADS, n_batch, N_MEM, XA_DH), BF16)],
        compiler_params=_params("arbitrary"),
        name="mix",
    )(h, hb, ret_o, dn_o, mem_k, mem_v, w_xq, w_gate, w_branch, w_out, ln_g, ln_b)


def _route(x, wr, br):
    logits = _dot_split(x, wr) + br
    lane_i = lax.broadcasted_iota(jnp.int32, logits.shape, 1)
    lane = lane_i.astype(F32)
    far = jnp.float32(LANES)
    neg = jnp.float32(-3.0e38)
    is_c = jnp.logical_and(lane_i >= MOE_NE, lane_i < MOE_NE + MOE_GROUPS)
    cl = jnp.where(is_c, logits, neg)
    cmax = jnp.max(cl, -1, keepdims=True)
    denom = jnp.sum(jnp.where(is_c, jnp.exp(jnp.where(is_c, logits - cmax, 0.0)), 0.0), -1, keepdims=True)
    p_grp = 1.0 / denom
    grp = jnp.min(jnp.where(jnp.logical_and(is_c, cl == cmax), lane - MOE_NE, far), -1, keepdims=True)
    in_grp = jnp.logical_and(lane_i < MOE_NE, _idiv(lane_i, MOE_EXPERTS).astype(F32) == grp)
    fl = jnp.where(in_grp, logits, neg)
    v1 = jnp.max(fl, -1, keepdims=True)
    i1 = jnp.min(jnp.where(jnp.logical_and(in_grp, fl == v1), lane, far), -1, keepdims=True)
    rest = jnp.logical_and(in_grp, lane != i1)
    fl2 = jnp.where(rest, logits, neg)
    v2 = jnp.max(fl2, -1, keepdims=True)
    i2 = jnp.min(jnp.where(jnp.logical_and(rest, fl2 == v2), lane, far), -1, keepdims=True)
    e2 = jnp.exp(v2 - v1)
    w1 = p_grp / (1.0 + e2)
    w2 = p_grp * e2 / (1.0 + e2)
    return jnp.where(lane == i1, w1, 0.0) + jnp.where(lane == i2, w2, 0.0), grp


def _route_kernel(h1_ref, wr_ref, br_ref, comb_ref, cnt_ref):
    comb, grp = _route(h1_ref[...], wr_ref[...], br_ref[...])
    lane_i = lax.broadcasted_iota(jnp.int32, comb.shape, 1)
    comb_ref[...] = jnp.where(lane_i == LANES - 1, grp, comb)
    onehot = jnp.where(lane_i.astype(F32) == grp, 1.0, 0.0)
    cnt_ref[0] = jnp.broadcast_to(jnp.sum(onehot, 0, keepdims=True), (8, LANES))


def _route_call(h1, w_route, b_route, tm):
    n_tok = h1.shape[0]
    return pl.pallas_call(
        _route_kernel,
        grid=(n_tok // tm,),
        in_specs=[pl.BlockSpec((tm, D_MODEL), lambda i: (i, 0)),
                  pl.BlockSpec((D_MODEL, LANES), lambda i: (0, 0)),
                  pl.BlockSpec((1, LANES), lambda i: (0, 0))],
        out_specs=[pl.BlockSpec((tm, LANES), lambda i: (i, 0)),
                   pl.BlockSpec((1, 8, LANES), lambda i: (i, 0, 0))],
        out_shape=[jax.ShapeDtypeStruct((n_tok, LANES), F32),
                   jax.ShapeDtypeStruct((n_tok // tm, 8, LANES), F32)],
        compiler_params=_params("parallel"),
        name="route",
    )(h1, w_route, b_route)


MOE_ALIGN = 16
MOE_EPS = 4


def _moe_sizes(tm):
    s0 = min(tm, tm // MOE_GROUPS + 64)
    return tuple(sorted({s0, max(s0, tm // 2), tm}))


def _moe_kernel(offs_ref, cls_ref, h1_ref, h1b_ref, comb_ref, wg_ref, wu_ref, wd_ref, g2_ref, b2_ref,
                y_ref, pt_scr, xs_scr, cs_scr, ys_scr, *, tm, nr1, sizes):
    i = pl.program_id(0)
    e = pl.program_id(1)
    nrt = xs_scr.shape[0]

    @pl.when(e == 0)
    def _():
        comb = comb_ref[...]
        lane_i = lax.broadcasted_iota(jnp.int32, (tm, LANES), 1)
        grp = comb[:, LANES - 1:LANES]
        is_g = lane_i.astype(F32) == grp
        ri = lax.broadcasted_iota(jnp.int32, (tm, tm), 0)
        ci = lax.broadcasted_iota(jnp.int32, (tm, tm), 1)
        tri = jnp.where(ri > ci, 1.0, 0.0).astype(BF16)
        prefix = jnp.dot(tri, jnp.where(is_g, 1.0, 0.0).astype(BF16), preferred_element_type=F32)
        rank = jnp.sum(jnp.where(is_g, prefix, 0.0), -1, keepdims=True)
        offv = jnp.zeros((tm, 1), F32)
        for g in range(MOE_GROUPS):
            offv = jnp.where(grp == g, offs_ref[i * MOE_GROUPS + g].astype(F32), offv)
        pos = offv + rank
        col = lax.broadcasted_iota(jnp.int32, (tm, nr1), 1).astype(F32)
        pt = jnp.where(col == pos, 1.0, 0.0).astype(BF16)
        pt_scr[...] = pt
        combw = jnp.where(lane_i < MOE_NE, comb, 0.0)
        c_hi = combw.astype(BF16)
        c_lo = (combw - c_hi.astype(F32)).astype(BF16)
        xs_scr[0:nr1, :] = _dot_tn(pt, h1b_ref[...]).astype(BF16)
        cs_scr[0:nr1, :] = _dot_tn(pt, c_hi) + _dot_tn(pt, c_lo)
        xs_scr[nr1:nrt, :] = jnp.zeros((nrt - nr1, D_MODEL), BF16)
        cs_scr[nr1:nrt, :] = jnp.zeros((nrt - nr1, LANES), F32)
        ys_scr[...] = jnp.zeros_like(ys_scr)

    g = e // (MOE_EXPERTS // MOE_EPS)
    off = pl.multiple_of(offs_ref[i * MOE_GROUPS + g], MOE_ALIGN)
    cls = cls_ref[i * MOE_GROUPS + g]
    for ci, m in enumerate(sizes):
        @pl.when(cls == ci)
        def _(m=m):
            rows = pl.ds(off, m)
            xb = xs_scr[rows, :]
            cb = cs_scr[rows, :]
            lane_i = lax.broadcasted_iota(jnp.int32, (m, LANES), 1)
            sub = range(MOE_EPS)
            colw = [jnp.sum(jnp.where(lane_i == e * MOE_EPS + j, cb, 0.0), -1, keepdims=True) for j in sub]
            gate = [jnp.dot(xb, wg_ref[j], preferred_element_type=F32) for j in sub]
            up = [jnp.dot(xb, wu_ref[j], preferred_element_type=F32) for j in sub]
            hg = [(_silu(gate[j]) * up[j] * colw[j]).astype(BF16) for j in sub]
            wd = wd_ref[...].reshape(MOE_EPS * MOE_DFF, D_MODEL)
            ys_scr[rows, :] += jnp.dot(jnp.concatenate(hg, axis=1), wd, preferred_element_type=F32)

    @pl.when(e == MOE_NE // MOE_EPS - 1)
    def _():
        moe = jnp.dot(pt_scr[...], ys_scr[0:nr1, :].astype(BF16), preferred_element_type=F32)
        y_ref[...] = _layernorm(ALPHA * h1_ref[...] + moe, g2_ref[...], b2_ref[...])


def _moe(h1, h1b, comb, cnt, w_gate, w_up, w_down, ln_g, ln_b, tm):
    n_tok = h1.shape[0]
    sizes = _moe_sizes(tm)
    nr1 = -(-(tm + MOE_GROUPS * MOE_ALIGN) // LANES) * LANES
    nrt = nr1 + tm
    c = cnt[:, 0, :MOE_GROUPS].astype(jnp.int32)
    seg = (c + MOE_ALIGN - 1) // MOE_ALIGN * MOE_ALIGN
    offs = (jnp.cumsum(seg, axis=1) - seg).reshape(-1)
    cls = sum((c > s).astype(jnp.int32) for s in sizes[:-1]).reshape(-1)
    grid_spec = pltpu.PrefetchScalarGridSpec(
        num_scalar_prefetch=2,
        grid=(n_tok // tm, MOE_NE // MOE_EPS),
        in_specs=[
            pl.BlockSpec((tm, D_MODEL), lambda i, e, o, k: (i, 0), pipeline_mode=pl.Buffered(1)),
            pl.BlockSpec((tm, D_MODEL), lambda i, e, o, k: (i, 0)),
            pl.BlockSpec((tm, LANES), lambda i, e, o, k: (i, 0)),
            pl.BlockSpec((MOE_EPS, D_MODEL, MOE_DFF), lambda i, e, o, k: (e, 0, 0)),
            pl.BlockSpec((MOE_EPS, D_MODEL, MOE_DFF), lambda i, e, o, k: (e, 0, 0)),
            pl.BlockSpec((MOE_EPS, MOE_DFF, D_MODEL), lambda i, e, o, k: (e, 0, 0)),
            pl.BlockSpec((1, D_MODEL), lambda i, e, o, k: (0, 0)),
            pl.BlockSpec((1, D_MODEL), lambda i, e, o, k: (0, 0)),
        ],
        out_specs=pl.BlockSpec((tm, D_MODEL), lambda i, e, o, k: (i, 0)),
        scratch_shapes=[pltpu.VMEM((tm, nr1), BF16), pltpu.VMEM((nrt, D_MODEL), BF16),
                        pltpu.VMEM((nrt, LANES), F32), pltpu.VMEM((nrt, D_MODEL), F32)],
    )
    return pl.pallas_call(
        functools.partial(_moe_kernel, tm=tm, nr1=nr1, sizes=sizes),
        grid_spec=grid_spec,
        out_shape=jax.ShapeDtypeStruct((n_tok, D_MODEL), F32),
        compiler_params=_params("parallel", "arbitrary"),
        name="moe",
    )(offs, cls, h1, h1b, comb, w_gate, w_up, w_down, ln_g, ln_b)


def _rope_tables(start, t):
    half = RET_DK // 2
    inv = 1.0 / (ROPE_BASE ** (np.arange(half, dtype=np.float64) / half))
    ang = (start + np.arange(t, dtype=np.float64))[:, None] * inv[None, :]
    cos = np.cos(ang)
    sin = np.sin(ang)
    return (jnp.asarray(np.concatenate([cos, cos], -1), F32),
            jnp.asarray(np.concatenate([-sin, sin], -1), F32))


def _group(x, pos0, mem_k, mem_v, ret_s0, dn_s0, conv_buf, wts, *, tm, tm_mix, tm_moe, ret_chunk, dn_blk,
           dn_chunk):
    nb, t, d = x.shape
    n_tok = nb * t
    h, hb, small, small_t, proj = _in_proj(x.reshape(n_tok, d), wts["ln_in_g"], wts["ln_in_b"],
                                           wts["w_in"], wts["w_small"], wts["w_small_t"], tm)
    proj3 = proj.reshape(nb, t, PROJ_W)
    cos2, sin2 = _rope_tables(pos0, t)
    ret_o, ret_s = _retention(proj3, cos2, sin2, ret_s0, wts["ret_gn_g"], wts["ret_gn_b"], ret_chunk)
    small_t3 = small_t.reshape(16, nb, t).transpose(1, 0, 2)
    conv_init = jnp.concatenate([jnp.zeros((nb, 8 - (DN_CONV_W - 1), DN_CONV_CH), F32), conv_buf], axis=1)
    dn_o, dn_s, conv_new = _deltanet(proj3, small, small_t3, wts["dn_conv"], conv_init, wts["dn_prow"],
                                     wts["dn_pcol"], wts["dn_norm_g"], dn_s0, dn_blk, dn_chunk)
    h1, h1b = _mix(h, hb, ret_o.reshape(n_tok, d), dn_o.reshape(n_tok, d), mem_k, mem_v,
                   wts["w_xq"], wts["w_gate"], wts["w_branch"], wts["w_out"], wts["ln1_g"], wts["ln1_b"],
                   tm_mix, t)
    comb, cnt = _route_call(h1, wts["w_route"], wts["b_route"], tm_moe)
    y = _moe(h1, h1b, comb, cnt, wts["moe_w_gate"], wts["moe_w_up"], wts["moe_w_down"],
             wts["ln2_g"], wts["ln2_b"], tm_moe)
    return y.reshape(nb, t, d), ret_s, dn_s, conv_new


def kernel(x_prompt, x_sample, mem_prompt, state_ret, state_dn, state_dn_conv, cache_mem_k, cache_mem_v,
           ln_in_g, ln_in_b, w_in, ret_gn_g, ret_gn_b, dn_conv, dn_A_log, dn_dt_bias, dn_norm_g,
           w_mem_kv, w_branch, w_out, ln1_g, ln1_b, moe_w_coarse, moe_b_coarse, moe_w_fine, moe_b_fine,
           moe_w_gate, moe_w_up, moe_w_down, ln2_g, ln2_b):
    bp, tp, d = x_prompt.shape
    bs, ts, _ = x_sample.shape
    l = 0
    wi = w_in[l].astype(BF16)
    dba = wi[:, PROJ_W:PROJ_W + 16]
    xq, gates = wi[:, 6160:7184], wi[:, 7184:10256]
    wts = {
        "ln_in_g": ln_in_g.reshape(1, d), "ln_in_b": ln_in_b.reshape(1, d),
        "w_in": wi,
        "w_small": jnp.pad(dba, ((0, 0), (0, LANES - 16))),
        "w_small_t": dba.T,
        "ret_gn_g": ret_gn_g[l].reshape(1, -1), "ret_gn_b": ret_gn_b[l].reshape(1, -1),
        "dn_conv": dn_conv[l],
        "dn_prow": jnp.zeros((8, LANES), F32).at[0, 8:16].set(-jnp.exp(dn_A_log[l])).at[1, 8:16].set(dn_dt_bias[l]),
        "dn_pcol": jnp.zeros((16, LANES), F32).at[8:16, 0].set(-jnp.exp(dn_A_log[l])).at[8:16, 1].set(dn_dt_bias[l]),
        "dn_norm_g": dn_norm_g[l].reshape(1, -1),
        "w_xq": xq, "w_gate": gates,
        "w_branch": w_branch[l].astype(BF16), "w_out": w_out[l].astype(BF16),
        "ln1_g": ln1_g[l].reshape(1, d), "ln1_b": ln1_b[l].reshape(1, d),
        "w_route": jnp.pad(jnp.concatenate([moe_w_fine[l], moe_w_coarse[l]], axis=1),
                           ((0, 0), (0, LANES - MOE_NE - MOE_GROUPS))),
        "b_route": jnp.pad(jnp.concatenate([moe_b_fine[l], moe_b_coarse[l]]),
                           (0, LANES - MOE_NE - MOE_GROUPS)).reshape(1, LANES),
        "moe_w_gate": moe_w_gate[l].reshape(MOE_NE, d, MOE_DFF).astype(BF16),
        "moe_w_up": moe_w_up[l].reshape(MOE_NE, d, MOE_DFF).astype(BF16),
        "moe_w_down": moe_w_down[l].reshape(MOE_NE, MOE_DFF, d).astype(BF16),
        "ln2_g": ln2_g[l].reshape(1, d), "ln2_b": ln2_b[l].reshape(1, d),
    }

    mkv = _memkv(mem_prompt.reshape(bp * N_MEM, d), w_mem_kv[l].astype(BF16))
    mk = mkv[:, :XA_HEADS * XA_DH].reshape(1, bp, N_MEM, XA_HEADS, XA_DH)
    mv = mkv[:, XA_HEADS * XA_DH:].reshape(1, bp, N_MEM, XA_HEADS, XA_DH)
    yp, rs_p, ds_p, cb_p = _group(
        x_prompt, 0, mk, mv,
        jnp.zeros((bp, RET_HEADS, RET_DK, RET_DV), F32),
        jnp.zeros((bp, DN_V_HEADS, DN_DK, DN_DV), F32),
        jnp.zeros((bp, DN_CONV_W - 1, DN_CONV_CH), F32),
        wts, tm=min(1024, tp), tm_mix=min(512, tp), tm_moe=min(1024, tp), ret_chunk=min(256, tp),
        dn_blk=min(128, tp),
        dn_chunk=min(DN_CHUNK, tp))
    ys, rs_s, ds_s, cb_s = _group(
        x_sample, PAST_LEN, cache_mem_k[l:l + 1], cache_mem_v[l:l + 1],
        state_ret[l], state_dn[l], state_dn_conv[l],
        wts, tm=bs * ts, tm_mix=bs * ts, tm_moe=bs * ts, ret_chunk=ts, dn_blk=ts, dn_chunk=min(DN_CHUNK, ts))
    return (yp, ys, rs_p[None], rs_s[None], ds_p[None], ds_s[None], cb_p[None], cb_s[None], mk, mv)
```

```python
import functools
import math

import numpy as np

import jax
import jax.numpy as jnp
from jax import lax
from jax.experimental import pallas as pl
from jax.experimental.pallas import tpu as pltpu

F32 = jnp.float32
BF16 = jnp.bfloat16

D_MODEL = 1024
PAST_LEN = 1024
RET_HEADS = 4
RET_DK = 128
RET_DV = 256
ROPE_BASE = 10000.0
DN_QK_HEADS = 4
DN_V_HEADS = 8
DN_DK = 128
DN_DV = 128
DN_QK = DN_QK_HEADS * DN_DK
DN_CONV_W = 4
DN_CONV_CH = 2048
DN_CHUNK = 64
DN_SUB = 16
DN_HEAD_GROUP = 8
DN_STREAMS = 2
XA_HEADS = 4
XA_DH = 256
N_MEM = 256
MOE_GROUPS = 4
MOE_EXPERTS = 8
MOE_NE = MOE_GROUPS * MOE_EXPERTS
MOE_DFF = 256
DEPTH = 1
ALPHA = (2.0 * DEPTH) ** 0.25
LANES = 128
PROJ_W = 6144
VMEM_LIMIT = 56 * 1024 * 1024


def _dot(a, b):
    return jnp.dot(a.astype(BF16), b.astype(BF16), preferred_element_type=F32)


def _dot_nt(a, b):
    return lax.dot_general(a.astype(BF16), b.astype(BF16), (((1,), (1,)), ((), ())),
                           preferred_element_type=F32)


def _dot_tn(a, b):
    return lax.dot_general(a.astype(BF16), b.astype(BF16), (((0,), (0,)), ((), ())),
                           preferred_element_type=F32)


def _dot_exact(a, b):
    return jnp.dot(a, b, preferred_element_type=F32, precision=lax.Precision.HIGHEST)


def _dot_split(a, b):
    ah = a.astype(BF16)
    al = (a - ah.astype(F32)).astype(BF16)
    bh = b.astype(BF16)
    bl = (b - bh.astype(F32)).astype(BF16)
    hi = jnp.dot(ah, bh, preferred_element_type=F32)
    return hi + (jnp.dot(ah, bl, preferred_element_type=F32) + jnp.dot(al, bh, preferred_element_type=F32))


def _layernorm(x, g, b, eps=1e-5):
    mu = jnp.mean(x, -1, keepdims=True)
    xc = x - mu
    var = jnp.mean(xc * xc, -1, keepdims=True)
    return xc * lax.rsqrt(var + eps) * g + b


def _silu(x):
    return x * (1.0 / (1.0 + jnp.exp(-x)))


def _sigmoid(x):
    return 1.0 / (1.0 + jnp.exp(-x))


def _softplus(x):
    return jnp.maximum(x, 0.0) + jnp.log(1.0 + jnp.exp(-jnp.abs(x)))


def _idiv(x, n):
    return jnp.right_shift(x, int(math.log2(n)))


def _params(*sem):
    return pltpu.CompilerParams(dimension_semantics=sem, vmem_limit_bytes=VMEM_LIMIT)


def _in_proj_kernel(x_ref, g_ref, b_ref, w_ref, ws_ref, wst_ref,
                    h_ref, hb_ref, small_ref, smallt_ref, proj_ref, hs_ref):
    @pl.when(pl.program_id(1) == 0)
    def _():
        h = _layernorm(x_ref[...], g_ref[...], b_ref[...])
        h_ref[...] = h
        hb = h.astype(BF16)
        hb_ref[...] = hb
        hs_ref[...] = hb
        small_ref[...] = jnp.dot(hb, ws_ref[...], preferred_element_type=F32)
        smallt_ref[...] = lax.dot_general(wst_ref[...], hb, (((1,), (1,)), ((), ())),
                                          preferred_element_type=F32)

    proj_ref[...] = jnp.dot(hs_ref[...], w_ref[...], preferred_element_type=F32).astype(BF16)


def _in_proj(x, ln_g, ln_b, w_main, w_small, w_small_t, tm, tn=2048):
    n_tok = x.shape[0]
    grid = (n_tok // tm, PROJ_W // tn)
    return pl.pallas_call(
        _in_proj_kernel,
        grid=grid,
        in_specs=[
            pl.BlockSpec((tm, D_MODEL), lambda i, n: (i, 0)),
            pl.BlockSpec((1, D_MODEL), lambda i, n: (0, 0)),
            pl.BlockSpec((1, D_MODEL), lambda i, n: (0, 0)),
            pl.BlockSpec((D_MODEL, tn), lambda i, n: (0, n)),
            pl.BlockSpec((D_MODEL, LANES), lambda i, n: (0, 0)),
            pl.BlockSpec((16, D_MODEL), lambda i, n: (0, 0)),
        ],
        out_specs=[
            pl.BlockSpec((tm, D_MODEL), lambda i, n: (i, 0)),
            pl.BlockSpec((tm, D_MODEL), lambda i, n: (i, 0)),
            pl.BlockSpec((tm, LANES), lambda i, n: (i, 0)),
            pl.BlockSpec((16, tm), lambda i, n: (0, i)),
            pl.BlockSpec((tm, tn), lambda i, n: (i, n)),
        ],
        out_shape=[
            jax.ShapeDtypeStruct((n_tok, D_MODEL), F32),
            jax.ShapeDtypeStruct((n_tok, D_MODEL), BF16),
            jax.ShapeDtypeStruct((n_tok, LANES), F32),
            jax.ShapeDtypeStruct((16, n_tok), F32),
            jax.ShapeDtypeStruct((n_tok, PROJ_W), BF16),
        ],
        scratch_shapes=[pltpu.VMEM((tm, D_MODEL), BF16)],
        compiler_params=_params("parallel", "arbitrary"),
        name="in_proj",
    )(x, ln_g, ln_b, w_main, w_small, w_small_t)


def _ret_kernel(q_ref, k_ref, v_ref, rg_ref, cos_ref, sin_ref, s0_ref, gng_ref, gnb_ref,
                o_ref, sout_ref, s_scr, *, chunk):
    @pl.when(pl.program_id(1) == 0)
    def _():
        s_scr[...] = s0_ref[0]

    heads = range(RET_HEADS)
    lg = [math.log(1.0 - 2.0 ** (-5.0 - h)) for h in heads]
    cos = cos_ref[...]
    sin = sin_ref[...]
    ri = lax.broadcasted_iota(jnp.int32, (chunk, chunk), 0)
    ci = lax.broadcasted_iota(jnp.int32, (chunk, chunk), 1)
    causal = ri >= ci
    diff = jnp.where(causal, (ri - ci).astype(F32), 0.0)
    idx = lax.broadcasted_iota(jnp.int32, (chunk, 1), 0).astype(F32)
    q = [q_ref[0, :, h * RET_DK:(h + 1) * RET_DK].astype(F32) for h in heads]
    k = [k_ref[0, :, h * RET_DK:(h + 1) * RET_DK].astype(F32) for h in heads]
    v = [v_ref[0, :, h * RET_DV:(h + 1) * RET_DV] for h in heads]
    qr = [q[h] * cos + pltpu.roll(q[h], RET_DK // 2, 1) * sin for h in heads]
    kr = [(k[h] * cos + pltpu.roll(k[h], RET_DK // 2, 1) * sin) * (RET_DK ** -0.5) for h in heads]
    scores = [_dot_nt(qr[h], kr[h]) * jnp.where(causal, jnp.exp(diff * lg[h]), 0.0) for h in heads]
    s = [s_scr[h] for h in heads]
    cross = [_dot(qr[h], s[h]) * jnp.exp((idx + 1.0) * lg[h]) for h in heads]
    o = [_dot(scores[h], v[h]) + cross[h] for h in heads]
    kv = [_dot_tn(kr[h] * jnp.exp((chunk - 1.0 - idx) * lg[h]), v[h]) for h in heads]
    for h in heads:
        s_new = s[h] * math.exp(chunk * lg[h]) + kv[h]
        s_scr[h] = s_new
        sout_ref[0, h] = s_new
        sl = slice(h * RET_DV, (h + 1) * RET_DV)
        gated = _layernorm(o[h], gng_ref[:, sl], gnb_ref[:, sl]) * _silu(rg_ref[0, :, sl].astype(F32))
        o_ref[0, :, sl] = gated.astype(BF16)


def _retention(proj, cos2, sin2, s0, gn_g, gn_b, chunk):
    nb, t = proj.shape[:2]
    grid = (nb, t // chunk)
    qk_w = RET_HEADS * RET_DK
    v_w = RET_HEADS * RET_DV
    return pl.pallas_call(
        functools.partial(_ret_kernel, chunk=chunk),
        grid=grid,
        in_specs=[
            pl.BlockSpec((1, chunk, qk_w), lambda b, c: (b, c, 0)),
            pl.BlockSpec((1, chunk, qk_w), lambda b, c: (b, c, 512 // qk_w)),
            pl.BlockSpec((1, chunk, v_w), lambda b, c: (b, c, 1024 // v_w)),
            pl.BlockSpec((1, chunk, v_w), lambda b, c: (b, c, 2048 // v_w)),
            pl.BlockSpec((chunk, RET_DK), lambda b, c: (c, 0)),
            pl.BlockSpec((chunk, RET_DK), lambda b, c: (c, 0)),
            pl.BlockSpec((1, RET_HEADS, RET_DK, RET_DV), lambda b, c: (b, 0, 0, 0)),
            pl.BlockSpec((1, v_w), lambda b, c: (0, 0)),
            pl.BlockSpec((1, v_w), lambda b, c: (0, 0)),
        ],
        out_specs=[
            pl.BlockSpec((1, chunk, v_w), lambda b, c: (b, c, 0)),
            pl.BlockSpec((1, RET_HEADS, RET_DK, RET_DV), lambda b, c: (b, 0, 0, 0)),
        ],
        out_shape=[
            jax.ShapeDtypeStruct((nb, t, v_w), BF16),
            jax.ShapeDtypeStruct((nb, RET_HEADS, RET_DK, RET_DV), F32),
        ],
        scratch_shapes=[pltpu.VMEM((RET_HEADS, RET_DK, RET_DV), F32)],
        compiler_params=_params("parallel", "arbitrary"),
        name="retention",
    )(proj, proj, proj, proj, cos2, sin2, s0, gn_g, gn_b)


def _dn_kernel(xqk_ref, xv_ref, z_ref, sm_ref, smt_ref, cw_ref, cinit_ref, prow_ref, pcol_ref, ng_ref, s0_ref,
               o_ref, sout_ref, cout_ref, xext, s_scr, *, blk_len, chunk):
    L = blk_len
    nbb = xqk_ref.shape[0]
    blk = pl.program_id(1)

    @pl.when(blk == 0)
    def _():
        xext[:, 0:8, :] = cinit_ref[...]
        s_scr[...] = s0_ref[...]

    ri = lax.broadcasted_iota(jnp.int32, (L, L), 0)
    ci = lax.broadcasted_iota(jnp.int32, (L, L), 1)
    same_chunk = _idiv(ri, chunk) == _idiv(ci, chunk)
    lower = jnp.logical_and(ri >= ci, same_chunk)
    strict = jnp.logical_and(ri > ci, same_chunk)
    same_sub = _idiv(ri, DN_SUB) == _idiv(ci, DN_SUB)
    tri = jnp.where(lower, 1.0, 0.0).astype(F32)
    tri_t = jnp.where(jnp.logical_and(ri <= ci, same_chunk), 1.0, 0.0).astype(F32)
    prow = prow_ref[...]
    pcol = pcol_ref[...]

    conv, beta_tm, gcum_tm, gcum_hm, egc_tm = [], [], [], [], []
    for b in range(nbb):
        x = jnp.concatenate([xqk_ref[b], xv_ref[b]], axis=1).astype(F32)
        xext[b, 8:8 + L, :] = x
        cv = x * cw_ref[3:4, :]
        for w in range(DN_CONV_W - 1):
            cv = cv + xext[b, 5 + w:5 + w + L, :] * cw_ref[w:w + 1, :]
        conv.append(_silu(cv))
        xext[b, 0:8, :] = x[L - 8:L, :]
        cout_ref[b] = x[L - (DN_CONV_W - 1):L, :]
        sm = sm_ref[b]
        beta_tm.append(_sigmoid(sm))
        g_tm = prow[0:1, :] * _softplus(sm + prow[1:2, :])
        g_hm = pcol[8:16, 0:1] * _softplus(smt_ref[b][8:16, :] + pcol[8:16, 1:2])
        gcum_tm.append(_dot_exact(tri, g_tm))
        gcum_hm.append(_dot_exact(g_hm, tri_t))
        egc_tm.append(jnp.exp(gcum_tm[b]))

    n_sub = L // chunk
    rep = DN_V_HEADS // DN_QK_HEADS
    bdot = lambda x, y: jnp.dot(x, y, preferred_element_type=F32)
    for h0 in range(0, DN_V_HEADS, DN_HEAD_GROUP):
        heads = [(b, hh) for b in range(nbb) for hh in range(h0, h0 + DN_HEAD_GROUP)]
        q, k, kk, qk = {}, {}, {}, {}
        for b in range(nbb):
            for j in range(h0 // rep, (h0 + DN_HEAD_GROUP) // rep):
                qj = conv[b][:, j * DN_DK:(j + 1) * DN_DK]
                kj = conv[b][:, DN_QK + j * DN_DK:DN_QK + (j + 1) * DN_DK]
                qj = qj * lax.rsqrt(jnp.sum(qj * qj, -1, keepdims=True) + 1e-6) * (DN_DK ** -0.5)
                kj = kj * lax.rsqrt(jnp.sum(kj * kj, -1, keepdims=True) + 1e-6)
                kq = _dot_nt(jnp.concatenate([kj, qj], axis=0), kj)
                q[b, j], k[b, j], kk[b, j], qk[b, j] = qj, kj, kq[0:L], kq[L:2 * L]
        gc = {(b, hh): gcum_tm[b][:, 8 + hh:9 + hh] for b, hh in heads}
        beta = {(b, hh): beta_tm[b][:, hh:hh + 1] for b, hh in heads}
        eg = {(b, hh): egc_tm[b][:, 8 + hh:9 + hh] for b, hh in heads}
        rel = {(b, hh): jnp.where(lower, jnp.exp(jnp.where(lower, gc[b, hh] - gcum_hm[b][hh:hh + 1, :], 0.0)),
                                  0.0) for b, hh in heads}
        a = {(b, hh): jnp.where(strict, beta[b, hh] * kk[b, hh // rep] * rel[b, hh], 0.0) for b, hh in heads}
        attn = {(b, hh): qk[b, hh // rep] * rel[b, hh] for b, hh in heads}
        pd = {hh: jnp.where(same_sub, -a[hh], 0.0) for hh in heads}
        e = {hh: a[hh] + pd[hh] for hh in heads}
        eb = {hh: e[hh].astype(BF16) for hh in heads}
        xacc = pd
        xaccb = {hh: pd[hh].astype(BF16) for hh in heads}
        pwb = xaccb
        for _ in range(3):
            pw = {hh: bdot(pwb[hh], pwb[hh]) for hh in heads}
            pwb = {hh: pw[hh].astype(BF16) for hh in heads}
            xacc = {hh: xacc[hh] + pw[hh] + bdot(xaccb[hh], pwb[hh]) for hh in heads}
            xaccb = {hh: xacc[hh].astype(BF16) for hh in heads}
        f = {hh: e[hh] + bdot(xaccb[hh], eb[hh]) for hh in heads}
        fb = {hh: f[hh].astype(BF16) for hh in heads}
        f2 = {hh: bdot(fb[hh], fb[hh]) for hh in heads}
        y = {hh: f2[hh] - f[hh] - bdot(fb[hh], f2[hh].astype(BF16)) for hh in heads}
        rinv = {hh: xacc[hh] + y[hh] + bdot(y[hh].astype(BF16), xaccb[hh]) for hh in heads}
        sol = {}
        for b, hh in heads:
            vh = conv[b][:, 2 * DN_QK + hh * DN_DV:2 * DN_QK + (hh + 1) * DN_DV]
            rhs = jnp.concatenate([vh * beta[b, hh], k[b, hh // rep] * (beta[b, hh] * eg[b, hh])], axis=1)
            sol[b, hh] = rhs + _dot(rinv[b, hh], rhs)
        u = {p: sol[p][:, 0:DN_DV] for p in heads}
        wmat = {p: sol[p][:, DN_DV:DN_DV + DN_DK] for p in heads}
        qe = {(b, hh): q[b, hh // rep] * eg[b, hh] for b, hh in heads}
        s = {(b, hh): s_scr[b, hh] for b, hh in heads}
        v_new = {p: [] for p in heads}
        q_s = {p: [] for p in heads}
        for sc in range(n_sub):
            r0 = sc * chunk
            for b, hh in heads:
                p = (b, hh)
                ws = _dot(jnp.concatenate([wmat[p][r0:r0 + chunk], qe[p][r0:r0 + chunk]], axis=0), s[p])
                vn = u[p][r0:r0 + chunk] - ws[0:chunk]
                v_new[p].append(vn)
                q_s[p].append(ws[chunk:2 * chunk])
                g_last = gcum_tm[b][r0 + chunk - 1:r0 + chunk, 8 + hh:9 + hh]
                kd = k[b, hh // rep][r0:r0 + chunk] * jnp.exp(g_last - gc[p][r0:r0 + chunk])
                s[p] = s[p] * jnp.exp(g_last) + _dot_tn(kd, vn)
        for b, hh in heads:
            p = (b, hh)
            s_scr[b, hh] = s[p]
            sout_ref[b, hh] = s[p]
            vn = v_new[p][0] if n_sub == 1 else jnp.concatenate(v_new[p], axis=0)
            qs = q_s[p][0] if n_sub == 1 else jnp.concatenate(q_s[p], axis=0)
            o = qs + _dot(attn[p], vn)
            zh = z_ref[b, :, hh * DN_DV:(hh + 1) * DN_DV].astype(F32)
            n = o * lax.rsqrt(jnp.mean(o * o, -1, keepdims=True) + 1e-6) * ng_ref[...]
            o_ref[b, :, hh * DN_DV:(hh + 1) * DN_DV] = (n * _silu(zh)).astype(BF16)


def _deltanet(proj, small, small_t, conv_w, conv_init, prow, pcol, norm_g, s0, blk_len, chunk):
    nb, t = proj.shape[:2]
    nblk = t // blk_len
    nbb = DN_STREAMS
    grid = (nb // nbb, nblk)
    return pl.pallas_call(
        functools.partial(_dn_kernel, blk_len=blk_len, chunk=chunk),
        grid=grid,
        in_specs=[
            pl.BlockSpec((nbb, blk_len, 1024), lambda b, i: (b, i, 3072 // 1024)),
            pl.BlockSpec((nbb, blk_len, 1024), lambda b, i: (b, i, 4096 // 1024)),
            pl.BlockSpec((nbb, blk_len, 1024), lambda b, i: (b, i, 5120 // 1024)),
            pl.BlockSpec((nbb, blk_len, LANES), lambda b, i: (b, i, 0)),
            pl.BlockSpec((nbb, 16, blk_len), lambda b, i: (b, 0, i)),
            pl.BlockSpec((DN_CONV_W, DN_CONV_CH), lambda b, i: (0, 0)),
            pl.BlockSpec((nbb, 8, DN_CONV_CH), lambda b, i: (b, 0, 0)),
            pl.BlockSpec((8, LANES), lambda b, i: (0, 0)),
            pl.BlockSpec((16, LANES), lambda b, i: (0, 0)),
            pl.BlockSpec((1, DN_DV), lambda b, i: (0, 0)),
            pl.BlockSpec((nbb, DN_V_HEADS, DN_DK, DN_DV), lambda b, i: (b, 0, 0, 0)),
        ],
        out_specs=[
            pl.BlockSpec((nbb, blk_len, 1024), lambda b, i: (b, i, 0)),
            pl.BlockSpec((nbb, DN_V_HEADS, DN_DK, DN_DV), lambda b, i: (b, 0, 0, 0)),
            pl.BlockSpec((nbb, DN_CONV_W - 1, DN_CONV_CH), lambda b, i: (b, 0, 0)),
        ],
        out_shape=[
            jax.ShapeDtypeStruct((nb, t, 1024), BF16),
            jax.ShapeDtypeStruct((nb, DN_V_HEADS, DN_DK, DN_DV), F32),
            jax.ShapeDtypeStruct((nb, DN_CONV_W - 1, DN_CONV_CH), F32),
        ],
        scratch_shapes=[pltpu.VMEM((nbb, blk_len + 8, DN_CONV_CH), F32),
                        pltpu.VMEM((nbb, DN_V_HEADS, DN_DK, DN_DV), F32)],
        compiler_params=_params("parallel", "arbitrary"),
        name="deltanet",
    )(proj, proj, proj, small, small_t, conv_w, conv_init, prow, pcol, norm_g, s0)


def _memkv_kernel(m_ref, w_ref, o_ref):
    o_ref[...] = _dot(m_ref[...], w_ref[...])


def _memkv(mem, w_kv):
    n = mem.shape[0]
    tn = 1024
    return pl.pallas_call(
        _memkv_kernel,
        grid=(w_kv.shape[1] // tn,),
        in_specs=[pl.BlockSpec((n, D_MODEL), lambda j: (0, 0)),
                  pl.BlockSpec((D_MODEL, tn), lambda j: (0, j))],
        out_specs=pl.BlockSpec((n, tn), lambda j: (0, j)),
        out_shape=jax.ShapeDtypeStruct((n, w_kv.shape[1]), F32),
        compiler_params=_params("parallel"),
        name="memkv",
    )(mem, w_kv)


def _mix_kernel(h_ref, hb_ref, ret_ref, dn_ref, mk_ref, mv_ref, wxq_ref, wg_ref, wb_ref, wo_ref,
                g1_ref, b1_ref, h1_ref, h1b_ref, mem_scr, *, n_batch, tiles_per_batch):
    @pl.when(pl.program_id(0) % tiles_per_batch == 0)
    def _():
        for b in range(n_batch):
            for hh in range(XA_HEADS):
                mem_scr[0, hh, b] = mk_ref[0, b, :, hh, :].astype(BF16)
                mem_scr[1, hh, b] = mv_ref[0, b, :, hh, :].astype(BF16)

    hb = hb_ref[...]
    xq = jnp.dot(hb, wxq_ref[...], preferred_element_type=F32)
    tb = xq.shape[0] // n_batch
    pairs = [(b, hh) for b in range(n_batch) for hh in range(XA_HEADS)]
    s = [_dot_nt(xq[b * tb:(b + 1) * tb, hh * XA_DH:(hh + 1) * XA_DH], mem_scr[0, hh, b]) * (XA_DH ** -0.5)
         for b, hh in pairs]
    p = [jnp.exp(si - jnp.max(si, -1, keepdims=True)) for si in s]
    p = [pi / jnp.sum(pi, -1, keepdims=True) for pi in p]
    xo = [_dot(pi, mem_scr[1, hh, b]) for pi, (b, hh) in zip(p, pairs)]
    xo = [jnp.concatenate(xo[b * XA_HEADS:(b + 1) * XA_HEADS], axis=1) for b in range(n_batch)]
    xo = xo[0] if n_batch == 1 else jnp.concatenate(xo, axis=0)
    mixed = None
    for n, br in enumerate((ret_ref[...], dn_ref[...], xo)):
        gate = _sigmoid(jnp.dot(hb, wg_ref[:, n * D_MODEL:(n + 1) * D_MODEL], preferred_element_type=F32))
        term = gate * _dot(br, wb_ref[n])
        mixed = term if mixed is None else mixed + term
    y = ALPHA * h_ref[...] + _dot(mixed, wo_ref[...])
    h1 = _layernorm(y, g1_ref[...], b1_ref[...])
    h1_ref[...] = h1
    h1b_ref[...] = h1.astype(BF16)


def _mix(h, hb, ret_o, dn_o, mem_k, mem_v, w_xq, w_gate, w_branch, w_out, ln_g, ln_b, tm, t):
    n_tok = h.shape[0]
    const2 = lambda i: (0, 0)
    row = lambda i: (i, 0)
    single = pl.Buffered(1)
    if tm <= t:
        n_batch = 1
        tiles_per_batch = t // tm
    else:
        n_batch = tm // t
        tiles_per_batch = 1
    mem_spec = pl.BlockSpec((1, n_batch, N_MEM, XA_HEADS, XA_DH), lambda i: (0, i // tiles_per_batch, 0, 0, 0))
    return pl.pallas_call(
        functools.partial(_mix_kernel, n_batch=n_batch, tiles_per_batch=tiles_per_batch),
        grid=(n_tok // tm,),
        in_specs=[
            pl.BlockSpec((tm, D_MODEL), row),
            pl.BlockSpec((tm, D_MODEL), row),
            pl.BlockSpec((tm, D_MODEL), row),
            pl.BlockSpec((tm, D_MODEL), row),
            mem_spec,
            mem_spec,
            pl.BlockSpec((D_MODEL, D_MODEL), const2, pipeline_mode=single),
            pl.BlockSpec((D_MODEL, 3 * D_MODEL), const2, pipeline_mode=single),
            pl.BlockSpec((3, D_MODEL, D_MODEL), lambda i: (0, 0, 0), pipeline_mode=single),
            pl.BlockSpec((D_MODEL, D_MODEL), const2, pipeline_mode=single),
            pl.BlockSpec((1, D_MODEL), const2),
            pl.BlockSpec((1, D_MODEL), const2),
        ],
        out_specs=[pl.BlockSpec((tm, D_MODEL), row), pl.BlockSpec((tm, D_MODEL), row)],
        out_shape=[jax.ShapeDtypeStruct((n_tok, D_MODEL), F32),
                   jax.ShapeDtypeStruct((n_tok, D_MODEL), BF16)],
        scratch_shapes=[pltpu.VMEM((2, XA_HEADS, n_batch, N_MEM, XA_DH), BF16)],
        compiler_params=_params("arbitrary"),
        name="mix",
    )(h, hb, ret_o, dn_o, mem_k, mem_v, w_xq, w_gate, w_branch, w_out, ln_g, ln_b)


def _route(x, wr, br):
    logits = _dot_split(x, wr) + br
    lane_i = lax.broadcasted_iota(jnp.int32, logits.shape, 1)
    lane = lane_i.astype(F32)
    far = jnp.float32(LANES)
    neg = jnp.float32(-3.0e38)
    is_c = jnp.logical_and(lane_i >= MOE_NE, lane_i < MOE_NE + MOE_GROUPS)
    cl = jnp.where(is_c, logits, neg)
    cmax = jnp.max(cl, -1, keepdims=True)
    denom = jnp.sum(jnp.where(is_c, jnp.exp(jnp.where(is_c, logits - cmax, 0.0)), 0.0), -1, keepdims=True)
    p_grp = 1.0 / denom
    grp = jnp.min(jnp.where(jnp.logical_and(is_c, cl == cmax), lane - MOE_NE, far), -1, keepdims=True)
    in_grp = jnp.logical_and(lane_i < MOE_NE, _idiv(lane_i, MOE_EXPERTS).astype(F32) == grp)
    fl = jnp.where(in_grp, logits, neg)
    v1 = jnp.max(fl, -1, keepdims=True)
    i1 = jnp.min(jnp.where(jnp.logical_and(in_grp, fl == v1), lane, far), -1, keepdims=True)
    rest = jnp.logical_and(in_grp, lane != i1)
    fl2 = jnp.where(rest, logits, neg)
    v2 = jnp.max(fl2, -1, keepdims=True)
    i2 = jnp.min(jnp.where(jnp.logical_and(rest, fl2 == v2), lane, far), -1, keepdims=True)
    e2 = jnp.exp(v2 - v1)
    w1 = p_grp / (1.0 + e2)
    w2 = p_grp * e2 / (1.0 + e2)
    return jnp.where(lane == i1, w1, 0.0) + jnp.where(lane == i2, w2, 0.0), grp


def _route_kernel(h1_ref, wr_ref, br_ref, comb_ref, cnt_ref):
    comb, grp = _route(h1_ref[...], wr_ref[...], br_ref[...])
    lane_i = lax.broadcasted_iota(jnp.int32, comb.shape, 1)
    comb_ref[...] = jnp.where(lane_i == LANES - 1, grp, comb)
    onehot = jnp.where(lane_i.astype(F32) == grp, 1.0, 0.0)
    cnt_ref[0] = jnp.broadcast_to(jnp.sum(onehot, 0, keepdims=True), (8, LANES))


def _route_call(h1, w_route, b_route, tm):
    n_tok = h1.shape[0]
    return pl.pallas_call(
        _route_kernel,
        grid=(n_tok // tm,),
        in_specs=[pl.BlockSpec((tm, D_MODEL), lambda i: (i, 0)),
                  pl.BlockSpec((D_MODEL, LANES), lambda i: (0, 0)),
                  pl.BlockSpec((1, LANES), lambda i: (0, 0))],
        out_specs=[pl.BlockSpec((tm, LANES), lambda i: (i, 0)),
                   pl.BlockSpec((1, 8, LANES), lambda i: (i, 0, 0))],
        out_shape=[jax.ShapeDtypeStruct((n_tok, LANES), F32),
                   jax.ShapeDtypeStruct((n_tok // tm, 8, LANES), F32)],
        compiler_params=_params("parallel"),
        name="route",
    )(h1, w_route, b_route)


MOE_ALIGN = 16
MOE_EPS = 4


def _moe_sizes(tm):
    s0 = min(tm, tm // MOE_GROUPS + 64)
    return tuple(sorted({s0, max(s0, tm // 2), tm}))


def _moe_kernel(offs_ref, cls_ref, h1_ref, h1b_ref, comb_ref, wg_ref, wu_ref, wd_ref, g2_ref, b2_ref,
                y_ref, pt_scr, xs_scr, cs_scr, ys_scr, *, tm, nr1, sizes):
    i = pl.program_id(0)
    e = pl.program_id(1)
    nrt = xs_scr.shape[0]

    @pl.when(e == 0)
    def _():
        comb = comb_ref[...]
        lane_i = lax.broadcasted_iota(jnp.int32, (tm, LANES), 1)
        grp = comb[:, LANES - 1:LANES]
        is_g = lane_i.astype(F32) == grp
        ri = lax.broadcasted_iota(jnp.int32, (tm, tm), 0)
        ci = lax.broadcasted_iota(jnp.int32, (tm, tm), 1)
        tri = jnp.where(ri > ci, 1.0, 0.0).astype(BF16)
        prefix = jnp.dot(tri, jnp.where(is_g, 1.0, 0.0).astype(BF16), preferred_element_type=F32)
        rank = jnp.sum(jnp.where(is_g, prefix, 0.0), -1, keepdims=True)
        offv = jnp.zeros((tm, 1), F32)
        for g in range(MOE_GROUPS):
            offv = jnp.where(grp == g, offs_ref[i * MOE_GROUPS + g].astype(F32), offv)
        pos = offv + rank
        col = lax.broadcasted_iota(jnp.int32, (tm, nr1), 1).astype(F32)
        pt = jnp.where(col == pos, 1.0, 0.0).astype(BF16)
        pt_scr[...] = pt
        combw = jnp.where(lane_i < MOE_NE, comb, 0.0)
        c_hi = combw.astype(BF16)
        c_lo = (combw - c_hi.astype(F32)).astype(BF16)
        xs_scr[0:nr1, :] = _dot_tn(pt, h1b_ref[...]).astype(BF16)
        cs_scr[0:nr1, :] = _dot_tn(pt, c_hi) + _dot_tn(pt, c_lo)
        xs_scr[nr1:nrt, :] = jnp.zeros((nrt - nr1, D_MODEL), BF16)
        cs_scr[nr1:nrt, :] = jnp.zeros((nrt - nr1, LANES), F32)
        ys_scr[...] = jnp.zeros_like(ys_scr)

    g = e // (MOE_EXPERTS // MOE_EPS)
    off = pl.multiple_of(offs_ref[i * MOE_GROUPS + g], MOE_ALIGN)
    cls = cls_ref[i * MOE_GROUPS + g]
    for ci, m in enumerate(sizes):
        @pl.when(cls == ci)
        def _(m=m):
            rows = pl.ds(off, m)
            xb = xs_scr[rows, :]
            cb = cs_scr[rows, :]
            lane_i = lax.broadcasted_iota(jnp.int32, (m, LANES), 1)
            sub = range(MOE_EPS)
            colw = [jnp.sum(jnp.where(lane_i == e * MOE_EPS + j, cb, 0.0), -1, keepdims=True) for j in sub]
            gate = [jnp.dot(xb, wg_ref[j], preferred_element_type=F32) for j in sub]
            up = [jnp.dot(xb, wu_ref[j], preferred_element_type=F32) for j in sub]
            hg = [(_silu(gate[j]) * up[j] * colw[j]).astype(BF16) for j in sub]
            wd = wd_ref[...].reshape(MOE_EPS * MOE_DFF, D_MODEL)
            ys_scr[rows, :] += jnp.dot(jnp.concatenate(hg, axis=1), wd, preferred_element_type=F32)

    @pl.when(e == MOE_NE // MOE_EPS - 1)
    def _():
        moe = jnp.dot(pt_scr[...], ys_scr[0:nr1, :].astype(BF16), preferred_element_type=F32)
        y_ref[...] = _layernorm(ALPHA * h1_ref[...] + moe, g2_ref[...], b2_ref[...])


def _moe(h1, h1b, comb, cnt, w_gate, w_up, w_down, ln_g, ln_b, tm):
    n_tok = h1.shape[0]
    sizes = _moe_sizes(tm)
    nr1 = -(-(tm + MOE_GROUPS * MOE_ALIGN) // LANES) * LANES
    nrt = nr1 + tm
    c = cnt[:, 0, :MOE_GROUPS].astype(jnp.int32)
    seg = (c + MOE_ALIGN - 1) // MOE_ALIGN * MOE_ALIGN
    offs = (jnp.cumsum(seg, axis=1) - seg).reshape(-1)
    cls = sum((c > s).astype(jnp.int32) for s in sizes[:-1]).reshape(-1)
    grid_spec = pltpu.PrefetchScalarGridSpec(
        num_scalar_prefetch=2,
        grid=(n_tok // tm, MOE_NE // MOE_EPS),
        in_specs=[
            pl.BlockSpec((tm, D_MODEL), lambda i, e, o, k: (i, 0), pipeline_mode=pl.Buffered(1)),
            pl.BlockSpec((tm, D_MODEL), lambda i, e, o, k: (i, 0)),
            pl.BlockSpec((tm, LANES), lambda i, e, o, k: (i, 0)),
            pl.BlockSpec((MOE_EPS, D_MODEL, MOE_DFF), lambda i, e, o, k: (e, 0, 0)),
            pl.BlockSpec((MOE_EPS, D_MODEL, MOE_DFF), lambda i, e, o, k: (e, 0, 0)),
            pl.BlockSpec((MOE_EPS, MOE_DFF, D_MODEL), lambda i, e, o, k: (e, 0, 0)),
            pl.BlockSpec((1, D_MODEL), lambda i, e, o, k: (0, 0)),
            pl.BlockSpec((1, D_MODEL), lambda i, e, o, k: (0, 0)),
        ],
        out_specs=pl.BlockSpec((tm, D_MODEL), lambda i, e, o, k: (i, 0)),
        scratch_shapes=[pltpu.VMEM((tm, nr1), BF16), pltpu.VMEM((nrt, D_MODEL), BF16),
                        pltpu.VMEM((nrt, LANES), F32), pltpu.VMEM((nrt, D_MODEL), F32)],
    )
    return pl.pallas_call(
        functools.partial(_moe_kernel, tm=tm, nr1=nr1, sizes=sizes),
        grid_spec=grid_spec,
        out_shape=jax.ShapeDtypeStruct((n_tok, D_MODEL), F32),
        compiler_params=_params("parallel", "arbitrary"),
        name="moe",
    )(offs, cls, h1, h1b, comb, w_gate, w_up, w_down, ln_g, ln_b)


def _rope_tables(start, t):
    half = RET_DK // 2
    inv = 1.0 / (ROPE_BASE ** (np.arange(half, dtype=np.float64) / half))
    ang = (start + np.arange(t, dtype=np.float64))[:, None] * inv[None, :]
    cos = np.cos(ang)
    sin = np.sin(ang)
    return (jnp.asarray(np.concatenate([cos, cos], -1), F32),
            jnp.asarray(np.concatenate([-sin, sin], -1), F32))


def _group(x, pos0, mem_k, mem_v, ret_s0, dn_s0, conv_buf, wts, *, tm, tm_mix, tm_moe, ret_chunk, dn_blk,
           dn_chunk):
    nb, t, d = x.shape
    n_tok = nb * t
    h, hb, small, small_t, proj = _in_proj(x.reshape(n_tok, d), wts["ln_in_g"], wts["ln_in_b"],
                                           wts["w_in"], wts["w_small"], wts["w_small_t"], tm)
    proj3 = proj.reshape(nb, t, PROJ_W)
    cos2, sin2 = _rope_tables(pos0, t)
    ret_o, ret_s = _retention(proj3, cos2, sin2, ret_s0, wts["ret_gn_g"], wts["ret_gn_b"], ret_chunk)
    small_t3 = small_t.reshape(16, nb, t).transpose(1, 0, 2)
    conv_init = jnp.concatenate([jnp.zeros((nb, 8 - (DN_CONV_W - 1), DN_CONV_CH), F32), conv_buf], axis=1)
    dn_o, dn_s, conv_new = _deltanet(proj3, small.reshape(nb, t, LANES), small_t3, wts["dn_conv"], conv_init,
                                     wts["dn_prow"],
                                     wts["dn_pcol"], wts["dn_norm_g"], dn_s0, dn_blk, dn_chunk)
    h1, h1b = _mix(h, hb, ret_o.reshape(n_tok, d), dn_o.reshape(n_tok, d), mem_k, mem_v,
                   wts["w_xq"], wts["w_gate"], wts["w_branch"], wts["w_out"], wts["ln1_g"], wts["ln1_b"],
                   tm_mix, t)
    comb, cnt = _route_call(h1, wts["w_route"], wts["b_route"], tm_moe)
    y = _moe(h1, h1b, comb, cnt, wts["moe_w_gate"], wts["moe_w_up"], wts["moe_w_down"],
             wts["ln2_g"], wts["ln2_b"], tm_moe)
    return y.reshape(nb, t, d), ret_s, dn_s, conv_new


def kernel(x_prompt, x_sample, mem_prompt, state_ret, state_dn, state_dn_conv, cache_mem_k, cache_mem_v,
           ln_in_g, ln_in_b, w_in, ret_gn_g, ret_gn_b, dn_conv, dn_A_log, dn_dt_bias, dn_norm_g,
           w_mem_kv, w_branch, w_out, ln1_g, ln1_b, moe_w_coarse, moe_b_coarse, moe_w_fine, moe_b_fine,
           moe_w_gate, moe_w_up, moe_w_down, ln2_g, ln2_b):
    bp, tp, d = x_prompt.shape
    bs, ts, _ = x_sample.shape
    l = 0
    wi = w_in[l].astype(BF16)
    dba = wi[:, PROJ_W:PROJ_W + 16]
    xq, gates = wi[:, 6160:7184], wi[:, 7184:10256]
    wts = {
        "ln_in_g": ln_in_g.reshape(1, d), "ln_in_b": ln_in_b.reshape(1, d),
        "w_in": wi,
        "w_small": jnp.pad(dba, ((0, 0), (0, LANES - 16))),
        "w_small_t": dba.T,
        "ret_gn_g": ret_gn_g[l].reshape(1, -1), "ret_gn_b": ret_gn_b[l].reshape(1, -1),
        "dn_conv": dn_conv[l],
        "dn_prow": jnp.zeros((8, LANES), F32).at[0, 8:16].set(-jnp.exp(dn_A_log[l])).at[1, 8:16].set(dn_dt_bias[l]),
        "dn_pcol": jnp.zeros((16, LANES), F32).at[8:16, 0].set(-jnp.exp(dn_A_log[l])).at[8:16, 1].set(dn_dt_bias[l]),
        "dn_norm_g": dn_norm_g[l].reshape(1, -1),
        "w_xq": xq, "w_gate": gates,
        "w_branch": w_branch[l].astype(BF16), "w_out": w_out[l].astype(BF16),
        "ln1_g": ln1_g[l].reshape(1, d), "ln1_b": ln1_b[l].reshape(1, d),
        "w_route": jnp.pad(jnp.concatenate([moe_w_fine[l], moe_w_coarse[l]], axis=1),
                           ((0, 0), (0, LANES - MOE_NE - MOE_GROUPS))),
        "b_route": jnp.pad(jnp.concatenate([moe_b_fine[l], moe_b_coarse[l]]),
                           (0, LANES - MOE_NE - MOE_GROUPS)).reshape(1, LANES),
        "moe_w_gate": moe_w_gate[l].reshape(MOE_NE, d, MOE_DFF).astype(BF16),
        "moe_w_up": moe_w_up[l].reshape(MOE_NE, d, MOE_DFF).astype(BF16),
        "moe_w_down": moe_w_down[l].reshape(MOE_NE, MOE_DFF, d).astype(BF16),
        "ln2_g": ln2_g[l].reshape(1, d), "ln2_b": ln2_b[l].reshape(1, d),
    }

    mkv = _memkv(mem_prompt.reshape(bp * N_MEM, d), w_mem_kv[l].astype(BF16))
    mk = mkv[:, :XA_HEADS * XA_DH].reshape(1, bp, N_MEM, XA_HEADS, XA_DH)
    mv = mkv[:, XA_HEADS * XA_DH:].reshape(1, bp, N_MEM, XA_HEADS, XA_DH)
    yp, rs_p, ds_p, cb_p = _group(
        x_prompt, 0, mk, mv,
        jnp.zeros((bp, RET_HEADS, RET_DK, RET_DV), F32),
        jnp.zeros((bp, DN_V_HEADS, DN_DK, DN_DV), F32),
        jnp.zeros((bp, DN_CONV_W - 1, DN_CONV_CH), F32),
        wts, tm=min(1024, tp), tm_mix=min(512, tp), tm_moe=min(1024, tp), ret_chunk=min(256, tp),
        dn_blk=min(128, tp),
        dn_chunk=min(DN_CHUNK, tp))
    ys, rs_s, ds_s, cb_s = _group(
        x_sample, PAST_LEN, cache_mem_k[l:l + 1], cache_mem_v[l:l + 1],
        state_ret[l], state_dn[l], state_dn_conv[l],
        wts, tm=bs * ts, tm_mix=bs * ts, tm_moe=bs * ts, ret_chunk=ts, dn_blk=ts, dn_chunk=min(DN_CHUNK, ts))
    return (yp, ys, rs_p[None], rs_s[None], ds_p[None], ds_s[None], cb_p[None], cb_s[None], mk, mv)
```

```python
import functools
import math

import numpy as np

import jax
import jax.numpy as jnp
from jax import lax
from jax.experimental import pallas as pl
from jax.experimental.pallas import tpu as pltpu

F32 = jnp.float32
BF16 = jnp.bfloat16

D_MODEL = 1024
PAST_LEN = 1024
RET_HEADS = 4
RET_DK = 128
RET_DV = 256
ROPE_BASE = 10000.0
DN_QK_HEADS = 4
DN_V_HEADS = 8
DN_DK = 128
DN_DV = 128
DN_QK = DN_QK_HEADS * DN_DK
DN_CONV_W = 4
DN_CONV_CH = 2048
DN_CHUNK = 64
DN_SUB = 16
DN_HEAD_GROUP = 8
DN_STREAMS = 2
XA_HEADS = 4
XA_DH = 256
N_MEM = 256
MOE_GROUPS = 4
MOE_EXPERTS = 8
MOE_NE = MOE_GROUPS * MOE_EXPERTS
MOE_DFF = 256
DEPTH = 1
ALPHA = (2.0 * DEPTH) ** 0.25
LANES = 128
PROJ_W = 6144
VMEM_LIMIT = 56 * 1024 * 1024


def _dot(a, b):
    return jnp.dot(a.astype(BF16), b.astype(BF16), preferred_element_type=F32)


def _dot_nt(a, b):
    return lax.dot_general(a.astype(BF16), b.astype(BF16), (((1,), (1,)), ((), ())),
                           preferred_element_type=F32)


def _dot_tn(a, b):
    return lax.dot_general(a.astype(BF16), b.astype(BF16), (((0,), (0,)), ((), ())),
                           preferred_element_type=F32)


def _dot_exact(a, b):
    return jnp.dot(a, b, preferred_element_type=F32, precision=lax.Precision.HIGHEST)


def _dot_split(a, b):
    ah = a.astype(BF16)
    al = (a - ah.astype(F32)).astype(BF16)
    bh = b.astype(BF16)
    bl = (b - bh.astype(F32)).astype(BF16)
    hi = jnp.dot(ah, bh, preferred_element_type=F32)
    return hi + (jnp.dot(ah, bl, preferred_element_type=F32) + jnp.dot(al, bh, preferred_element_type=F32))


def _layernorm(x, g, b, eps=1e-5):
    mu = jnp.mean(x, -1, keepdims=True)
    xc = x - mu
    var = jnp.mean(xc * xc, -1, keepdims=True)
    return xc * lax.rsqrt(var + eps) * g + b


def _silu(x):
    return x * (1.0 / (1.0 + jnp.exp(-x)))


def _sigmoid(x):
    return 1.0 / (1.0 + jnp.exp(-x))


def _softplus(x):
    return jnp.maximum(x, 0.0) + jnp.log(1.0 + jnp.exp(-jnp.abs(x)))


def _idiv(x, n):
    return jnp.right_shift(x, int(math.log2(n)))


def _params(*sem):
    return pltpu.CompilerParams(dimension_semantics=sem, vmem_limit_bytes=VMEM_LIMIT)


def _in_proj_kernel(x_ref, g_ref, b_ref, w_ref, ws_ref, wst_ref,
                    h_ref, hb_ref, small_ref, smallt_ref, proj_ref, hs_ref):
    @pl.when(pl.program_id(1) == 0)
    def _():
        h = _layernorm(x_ref[...], g_ref[...], b_ref[...])
        h_ref[...] = h
        hb = h.astype(BF16)
        hb_ref[...] = hb
        hs_ref[...] = hb
        small_ref[...] = jnp.dot(hb, ws_ref[...], preferred_element_type=F32)
        smallt_ref[...] = lax.dot_general(wst_ref[...], hb, (((1,), (1,)), ((), ())),
                                          preferred_element_type=F32)

    proj_ref[...] = jnp.dot(hs_ref[...], w_ref[...], preferred_element_type=F32).astype(BF16)


def _in_proj(x, ln_g, ln_b, w_main, w_small, w_small_t, tm, tn=2048):
    n_tok = x.shape[0]
    grid = (n_tok // tm, PROJ_W // tn)
    return pl.pallas_call(
        _in_proj_kernel,
        grid=grid,
        in_specs=[
            pl.BlockSpec((tm, D_MODEL), lambda i, n: (i, 0)),
            pl.BlockSpec((1, D_MODEL), lambda i, n: (0, 0)),
            pl.BlockSpec((1, D_MODEL), lambda i, n: (0, 0)),
            pl.BlockSpec((D_MODEL, tn), lambda i, n: (0, n)),
            pl.BlockSpec((D_MODEL, LANES), lambda i, n: (0, 0)),
            pl.BlockSpec((16, D_MODEL), lambda i, n: (0, 0)),
        ],
        out_specs=[
            pl.BlockSpec((tm, D_MODEL), lambda i, n: (i, 0)),
            pl.BlockSpec((tm, D_MODEL), lambda i, n: (i, 0)),
            pl.BlockSpec((tm, LANES), lambda i, n: (i, 0)),
            pl.BlockSpec((16, tm), lambda i, n: (0, i)),
            pl.BlockSpec((tm, tn), lambda i, n: (i, n)),
        ],
        out_shape=[
            jax.ShapeDtypeStruct((n_tok, D_MODEL), F32),
            jax.ShapeDtypeStruct((n_tok, D_MODEL), BF16),
            jax.ShapeDtypeStruct((n_tok, LANES), F32),
            jax.ShapeDtypeStruct((16, n_tok), F32),
            jax.ShapeDtypeStruct((n_tok, PROJ_W), BF16),
        ],
        scratch_shapes=[pltpu.VMEM((tm, D_MODEL), BF16)],
        compiler_params=_params("parallel", "arbitrary"),
        name="in_proj",
    )(x, ln_g, ln_b, w_main, w_small, w_small_t)


def _ret_kernel(q_ref, k_ref, v_ref, rg_ref, cos_ref, sin_ref, s0_ref, gng_ref, gnb_ref,
                o_ref, sout_ref, s_scr, *, chunk):
    @pl.when(pl.program_id(1) == 0)
    def _():
        s_scr[...] = s0_ref[0]

    heads = range(RET_HEADS)
    lg = [math.log(1.0 - 2.0 ** (-5.0 - h)) for h in heads]
    cos = cos_ref[...]
    sin = sin_ref[...]
    ri = lax.broadcasted_iota(jnp.int32, (chunk, chunk), 0)
    ci = lax.broadcasted_iota(jnp.int32, (chunk, chunk), 1)
    causal = ri >= ci
    diff = jnp.where(causal, (ri - ci).astype(F32), 0.0)
    idx = lax.broadcasted_iota(jnp.int32, (chunk, 1), 0).astype(F32)
    q = [q_ref[0, :, h * RET_DK:(h + 1) * RET_DK].astype(F32) for h in heads]
    k = [k_ref[0, :, h * RET_DK:(h + 1) * RET_DK].astype(F32) for h in heads]
    v = [v_ref[0, :, h * RET_DV:(h + 1) * RET_DV] for h in heads]
    qr = [q[h] * cos + pltpu.roll(q[h], RET_DK // 2, 1) * sin for h in heads]
    kr = [(k[h] * cos + pltpu.roll(k[h], RET_DK // 2, 1) * sin) * (RET_DK ** -0.5) for h in heads]
    scores = [_dot_nt(qr[h], kr[h]) * jnp.where(causal, jnp.exp(diff * lg[h]), 0.0) for h in heads]
    s = [s_scr[h] for h in heads]
    cross = [_dot(qr[h], s[h]) * jnp.exp((idx + 1.0) * lg[h]) for h in heads]
    o = [_dot(scores[h], v[h]) + cross[h] for h in heads]
    kv = [_dot_tn(kr[h] * jnp.exp((chunk - 1.0 - idx) * lg[h]), v[h]) for h in heads]
    for h in heads:
        s_new = s[h] * math.exp(chunk * lg[h]) + kv[h]
        s_scr[h] = s_new
        sout_ref[0, h] = s_new
        sl = slice(h * RET_DV, (h + 1) * RET_DV)
        gated = _layernorm(o[h], gng_ref[:, sl], gnb_ref[:, sl]) * _silu(rg_ref[0, :, sl].astype(F32))
        o_ref[0, :, sl] = gated.astype(BF16)


def _retention(proj, cos2, sin2, s0, gn_g, gn_b, chunk):
    nb, t = proj.shape[:2]
    grid = (nb, t // chunk)
    qk_w = RET_HEADS * RET_DK
    v_w = RET_HEADS * RET_DV
    return pl.pallas_call(
        functools.partial(_ret_kernel, chunk=chunk),
        grid=grid,
        in_specs=[
            pl.BlockSpec((1, chunk, qk_w), lambda b, c: (b, c, 0)),
            pl.BlockSpec((1, chunk, qk_w), lambda b, c: (b, c, 512 // qk_w)),
            pl.BlockSpec((1, chunk, v_w), lambda b, c: (b, c, 1024 // v_w)),
            pl.BlockSpec((1, chunk, v_w), lambda b, c: (b, c, 2048 // v_w)),
            pl.BlockSpec((chunk, RET_DK), lambda b, c: (c, 0)),
            pl.BlockSpec((chunk, RET_DK), lambda b, c: (c, 0)),
            pl.BlockSpec((1, RET_HEADS, RET_DK, RET_DV), lambda b, c: (b, 0, 0, 0)),
            pl.BlockSpec((1, v_w), lambda b, c: (0, 0)),
            pl.BlockSpec((1, v_w), lambda b, c: (0, 0)),
        ],
        out_specs=[
            pl.BlockSpec((1, chunk, v_w), lambda b, c: (b, c, 0)),
            pl.BlockSpec((1, RET_HEADS, RET_DK, RET_DV), lambda b, c: (b, 0, 0, 0)),
        ],
        out_shape=[
            jax.ShapeDtypeStruct((nb, t, v_w), BF16),
            jax.ShapeDtypeStruct((nb, RET_HEADS, RET_DK, RET_DV), F32),
        ],
        scratch_shapes=[pltpu.VMEM((RET_HEADS, RET_DK, RET_DV), F32)],
        compiler_params=_params("parallel", "arbitrary"),
        name="retention",
    )(proj, proj, proj, proj, cos2, sin2, s0, gn_g, gn_b)


def _dn_kernel(xqk_ref, xv_ref, z_ref, sm_ref, smt_ref, cw_ref, cinit_ref, prow_ref, pcol_ref, ng_ref, s0_ref,
               o_ref, sout_ref, cout_ref, xext, s_scr, *, blk_len, chunk):
    L = blk_len
    nbb = xqk_ref.shape[0]
    blk = pl.program_id(1)

    @pl.when(blk == 0)
    def _():
        xext[:, 0:8, :] = cinit_ref[...]
        s_scr[...] = s0_ref[...]

    ri = lax.broadcasted_iota(jnp.int32, (L, L), 0)
    ci = lax.broadcasted_iota(jnp.int32, (L, L), 1)
    same_chunk = _idiv(ri, chunk) == _idiv(ci, chunk)
    lower = jnp.logical_and(ri >= ci, same_chunk)
    strict = jnp.logical_and(ri > ci, same_chunk)
    same_sub = _idiv(ri, DN_SUB) == _idiv(ci, DN_SUB)
    tri = jnp.where(lower, 1.0, 0.0).astype(F32)
    tri_t = jnp.where(jnp.logical_and(ri <= ci, same_chunk), 1.0, 0.0).astype(F32)
    prow = prow_ref[...]
    pcol = pcol_ref[...]

    conv, beta_tm, gcum_tm, gcum_hm, egc_tm = [], [], [], [], []
    for b in range(nbb):
        x = jnp.concatenate([xqk_ref[b], xv_ref[b]], axis=1).astype(F32)
        xext[b, 8:8 + L, :] = x
        cv = x * cw_ref[3:4, :]
        for w in range(DN_CONV_W - 1):
            cv = cv + xext[b, 5 + w:5 + w + L, :] * cw_ref[w:w + 1, :]
        conv.append(_silu(cv))
        xext[b, 0:8, :] = x[L - 8:L, :]
        cout_ref[b] = x[L - (DN_CONV_W - 1):L, :]
        sm = sm_ref[b]
        beta_tm.append(_sigmoid(sm))
        g_tm = prow[0:1, :] * _softplus(sm + prow[1:2, :])
        g_hm = pcol[8:16, 0:1] * _softplus(smt_ref[b][8:16, :] + pcol[8:16, 1:2])
        gcum_tm.append(_dot_exact(tri, g_tm))
        gcum_hm.append(_dot_exact(g_hm, tri_t))
        egc_tm.append(jnp.exp(gcum_tm[b]))

    n_sub = L // chunk
    rep = DN_V_HEADS // DN_QK_HEADS
    bdot = lambda x, y: jnp.dot(x, y, preferred_element_type=F32)
    for h0 in range(0, DN_V_HEADS, DN_HEAD_GROUP):
        heads = [(b, hh) for b in range(nbb) for hh in range(h0, h0 + DN_HEAD_GROUP)]
        q, k, kk, qk = {}, {}, {}, {}
        for b in range(nbb):
            for j in range(h0 // rep, (h0 + DN_HEAD_GROUP) // rep):
                qj = conv[b][:, j * DN_DK:(j + 1) * DN_DK]
                kj = conv[b][:, DN_QK + j * DN_DK:DN_QK + (j + 1) * DN_DK]
                qj = qj * lax.rsqrt(jnp.sum(qj * qj, -1, keepdims=True) + 1e-6) * (DN_DK ** -0.5)
                kj = kj * lax.rsqrt(jnp.sum(kj * kj, -1, keepdims=True) + 1e-6)
                kq = _dot_nt(jnp.concatenate([kj, qj], axis=0), kj)
                q[b, j], k[b, j], kk[b, j], qk[b, j] = qj, kj, kq[0:L], kq[L:2 * L]
        gc = {(b, hh): gcum_tm[b][:, 8 + hh:9 + hh] for b, hh in heads}
        beta = {(b, hh): beta_tm[b][:, hh:hh + 1] for b, hh in heads}
        eg = {(b, hh): egc_tm[b][:, 8 + hh:9 + hh] for b, hh in heads}
        rel = {(b, hh): jnp.where(lower, jnp.exp(jnp.where(lower, gc[b, hh] - gcum_hm[b][hh:hh + 1, :], 0.0)),
                                  0.0) for b, hh in heads}
        a = {(b, hh): jnp.where(strict, beta[b, hh] * kk[b, hh // rep] * rel[b, hh], 0.0) for b, hh in heads}
        attn = {(b, hh): qk[b, hh // rep] * rel[b, hh] for b, hh in heads}
        pd = {hh: jnp.where(same_sub, -a[hh], 0.0) for hh in heads}
        e = {hh: a[hh] + pd[hh] for hh in heads}
        eb = {hh: e[hh].astype(BF16) for hh in heads}
        xacc = pd
        xaccb = {hh: pd[hh].astype(BF16) for hh in heads}
        pwb = xaccb
        for _ in range(3):
            pw = {hh: bdot(pwb[hh], pwb[hh]) for hh in heads}
            pwb = {hh: pw[hh].astype(BF16) for hh in heads}
            xacc = {hh: xacc[hh] + pw[hh] + bdot(xaccb[hh], pwb[hh]) for hh in heads}
            xaccb = {hh: xacc[hh].astype(BF16) for hh in heads}
        f = {hh: e[hh] + bdot(xaccb[hh], eb[hh]) for hh in heads}
        fb = {hh: f[hh].astype(BF16) for hh in heads}
        f2 = {hh: bdot(fb[hh], fb[hh]) for hh in heads}
        y = {hh: f2[hh] - f[hh] - bdot(fb[hh], f2[hh].astype(BF16)) for hh in heads}
        rinv = {hh: xacc[hh] + y[hh] + bdot(y[hh].astype(BF16), xaccb[hh]) for hh in heads}
        sol = {}
        for b, hh in heads:
            vh = conv[b][:, 2 * DN_QK + hh * DN_DV:2 * DN_QK + (hh + 1) * DN_DV]
            rhs = jnp.concatenate([vh * beta[b, hh], k[b, hh // rep] * (beta[b, hh] * eg[b, hh])], axis=1)
            sol[b, hh] = rhs + _dot(rinv[b, hh], rhs)
        u = {p: sol[p][:, 0:DN_DV] for p in heads}
        wmat = {p: sol[p][:, DN_DV:DN_DV + DN_DK] for p in heads}
        qe = {(b, hh): q[b, hh // rep] * eg[b, hh] for b, hh in heads}
        s = {(b, hh): s_scr[b, hh] for b, hh in heads}
        v_new = {p: [] for p in heads}
        q_s = {p: [] for p in heads}
        for sc in range(n_sub):
            r0 = sc * chunk
            for b, hh in heads:
                p = (b, hh)
                ws = _dot(jnp.concatenate([wmat[p][r0:r0 + chunk], qe[p][r0:r0 + chunk]], axis=0), s[p])
                vn = u[p][r0:r0 + chunk] - ws[0:chunk]
                v_new[p].append(vn)
                q_s[p].append(ws[chunk:2 * chunk])
                g_last = gcum_tm[b][r0 + chunk - 1:r0 + chunk, 8 + hh:9 + hh]
                kd = k[b, hh // rep][r0:r0 + chunk] * jnp.exp(g_last - gc[p][r0:r0 + chunk])
                s[p] = s[p] * jnp.exp(g_last) + _dot_tn(kd, vn)
        for b, hh in heads:
            p = (b, hh)
            s_scr[b, hh] = s[p]
            sout_ref[b, hh] = s[p]
            vn = v_new[p][0] if n_sub == 1 else jnp.concatenate(v_new[p], axis=0)
            qs = q_s[p][0] if n_sub == 1 else jnp.concatenate(q_s[p], axis=0)
            o = qs + _dot(attn[p], vn)
            zh = z_ref[b, :, hh * DN_DV:(hh + 1) * DN_DV].astype(F32)
            n = o * lax.rsqrt(jnp.mean(o * o, -1, keepdims=True) + 1e-6) * ng_ref[...]
            o_ref[b, :, hh * DN_DV:(hh + 1) * DN_DV] = (n * _silu(zh)).astype(BF16)


def _deltanet(proj, small, small_t, conv_w, conv_init, prow, pcol, norm_g, s0, blk_len, chunk):
    nb, t = proj.shape[:2]
    nblk = t // blk_len
    nbb = DN_STREAMS
    grid = (nb // nbb, nblk)
    return pl.pallas_call(
        functools.partial(_dn_kernel, blk_len=blk_len, chunk=chunk),
        grid=grid,
        in_specs=[
            pl.BlockSpec((nbb, blk_len, 1024), lambda b, i: (b, i, 3072 // 1024)),
            pl.BlockSpec((nbb, blk_len, 1024), lambda b, i: (b, i, 4096 // 1024)),
            pl.BlockSpec((nbb, blk_len, 1024), lambda b, i: (b, i, 5120 // 1024)),
            pl.BlockSpec((nbb, blk_len, LANES), lambda b, i: (b, i, 0)),
            pl.BlockSpec((nbb, 16, blk_len), lambda b, i: (b, 0, i)),
            pl.BlockSpec((DN_CONV_W, DN_CONV_CH), lambda b, i: (0, 0)),
            pl.BlockSpec((nbb, 8, DN_CONV_CH), lambda b, i: (b, 0, 0)),
            pl.BlockSpec((8, LANES), lambda b, i: (0, 0)),
            pl.BlockSpec((16, LANES), lambda b, i: (0, 0)),
            pl.BlockSpec((1, DN_DV), lambda b, i: (0, 0)),
            pl.BlockSpec((nbb, DN_V_HEADS, DN_DK, DN_DV), lambda b, i: (b, 0, 0, 0)),
        ],
        out_specs=[
            pl.BlockSpec((nbb, blk_len, 1024), lambda b, i: (b, i, 0)),
            pl.BlockSpec((nbb, DN_V_HEADS, DN_DK, DN_DV), lambda b, i: (b, 0, 0, 0)),
            pl.BlockSpec((nbb, DN_CONV_W - 1, DN_CONV_CH), lambda b, i: (b, 0, 0)),
        ],
        out_shape=[
            jax.ShapeDtypeStruct((nb, t, 1024), BF16),
            jax.ShapeDtypeStruct((nb, DN_V_HEADS, DN_DK, DN_DV), F32),
            jax.ShapeDtypeStruct((nb, DN_CONV_W - 1, DN_CONV_CH), F32),
        ],
        scratch_shapes=[pltpu.VMEM((nbb, blk_len + 8, DN_CONV_CH), F32),
                        pltpu.VMEM((nbb, DN_V_HEADS, DN_DK, DN_DV), F32)],
        compiler_params=_params("parallel", "arbitrary"),
        name="deltanet",
    )(proj, proj, proj, small, small_t, conv_w, conv_init, prow, pcol, norm_g, s0)


def _memkv_kernel(m_ref, w_ref, o_ref):
    o_ref[...] = _dot(m_ref[...], w_ref[...])


def _memkv(mem, w_kv):
    n = mem.shape[0]
    tn = 1024
    return pl.pallas_call(
        _memkv_kernel,
        grid=(w_kv.shape[1] // tn,),
        in_specs=[pl.BlockSpec((n, D_MODEL), lambda j: (0, 0)),
                  pl.BlockSpec((D_MODEL, tn), lambda j: (0, j))],
        out_specs=pl.BlockSpec((n, tn), lambda j: (0, j)),
        out_shape=jax.ShapeDtypeStruct((n, w_kv.shape[1]), F32),
        compiler_params=_params("parallel"),
        name="memkv",
    )(mem, w_kv)


def _mix_kernel(h_ref, hb_ref, ret_ref, dn_ref, mk_ref, mv_ref, wxq_ref, wg_ref, wb_ref, wo_ref,
                g1_ref, b1_ref, h1_ref, h1b_ref, mem_scr, *, n_batch, tiles_per_batch):
    @pl.when(pl.program_id(0) % tiles_per_batch == 0)
    def _():
        for b in range(n_batch):
            for hh in range(XA_HEADS):
                mem_scr[0, hh, b] = mk_ref[0, b, :, hh, :].astype(BF16)
                mem_scr[1, hh, b] = mv_ref[0, b, :, hh, :].astype(BF16)

    hb = hb_ref[...]
    xq = jnp.dot(hb, wxq_ref[...], preferred_element_type=F32)
    tb = xq.shape[0] // n_batch
    pairs = [(b, hh) for b in range(n_batch) for hh in range(XA_HEADS)]
    s = [_dot_nt(xq[b * tb:(b + 1) * tb, hh * XA_DH:(hh + 1) * XA_DH], mem_scr[0, hh, b]) * (XA_DH ** -0.5)
         for b, hh in pairs]
    p = [jnp.exp(si - jnp.max(si, -1, keepdims=True)) for si in s]
    p = [pi / jnp.sum(pi, -1, keepdims=True) for pi in p]
    xo = [_dot(pi, mem_scr[1, hh, b]) for pi, (b, hh) in zip(p, pairs)]
    xo = [jnp.concatenate(xo[b * XA_HEADS:(b + 1) * XA_HEADS], axis=1) for b in range(n_batch)]
    xo = xo[0] if n_batch == 1 else jnp.concatenate(xo, axis=0)
    mixed = None
    for n, br in enumerate((ret_ref[...], dn_ref[...], xo)):
        gate = _sigmoid(jnp.dot(hb, wg_ref[:, n * D_MODEL:(n + 1) * D_MODEL], preferred_element_type=F32))
        term = gate * _dot(br, wb_ref[n])
        mixed = term if mixed is None else mixed + term
    y = ALPHA * h_ref[...] + _dot(mixed, wo_ref[...])
    h1 = _layernorm(y, g1_ref[...], b1_ref[...])
    h1_ref[...] = h1
    h1b_ref[...] = h1.astype(BF16)


def _mix(h, hb, ret_o, dn_o, mem_k, mem_v, w_xq, w_gate, w_branch, w_out, ln_g, ln_b, tm, t):
    n_tok = h.shape[0]
    const2 = lambda i: (0, 0)
    row = lambda i: (i, 0)
    single = pl.Buffered(1)
    if tm <= t:
        n_batch = 1
        tiles_per_batch = t // tm
    else:
        n_batch = tm // t
        tiles_per_batch = 1
    mem_spec = pl.BlockSpec((1, n_batch, N_MEM, XA_HEADS, XA_DH), lambda i: (0, i // tiles_per_batch, 0, 0, 0))
    return pl.pallas_call(
        functools.partial(_mix_kernel, n_batch=n_batch, tiles_per_batch=tiles_per_batch),
        grid=(n_tok // tm,),
        in_specs=[
            pl.BlockSpec((tm, D_MODEL), row),
            pl.BlockSpec((tm, D_MODEL), row),
            pl.BlockSpec((tm, D_MODEL), row),
            pl.BlockSpec((tm, D_MODEL), row),
            mem_spec,
            mem_spec,
            pl.BlockSpec((D_MODEL, D_MODEL), const2, pipeline_mode=single),
            pl.BlockSpec((D_MODEL, 3 * D_MODEL), const2, pipeline_mode=single),
            pl.BlockSpec((3, D_MODEL, D_MODEL), lambda i: (0, 0, 0), pipeline_mode=single),
            pl.BlockSpec((D_MODEL, D_MODEL), const2, pipeline_mode=single),
            pl.BlockSpec((1, D_MODEL), const2),
            pl.BlockSpec((1, D_MODEL), const2),
        ],
        out_specs=[pl.BlockSpec((tm, D_MODEL), row), pl.BlockSpec((tm, D_MODEL), row)],
        out_shape=[jax.ShapeDtypeStruct((n_tok, D_MODEL), F32),
                   jax.ShapeDtypeStruct((n_tok, D_MODEL), BF16)],
        scratch_shapes=[pltpu.VMEM((2, XA_HEADS, n_batch, N_MEM, XA_DH), BF16)],
        compiler_params=_params("arbitrary"),
        name="mix",
    )(h, hb, ret_o, dn_o, mem_k, mem_v, w_xq, w_gate, w_branch, w_out, ln_g, ln_b)


def _route(x, wr, br):
    logits = _dot_split(x, wr) + br
    lane_i = lax.broadcasted_iota(jnp.int32, logits.shape, 1)
    lane = lane_i.astype(F32)
    far = jnp.float32(LANES)
    neg = jnp.float32(-3.0e38)
    is_c = jnp.logical_and(lane_i >= MOE_NE, lane_i < MOE_NE + MOE_GROUPS)
    cl = jnp.where(is_c, logits, neg)
    cmax = jnp.max(cl, -1, keepdims=True)
    denom = jnp.sum(jnp.where(is_c, jnp.exp(jnp.where(is_c, logits - cmax, 0.0)), 0.0), -1, keepdims=True)
    p_grp = 1.0 / denom
    grp = jnp.min(jnp.where(jnp.logical_and(is_c, cl == cmax), lane - MOE_NE, far), -1, keepdims=True)
    in_grp = jnp.logical_and(lane_i < MOE_NE, _idiv(lane_i, MOE_EXPERTS).astype(F32) == grp)
    fl = jnp.where(in_grp, logits, neg)
    v1 = jnp.max(fl, -1, keepdims=True)
    i1 = jnp.min(jnp.where(jnp.logical_and(in_grp, fl == v1), lane, far), -1, keepdims=True)
    rest = jnp.logical_and(in_grp, lane != i1)
    fl2 = jnp.where(rest, logits, neg)
    v2 = jnp.max(fl2, -1, keepdims=True)
    i2 = jnp.min(jnp.where(jnp.logical_and(rest, fl2 == v2), lane, far), -1, keepdims=True)
    e2 = jnp.exp(v2 - v1)
    w1 = p_grp / (1.0 + e2)
    w2 = p_grp * e2 / (1.0 + e2)
    return jnp.where(lane == i1, w1, 0.0) + jnp.where(lane == i2, w2, 0.0), grp


def _route_kernel(h1_ref, wr_ref, br_ref, comb_ref, cnt_ref):
    comb, grp = _route(h1_ref[...], wr_ref[...], br_ref[...])
    lane_i = lax.broadcasted_iota(jnp.int32, comb.shape, 1)
    comb_ref[...] = jnp.where(lane_i == LANES - 1, grp, comb)
    onehot = jnp.where(lane_i.astype(F32) == grp, 1.0, 0.0)
    cnt_ref[0] = jnp.broadcast_to(jnp.sum(onehot, 0, keepdims=True), (8, LANES))


def _route_call(h1, w_route, b_route, tm):
    n_tok = h1.shape[0]
    return pl.pallas_call(
        _route_kernel,
        grid=(n_tok // tm,),
        in_specs=[pl.BlockSpec((tm, D_MODEL), lambda i: (i, 0)),
                  pl.BlockSpec((D_MODEL, LANES), lambda i: (0, 0)),
                  pl.BlockSpec((1, LANES), lambda i: (0, 0))],
        out_specs=[pl.BlockSpec((tm, LANES), lambda i: (i, 0)),
                   pl.BlockSpec((1, 8, LANES), lambda i: (i, 0, 0))],
        out_shape=[jax.ShapeDtypeStruct((n_tok, LANES), F32),
                   jax.ShapeDtypeStruct((n_tok // tm, 8, LANES), F32)],
        compiler_params=_params("parallel"),
        name="route",
    )(h1, w_route, b_route)


MOE_ALIGN = 16
MOE_EPS = 4


def _moe_sizes(tm):
    base = tm // MOE_GROUPS
    fine = {min(tm, base + MOE_ALIGN * i) for i in range(5)}
    return tuple(sorted(fine | {tm // 2, tm}))


def _moe_kernel(offs_ref, cls_ref, h1_ref, h1b_ref, comb_ref, tri_ref, wg_ref, wu_ref, wd_ref,
                g2_ref, b2_ref, y_ref, pt_scr, xs_scr, cs_scr, ys_scr, *, tm, nr1, sizes):
    i = pl.program_id(0)
    e = pl.program_id(1)
    nrt = xs_scr.shape[0]

    @pl.when(e == 0)
    def _():
        comb = comb_ref[...]
        lane_i = lax.broadcasted_iota(jnp.int32, (tm, LANES), 1)
        grp = comb[:, LANES - 1:LANES]
        is_g = lane_i.astype(F32) == grp
        prefix = jnp.dot(tri_ref[...], jnp.where(is_g, 1.0, 0.0).astype(BF16), preferred_element_type=F32)
        rank = jnp.sum(jnp.where(is_g, prefix, 0.0), -1, keepdims=True)
        offv = jnp.zeros((tm, 1), F32)
        for g in range(MOE_GROUPS):
            offv = jnp.where(grp == g, offs_ref[i * MOE_GROUPS + g].astype(F32), offv)
        pos = offv + rank
        col = lax.broadcasted_iota(jnp.int32, (tm, nr1), 1).astype(F32)
        pt = jnp.where(col == pos, 1.0, 0.0).astype(BF16)
        pt_scr[...] = pt
        combw = jnp.where(lane_i < MOE_NE, comb, 0.0)
        c_hi = combw.astype(BF16)
        c_lo = (combw - c_hi.astype(F32)).astype(BF16)
        xs_scr[0:nr1, :] = _dot_tn(pt, h1b_ref[...]).astype(BF16)
        cs_scr[0:nr1, :] = _dot_tn(pt, c_hi) + _dot_tn(pt, c_lo)
        xs_scr[nr1:nrt, :] = jnp.zeros((nrt - nr1, D_MODEL), BF16)
        cs_scr[nr1:nrt, :] = jnp.zeros((nrt - nr1, LANES), F32)
        ys_scr[tm:nr1, :] = jnp.zeros((nr1 - tm, D_MODEL), F32)

    g = e // (MOE_EXPERTS // MOE_EPS)
    off = pl.multiple_of(offs_ref[i * MOE_GROUPS + g], MOE_ALIGN)
    cls = cls_ref[i * MOE_GROUPS + g]
    first = e % (MOE_EXPERTS // MOE_EPS) == 0
    for ci, m in enumerate(sizes):
        @pl.when(cls == ci)
        def _(m=m):
            rows = pl.ds(off, m)
            xb = xs_scr[rows, :]
            cb = cs_scr[rows, :]
            lane_i = lax.broadcasted_iota(jnp.int32, (m, LANES), 1)
            sub = range(MOE_EPS)
            colw = [jnp.sum(jnp.where(lane_i == e * MOE_EPS + j, cb, 0.0), -1, keepdims=True) for j in sub]
            gate = [jnp.dot(xb, wg_ref[j], preferred_element_type=F32) for j in sub]
            up = [jnp.dot(xb, wu_ref[j], preferred_element_type=F32) for j in sub]
            hg = [(_silu(gate[j]) * up[j] * colw[j]).astype(BF16) for j in sub]
            wd = wd_ref[...].reshape(MOE_EPS * MOE_DFF, D_MODEL)
            res = jnp.dot(jnp.concatenate(hg, axis=1), wd, preferred_element_type=F32)

            @pl.when(first)
            def _():
                ys_scr[rows, :] = res

            @pl.when(jnp.logical_not(first))
            def _():
                ys_scr[rows, :] += res

    @pl.when(e == MOE_NE // MOE_EPS - 1)
    def _():
        moe = jnp.dot(pt_scr[...], ys_scr[0:nr1, :].astype(BF16), preferred_element_type=F32)
        y_ref[...] = _layernorm(ALPHA * h1_ref[...] + moe, g2_ref[...], b2_ref[...])


def _moe(h1, h1b, comb, cnt, w_gate, w_up, w_down, ln_g, ln_b, tm):
    n_tok = h1.shape[0]
    sizes = _moe_sizes(tm)
    nr1 = -(-(tm + MOE_GROUPS * MOE_ALIGN) // LANES) * LANES
    nrt = nr1 + tm
    c = cnt[:, 0, :MOE_GROUPS].astype(jnp.int32)
    seg = (c + MOE_ALIGN - 1) // MOE_ALIGN * MOE_ALIGN
    offs = (jnp.cumsum(seg, axis=1) - seg).reshape(-1)
    cls = sum((c > s).astype(jnp.int32) for s in sizes[:-1]).reshape(-1)
    tri =jnp.asarray(np.tril(np.ones((tm, tm), np.float32), -1), BF16)
    grid_spec = pltpu.PrefetchScalarGridSpec(
        num_scalar_prefetch=2,
        grid=(n_tok // tm, MOE_NE // MOE_EPS),
        in_specs=[
            pl.BlockSpec((tm, D_MODEL), lambda i, e, *_: (i, 0), pipeline_mode=pl.Buffered(1)),
            pl.BlockSpec((tm, D_MODEL), lambda i, e, *_: (i, 0)),
            pl.BlockSpec((tm, LANES), lambda i, e, *_: (i, 0)),
            pl.BlockSpec((tm, tm), lambda i, e, *_: (0, 0), pipeline_mode=pl.Buffered(1)),
            pl.BlockSpec((MOE_EPS, D_MODEL, MOE_DFF), lambda i, e, *_: (e, 0, 0)),
            pl.BlockSpec((MOE_EPS, D_MODEL, MOE_DFF), lambda i, e, *_: (e, 0, 0)),
            pl.BlockSpec((MOE_EPS, MOE_DFF, D_MODEL), lambda i, e, *_: (e, 0, 0)),
            pl.BlockSpec((1, D_MODEL), lambda i, e, *_: (0, 0)),
            pl.BlockSpec((1, D_MODEL), lambda i, e, *_: (0, 0)),
        ],
        out_specs=pl.BlockSpec((tm, D_MODEL), lambda i, e, *_: (i, 0)),
        scratch_shapes=[pltpu.VMEM((tm, nr1), BF16), pltpu.VMEM((nrt, D_MODEL), BF16),
                        pltpu.VMEM((nrt, LANES), F32), pltpu.VMEM((nrt, D_MODEL), F32)],
    )
    return pl.pallas_call(
        functools.partial(_moe_kernel, tm=tm, nr1=nr1, sizes=sizes),
        grid_spec=grid_spec,
        out_shape=jax.ShapeDtypeStruct((n_tok, D_MODEL), F32),
        compiler_params=_params("parallel", "arbitrary"),
        name="moe",
    )(offs, cls, h1, h1b, comb, tri, w_gate, w_up, w_down, ln_g, ln_b)


def _rope_tables(start, t):
    half = RET_DK // 2
    inv = 1.0 / (ROPE_BASE ** (np.arange(half, dtype=np.float64) / half))
    ang = (start + np.arange(t, dtype=np.float64))[:, None] * inv[None, :]
    cos = np.cos(ang)
    sin = np.sin(ang)
    return (jnp.asarray(np.concatenate([cos, cos], -1), F32),
            jnp.asarray(np.concatenate([-sin, sin], -1), F32))


def _group(x, pos0, mem_k, mem_v, ret_s0, dn_s0, conv_buf, wts, *, tm, tm_mix, tm_moe, ret_chunk, dn_blk,
           dn_chunk):
    nb, t, d = x.shape
    n_tok = nb * t
    h, hb, small, small_t, proj = _in_proj(x.reshape(n_tok, d), wts["ln_in_g"], wts["ln_in_b"],
                                           wts["w_in"], wts["w_small"], wts["w_small_t"], tm)
    proj3 = proj.reshape(nb, t, PROJ_W)
    cos2, sin2 = _rope_tables(pos0, t)
    ret_o, ret_s = _retention(proj3, cos2, sin2, ret_s0, wts["ret_gn_g"], wts["ret_gn_b"], ret_chunk)
    small_t3 = small_t.reshape(16, nb, t).transpose(1, 0, 2)
    conv_init = jnp.concatenate([jnp.zeros((nb, 8 - (DN_CONV_W - 1), DN_CONV_CH), F32), conv_buf], axis=1)
    dn_o, dn_s, conv_new = _deltanet(proj3, small.reshape(nb, t, LANES), small_t3, wts["dn_conv"], conv_init,
                                     wts["dn_prow"],
                                     wts["dn_pcol"], wts["dn_norm_g"], dn_s0, dn_blk, dn_chunk)
    h1, h1b = _mix(h, hb, ret_o.reshape(n_tok, d), dn_o.reshape(n_tok, d), mem_k, mem_v,
                   wts["w_xq"], wts["w_gate"], wts["w_branch"], wts["w_out"], wts["ln1_g"], wts["ln1_b"],
                   tm_mix, t)
    comb, cnt = _route_call(h1, wts["w_route"], wts["b_route"], tm_moe)
    y = _moe(h1, h1b, comb, cnt, wts["moe_w_gate"], wts["moe_w_up"], wts["moe_w_down"],
             wts["ln2_g"], wts["ln2_b"], tm_moe)
    return y.reshape(nb, t, d), ret_s, dn_s, conv_new


def kernel(x_prompt, x_sample, mem_prompt, state_ret, state_dn, state_dn_conv, cache_mem_k, cache_mem_v,
           ln_in_g, ln_in_b, w_in, ret_gn_g, ret_gn_b, dn_conv, dn_A_log, dn_dt_bias, dn_norm_g,
           w_mem_kv, w_branch, w_out, ln1_g, ln1_b, moe_w_coarse, moe_b_coarse, moe_w_fine, moe_b_fine,
           moe_w_gate, moe_w_up, moe_w_down, ln2_g, ln2_b):
    bp, tp, d = x_prompt.shape
    bs, ts, _ = x_sample.shape
    l = 0
    wi = w_in[l].astype(BF16)
    dba = wi[:, PROJ_W:PROJ_W + 16]
    xq, gates = wi[:, 6160:7184], wi[:, 7184:10256]
    wts = {
        "ln_in_g": ln_in_g.reshape(1, d), "ln_in_b": ln_in_b.reshape(1, d),
        "w_in": wi,
        "w_small": jnp.pad(dba, ((0, 0), (0, LANES - 16))),
        "w_small_t": dba.T,
        "ret_gn_g": ret_gn_g[l].reshape(1, -1), "ret_gn_b": ret_gn_b[l].reshape(1, -1),
        "dn_conv": dn_conv[l],
        "dn_prow": jnp.zeros((8, LANES), F32).at[0, 8:16].set(-jnp.exp(dn_A_log[l])).at[1, 8:16].set(dn_dt_bias[l]),
        "dn_pcol": jnp.zeros((16, LANES), F32).at[8:16, 0].set(-jnp.exp(dn_A_log[l])).at[8:16, 1].set(dn_dt_bias[l]),
        "dn_norm_g": dn_norm_g[l].reshape(1, -1),
        "w_xq": xq, "w_gate": gates,
        "w_branch": w_branch[l].astype(BF16), "w_out": w_out[l].astype(BF16),
        "ln1_g": ln1_g[l].reshape(1, d), "ln1_b": ln1_b[l].reshape(1, d),
        "w_route": jnp.pad(jnp.concatenate([moe_w_fine[l], moe_w_coarse[l]], axis=1),
                           ((0, 0), (0, LANES - MOE_NE - MOE_GROUPS))),
        "b_route": jnp.pad(jnp.concatenate([moe_b_fine[l], moe_b_coarse[l]]),
                           (0, LANES - MOE_NE - MOE_GROUPS)).reshape(1, LANES),
        "moe_w_gate": moe_w_gate[l].reshape(MOE_NE, d, MOE_DFF).astype(BF16),
        "moe_w_up": moe_w_up[l].reshape(MOE_NE, d, MOE_DFF).astype(BF16),
        "moe_w_down": moe_w_down[l].reshape(MOE_NE, MOE_DFF, d).astype(BF16),
        "ln2_g": ln2_g[l].reshape(1, d), "ln2_b": ln2_b[l].reshape(1, d),
    }

    mkv = _memkv(mem_prompt.reshape(bp * N_MEM, d), w_mem_kv[l].astype(BF16))
    mk = mkv[:, :XA_HEADS * XA_DH].reshape(1, bp, N_MEM, XA_HEADS, XA_DH)
    mv = mkv[:, XA_HEADS * XA_DH:].reshape(1, bp, N_MEM, XA_HEADS, XA_DH)
    yp, rs_p, ds_p, cb_p = _group(
        x_prompt, 0, mk, mv,
        jnp.zeros((bp, RET_HEADS, RET_DK, RET_DV), F32),
        jnp.zeros((bp, DN_V_HEADS, DN_DK, DN_DV), F32),
        jnp.zeros((bp, DN_CONV_W - 1, DN_CONV_CH), F32),
        wts, tm=min(1024, tp), tm_mix=min(512, tp), tm_moe=min(1024, tp), ret_chunk=min(256, tp),
        dn_blk=min(128, tp),
        dn_chunk=min(DN_CHUNK, tp))
    ys, rs_s, ds_s, cb_s = _group(
        x_sample, PAST_LEN, cache_mem_k[l:l + 1], cache_mem_v[l:l + 1],
        state_ret[l], state_dn[l], state_dn_conv[l],
        wts, tm=bs * ts, tm_mix=bs * ts, tm_moe=bs * ts, ret_chunk=ts, dn_blk=ts, dn_chunk=min(DN_CHUNK, ts))
    return (yp, ys, rs_p[None], rs_s[None], ds_p[None], ds_s[None], cb_p[None], cb_s[None], mk, mv)
```

```python
import functools
import math

import numpy as np

import jax
import jax.numpy as jnp
from jax import lax
from jax.experimental import pallas as pl
from jax.experimental.pallas import tpu as pltpu

F32 = jnp.float32
BF16 = jnp.bfloat16

D_MODEL = 1024
PAST_LEN = 1024
RET_HEADS = 4
RET_DK = 128
RET_DV = 256
ROPE_BASE = 10000.0
DN_QK_HEADS = 4
DN_V_HEADS = 8
DN_DK = 128
DN_DV = 128
DN_QK = DN_QK_HEADS * DN_DK
DN_CONV_W = 4
DN_CONV_CH = 2048
DN_CHUNK = 64
DN_SUB = 16
DN_HEAD_GROUP = 8
DN_STREAMS = 2
XA_HEADS = 4
XA_DH = 256
N_MEM = 256
MOE_GROUPS = 4
MOE_EXPERTS = 8
MOE_NE = MOE_GROUPS * MOE_EXPERTS
MOE_DFF = 256
DEPTH = 1
ALPHA = (2.0 * DEPTH) ** 0.25
LANES = 128
PROJ_W = 6144
VMEM_LIMIT = 56 * 1024 * 1024


def _dot(a, b):
    return jnp.dot(a.astype(BF16), b.astype(BF16), preferred_element_type=F32)


def _dot_nt(a, b):
    return lax.dot_general(a.astype(BF16), b.astype(BF16), (((1,), (1,)), ((), ())),
                           preferred_element_type=F32)


def _dot_tn(a, b):
    return lax.dot_general(a.astype(BF16), b.astype(BF16), (((0,), (0,)), ((), ())),
                           preferred_element_type=F32)


def _dot_exact(a, b):
    return jnp.dot(a, b, preferred_element_type=F32, precision=lax.Precision.HIGHEST)


def _layernorm(x, g, b, eps=1e-5):
    mu = jnp.mean(x, -1, keepdims=True)
    xc = x - mu
    var = jnp.mean(xc * xc, -1, keepdims=True)
    return xc * lax.rsqrt(var + eps) * g + b


def _silu(x):
    return x * (1.0 / (1.0 + jnp.exp(-x)))


def _sigmoid(x):
    return 1.0 / (1.0 + jnp.exp(-x))


def _softplus(x):
    return jnp.maximum(x, 0.0) + jnp.log(1.0 + jnp.exp(-jnp.abs(x)))


def _idiv(x, n):
    return jnp.right_shift(x, int(math.log2(n)))


def _params(*sem):
    return pltpu.CompilerParams(dimension_semantics=sem, vmem_limit_bytes=VMEM_LIMIT)


def _in_proj_kernel(x_ref, g_ref, b_ref, w_ref, ws_ref, wst_ref,
                    h_ref, hb_ref, small_ref, smallt_ref, proj_ref, hs_ref):
    @pl.when(pl.program_id(1) == 0)
    def _():
        h = _layernorm(x_ref[...], g_ref[...], b_ref[...])
        h_ref[...] = h
        hb = h.astype(BF16)
        hb_ref[...] = hb
        hs_ref[...] = hb
        small_ref[...] = jnp.dot(hb, ws_ref[...], preferred_element_type=F32)
        smallt_ref[...] = lax.dot_general(wst_ref[...], hb, (((1,), (1,)), ((), ())),
                                          preferred_element_type=F32)

    proj_ref[...] = jnp.dot(hs_ref[...], w_ref[...], preferred_element_type=F32).astype(BF16)


def _in_proj(x, ln_g, ln_b, w_main, w_small, w_small_t, tm, tn=2048):
    n_tok = x.shape[0]
    grid = (n_tok // tm, PROJ_W // tn)
    return pl.pallas_call(
        _in_proj_kernel,
        grid=grid,
        in_specs=[
            pl.BlockSpec((tm, D_MODEL), lambda i, n: (i, 0)),
            pl.BlockSpec((1, D_MODEL), lambda i, n: (0, 0)),
            pl.BlockSpec((1, D_MODEL), lambda i, n: (0, 0)),
            pl.BlockSpec((D_MODEL, tn), lambda i, n: (0, n)),
            pl.BlockSpec((D_MODEL, LANES), lambda i, n: (0, 0)),
            pl.BlockSpec((16, D_MODEL), lambda i, n: (0, 0)),
        ],
        out_specs=[
            pl.BlockSpec((tm, D_MODEL), lambda i, n: (i, 0)),
            pl.BlockSpec((tm, D_MODEL), lambda i, n: (i, 0)),
            pl.BlockSpec((tm, LANES), lambda i, n: (i, 0)),
            pl.BlockSpec((16, tm), lambda i, n: (0, i)),
            pl.BlockSpec((tm, tn), lambda i, n: (i, n)),
        ],
        out_shape=[
            jax.ShapeDtypeStruct((n_tok, D_MODEL), F32),
            jax.ShapeDtypeStruct((n_tok, D_MODEL), BF16),
            jax.ShapeDtypeStruct((n_tok, LANES), F32),
            jax.ShapeDtypeStruct((16, n_tok), F32),
            jax.ShapeDtypeStruct((n_tok, PROJ_W), BF16),
        ],
        scratch_shapes=[pltpu.VMEM((tm, D_MODEL), BF16)],
        compiler_params=_params("parallel", "arbitrary"),
        name="in_proj",
    )(x, ln_g, ln_b, w_main, w_small, w_small_t)


def _ret_kernel(q_ref, k_ref, v_ref, rg_ref, cos_ref, sin_ref, s0_ref, gng_ref, gnb_ref,
                o_ref, sout_ref, s_scr, dec_scr, *, chunk):
    heads = range(RET_HEADS)
    lg = [math.log(1.0 - 2.0 ** (-5.0 - h)) for h in heads]

    @pl.when(pl.program_id(1) == 0)
    def _():
        s_scr[...] = s0_ref[0]
        ri = lax.broadcasted_iota(jnp.int32, (chunk, chunk), 0)
        ci = lax.broadcasted_iota(jnp.int32, (chunk, chunk), 1)
        causal = ri >= ci
        diff = jnp.where(causal, (ri - ci).astype(F32), 0.0)
        for h in heads:
            dec_scr[h] = jnp.where(causal, jnp.exp(diff * lg[h]), 0.0)

    cos = cos_ref[...]
    sin = sin_ref[...]
    idx = lax.broadcasted_iota(jnp.int32, (chunk, 1), 0).astype(F32)
    q = [q_ref[0, :, h * RET_DK:(h + 1) * RET_DK].astype(F32) for h in heads]
    k = [k_ref[0, :, h * RET_DK:(h + 1) * RET_DK].astype(F32) for h in heads]
    v = [v_ref[0, :, h * RET_DV:(h + 1) * RET_DV] for h in heads]
    qr = [q[h] * cos + pltpu.roll(q[h], RET_DK // 2, 1) * sin for h in heads]
    kr = [(k[h] * cos + pltpu.roll(k[h], RET_DK // 2, 1) * sin) * (RET_DK ** -0.5) for h in heads]
    scores = [_dot_nt(qr[h], kr[h]) * dec_scr[h] for h in heads]
    s = [s_scr[h] for h in heads]
    cross = [_dot(qr[h], s[h]) * jnp.exp((idx + 1.0) * lg[h]) for h in heads]
    o = [_dot(scores[h], v[h]) + cross[h] for h in heads]
    kv = [_dot_tn(kr[h] * jnp.exp((chunk - 1.0 - idx) * lg[h]), v[h]) for h in heads]
    for h in heads:
        s_new = s[h] * math.exp(chunk * lg[h]) + kv[h]
        s_scr[h] = s_new
        sout_ref[0, h] = s_new
        sl = slice(h * RET_DV, (h + 1) * RET_DV)
        gated = _layernorm(o[h], gng_ref[:, sl], gnb_ref[:, sl]) * _silu(rg_ref[0, :, sl].astype(F32))
        o_ref[0, :, sl] = gated.astype(BF16)


def _retention(proj, cos2, sin2, s0, gn_g, gn_b, chunk):
    nb, t = proj.shape[:2]
    grid = (nb, t // chunk)
    qk_w = RET_HEADS * RET_DK
    v_w = RET_HEADS * RET_DV
    return pl.pallas_call(
        functools.partial(_ret_kernel, chunk=chunk),
        grid=grid,
        in_specs=[
            pl.BlockSpec((1, chunk, qk_w), lambda b, c: (b, c, 0)),
            pl.BlockSpec((1, chunk, qk_w), lambda b, c: (b, c, 512 // qk_w)),
            pl.BlockSpec((1, chunk, v_w), lambda b, c: (b, c, 1024 // v_w)),
            pl.BlockSpec((1, chunk, v_w), lambda b, c: (b, c, 2048 // v_w)),
            pl.BlockSpec((chunk, RET_DK), lambda b, c: (c, 0)),
            pl.BlockSpec((chunk, RET_DK), lambda b, c: (c, 0)),
            pl.BlockSpec((1, RET_HEADS, RET_DK, RET_DV), lambda b, c: (b, 0, 0, 0)),
            pl.BlockSpec((1, v_w), lambda b, c: (0, 0)),
            pl.BlockSpec((1, v_w), lambda b, c: (0, 0)),
        ],
        out_specs=[
            pl.BlockSpec((1, chunk, v_w), lambda b, c: (b, c, 0)),
            pl.BlockSpec((1, RET_HEADS, RET_DK, RET_DV), lambda b, c: (b, 0, 0, 0)),
        ],
        out_shape=[
            jax.ShapeDtypeStruct((nb, t, v_w), BF16),
            jax.ShapeDtypeStruct((nb, RET_HEADS, RET_DK, RET_DV), F32),
        ],
        scratch_shapes=[pltpu.VMEM((RET_HEADS, RET_DK, RET_DV), F32),
                        pltpu.VMEM((RET_HEADS, chunk, chunk), F32)],
        compiler_params=_params("parallel", "arbitrary"),
        name="retention",
    )(proj, proj, proj, proj, cos2, sin2, s0, gn_g, gn_b)


def _dn_kernel(xqk_ref, xv_ref, z_ref, sm_ref, smt_ref, cw_ref, cinit_ref, prow_ref, pcol_ref, ng_ref, s0_ref,
               o_ref, sout_ref, cout_ref, xext, s_scr, *, blk_len, chunk):
    L = blk_len
    nbb = xqk_ref.shape[0]
    blk = pl.program_id(1)

    @pl.when(blk == 0)
    def _():
        xext[:, 0:8, :] = cinit_ref[...]
        s_scr[...] = s0_ref[...]

    ri = lax.broadcasted_iota(jnp.int32, (L, L), 0)
    ci = lax.broadcasted_iota(jnp.int32, (L, L), 1)
    same_chunk = _idiv(ri, chunk) == _idiv(ci, chunk)
    lower = jnp.logical_and(ri >= ci, same_chunk)
    strict = jnp.logical_and(ri > ci, same_chunk)
    same_sub = _idiv(ri, DN_SUB) == _idiv(ci, DN_SUB)
    tri = jnp.where(lower, 1.0, 0.0).astype(F32)
    tri_t = jnp.where(jnp.logical_and(ri <= ci, same_chunk), 1.0, 0.0).astype(F32)
    prow = prow_ref[...]
    pcol = pcol_ref[...]

    conv, beta_tm, gcum_tm, gcum_hm, egc_tm = [], [], [], [], []
    for b in range(nbb):
        x = jnp.concatenate([xqk_ref[b], xv_ref[b]], axis=1).astype(F32)
        xext[b, 8:8 + L, :] = x
        cv = x * cw_ref[3:4, :]
        for w in range(DN_CONV_W - 1):
            cv = cv + xext[b, 5 + w:5 + w + L, :] * cw_ref[w:w + 1, :]
        conv.append(_silu(cv))
        xext[b, 0:8, :] = x[L - 8:L, :]
        cout_ref[b] = x[L - (DN_CONV_W - 1):L, :]
        sm = sm_ref[b]
        beta_tm.append(_sigmoid(sm))
        g_tm = prow[0:1, :] * _softplus(sm + prow[1:2, :])
        g_hm = pcol[8:16, 0:1] * _softplus(smt_ref[b][8:16, :] + pcol[8:16, 1:2])
        gcum_tm.append(_dot_exact(tri, g_tm))
        gcum_hm.append(_dot_exact(g_hm, tri_t))
        egc_tm.append(jnp.exp(gcum_tm[b]))

    n_sub = L // chunk
    rep = DN_V_HEADS // DN_QK_HEADS
    bdot = lambda x, y: jnp.dot(x, y, preferred_element_type=F32)
    for h0 in range(0, DN_V_HEADS, DN_HEAD_GROUP):
        heads = [(b, hh) for b in range(nbb) for hh in range(h0, h0 + DN_HEAD_GROUP)]
        q, k, kk, qk = {}, {}, {}, {}
        for b in range(nbb):
            for j in range(h0 // rep, (h0 + DN_HEAD_GROUP) // rep):
                qj = conv[b][:, j * DN_DK:(j + 1) * DN_DK]
                kj = conv[b][:, DN_QK + j * DN_DK:DN_QK + (j + 1) * DN_DK]
                qj = qj * lax.rsqrt(jnp.sum(qj * qj, -1, keepdims=True) + 1e-6) * (DN_DK ** -0.5)
                kj = kj * lax.rsqrt(jnp.sum(kj * kj, -1, keepdims=True) + 1e-6)
                kq = _dot_nt(jnp.concatenate([kj, qj], axis=0), kj)
                q[b, j], k[b, j], kk[b, j], qk[b, j] = qj, kj, kq[0:L], kq[L:2 * L]
        gc = {(b, hh): gcum_tm[b][:, 8 + hh:9 + hh] for b, hh in heads}
        beta = {(b, hh): beta_tm[b][:, hh:hh + 1] for b, hh in heads}
        eg = {(b, hh): egc_tm[b][:, 8 + hh:9 + hh] for b, hh in heads}
        rel = {(b, hh): jnp.where(lower, jnp.exp(jnp.where(lower, gc[b, hh] - gcum_hm[b][hh:hh + 1, :], 0.0)),
                                  0.0) for b, hh in heads}
        a = {(b, hh): jnp.where(strict, beta[b, hh] * kk[b, hh // rep] * rel[b, hh], 0.0) for b, hh in heads}
        attn = {(b, hh): qk[b, hh // rep] * rel[b, hh] for b, hh in heads}
        pd = {hh: jnp.where(same_sub, -a[hh], 0.0) for hh in heads}
        e = {hh: a[hh] + pd[hh] for hh in heads}
        eb = {hh: e[hh].astype(BF16) for hh in heads}
        xacc = pd
        xaccb = {hh: pd[hh].astype(BF16) for hh in heads}
        pwb = xaccb
        for _ in range(3):
            pw = {hh: bdot(pwb[hh], pwb[hh]) for hh in heads}
            pwb = {hh: pw[hh].astype(BF16) for hh in heads}
            xacc = {hh: xacc[hh] + pw[hh] + bdot(xaccb[hh], pwb[hh]) for hh in heads}
            xaccb = {hh: xacc[hh].astype(BF16) for hh in heads}
        f = {hh: e[hh] + bdot(xaccb[hh], eb[hh]) for hh in heads}
        fb = {hh: f[hh].astype(BF16) for hh in heads}
        f2 = {hh: bdot(fb[hh], fb[hh]) for hh in heads}
        y = {hh: f2[hh] - f[hh] - bdot(fb[hh], f2[hh].astype(BF16)) for hh in heads}
        rinv = {hh: xacc[hh] + y[hh] + bdot(y[hh].astype(BF16), xaccb[hh]) for hh in heads}
        sol = {}
        for b, hh in heads:
            vh = conv[b][:, 2 * DN_QK + hh * DN_DV:2 * DN_QK + (hh + 1) * DN_DV]
            rhs = jnp.concatenate([vh * beta[b, hh], k[b, hh // rep] * (beta[b, hh] * eg[b, hh])], axis=1)
            sol[b, hh] = rhs + _dot(rinv[b, hh], rhs)
        u = {p: sol[p][:, 0:DN_DV] for p in heads}
        wmat = {p: sol[p][:, DN_DV:DN_DV + DN_DK] for p in heads}
        qe = {(b, hh): q[b, hh // rep] * eg[b, hh] for b, hh in heads}
        s = {(b, hh): s_scr[b, hh] for b, hh in heads}
        v_new = {p: [] for p in heads}
        q_s = {p: [] for p in heads}
        for sc in range(n_sub):
            r0 = sc * chunk
            for b, hh in heads:
                p = (b, hh)
                ws = _dot(jnp.concatenate([wmat[p][r0:r0 + chunk], qe[p][r0:r0 + chunk]], axis=0), s[p])
                vn = u[p][r0:r0 + chunk] - ws[0:chunk]
                v_new[p].append(vn)
                q_s[p].append(ws[chunk:2 * chunk])
                g_last = gcum_tm[b][r0 + chunk - 1:r0 + chunk, 8 + hh:9 + hh]
                kd = k[b, hh // rep][r0:r0 + chunk] * jnp.exp(g_last - gc[p][r0:r0 + chunk])
                s[p] = s[p] * jnp.exp(g_last) + _dot_tn(kd, vn)
        for b, hh in heads:
            p = (b, hh)
            s_scr[b, hh] = s[p]
            sout_ref[b, hh] = s[p]
            vn = v_new[p][0] if n_sub == 1 else jnp.concatenate(v_new[p], axis=0)
            qs = q_s[p][0] if n_sub == 1 else jnp.concatenate(q_s[p], axis=0)
            o = qs + _dot(attn[p], vn)
            zh = z_ref[b, :, hh * DN_DV:(hh + 1) * DN_DV].astype(F32)
            n = o * lax.rsqrt(jnp.mean(o * o, -1, keepdims=True) + 1e-6) * ng_ref[...]
            o_ref[b, :, hh * DN_DV:(hh + 1) * DN_DV] = (n * _silu(zh)).astype(BF16)


def _deltanet(proj, small, small_t, conv_w, conv_init, prow, pcol, norm_g, s0, blk_len, chunk):
    nb, t = proj.shape[:2]
    nblk = t // blk_len
    nbb = DN_STREAMS
    grid = (nb // nbb, nblk)
    return pl.pallas_call(
        functools.partial(_dn_kernel, blk_len=blk_len, chunk=chunk),
        grid=grid,
        in_specs=[
            pl.BlockSpec((nbb, blk_len, 1024), lambda b, i: (b, i, 3072 // 1024)),
            pl.BlockSpec((nbb, blk_len, 1024), lambda b, i: (b, i, 4096 // 1024)),
            pl.BlockSpec((nbb, blk_len, 1024), lambda b, i: (b, i, 5120 // 1024)),
            pl.BlockSpec((nbb, blk_len, LANES), lambda b, i: (b, i, 0)),
            pl.BlockSpec((nbb, 16, blk_len), lambda b, i: (b, 0, i)),
            pl.BlockSpec((DN_CONV_W, DN_CONV_CH), lambda b, i: (0, 0)),
            pl.BlockSpec((nbb, 8, DN_CONV_CH), lambda b, i: (b, 0, 0)),
            pl.BlockSpec((8, LANES), lambda b, i: (0, 0)),
            pl.BlockSpec((16, LANES), lambda b, i: (0, 0)),
            pl.BlockSpec((1, DN_DV), lambda b, i: (0, 0)),
            pl.BlockSpec((nbb, DN_V_HEADS, DN_DK, DN_DV), lambda b, i: (b, 0, 0, 0)),
        ],
        out_specs=[
            pl.BlockSpec((nbb, blk_len, 1024), lambda b, i: (b, i, 0)),
            pl.BlockSpec((nbb, DN_V_HEADS, DN_DK, DN_DV), lambda b, i: (b, 0, 0, 0)),
            pl.BlockSpec((nbb, DN_CONV_W - 1, DN_CONV_CH), lambda b, i: (b, 0, 0)),
        ],
        out_shape=[
            jax.ShapeDtypeStruct((nb, t, 1024), BF16),
            jax.ShapeDtypeStruct((nb, DN_V_HEADS, DN_DK, DN_DV), F32),
            jax.ShapeDtypeStruct((nb, DN_CONV_W - 1, DN_CONV_CH), F32),
        ],
        scratch_shapes=[pltpu.VMEM((nbb, blk_len + 8, DN_CONV_CH), F32),
                        pltpu.VMEM((nbb, DN_V_HEADS, DN_DK, DN_DV), F32)],
        compiler_params=_params("parallel", "arbitrary"),
        name="deltanet",
    )(proj, proj, proj, small, small_t, conv_w, conv_init, prow, pcol, norm_g, s0)


def _memkv_kernel(m_ref, w_ref, o_ref):
    o_ref[...] = _dot(m_ref[...], w_ref[...])


def _memkv(mem, w_kv):
    n = mem.shape[0]
    tn = 1024
    return pl.pallas_call(
        _memkv_kernel,
        grid=(w_kv.shape[1] // tn,),
        in_specs=[pl.BlockSpec((n, D_MODEL), lambda j: (0, 0)),
                  pl.BlockSpec((D_MODEL, tn), lambda j: (0, j))],
        out_specs=pl.BlockSpec((n, tn), lambda j: (0, j)),
        out_shape=jax.ShapeDtypeStruct((n, w_kv.shape[1]), F32),
        compiler_params=_params("parallel"),
        name="memkv",
    )(mem, w_kv)


def _mix_kernel(h_ref, hb_ref, ret_ref, dn_ref, mk_ref, mv_ref, wxq_ref, wg_ref, wb_ref, wo_ref,
                g1_ref, b1_ref, h1_ref, h1b_ref, mem_scr, *, n_batch, tiles_per_batch):
    @pl.when(pl.program_id(0) % tiles_per_batch == 0)
    def _():
        for b in range(n_batch):
            for hh in range(XA_HEADS):
                mem_scr[0, hh, b] = mk_ref[0, b, :, hh, :].astype(BF16)
                mem_scr[1, hh, b] = mv_ref[0, b, :, hh, :].astype(BF16)

    hb = hb_ref[...]
    xq = jnp.dot(hb, wxq_ref[...], preferred_element_type=F32)
    tb = xq.shape[0] // n_batch
    pairs = [(b, hh) for b in range(n_batch) for hh in range(XA_HEADS)]
    s = [_dot_nt(xq[b * tb:(b + 1) * tb, hh * XA_DH:(hh + 1) * XA_DH], mem_scr[0, hh, b]) * (XA_DH ** -0.5)
         for b, hh in pairs]
    p = [jnp.exp(si - jnp.max(si, -1, keepdims=True)) for si in s]
    p = [pi / jnp.sum(pi, -1, keepdims=True) for pi in p]
    xo = [_dot(pi, mem_scr[1, hh, b]) for pi, (b, hh) in zip(p, pairs)]
    xo = [jnp.concatenate(xo[b * XA_HEADS:(b + 1) * XA_HEADS], axis=1) for b in range(n_batch)]
    xo = xo[0] if n_batch == 1 else jnp.concatenate(xo, axis=0)
    mixed = None
    for n, br in enumerate((ret_ref[...], dn_ref[...], xo)):
        gate = _sigmoid(jnp.dot(hb, wg_ref[:, n * D_MODEL:(n + 1) * D_MODEL], preferred_element_type=F32))
        term = gate * _dot(br, wb_ref[n])
        mixed = term if mixed is None else mixed + term
    y = ALPHA * h_ref[...] + _dot(mixed, wo_ref[...])
    h1 = _layernorm(y, g1_ref[...], b1_ref[...])
    h1_ref[...] = h1
    h1b_ref[...] = h1.astype(BF16)


def _mix(h, hb, ret_o, dn_o, mem_k, mem_v, w_xq, w_gate, w_branch, w_out, ln_g, ln_b, tm, t):
    n_tok = h.shape[0]
    const2 = lambda i: (0, 0)
    row = lambda i: (i, 0)
    single = pl.Buffered(1)
    if tm <= t:
        n_batch = 1
        tiles_per_batch = t // tm
    else:
        n_batch = tm // t
        tiles_per_batch = 1
    mem_spec = pl.BlockSpec((1, n_batch, N_MEM, XA_HEADS, XA_DH), lambda i: (0, i // tiles_per_batch, 0, 0, 0))
    return pl.pallas_call(
        functools.partial(_mix_kernel, n_batch=n_batch, tiles_per_batch=tiles_per_batch),
        grid=(n_tok // tm,),
        in_specs=[
            pl.BlockSpec((tm, D_MODEL), row),
            pl.BlockSpec((tm, D_MODEL), row),
            pl.BlockSpec((tm, D_MODEL), row),
            pl.BlockSpec((tm, D_MODEL), row),
            mem_spec,
            mem_spec,
            pl.BlockSpec((D_MODEL, D_MODEL), const2, pipeline_mode=single),
            pl.BlockSpec((D_MODEL, 3 * D_MODEL), const2, pipeline_mode=single),
            pl.BlockSpec((3, D_MODEL, D_MODEL), lambda i: (0, 0, 0), pipeline_mode=single),
            pl.BlockSpec((D_MODEL, D_MODEL), const2, pipeline_mode=single),
            pl.BlockSpec((1, D_MODEL), const2),
            pl.BlockSpec((1, D_MODEL), const2),
        ],
        out_specs=[pl.BlockSpec((tm, D_MODEL), row), pl.BlockSpec((tm, D_MODEL), row)],
        out_shape=[jax.ShapeDtypeStruct((n_tok, D_MODEL), F32),
                   jax.ShapeDtypeStruct((n_tok, D_MODEL), BF16)],
        scratch_shapes=[pltpu.VMEM((2, XA_HEADS, n_batch, N_MEM, XA_DH), BF16)],
        compiler_params=_params("arbitrary"),
        name="mix",
    )(h, hb, ret_o, dn_o, mem_k, mem_v, w_xq, w_gate, w_branch, w_out, ln_g, ln_b)


def _route_t(x, wrt, brt):
    xh = x.astype(BF16)
    xl = (x - xh.astype(F32)).astype(BF16)
    wh = wrt.astype(BF16)
    wl = (wrt - wh.astype(F32)).astype(BF16)
    nt = lambda a, b: lax.dot_general(a, b, (((1,), (1,)), ((), ())), preferred_element_type=F32)
    logits = nt(wh, xh) + (nt(wl, xh) + nt(wh, xl)) + brt
    row_i = lax.broadcasted_iota(jnp.int32, logits.shape, 0)
    row = row_i.astype(F32)
    far = jnp.float32(LANES)
    neg = jnp.float32(-3.0e38)
    is_c = jnp.logical_and(row_i >= MOE_NE, row_i < MOE_NE + MOE_GROUPS)
    cl = jnp.where(is_c, logits, neg)
    cmax = jnp.max(cl, 0, keepdims=True)
    denom = jnp.sum(jnp.where(is_c, jnp.exp(jnp.where(is_c, logits - cmax, 0.0)), 0.0), 0, keepdims=True)
    p_grp = 1.0 / denom
    grp = jnp.min(jnp.where(jnp.logical_and(is_c, cl == cmax), row - MOE_NE, far), 0, keepdims=True)
    in_grp = jnp.logical_and(row_i < MOE_NE, _idiv(row_i, MOE_EXPERTS).astype(F32) == grp)
    fl = jnp.where(in_grp, logits, neg)
    v1 = jnp.max(fl, 0, keepdims=True)
    i1 = jnp.min(jnp.where(jnp.logical_and(in_grp, fl == v1), row, far), 0, keepdims=True)
    rest = jnp.logical_and(in_grp, row != i1)
    fl2 = jnp.where(rest, logits, neg)
    v2 = jnp.max(fl2, 0, keepdims=True)
    i2 = jnp.min(jnp.where(jnp.logical_and(rest, fl2 == v2), row, far), 0, keepdims=True)
    e2 = jnp.exp(v2 - v1)
    w1 = p_grp / (1.0 + e2)
    w2 = p_grp * e2 / (1.0 + e2)
    comb_t = jnp.where(row == i1, w1, 0.0) + jnp.where(row == i2, w2, 0.0)
    return jnp.where(row_i == LANES - 1, grp, comb_t)


def _route_kernel(h1_ref, wrt_ref, brt_ref, comb_ref, cnt_ref):
    comb = _route_t(h1_ref[...], wrt_ref[...], brt_ref[...]).T
    comb_ref[...] = comb
    lane = lax.broadcasted_iota(jnp.int32, comb.shape, 1).astype(F32)
    onehot = jnp.where(lane == comb[:, LANES - 1:LANES], 1.0, 0.0)
    cnt_ref[0] = jnp.broadcast_to(jnp.sum(onehot, 0, keepdims=True), (8, LANES))


def _route_call(h1, w_route, b_route, tm):
    n_tok = h1.shape[0]
    return pl.pallas_call(
        _route_kernel,
        grid=(n_tok // tm,),
        in_specs=[pl.BlockSpec((tm, D_MODEL), lambda i: (i, 0)),
                  pl.BlockSpec((LANES, D_MODEL), lambda i: (0, 0)),
                  pl.BlockSpec((LANES, 1), lambda i: (0, 0))],
        out_specs=[pl.BlockSpec((tm, LANES), lambda i: (i, 0)),
                   pl.BlockSpec((1, 8, LANES), lambda i: (i, 0, 0))],
        out_shape=[jax.ShapeDtypeStruct((n_tok, LANES), F32),
                   jax.ShapeDtypeStruct((n_tok // tm, 8, LANES), F32)],
        compiler_params=_params("parallel"),
        name="route",
    )(h1, w_route, b_route)


MOE_ALIGN = 16
MOE_EPS = 4


def _moe_sizes(tm):
    base = tm // MOE_GROUPS
    fine = {min(tm, base + MOE_ALIGN * i) for i in range(5)}
    return tuple(sorted(fine | {tm // 2, tm}))


def _moe_kernel(offs_ref, cls_ref, h1_ref, h1b_ref, comb_ref, tri_ref, wg_ref, wu_ref, wd_ref,
                g2_ref, b2_ref, y_ref, pt_scr, xs_scr, cs_scr, ys_scr, *, tm, nr1, sizes):
    i = pl.program_id(0)
    e = pl.program_id(1)
    nrt = xs_scr.shape[0]

    @pl.when(e == 0)
    def _():
        comb = comb_ref[...]
        lane_i = lax.broadcasted_iota(jnp.int32, (tm, LANES), 1)
        grp = comb[:, LANES - 1:LANES]
        is_g = lane_i.astype(F32) == grp
        prefix = jnp.dot(tri_ref[...], jnp.where(is_g, 1.0, 0.0).astype(BF16), preferred_element_type=F32)
        rank = jnp.sum(jnp.where(is_g, prefix, 0.0), -1, keepdims=True)
        offv = jnp.zeros((tm, 1), F32)
        for g in range(MOE_GROUPS):
            offv = jnp.where(grp == g, offs_ref[i * MOE_GROUPS + g].astype(F32), offv)
        pos = offv + rank
        col = lax.broadcasted_iota(jnp.int32, (tm, nr1), 1).astype(F32)
        pt = jnp.where(col == pos, 1.0, 0.0).astype(BF16)
        pt_scr[...] = pt
        combw = jnp.where(lane_i < MOE_NE, comb, 0.0)
        c_hi = combw.astype(BF16)
        c_lo = (combw - c_hi.astype(F32)).astype(BF16)
        xs_scr[0:nr1, :] = _dot_tn(pt, h1b_ref[...]).astype(BF16)
        c_pair = _dot_tn(pt, jnp.concatenate([c_hi, c_lo], axis=1))
        cs_scr[0:nr1, :] = c_pair[:, 0:LANES] + c_pair[:, LANES:2 * LANES]
        xs_scr[nr1:nrt, :] = jnp.zeros((nrt - nr1, D_MODEL), BF16)
        cs_scr[nr1:nrt, :] = jnp.zeros((nrt - nr1, LANES), F32)
        ys_scr[tm:nr1, :] = jnp.zeros((nr1 - tm, D_MODEL), F32)

    g = e // (MOE_EXPERTS // MOE_EPS)
    off = pl.multiple_of(offs_ref[i * MOE_GROUPS + g], MOE_ALIGN)
    cls = cls_ref[i * MOE_GROUPS + g]
    first = e % (MOE_EXPERTS // MOE_EPS) == 0
    for ci, m in enumerate(sizes):
        @pl.when(cls == ci)
        def _(m=m):
            rows = pl.ds(off, m)
            xb = xs_scr[rows, :]
            cb = cs_scr[rows, :]
            lane_i = lax.broadcasted_iota(jnp.int32, (m, LANES), 1)
            sub = range(MOE_EPS)
            colw = [jnp.sum(jnp.where(lane_i == e * MOE_EPS + j, cb, 0.0), -1, keepdims=True) for j in sub]
            gate = [jnp.dot(xb, wg_ref[j], preferred_element_type=F32) for j in sub]
            up = [jnp.dot(xb, wu_ref[j], preferred_element_type=F32) for j in sub]
            hg = [(_silu(gate[j]) * up[j] * colw[j]).astype(BF16) for j in sub]
            wd = wd_ref[...].reshape(MOE_EPS * MOE_DFF, D_MODEL)
            res = jnp.dot(jnp.concatenate(hg, axis=1), wd, preferred_element_type=F32)

            @pl.when(first)
            def _():
                ys_scr[rows, :] = res

            @pl.when(jnp.logical_not(first))
            def _():
                ys_scr[rows, :] += res

    @pl.when(e == MOE_NE // MOE_EPS - 1)
    def _():
        moe = jnp.dot(pt_scr[...], ys_scr[0:nr1, :].astype(BF16), preferred_element_type=F32)
        y_ref[...] = _layernorm(ALPHA * h1_ref[...] + moe, g2_ref[...], b2_ref[...])


def _moe(h1, h1b, comb, cnt, w_gate, w_up, w_down, ln_g, ln_b, tm):
    n_tok = h1.shape[0]
    sizes = _moe_sizes(tm)
    nr1 = -(-(tm + MOE_GROUPS * MOE_ALIGN) // LANES) * LANES
    nrt = nr1 + tm
    c = cnt[:, 0, :MOE_GROUPS].astype(jnp.int32)
    seg = (c + MOE_ALIGN - 1) // MOE_ALIGN * MOE_ALIGN
    offs = (jnp.cumsum(seg, axis=1) - seg).reshape(-1)
    cls = sum((c > s).astype(jnp.int32) for s in sizes[:-1]).reshape(-1)
    tri =jnp.asarray(np.tril(np.ones((tm, tm), np.float32), -1), BF16)
    grid_spec = pltpu.PrefetchScalarGridSpec(
        num_scalar_prefetch=2,
        grid=(n_tok // tm, MOE_NE // MOE_EPS),
        in_specs=[
            pl.BlockSpec((tm, D_MODEL), lambda i, e, *_: (i, 0), pipeline_mode=pl.Buffered(1)),
            pl.BlockSpec((tm, D_MODEL), lambda i, e, *_: (i, 0)),
            pl.BlockSpec((tm, LANES), lambda i, e, *_: (i, 0)),
            pl.BlockSpec((tm, tm), lambda i, e, *_: (0, 0), pipeline_mode=pl.Buffered(1)),
            pl.BlockSpec((MOE_EPS, D_MODEL, MOE_DFF), lambda i, e, *_: (e, 0, 0)),
            pl.BlockSpec((MOE_EPS, D_MODEL, MOE_DFF), lambda i, e, *_: (e, 0, 0)),
            pl.BlockSpec((MOE_EPS, MOE_DFF, D_MODEL), lambda i, e, *_: (e, 0, 0)),
            pl.BlockSpec((1, D_MODEL), lambda i, e, *_: (0, 0)),
            pl.BlockSpec((1, D_MODEL), lambda i, e, *_: (0, 0)),
        ],
        out_specs=pl.BlockSpec((tm, D_MODEL), lambda i, e, *_: (i, 0)),
        scratch_shapes=[pltpu.VMEM((tm, nr1), BF16), pltpu.VMEM((nrt, D_MODEL), BF16),
                        pltpu.VMEM((nrt, LANES), F32), pltpu.VMEM((nrt, D_MODEL), F32)],
    )
    return pl.pallas_call(
        functools.partial(_moe_kernel, tm=tm, nr1=nr1, sizes=sizes),
        grid_spec=grid_spec,
        out_shape=jax.ShapeDtypeStruct((n_tok, D_MODEL), F32),
        compiler_params=_params("parallel", "arbitrary"),
        name="moe",
    )(offs, cls, h1, h1b, comb, tri, w_gate, w_up, w_down, ln_g, ln_b)


def _rope_tables(start, t):
    half = RET_DK // 2
    inv = 1.0 / (ROPE_BASE ** (np.arange(half, dtype=np.float64) / half))
    ang = (start + np.arange(t, dtype=np.float64))[:, None] * inv[None, :]
    cos = np.cos(ang)
    sin = np.sin(ang)
    return (jnp.asarray(np.concatenate([cos, cos], -1), F32),
            jnp.asarray(np.concatenate([-sin, sin], -1), F32))


def _group(x, pos0, mem_k, mem_v, ret_s0, dn_s0, conv_buf, wts, *, tm, tm_mix, tm_moe, ret_chunk, dn_blk,
           dn_chunk):
    nb, t, d = x.shape
    n_tok = nb * t
    h, hb, small, small_t, proj = _in_proj(x.reshape(n_tok, d), wts["ln_in_g"], wts["ln_in_b"],
                                           wts["w_in"], wts["w_small"], wts["w_small_t"], tm)
    proj3 = proj.reshape(nb, t, PROJ_W)
    cos2, sin2 = _rope_tables(pos0, t)
    ret_o, ret_s = _retention(proj3, cos2, sin2, ret_s0, wts["ret_gn_g"], wts["ret_gn_b"], ret_chunk)
    small_t3 = small_t.reshape(16, nb, t).transpose(1, 0, 2)
    conv_init = jnp.concatenate([jnp.zeros((nb, 8 - (DN_CONV_W - 1), DN_CONV_CH), F32), conv_buf], axis=1)
    dn_o, dn_s, conv_new = _deltanet(proj3, small.reshape(nb, t, LANES), small_t3, wts["dn_conv"], conv_init,
                                     wts["dn_prow"],
                                     wts["dn_pcol"], wts["dn_norm_g"], dn_s0, dn_blk, dn_chunk)
    h1, h1b = _mix(h, hb, ret_o.reshape(n_tok, d), dn_o.reshape(n_tok, d), mem_k, mem_v,
                   wts["w_xq"], wts["w_gate"], wts["w_branch"], wts["w_out"], wts["ln1_g"], wts["ln1_b"],
                   tm_mix, t)
    comb, cnt = _route_call(h1, wts["w_route"], wts["b_route"], tm_moe)
    y = _moe(h1, h1b, comb, cnt, wts["moe_w_gate"], wts["moe_w_up"], wts["moe_w_down"],
             wts["ln2_g"], wts["ln2_b"], tm_moe)
    return y.reshape(nb, t, d), ret_s, dn_s, conv_new


def kernel(x_prompt, x_sample, mem_prompt, state_ret, state_dn, state_dn_conv, cache_mem_k, cache_mem_v,
           ln_in_g, ln_in_b, w_in, ret_gn_g, ret_gn_b, dn_conv, dn_A_log, dn_dt_bias, dn_norm_g,
           w_mem_kv, w_branch, w_out, ln1_g, ln1_b, moe_w_coarse, moe_b_coarse, moe_w_fine, moe_b_fine,
           moe_w_gate, moe_w_up, moe_w_down, ln2_g, ln2_b):
    bp, tp, d = x_prompt.shape
    bs, ts, _ = x_sample.shape
    l = 0
    wi = w_in[l].astype(BF16)
    dba = wi[:, PROJ_W:PROJ_W + 16]
    xq, gates = wi[:, 6160:7184], wi[:, 7184:10256]
    wts = {
        "ln_in_g": ln_in_g.reshape(1, d), "ln_in_b": ln_in_b.reshape(1, d),
        "w_in": wi,
        "w_small": jnp.pad(dba, ((0, 0), (0, LANES - 16))),
        "w_small_t": dba.T,
        "ret_gn_g": ret_gn_g[l].reshape(1, -1), "ret_gn_b": ret_gn_b[l].reshape(1, -1),
        "dn_conv": dn_conv[l],
        "dn_prow": jnp.zeros((8, LANES), F32).at[0, 8:16].set(-jnp.exp(dn_A_log[l])).at[1, 8:16].set(dn_dt_bias[l]),
        "dn_pcol": jnp.zeros((16, LANES), F32).at[8:16, 0].set(-jnp.exp(dn_A_log[l])).at[8:16, 1].set(dn_dt_bias[l]),
        "dn_norm_g": dn_norm_g[l].reshape(1, -1),
        "w_xq": xq, "w_gate": gates,
        "w_branch": w_branch[l].astype(BF16), "w_out": w_out[l].astype(BF16),
        "ln1_g": ln1_g[l].reshape(1, d), "ln1_b": ln1_b[l].reshape(1, d),
        "w_route": jnp.pad(jnp.concatenate([moe_w_fine[l], moe_w_coarse[l]], axis=1).T,
                           ((0, LANES - MOE_NE - MOE_GROUPS), (0, 0))),
        "b_route": jnp.pad(jnp.concatenate([moe_b_fine[l], moe_b_coarse[l]]),
                           (0, LANES - MOE_NE - MOE_GROUPS)).reshape(LANES, 1),
        "moe_w_gate": moe_w_gate[l].reshape(MOE_NE, d, MOE_DFF).astype(BF16),
        "moe_w_up": moe_w_up[l].reshape(MOE_NE, d, MOE_DFF).astype(BF16),
        "moe_w_down": moe_w_down[l].reshape(MOE_NE, MOE_DFF, d).astype(BF16),
        "ln2_g": ln2_g[l].reshape(1, d), "ln2_b": ln2_b[l].reshape(1, d),
    }

    mkv = _memkv(mem_prompt.reshape(bp * N_MEM, d), w_mem_kv[l].astype(BF16))
    mk = mkv[:, :XA_HEADS * XA_DH].reshape(1, bp, N_MEM, XA_HEADS, XA_DH)
    mv = mkv[:, XA_HEADS * XA_DH:].reshape(1, bp, N_MEM, XA_HEADS, XA_DH)
    yp, rs_p, ds_p, cb_p = _group(
        x_prompt, 0, mk, mv,
        jnp.zeros((bp, RET_HEADS, RET_DK, RET_DV), F32),
        jnp.zeros((bp, DN_V_HEADS, DN_DK, DN_DV), F32),
        jnp.zeros((bp, DN_CONV_W - 1, DN_CONV_CH), F32),
        wts, tm=min(1024, tp), tm_mix=min(512, tp), tm_moe=min(1024, tp), ret_chunk=min(256, tp),
        dn_blk=min(128, tp),
        dn_chunk=min(DN_CHUNK, tp))
    ys, rs_s, ds_s, cb_s = _group(
        x_sample, PAST_LEN, cache_mem_k[l:l + 1], cache_mem_v[l:l + 1],
        state_ret[l], state_dn[l], state_dn_conv[l],
        wts, tm=bs * ts, tm_mix=bs * ts, tm_moe=bs * ts, ret_chunk=ts, dn_blk=ts, dn_chunk=min(DN_CHUNK, ts))
    return (yp, ys, rs_p[None], rs_s[None], ds_p[None], ds_s[None], cb_p[None], cb_s[None], mk, mv)
```

```python
import functools
import math

import numpy as np

import jax
import jax.numpy as jnp
from jax import lax
from jax.experimental import pallas as pl
from jax.experimental.pallas import tpu as pltpu

F32 = jnp.float32
BF16 = jnp.bfloat16

D_MODEL = 1024
PAST_LEN = 1024
RET_HEADS = 4
RET_DK = 128
RET_DV = 256
ROPE_BASE = 10000.0
DN_QK_HEADS = 4
DN_V_HEADS = 8
DN_DK = 128
DN_DV = 128
DN_QK = DN_QK_HEADS * DN_DK
DN_CONV_W = 4
DN_CONV_CH = 2048
DN_CHUNK = 64
DN_SUB = 16
DN_HEAD_GROUP = 8
DN_STREAMS = 2
DN_STREAM_ROWS = 256
XA_HEADS = 4
XA_DH = 256
N_MEM = 256
MOE_GROUPS = 4
MOE_EXPERTS = 8
MOE_NE = MOE_GROUPS * MOE_EXPERTS
MOE_DFF = 256
DEPTH = 1
ALPHA = (2.0 * DEPTH) ** 0.25
LANES = 128
PROJ_W = 6144
VMEM_LIMIT = 56 * 1024 * 1024


def _dot(a, b):
    return jnp.dot(a.astype(BF16), b.astype(BF16), preferred_element_type=F32)


def _dot_nt(a, b):
    return lax.dot_general(a.astype(BF16), b.astype(BF16), (((1,), (1,)), ((), ())),
                           preferred_element_type=F32)


def _dot_tn(a, b):
    return lax.dot_general(a.astype(BF16), b.astype(BF16), (((0,), (0,)), ((), ())),
                           preferred_element_type=F32)


def _dot_exact(a, b):
    return jnp.dot(a, b, preferred_element_type=F32, precision=lax.Precision.HIGHEST)


def _layernorm(x, g, b, eps=1e-5):
    mu = jnp.mean(x, -1, keepdims=True)
    xc = x - mu
    var = jnp.mean(xc * xc, -1, keepdims=True)
    return xc * lax.rsqrt(var + eps) * g + b


def _silu(x):
    return x * (1.0 / (1.0 + jnp.exp(-x)))


def _sigmoid(x):
    return 1.0 / (1.0 + jnp.exp(-x))


def _softplus(x):
    return jnp.maximum(x, 0.0) + jnp.log(1.0 + jnp.exp(-jnp.abs(x)))


def _idiv(x, n):
    return jnp.right_shift(x, int(math.log2(n)))


def _params(*sem):
    return pltpu.CompilerParams(dimension_semantics=sem, vmem_limit_bytes=VMEM_LIMIT)


def _in_proj_kernel(x_ref, g_ref, b_ref, w_ref, ws_ref, wst_ref,
                    h_ref, hb_ref, small_ref, smallt_ref, proj_ref, hs_ref):
    @pl.when(pl.program_id(1) == 0)
    def _():
        h = _layernorm(x_ref[...], g_ref[...], b_ref[...])
        h_ref[...] = h
        hb = h.astype(BF16)
        hb_ref[...] = hb
        hs_ref[...] = hb
        small_ref[...] = jnp.dot(hb, ws_ref[...], preferred_element_type=F32)
        smallt_ref[...] = lax.dot_general(wst_ref[...], hb, (((1,), (1,)), ((), ())),
                                          preferred_element_type=F32)

    proj_ref[...] = jnp.dot(hs_ref[...], w_ref[...], preferred_element_type=F32).astype(BF16)


def _in_proj(x, ln_g, ln_b, w_main, w_small, w_small_t, tm, tn=2048):
    n_tok = x.shape[0]
    grid = (n_tok // tm, PROJ_W // tn)
    return pl.pallas_call(
        _in_proj_kernel,
        grid=grid,
        in_specs=[
            pl.BlockSpec((tm, D_MODEL), lambda i, n: (i, 0)),
            pl.BlockSpec((1, D_MODEL), lambda i, n: (0, 0)),
            pl.BlockSpec((1, D_MODEL), lambda i, n: (0, 0)),
            pl.BlockSpec((D_MODEL, tn), lambda i, n: (0, n)),
            pl.BlockSpec((D_MODEL, LANES), lambda i, n: (0, 0)),
            pl.BlockSpec((16, D_MODEL), lambda i, n: (0, 0)),
        ],
        out_specs=[
            pl.BlockSpec((tm, D_MODEL), lambda i, n: (i, 0)),
            pl.BlockSpec((tm, D_MODEL), lambda i, n: (i, 0)),
            pl.BlockSpec((tm, LANES), lambda i, n: (i, 0)),
            pl.BlockSpec((16, tm), lambda i, n: (0, i)),
            pl.BlockSpec((tm, tn), lambda i, n: (i, n)),
        ],
        out_shape=[
            jax.ShapeDtypeStruct((n_tok, D_MODEL), F32),
            jax.ShapeDtypeStruct((n_tok, D_MODEL), BF16),
            jax.ShapeDtypeStruct((n_tok, LANES), F32),
            jax.ShapeDtypeStruct((16, n_tok), F32),
            jax.ShapeDtypeStruct((n_tok, PROJ_W), BF16),
        ],
        scratch_shapes=[pltpu.VMEM((tm, D_MODEL), BF16)],
        compiler_params=_params("parallel", "arbitrary"),
        name="in_proj",
    )(x, ln_g, ln_b, w_main, w_small, w_small_t)


def _ret_kernel(q_ref, k_ref, v_ref, rg_ref, cos_ref, sin_ref, s0_ref, gng_ref, gnb_ref,
                o_ref, sout_ref, s_scr, dec_scr, *, chunk):
    heads = range(RET_HEADS)
    lg = [math.log(1.0 - 2.0 ** (-5.0 - h)) for h in heads]

    @pl.when(pl.program_id(1) == 0)
    def _():
        s_scr[...] = s0_ref[0]
        ri = lax.broadcasted_iota(jnp.int32, (chunk, chunk), 0)
        ci = lax.broadcasted_iota(jnp.int32, (chunk, chunk), 1)
        causal = ri >= ci
        diff = jnp.where(causal, (ri - ci).astype(F32), 0.0)
        for h in heads:
            dec_scr[h] = jnp.where(causal, jnp.exp(diff * lg[h]), 0.0)

    cos = cos_ref[...]
    sin = sin_ref[...]
    idx = lax.broadcasted_iota(jnp.int32, (chunk, 1), 0).astype(F32)
    q = [q_ref[0, :, h * RET_DK:(h + 1) * RET_DK].astype(F32) for h in heads]
    k = [k_ref[0, :, h * RET_DK:(h + 1) * RET_DK].astype(F32) for h in heads]
    v = [v_ref[0, :, h * RET_DV:(h + 1) * RET_DV] for h in heads]
    qr = [q[h] * cos + pltpu.roll(q[h], RET_DK // 2, 1) * sin for h in heads]
    kr = [(k[h] * cos + pltpu.roll(k[h], RET_DK // 2, 1) * sin) * (RET_DK ** -0.5) for h in heads]
    scores = [_dot_nt(qr[h], kr[h]) * dec_scr[h] for h in heads]
    s = [s_scr[h] for h in heads]
    cross = [_dot(qr[h], s[h]) * jnp.exp((idx + 1.0) * lg[h]) for h in heads]
    o = [_dot(scores[h], v[h]) + cross[h] for h in heads]
    kv = [_dot_tn(kr[h] * jnp.exp((chunk - 1.0 - idx) * lg[h]), v[h]) for h in heads]
    for h in heads:
        s_new = s[h] * math.exp(chunk * lg[h]) + kv[h]
        s_scr[h] = s_new
        sout_ref[0, h] = s_new
        sl = slice(h * RET_DV, (h + 1) * RET_DV)
        gated = _layernorm(o[h], gng_ref[:, sl], gnb_ref[:, sl]) * _silu(rg_ref[0, :, sl].astype(F32))
        o_ref[0, :, sl] = gated.astype(BF16)


def _retention(proj, cos2, sin2, s0, gn_g, gn_b, chunk):
    nb, t = proj.shape[:2]
    grid = (nb, t // chunk)
    qk_w = RET_HEADS * RET_DK
    v_w = RET_HEADS * RET_DV
    return pl.pallas_call(
        functools.partial(_ret_kernel, chunk=chunk),
        grid=grid,
        in_specs=[
            pl.BlockSpec((1, chunk, qk_w), lambda b, c: (b, c, 0)),
            pl.BlockSpec((1, chunk, qk_w), lambda b, c: (b, c, 512 // qk_w)),
            pl.BlockSpec((1, chunk, v_w), lambda b, c: (b, c, 1024 // v_w)),
            pl.BlockSpec((1, chunk, v_w), lambda b, c: (b, c, 2048 // v_w)),
            pl.BlockSpec((chunk, RET_DK), lambda b, c: (c, 0)),
            pl.BlockSpec((chunk, RET_DK), lambda b, c: (c, 0)),
            pl.BlockSpec((1, RET_HEADS, RET_DK, RET_DV), lambda b, c: (b, 0, 0, 0)),
            pl.BlockSpec((1, v_w), lambda b, c: (0, 0)),
            pl.BlockSpec((1, v_w), lambda b, c: (0, 0)),
        ],
        out_specs=[
            pl.BlockSpec((1, chunk, v_w), lambda b, c: (b, c, 0)),
            pl.BlockSpec((1, RET_HEADS, RET_DK, RET_DV), lambda b, c: (b, 0, 0, 0)),
        ],
        out_shape=[
            jax.ShapeDtypeStruct((nb, t, v_w), BF16),
            jax.ShapeDtypeStruct((nb, RET_HEADS, RET_DK, RET_DV), F32),
        ],
        scratch_shapes=[pltpu.VMEM((RET_HEADS, RET_DK, RET_DV), F32),
                        pltpu.VMEM((RET_HEADS, chunk, chunk), F32)],
        compiler_params=_params("parallel", "arbitrary"),
        name="retention",
    )(proj, proj, proj, proj, cos2, sin2, s0, gn_g, gn_b)


def _dn_kernel(xqk_ref, xv_ref, z_ref, sm_ref, smt_ref, cw_ref, cinit_ref, prow_ref, pcol_ref, ng_ref, s0_ref,
               o_ref, sout_ref, cout_ref, xext, s_scr, *, blk_len, chunk):
    L = blk_len
    nbb = xqk_ref.shape[0]
    blk = pl.program_id(1)

    @pl.when(blk == 0)
    def _():
        xext[:, 0:8, :] = cinit_ref[...]
        s_scr[...] = s0_ref[...]

    ri = lax.broadcasted_iota(jnp.int32, (L, L), 0)
    ci = lax.broadcasted_iota(jnp.int32, (L, L), 1)
    same_chunk = _idiv(ri, chunk) == _idiv(ci, chunk)
    lower = jnp.logical_and(ri >= ci, same_chunk)
    strict = jnp.logical_and(ri > ci, same_chunk)
    same_sub = _idiv(ri, DN_SUB) == _idiv(ci, DN_SUB)
    tri = jnp.where(lower, 1.0, 0.0).astype(F32)
    tri_t = jnp.where(jnp.logical_and(ri <= ci, same_chunk), 1.0, 0.0).astype(F32)
    prow = prow_ref[...]
    pcol = pcol_ref[...]

    conv, beta_tm, gcum_tm, gcum_hm, egc_tm = [], [], [], [], []
    for b in range(nbb):
        x = jnp.concatenate([xqk_ref[b], xv_ref[b]], axis=1).astype(F32)
        xext[b, 8:8 + L, :] = x
        cv = x * cw_ref[3:4, :]
        for w in range(DN_CONV_W - 1):
            cv = cv + xext[b, 5 + w:5 + w + L, :] * cw_ref[w:w + 1, :]
        conv.append(_silu(cv))
        xext[b, 0:8, :] = x[L - 8:L, :]
        cout_ref[b] = x[L - (DN_CONV_W - 1):L, :]
        sm = sm_ref[b]
        beta_tm.append(_sigmoid(sm))
        g_tm = prow[0:1, :] * _softplus(sm + prow[1:2, :])
        g_hm = pcol[8:16, 0:1] * _softplus(smt_ref[b][8:16, :] + pcol[8:16, 1:2])
        gcum_tm.append(_dot_exact(tri, g_tm))
        gcum_hm.append(_dot_exact(g_hm, tri_t))
        egc_tm.append(jnp.exp(gcum_tm[b]))

    n_sub = L // chunk
    rep = DN_V_HEADS // DN_QK_HEADS
    bdot = lambda x, y: jnp.dot(x, y, preferred_element_type=F32)
    for h0 in range(0, DN_V_HEADS, DN_HEAD_GROUP):
        heads = [(b, hh) for b in range(nbb) for hh in range(h0, h0 + DN_HEAD_GROUP)]
        q, k, kk, qk = {}, {}, {}, {}
        for b in range(nbb):
            for j in range(h0 // rep, (h0 + DN_HEAD_GROUP) // rep):
                qj = conv[b][:, j * DN_DK:(j + 1) * DN_DK]
                kj = conv[b][:, DN_QK + j * DN_DK:DN_QK + (j + 1) * DN_DK]
                qj = qj * lax.rsqrt(jnp.sum(qj * qj, -1, keepdims=True) + 1e-6) * (DN_DK ** -0.5)
                kj = kj * lax.rsqrt(jnp.sum(kj * kj, -1, keepdims=True) + 1e-6)
                kq = _dot_nt(jnp.concatenate([kj, qj], axis=0), kj)
                q[b, j], k[b, j], kk[b, j], qk[b, j] = qj, kj, kq[0:L], kq[L:2 * L]
        gc = {(b, hh): gcum_tm[b][:, 8 + hh:9 + hh] for b, hh in heads}
        beta = {(b, hh): beta_tm[b][:, hh:hh + 1] for b, hh in heads}
        eg = {(b, hh): egc_tm[b][:, 8 + hh:9 + hh] for b, hh in heads}
        rel = {(b, hh): jnp.where(lower, jnp.exp(jnp.where(lower, gc[b, hh] - gcum_hm[b][hh:hh + 1, :], 0.0)),
                                  0.0) for b, hh in heads}
        a = {(b, hh): jnp.where(strict, beta[b, hh] * kk[b, hh // rep] * rel[b, hh], 0.0) for b, hh in heads}
        attn = {(b, hh): qk[b, hh // rep] * rel[b, hh] for b, hh in heads}
        pd = {hh: jnp.where(same_sub, -a[hh], 0.0) for hh in heads}
        e = {hh: a[hh] + pd[hh] for hh in heads}
        eb = {hh: e[hh].astype(BF16) for hh in heads}
        xacc = pd
        xaccb = {hh: pd[hh].astype(BF16) for hh in heads}
        pwb = xaccb
        for _ in range(3):
            pw = {hh: bdot(pwb[hh], pwb[hh]) for hh in heads}
            pwb = {hh: pw[hh].astype(BF16) for hh in heads}
            xacc = {hh: xacc[hh] + pw[hh] + bdot(xaccb[hh], pwb[hh]) for hh in heads}
            xaccb = {hh: xacc[hh].astype(BF16) for hh in heads}
        f = {hh: e[hh] + bdot(xaccb[hh], eb[hh]) for hh in heads}
        fb = {hh: f[hh].astype(BF16) for hh in heads}
        f2 = {hh: bdot(fb[hh], fb[hh]) for hh in heads}
        y = {hh: f2[hh] - f[hh] - bdot(fb[hh], f2[hh].astype(BF16)) for hh in heads}
        rinv = {hh: xacc[hh] + y[hh] + bdot(y[hh].astype(BF16), xaccb[hh]) for hh in heads}
        sol = {}
        for b, hh in heads:
            vh = conv[b][:, 2 * DN_QK + hh * DN_DV:2 * DN_QK + (hh + 1) * DN_DV]
            rhs = jnp.concatenate([vh * beta[b, hh], k[b, hh // rep] * (beta[b, hh] * eg[b, hh])], axis=1)
            sol[b, hh] = rhs + _dot(rinv[b, hh], rhs)
        u = {p: sol[p][:, 0:DN_DV] for p in heads}
        wmat = {p: sol[p][:, DN_DV:DN_DV + DN_DK] for p in heads}
        qe = {(b, hh): q[b, hh // rep] * eg[b, hh] for b, hh in heads}
        s = {(b, hh): s_scr[b, hh] for b, hh in heads}
        v_new = {p: [] for p in heads}
        q_s = {p: [] for p in heads}
        for sc in range(n_sub):
            r0 = sc * chunk
            for b, hh in heads:
                p = (b, hh)
                ws = _dot(jnp.concatenate([wmat[p][r0:r0 + chunk], qe[p][r0:r0 + chunk]], axis=0), s[p])
                vn = u[p][r0:r0 + chunk] - ws[0:chunk]
                v_new[p].append(vn)
                q_s[p].append(ws[chunk:2 * chunk])
                g_last = gcum_tm[b][r0 + chunk - 1:r0 + chunk, 8 + hh:9 + hh]
                kd = k[b, hh // rep][r0:r0 + chunk] * jnp.exp(g_last - gc[p][r0:r0 + chunk])
                s[p] = s[p] * jnp.exp(g_last) + _dot_tn(kd, vn)
        for b, hh in heads:
            p = (b, hh)
            s_scr[b, hh] = s[p]
            sout_ref[b, hh] = s[p]
            vn = v_new[p][0] if n_sub == 1 else jnp.concatenate(v_new[p], axis=0)
            qs = q_s[p][0] if n_sub == 1 else jnp.concatenate(q_s[p], axis=0)
            o = qs + _dot(attn[p], vn)
            zh = z_ref[b, :, hh * DN_DV:(hh + 1) * DN_DV].astype(F32)
            n = o * lax.rsqrt(jnp.mean(o * o, -1, keepdims=True) + 1e-6) * ng_ref[...]
            o_ref[b, :, hh * DN_DV:(hh + 1) * DN_DV] = (n * _silu(zh)).astype(BF16)


def _deltanet(proj, small, small_t, conv_w, conv_init, prow, pcol, norm_g, s0, blk_len, chunk):
    nb, t = proj.shape[:2]
    nblk = t // blk_len
    nbb = max(DN_STREAMS, min(nb, DN_STREAM_ROWS // blk_len))
    grid = (nb // nbb, nblk)
    return pl.pallas_call(
        functools.partial(_dn_kernel, blk_len=blk_len, chunk=chunk),
        grid=grid,
        in_specs=[
            pl.BlockSpec((nbb, blk_len, 1024), lambda b, i: (b, i, 3072 // 1024)),
            pl.BlockSpec((nbb, blk_len, 1024), lambda b, i: (b, i, 4096 // 1024)),
            pl.BlockSpec((nbb, blk_len, 1024), lambda b, i: (b, i, 5120 // 1024)),
            pl.BlockSpec((nbb, blk_len, LANES), lambda b, i: (b, i, 0)),
            pl.BlockSpec((nbb, 16, blk_len), lambda b, i: (b, 0, i)),
            pl.BlockSpec((DN_CONV_W, DN_CONV_CH), lambda b, i: (0, 0)),
            pl.BlockSpec((nbb, 8, DN_CONV_CH), lambda b, i: (b, 0, 0)),
            pl.BlockSpec((8, LANES), lambda b, i: (0, 0)),
            pl.BlockSpec((16, LANES), lambda b, i: (0, 0)),
            pl.BlockSpec((1, DN_DV), lambda b, i: (0, 0)),
            pl.BlockSpec((nbb, DN_V_HEADS, DN_DK, DN_DV), lambda b, i: (b, 0, 0, 0)),
        ],
        out_specs=[
            pl.BlockSpec((nbb, blk_len, 1024), lambda b, i: (b, i, 0)),
            pl.BlockSpec((nbb, DN_V_HEADS, DN_DK, DN_DV), lambda b, i: (b, 0, 0, 0)),
            pl.BlockSpec((nbb, DN_CONV_W - 1, DN_CONV_CH), lambda b, i: (b, 0, 0)),
        ],
        out_shape=[
            jax.ShapeDtypeStruct((nb, t, 1024), BF16),
            jax.ShapeDtypeStruct((nb, DN_V_HEADS, DN_DK, DN_DV), F32),
            jax.ShapeDtypeStruct((nb, DN_CONV_W - 1, DN_CONV_CH), F32),
        ],
        scratch_shapes=[pltpu.VMEM((nbb, blk_len + 8, DN_CONV_CH), F32),
                        pltpu.VMEM((nbb, DN_V_HEADS, DN_DK, DN_DV), F32)],
        compiler_params=_params("parallel", "arbitrary"),
        name="deltanet",
    )(proj, proj, proj, small, small_t, conv_w, conv_init, prow, pcol, norm_g, s0)


def _memkv_kernel(m_ref, w_ref, o_ref):
    o_ref[...] = _dot(m_ref[...], w_ref[...])


def _memkv(mem, w_kv):
    n = mem.shape[0]
    tn = 1024
    return pl.pallas_call(
        _memkv_kernel,
        grid=(w_kv.shape[1] // tn,),
        in_specs=[pl.BlockSpec((n, D_MODEL), lambda j: (0, 0)),
                  pl.BlockSpec((D_MODEL, tn), lambda j: (0, j))],
        out_specs=pl.BlockSpec((n, tn), lambda j: (0, j)),
        out_shape=jax.ShapeDtypeStruct((n, w_kv.shape[1]), F32),
        compiler_params=_params("parallel"),
        name="memkv",
    )(mem, w_kv)


def _mix_kernel(h_ref, hb_ref, ret_ref, dn_ref, mk_ref, mv_ref, wxq_ref, wg_ref, wb_ref, wo_ref,
                g1_ref, b1_ref, h1_ref, h1b_ref, mem_scr, *, n_batch, tiles_per_batch):
    @pl.when(pl.program_id(0) % tiles_per_batch == 0)
    def _():
        for b in range(n_batch):
            for hh in range(XA_HEADS):
                mem_scr[0, hh, b] = mk_ref[0, b, :, hh, :].astype(BF16)
                mem_scr[1, hh, b] = mv_ref[0, b, :, hh, :].astype(BF16)

    hb = hb_ref[...]
    xq = jnp.dot(hb, wxq_ref[...], preferred_element_type=F32)
    tb = xq.shape[0] // n_batch
    pairs = [(b, hh) for b in range(n_batch) for hh in range(XA_HEADS)]
    s = [_dot_nt(xq[b * tb:(b + 1) * tb, hh * XA_DH:(hh + 1) * XA_DH], mem_scr[0, hh, b]) * (XA_DH ** -0.5)
         for b, hh in pairs]
    p = [jnp.exp(si - jnp.max(si, -1, keepdims=True)) for si in s]
    p = [pi / jnp.sum(pi, -1, keepdims=True) for pi in p]
    xo = [_dot(pi, mem_scr[1, hh, b]) for pi, (b, hh) in zip(p, pairs)]
    xo = [jnp.concatenate(xo[b * XA_HEADS:(b + 1) * XA_HEADS], axis=1) for b in range(n_batch)]
    xo = xo[0] if n_batch == 1 else jnp.concatenate(xo, axis=0)
    mixed = None
    for n, br in enumerate((ret_ref[...], dn_ref[...], xo)):
        gate = _sigmoid(jnp.dot(hb, wg_ref[:, n * D_MODEL:(n + 1) * D_MODEL], preferred_element_type=F32))
        term = gate * _dot(br, wb_ref[n])
        mixed = term if mixed is None else mixed + term
    y = ALPHA * h_ref[...] + _dot(mixed, wo_ref[...])
    h1 = _layernorm(y, g1_ref[...], b1_ref[...])
    h1_ref[...] = h1
    h1b_ref[...] = h1.astype(BF16)


def _mix(h, hb, ret_o, dn_o, mem_k, mem_v, w_xq, w_gate, w_branch, w_out, ln_g, ln_b, tm, t):
    n_tok = h.shape[0]
    const2 = lambda i: (0, 0)
    row = lambda i: (i, 0)
    single = pl.Buffered(1)
    if tm <= t:
        n_batch = 1
        tiles_per_batch = t // tm
    else:
        n_batch = tm // t
        tiles_per_batch = 1
    mem_spec = pl.BlockSpec((1, n_batch, N_MEM, XA_HEADS, XA_DH), lambda i: (0, i // tiles_per_batch, 0, 0, 0))
    return pl.pallas_call(
        functools.partial(_mix_kernel, n_batch=n_batch, tiles_per_batch=tiles_per_batch),
        grid=(n_tok // tm,),
        in_specs=[
            pl.BlockSpec((tm, D_MODEL), row),
            pl.BlockSpec((tm, D_MODEL), row),
            pl.BlockSpec((tm, D_MODEL), row),
            pl.BlockSpec((tm, D_MODEL), row),
            mem_spec,
            mem_spec,
            pl.BlockSpec((D_MODEL, D_MODEL), const2, pipeline_mode=single),
            pl.BlockSpec((D_MODEL, 3 * D_MODEL), const2, pipeline_mode=single),
            pl.BlockSpec((3, D_MODEL, D_MODEL), lambda i: (0, 0, 0), pipeline_mode=single),
            pl.BlockSpec((D_MODEL, D_MODEL), const2, pipeline_mode=single),
            pl.BlockSpec((1, D_MODEL), const2),
            pl.BlockSpec((1, D_MODEL), const2),
        ],
        out_specs=[pl.BlockSpec((tm, D_MODEL), row), pl.BlockSpec((tm, D_MODEL), row)],
        out_shape=[jax.ShapeDtypeStruct((n_tok, D_MODEL), F32),
                   jax.ShapeDtypeStruct((n_tok, D_MODEL), BF16)],
        scratch_shapes=[pltpu.VMEM((2, XA_HEADS, n_batch, N_MEM, XA_DH), BF16)],
        compiler_params=_params("arbitrary"),
        name="mix",
    )(h, hb, ret_o, dn_o, mem_k, mem_v, w_xq, w_gate, w_branch, w_out, ln_g, ln_b)


def _route_t(x, wrt, brt):
    xh = x.astype(BF16)
    xl = (x - xh.astype(F32)).astype(BF16)
    wh = wrt.astype(BF16)
    wl = (wrt - wh.astype(F32)).astype(BF16)
    nt = lambda a, b: lax.dot_general(a, b, (((1,), (1,)), ((), ())), preferred_element_type=F32)
    logits = nt(wh, xh) + (nt(wl, xh) + nt(wh, xl)) + brt
    row_i = lax.broadcasted_iota(jnp.int32, logits.shape, 0)
    row = row_i.astype(F32)
    far = jnp.float32(LANES)
    neg = jnp.float32(-3.0e38)
    is_c = jnp.logical_and(row_i >= MOE_NE, row_i < MOE_NE + MOE_GROUPS)
    cl = jnp.where(is_c, logits, neg)
    cmax = jnp.max(cl, 0, keepdims=True)
    denom = jnp.sum(jnp.where(is_c, jnp.exp(jnp.where(is_c, logits - cmax, 0.0)), 0.0), 0, keepdims=True)
    p_grp = 1.0 / denom
    grp = jnp.min(jnp.where(jnp.logical_and(is_c, cl == cmax), row - MOE_NE, far), 0, keepdims=True)
    in_grp = jnp.logical_and(row_i < MOE_NE, _idiv(row_i, MOE_EXPERTS).astype(F32) == grp)
    fl = jnp.where(in_grp, logits, neg)
    v1 = jnp.max(fl, 0, keepdims=True)
    i1 = jnp.min(jnp.where(jnp.logical_and(in_grp, fl == v1), row, far), 0, keepdims=True)
    rest = jnp.logical_and(in_grp, row != i1)
    fl2 = jnp.where(rest, logits, neg)
    v2 = jnp.max(fl2, 0, keepdims=True)
    i2 = jnp.min(jnp.where(jnp.logical_and(rest, fl2 == v2), row, far), 0, keepdims=True)
    e2 = jnp.exp(v2 - v1)
    w1 = p_grp / (1.0 + e2)
    w2 = p_grp * e2 / (1.0 + e2)
    comb_t = jnp.where(row == i1, w1, 0.0) + jnp.where(row == i2, w2, 0.0)
    return jnp.where(row_i == LANES - 1, grp, comb_t)


def _route_kernel(h1_ref, wrt_ref, brt_ref, comb_ref, cnt_ref):
    comb = _route_t(h1_ref[...], wrt_ref[...], brt_ref[...]).T
    comb_ref[...] = comb
    lane = lax.broadcasted_iota(jnp.int32, comb.shape, 1).astype(F32)
    onehot = jnp.where(lane == comb[:, LANES - 1:LANES], 1.0, 0.0)
    cnt_ref[0] = jnp.broadcast_to(jnp.sum(onehot, 0, keepdims=True), (8, LANES))


def _route_call(h1, w_route, b_route, tm):
    n_tok = h1.shape[0]
    return pl.pallas_call(
        _route_kernel,
        grid=(n_tok // tm,),
        in_specs=[pl.BlockSpec((tm, D_MODEL), lambda i: (i, 0)),
                  pl.BlockSpec((LANES, D_MODEL), lambda i: (0, 0)),
                  pl.BlockSpec((LANES, 1), lambda i: (0, 0))],
        out_specs=[pl.BlockSpec((tm, LANES), lambda i: (i, 0)),
                   pl.BlockSpec((1, 8, LANES), lambda i: (i, 0, 0))],
        out_shape=[jax.ShapeDtypeStruct((n_tok, LANES), F32),
                   jax.ShapeDtypeStruct((n_tok // tm, 8, LANES), F32)],
        compiler_params=_params("parallel"),
        name="route",
    )(h1, w_route, b_route)


MOE_ALIGN = 16
MOE_EPS = 4


def _moe_sizes(tm):
    base = tm // MOE_GROUPS
    fine = {min(tm, base + MOE_ALIGN * i) for i in range(5)}
    return tuple(sorted(fine | {tm // 2, tm}))


def _moe_kernel(offs_ref, cls_ref, h1_ref, h1b_ref, comb_ref, tri_ref, wg_ref, wu_ref, wd_ref,
                g2_ref, b2_ref, y_ref, pt_scr, xs_scr, cs_scr, ys_scr, *, tm, nr1, sizes):
    i = pl.program_id(0)
    e = pl.program_id(1)
    nrt = xs_scr.shape[0]

    @pl.when(e == 0)
    def _():
        comb = comb_ref[...]
        lane_i = lax.broadcasted_iota(jnp.int32, (tm, LANES), 1)
        grp = comb[:, LANES - 1:LANES]
        is_g = lane_i.astype(F32) == grp
        prefix = jnp.dot(tri_ref[...], jnp.where(is_g, 1.0, 0.0).astype(BF16), preferred_element_type=F32)
        rank = jnp.sum(jnp.where(is_g, prefix, 0.0), -1, keepdims=True)
        offv = jnp.zeros((tm, 1), F32)
        for g in range(MOE_GROUPS):
            offv = jnp.where(grp == g, offs_ref[i * MOE_GROUPS + g].astype(F32), offv)
        pos = offv + rank
        col = lax.broadcasted_iota(jnp.int32, (tm, nr1), 1).astype(F32)
        pt = jnp.where(col == pos, 1.0, 0.0).astype(BF16)
        pt_scr[...] = pt
        combw = jnp.where(lane_i < MOE_NE, comb, 0.0)
        c_hi = combw.astype(BF16)
        c_lo = (combw - c_hi.astype(F32)).astype(BF16)
        xs_scr[0:nr1, :] = _dot_tn(pt, h1b_ref[...]).astype(BF16)
        c_pair = _dot_tn(pt, jnp.concatenate([c_hi, c_lo], axis=1))
        cs_scr[0:nr1, :] = c_pair[:, 0:LANES] + c_pair[:, LANES:2 * LANES]
        xs_scr[nr1:nrt, :] = jnp.zeros((nrt - nr1, D_MODEL), BF16)
        cs_scr[nr1:nrt, :] = jnp.zeros((nrt - nr1, LANES), F32)
        ys_scr[tm:nr1, :] = jnp.zeros((nr1 - tm, D_MODEL), F32)

    g = e // (MOE_EXPERTS // MOE_EPS)
    off = pl.multiple_of(offs_ref[i * MOE_GROUPS + g], MOE_ALIGN)
    cls = cls_ref[i * MOE_GROUPS + g]
    first = e % (MOE_EXPERTS // MOE_EPS) == 0
    for ci, m in enumerate(sizes):
        @pl.when(cls == ci)
        def _(m=m):
            rows = pl.ds(off, m)
            xb = xs_scr[rows, :]
            cb = cs_scr[rows, :]
            lane_i = lax.broadcasted_iota(jnp.int32, (m, LANES), 1)
            sub = range(MOE_EPS)
            colw = [jnp.sum(jnp.where(lane_i == e * MOE_EPS + j, cb, 0.0), -1, keepdims=True) for j in sub]
            gate = [jnp.dot(xb, wg_ref[j], preferred_element_type=F32) for j in sub]
            up = [jnp.dot(xb, wu_ref[j], preferred_element_type=F32) for j in sub]
            hg = [(_silu(gate[j]) * up[j] * colw[j]).astype(BF16) for j in sub]
            wd = wd_ref[...].reshape(MOE_EPS * MOE_DFF, D_MODEL)
            res = jnp.dot(jnp.concatenate(hg, axis=1), wd, preferred_element_type=F32)

            @pl.when(first)
            def _():
                ys_scr[rows, :] = res

            @pl.when(jnp.logical_not(first))
            def _():
                ys_scr[rows, :] += res

    @pl.when(e == MOE_NE // MOE_EPS - 1)
    def _():
        moe = jnp.dot(pt_scr[...], ys_scr[0:nr1, :].astype(BF16), preferred_element_type=F32)
        y_ref[...] = _layernorm(ALPHA * h1_ref[...] + moe, g2_ref[...], b2_ref[...])


def _moe(h1, h1b, comb, cnt, w_gate, w_up, w_down, ln_g, ln_b, tm):
    n_tok = h1.shape[0]
    sizes = _moe_sizes(tm)
    nr1 = -(-(tm + MOE_GROUPS * MOE_ALIGN) // LANES) * LANES
    nrt = nr1 + max(b - a for a, b in zip((0,) + sizes, sizes))
    c = cnt[:, 0, :MOE_GROUPS].astype(jnp.int32)
    seg = (c + MOE_ALIGN - 1) // MOE_ALIGN * MOE_ALIGN
    offs = (jnp.cumsum(seg, axis=1) - seg).reshape(-1)
    cls = sum((c > s).astype(jnp.int32) for s in sizes[:-1]).reshape(-1)
    tri =jnp.asarray(np.tril(np.ones((tm, tm), np.float32), -1), BF16)
    grid_spec = pltpu.PrefetchScalarGridSpec(
        num_scalar_prefetch=2,
        grid=(n_tok // tm, MOE_NE // MOE_EPS),
        in_specs=[
            pl.BlockSpec((tm, D_MODEL), lambda i, e, *_: (i, 0)),
            pl.BlockSpec((tm, D_MODEL), lambda i, e, *_: (i, 0)),
            pl.BlockSpec((tm, LANES), lambda i, e, *_: (i, 0)),
            pl.BlockSpec((tm, tm), lambda i, e, *_: (0, 0), pipeline_mode=pl.Buffered(1)),
            pl.BlockSpec((MOE_EPS, D_MODEL, MOE_DFF), lambda i, e, *_: (e, 0, 0)),
            pl.BlockSpec((MOE_EPS, D_MODEL, MOE_DFF), lambda i, e, *_: (e, 0, 0)),
            pl.BlockSpec((MOE_EPS, MOE_DFF, D_MODEL), lambda i, e, *_: (e, 0, 0)),
            pl.BlockSpec((1, D_MODEL), lambda i, e, *_: (0, 0)),
            pl.BlockSpec((1, D_MODEL), lambda i, e, *_: (0, 0)),
        ],
        out_specs=pl.BlockSpec((tm, D_MODEL), lambda i, e, *_: (i, 0)),
        scratch_shapes=[pltpu.VMEM((tm, nr1), BF16), pltpu.VMEM((nrt, D_MODEL), BF16),
                        pltpu.VMEM((nrt, LANES), F32), pltpu.VMEM((nrt, D_MODEL), F32)],
    )
    return pl.pallas_call(
        functools.partial(_moe_kernel, tm=tm, nr1=nr1, sizes=sizes),
        grid_spec=grid_spec,
        out_shape=jax.ShapeDtypeStruct((n_tok, D_MODEL), F32),
        compiler_params=_params("parallel", "arbitrary"),
        name="moe",
    )(offs, cls, h1, h1b, comb, tri, w_gate, w_up, w_down, ln_g, ln_b)


def _rope_tables(start, t):
    half = RET_DK // 2
    inv = 1.0 / (ROPE_BASE ** (np.arange(half, dtype=np.float64) / half))
    ang = (start + np.arange(t, dtype=np.float64))[:, None] * inv[None, :]
    cos = np.cos(ang)
    sin = np.sin(ang)
    return (jnp.asarray(np.concatenate([cos, cos], -1), F32),
            jnp.asarray(np.concatenate([-sin, sin], -1), F32))


def _group(x, pos0, mem_k, mem_v, ret_s0, dn_s0, conv_buf, wts, *, tm, tm_mix, tm_moe, ret_chunk, dn_blk,
           dn_chunk):
    nb, t, d = x.shape
    n_tok = nb * t
    h, hb, small, small_t, proj = _in_proj(x.reshape(n_tok, d), wts["ln_in_g"], wts["ln_in_b"],
                                           wts["w_in"], wts["w_small"], wts["w_small_t"], tm)
    proj3 = proj.reshape(nb, t, PROJ_W)
    cos2, sin2 = _rope_tables(pos0, t)
    ret_o, ret_s = _retention(proj3, cos2, sin2, ret_s0, wts["ret_gn_g"], wts["ret_gn_b"], ret_chunk)
    small_t3 = small_t.reshape(16, nb, t).transpose(1, 0, 2)
    conv_init = jnp.concatenate([jnp.zeros((nb, 8 - (DN_CONV_W - 1), DN_CONV_CH), F32), conv_buf], axis=1)
    dn_o, dn_s, conv_new = _deltanet(proj3, small.reshape(nb, t, LANES), small_t3, wts["dn_conv"], conv_init,
                                     wts["dn_prow"],
                                     wts["dn_pcol"], wts["dn_norm_g"], dn_s0, dn_blk, dn_chunk)
    h1, h1b = _mix(h, hb, ret_o.reshape(n_tok, d), dn_o.reshape(n_tok, d), mem_k, mem_v,
                   wts["w_xq"], wts["w_gate"], wts["w_branch"], wts["w_out"], wts["ln1_g"], wts["ln1_b"],
                   tm_mix, t)
    comb, cnt = _route_call(h1, wts["w_route"], wts["b_route"], tm_moe)
    y = _moe(h1, h1b, comb, cnt, wts["moe_w_gate"], wts["moe_w_up"], wts["moe_w_down"],
             wts["ln2_g"], wts["ln2_b"], tm_moe)
    return y.reshape(nb, t, d), ret_s, dn_s, conv_new


def kernel(x_prompt, x_sample, mem_prompt, state_ret, state_dn, state_dn_conv, cache_mem_k, cache_mem_v,
           ln_in_g, ln_in_b, w_in, ret_gn_g, ret_gn_b, dn_conv, dn_A_log, dn_dt_bias, dn_norm_g,
           w_mem_kv, w_branch, w_out, ln1_g, ln1_b, moe_w_coarse, moe_b_coarse, moe_w_fine, moe_b_fine,
           moe_w_gate, moe_w_up, moe_w_down, ln2_g, ln2_b):
    bp, tp, d = x_prompt.shape
    bs, ts, _ = x_sample.shape
    l = 0
    wi = w_in[l].astype(BF16)
    dba = wi[:, PROJ_W:PROJ_W + 16]
    xq, gates = wi[:, 6160:7184], wi[:, 7184:10256]
    wts = {
        "ln_in_g": ln_in_g.reshape(1, d), "ln_in_b": ln_in_b.reshape(1, d),
        "w_in": wi,
        "w_small": jnp.pad(dba, ((0, 0), (0, LANES - 16))),
        "w_small_t": dba.T,
        "ret_gn_g": ret_gn_g[l].reshape(1, -1), "ret_gn_b": ret_gn_b[l].reshape(1, -1),
        "dn_conv": dn_conv[l],
        "dn_prow": jnp.zeros((8, LANES), F32).at[0, 8:16].set(-jnp.exp(dn_A_log[l])).at[1, 8:16].set(dn_dt_bias[l]),
        "dn_pcol": jnp.zeros((16, LANES), F32).at[8:16, 0].set(-jnp.exp(dn_A_log[l])).at[8:16, 1].set(dn_dt_bias[l]),
        "dn_norm_g": dn_norm_g[l].reshape(1, -1),
        "w_xq": xq, "w_gate": gates,
        "w_branch": w_branch[l].astype(BF16), "w_out": w_out[l].astype(BF16),
        "ln1_g": ln1_g[l].reshape(1, d), "ln1_b": ln1_b[l].reshape(1, d),
        "w_route": jnp.pad(jnp.concatenate([moe_w_fine[l], moe_w_coarse[l]], axis=1).T,
                           ((0, LANES - MOE_NE - MOE_GROUPS), (0, 0))),
        "b_route": jnp.pad(jnp.concatenate([moe_b_fine[l], moe_b_coarse[l]]),
                           (0, LANES - MOE_NE - MOE_GROUPS)).reshape(LANES, 1),
        "moe_w_gate": moe_w_gate[l].reshape(MOE_NE, d, MOE_DFF).astype(BF16),
        "moe_w_up": moe_w_up[l].reshape(MOE_NE, d, MOE_DFF).astype(BF16),
        "moe_w_down": moe_w_down[l].reshape(MOE_NE, MOE_DFF, d).astype(BF16),
        "ln2_g": ln2_g[l].reshape(1, d), "ln2_b": ln2_b[l].reshape(1, d),
    }

    mkv = _memkv(mem_prompt.reshape(bp * N_MEM, d), w_mem_kv[l].astype(BF16))
    mk = mkv[:, :XA_HEADS * XA_DH].reshape(1, bp, N_MEM, XA_HEADS, XA_DH)
    mv = mkv[:, XA_HEADS * XA_DH:].reshape(1, bp, N_MEM, XA_HEADS, XA_DH)
    yp, rs_p, ds_p, cb_p = _group(
        x_prompt, 0, mk, mv,
        jnp.zeros((bp, RET_HEADS, RET_DK, RET_DV), F32),
        jnp.zeros((bp, DN_V_HEADS, DN_DK, DN_DV), F32),
        jnp.zeros((bp, DN_CONV_W - 1, DN_CONV_CH), F32),
        wts, tm=min(1024, tp), tm_mix=min(512, tp), tm_moe=min(1024, tp), ret_chunk=min(256, tp),
        dn_blk=min(128, tp),
        dn_chunk=min(DN_CHUNK, tp))
    ys, rs_s, ds_s, cb_s = _group(
        x_sample, PAST_LEN, cache_mem_k[l:l + 1], cache_mem_v[l:l + 1],
        state_ret[l], state_dn[l], state_dn_conv[l],
        wts, tm=bs * ts, tm_mix=bs * ts, tm_moe=bs * ts, ret_chunk=ts, dn_blk=ts, dn_chunk=min(DN_CHUNK, ts))
    return (yp, ys, rs_p[None], rs_s[None], ds_p[None], ds_s[None], cb_p[None], cb_s[None], mk, mv)
```

```python
import functools
import math

import numpy as np

import jax
import jax.numpy as jnp
from jax import lax
from jax.experimental import pallas as pl
from jax.experimental.pallas import tpu as pltpu

F32 = jnp.float32
BF16 = jnp.bfloat16

D_MODEL = 1024
PAST_LEN = 1024
RET_HEADS = 4
RET_DK = 128
RET_DV = 256
ROPE_BASE = 10000.0
DN_QK_HEADS = 4
DN_V_HEADS = 8
DN_DK = 128
DN_DV = 128
DN_QK = DN_QK_HEADS * DN_DK
DN_CONV_W = 4
DN_CONV_CH = 2048
DN_CHUNK = 64
DN_SUB = 16
DN_HEAD_GROUP = 8
DN_STREAMS = 2
DN_STREAM_ROWS = 256
XA_HEADS = 4
XA_DH = 256
N_MEM = 256
MOE_GROUPS = 4
MOE_EXPERTS = 8
MOE_NE = MOE_GROUPS * MOE_EXPERTS
MOE_DFF = 256
DEPTH = 1
ALPHA = (2.0 * DEPTH) ** 0.25
LANES = 128
PROJ_W = 6144
VMEM_LIMIT = 56 * 1024 * 1024


def _dot(a, b):
    return jnp.dot(a.astype(BF16), b.astype(BF16), preferred_element_type=F32)


def _dot_nt(a, b):
    return lax.dot_general(a.astype(BF16), b.astype(BF16), (((1,), (1,)), ((), ())),
                           preferred_element_type=F32)


def _dot_tn(a, b):
    return lax.dot_general(a.astype(BF16), b.astype(BF16), (((0,), (0,)), ((), ())),
                           preferred_element_type=F32)


def _dot_exact(a, b):
    return jnp.dot(a, b, preferred_element_type=F32, precision=lax.Precision.HIGHEST)


def _layernorm(x, g, b, eps=1e-5):
    mu = jnp.mean(x, -1, keepdims=True)
    xc = x - mu
    var = jnp.mean(xc * xc, -1, keepdims=True)
    return xc * lax.rsqrt(var + eps) * g + b


def _silu(x):
    return x * (1.0 / (1.0 + jnp.exp(-x)))


def _sigmoid(x):
    return 1.0 / (1.0 + jnp.exp(-x))


def _softplus(x):
    return jnp.maximum(x, 0.0) + jnp.log(1.0 + jnp.exp(-jnp.abs(x)))


def _idiv(x, n):
    return jnp.right_shift(x, int(math.log2(n)))


def _params(*sem):
    return pltpu.CompilerParams(dimension_semantics=sem, vmem_limit_bytes=VMEM_LIMIT)


def _in_proj_kernel(x_ref, g_ref, b_ref, w_ref, ws_ref, wst_ref,
                    h_ref, small_ref, smallt_ref, proj_ref, hs_ref):
    @pl.when(pl.program_id(1) == 0)
    def _():
        h = _layernorm(x_ref[...], g_ref[...], b_ref[...])
        h_ref[...] = h
        hb = h.astype(BF16)
        hs_ref[...] = hb
        small_ref[...] = jnp.dot(hb, ws_ref[...], preferred_element_type=F32)
        smallt_ref[...] = lax.dot_general(wst_ref[...], hb, (((1,), (1,)), ((), ())),
                                          preferred_element_type=F32)

    proj_ref[...] = jnp.dot(hs_ref[...], w_ref[...], preferred_element_type=F32).astype(BF16)


def _in_proj(x, ln_g, ln_b, w_main, w_small, w_small_t, tm, tn=2048):
    n_tok = x.shape[0]
    grid = (n_tok // tm, PROJ_W // tn)
    return pl.pallas_call(
        _in_proj_kernel,
        grid=grid,
        in_specs=[
            pl.BlockSpec((tm, D_MODEL), lambda i, n: (i, 0)),
            pl.BlockSpec((1, D_MODEL), lambda i, n: (0, 0)),
            pl.BlockSpec((1, D_MODEL), lambda i, n: (0, 0)),
            pl.BlockSpec((D_MODEL, tn), lambda i, n: (0, n)),
            pl.BlockSpec((D_MODEL, LANES), lambda i, n: (0, 0)),
            pl.BlockSpec((16, D_MODEL), lambda i, n: (0, 0)),
        ],
        out_specs=[
            pl.BlockSpec((tm, D_MODEL), lambda i, n: (i, 0)),
            pl.BlockSpec((tm, LANES), lambda i, n: (i, 0)),
            pl.BlockSpec((16, tm), lambda i, n: (0, i)),
            pl.BlockSpec((tm, tn), lambda i, n: (i, n)),
        ],
        out_shape=[
            jax.ShapeDtypeStruct((n_tok, D_MODEL), F32),
            jax.ShapeDtypeStruct((n_tok, LANES), F32),
            jax.ShapeDtypeStruct((16, n_tok), F32),
            jax.ShapeDtypeStruct((n_tok, PROJ_W), BF16),
        ],
        scratch_shapes=[pltpu.VMEM((tm, D_MODEL), BF16)],
        compiler_params=_params("parallel", "arbitrary"),
        name="in_proj",
    )(x, ln_g, ln_b, w_main, w_small, w_small_t)


def _ret_kernel(q_ref, k_ref, v_ref, rg_ref, cos_ref, sin_ref, s0_ref, gng_ref, gnb_ref,
                o_ref, sout_ref, s_scr, dec_scr, *, chunk):
    heads = range(RET_HEADS)
    lg = [math.log(1.0 - 2.0 ** (-5.0 - h)) for h in heads]

    @pl.when(pl.program_id(1) == 0)
    def _():
        s_scr[...] = s0_ref[0]
        ri = lax.broadcasted_iota(jnp.int32, (chunk, chunk), 0)
        ci = lax.broadcasted_iota(jnp.int32, (chunk, chunk), 1)
        causal = ri >= ci
        diff = jnp.where(causal, (ri - ci).astype(F32), 0.0)
        for h in heads:
            dec_scr[h] = jnp.where(causal, jnp.exp(diff * lg[h]), 0.0)

    cos = cos_ref[...]
    sin = sin_ref[...]
    idx = lax.broadcasted_iota(jnp.int32, (chunk, 1), 0).astype(F32)
    q = [q_ref[0, :, h * RET_DK:(h + 1) * RET_DK].astype(F32) for h in heads]
    k = [k_ref[0, :, h * RET_DK:(h + 1) * RET_DK].astype(F32) for h in heads]
    v = [v_ref[0, :, h * RET_DV:(h + 1) * RET_DV] for h in heads]
    qr = [q[h] * cos + pltpu.roll(q[h], RET_DK // 2, 1) * sin for h in heads]
    kr = [(k[h] * cos + pltpu.roll(k[h], RET_DK // 2, 1) * sin) * (RET_DK ** -0.5) for h in heads]
    scores = [_dot_nt(qr[h], kr[h]) * dec_scr[h] for h in heads]
    s = [s_scr[h] for h in heads]
    cross = [_dot(qr[h], s[h]) * jnp.exp((idx + 1.0) * lg[h]) for h in heads]
    o = [_dot(scores[h], v[h]) + cross[h] for h in heads]
    kv = [_dot_tn(kr[h] * jnp.exp((chunk - 1.0 - idx) * lg[h]), v[h]) for h in heads]
    for h in heads:
        s_new = s[h] * math.exp(chunk * lg[h]) + kv[h]
        s_scr[h] = s_new
        sout_ref[0, h] = s_new
        sl = slice(h * RET_DV, (h + 1) * RET_DV)
        gated = _layernorm(o[h], gng_ref[:, sl], gnb_ref[:, sl]) * _silu(rg_ref[0, :, sl].astype(F32))
        o_ref[0, :, sl] = gated.astype(BF16)


def _retention(proj, cos2, sin2, s0, gn_g, gn_b, chunk):
    nb, t = proj.shape[:2]
    grid = (nb, t // chunk)
    qk_w = RET_HEADS * RET_DK
    v_w = RET_HEADS * RET_DV
    return pl.pallas_call(
        functools.partial(_ret_kernel, chunk=chunk),
        grid=grid,
        in_specs=[
            pl.BlockSpec((1, chunk, qk_w), lambda b, c: (b, c, 0)),
            pl.BlockSpec((1, chunk, qk_w), lambda b, c: (b, c, 512 // qk_w)),
            pl.BlockSpec((1, chunk, v_w), lambda b, c: (b, c, 1024 // v_w)),
            pl.BlockSpec((1, chunk, v_w), lambda b, c: (b, c, 2048 // v_w)),
            pl.BlockSpec((chunk, RET_DK), lambda b, c: (c, 0)),
            pl.BlockSpec((chunk, RET_DK), lambda b, c: (c, 0)),
            pl.BlockSpec((1, RET_HEADS, RET_DK, RET_DV), lambda b, c: (b, 0, 0, 0)),
            pl.BlockSpec((1, v_w), lambda b, c: (0, 0)),
            pl.BlockSpec((1, v_w), lambda b, c: (0, 0)),
        ],
        out_specs=[
            pl.BlockSpec((1, chunk, v_w), lambda b, c: (b, c, 0)),
            pl.BlockSpec((1, RET_HEADS, RET_DK, RET_DV), lambda b, c: (b, 0, 0, 0)),
        ],
        out_shape=[
            jax.ShapeDtypeStruct((nb, t, v_w), BF16),
            jax.ShapeDtypeStruct((nb, RET_HEADS, RET_DK, RET_DV), F32),
        ],
        scratch_shapes=[pltpu.VMEM((RET_HEADS, RET_DK, RET_DV), F32),
                        pltpu.VMEM((RET_HEADS, chunk, chunk), F32)],
        compiler_params=_params("parallel", "arbitrary"),
        name="retention",
    )(proj, proj, proj, proj, cos2, sin2, s0, gn_g, gn_b)


def _dn_kernel(xqk_ref, xv_ref, z_ref, sm_ref, smt_ref, cw_ref, cinit_ref, prow_ref, pcol_ref, ng_ref, s0_ref,
               o_ref, sout_ref, cout_ref, xext, s_scr, *, blk_len, chunk):
    L = blk_len
    nbb = xqk_ref.shape[0]
    blk = pl.program_id(1)

    @pl.when(blk == 0)
    def _():
        xext[:, 0:8, :] = cinit_ref[...]
        s_scr[...] = s0_ref[...]

    ri = lax.broadcasted_iota(jnp.int32, (L, L), 0)
    ci = lax.broadcasted_iota(jnp.int32, (L, L), 1)
    same_chunk = _idiv(ri, chunk) == _idiv(ci, chunk)
    lower = jnp.logical_and(ri >= ci, same_chunk)
    strict = jnp.logical_and(ri > ci, same_chunk)
    same_sub = _idiv(ri, DN_SUB) == _idiv(ci, DN_SUB)
    tri = jnp.where(lower, 1.0, 0.0).astype(F32)
    tri_t = jnp.where(jnp.logical_and(ri <= ci, same_chunk), 1.0, 0.0).astype(F32)
    prow = prow_ref[...]
    pcol = pcol_ref[...]

    conv, beta_tm, gcum_tm, gcum_hm, egc_tm = [], [], [], [], []
    for b in range(nbb):
        x = jnp.concatenate([xqk_ref[b], xv_ref[b]], axis=1).astype(F32)
        xext[b, 8:8 + L, :] = x
        cv = x * cw_ref[3:4, :]
        for w in range(DN_CONV_W - 1):
            cv = cv + xext[b, 5 + w:5 + w + L, :] * cw_ref[w:w + 1, :]
        conv.append(_silu(cv))
        xext[b, 0:8, :] = x[L - 8:L, :]
        cout_ref[b] = x[L - (DN_CONV_W - 1):L, :]
        sm = sm_ref[b]
        beta_tm.append(_sigmoid(sm))
        g_tm = prow[0:1, :] * _softplus(sm + prow[1:2, :])
        g_hm = pcol[8:16, 0:1] * _softplus(smt_ref[b][8:16, :] + pcol[8:16, 1:2])
        gcum_tm.append(_dot_exact(tri, g_tm))
        gcum_hm.append(_dot_exact(g_hm, tri_t))
        egc_tm.append(jnp.exp(gcum_tm[b]))

    n_sub = L // chunk
    rep = DN_V_HEADS // DN_QK_HEADS
    bdot = lambda x, y: jnp.dot(x, y, preferred_element_type=F32)
    for h0 in range(0, DN_V_HEADS, DN_HEAD_GROUP):
        heads = [(b, hh) for b in range(nbb) for hh in range(h0, h0 + DN_HEAD_GROUP)]
        q, k, kk, qk = {}, {}, {}, {}
        for b in range(nbb):
            for j in range(h0 // rep, (h0 + DN_HEAD_GROUP) // rep):
                qj = conv[b][:, j * DN_DK:(j + 1) * DN_DK]
                kj = conv[b][:, DN_QK + j * DN_DK:DN_QK + (j + 1) * DN_DK]
                qj = qj * lax.rsqrt(jnp.sum(qj * qj, -1, keepdims=True) + 1e-6) * (DN_DK ** -0.5)
                kj = kj * lax.rsqrt(jnp.sum(kj * kj, -1, keepdims=True) + 1e-6)
                kq = _dot_nt(jnp.concatenate([kj, qj], axis=0), kj)
                q[b, j], k[b, j], kk[b, j], qk[b, j] = qj, kj, kq[0:L], kq[L:2 * L]
        gc = {(b, hh): gcum_tm[b][:, 8 + hh:9 + hh] for b, hh in heads}
        beta = {(b, hh): beta_tm[b][:, hh:hh + 1] for b, hh in heads}
        eg = {(b, hh): egc_tm[b][:, 8 + hh:9 + hh] for b, hh in heads}
        rel = {(b, hh): jnp.where(lower, jnp.exp(jnp.where(lower, gc[b, hh] - gcum_hm[b][hh:hh + 1, :], 0.0)),
                                  0.0) for b, hh in heads}
        a = {(b, hh): jnp.where(strict, beta[b, hh] * kk[b, hh // rep] * rel[b, hh], 0.0) for b, hh in heads}
        attn = {(b, hh): qk[b, hh // rep] * rel[b, hh] for b, hh in heads}
        pd = {hh: jnp.where(same_sub, -a[hh], 0.0) for hh in heads}
        e = {hh: a[hh] + pd[hh] for hh in heads}
        eb = {hh: e[hh].astype(BF16) for hh in heads}
        xacc = pd
        xaccb = {hh: pd[hh].astype(BF16) for hh in heads}
        pwb = xaccb
        for _ in range(3):
            pw = {hh: bdot(pwb[hh], pwb[hh]) for hh in heads}
            pwb = {hh: pw[hh].astype(BF16) for hh in heads}
            xacc = {hh: xacc[hh] + pw[hh] + bdot(xaccb[hh], pwb[hh]) for hh in heads}
            xaccb = {hh: xacc[hh].astype(BF16) for hh in heads}
        f = {hh: e[hh] + bdot(xaccb[hh], eb[hh]) for hh in heads}
        fb = {hh: f[hh].astype(BF16) for hh in heads}
        f2 = {hh: bdot(fb[hh], fb[hh]) for hh in heads}
        y = {hh: f2[hh] - f[hh] - bdot(fb[hh], f2[hh].astype(BF16)) for hh in heads}
        rinv = {hh: xacc[hh] + y[hh] + bdot(y[hh].astype(BF16), xaccb[hh]) for hh in heads}
        sol = {}
        for b, hh in heads:
            vh = conv[b][:, 2 * DN_QK + hh * DN_DV:2 * DN_QK + (hh + 1) * DN_DV]
            rhs = jnp.concatenate([vh * beta[b, hh], k[b, hh // rep] * (beta[b, hh] * eg[b, hh])], axis=1)
            sol[b, hh] = rhs + _dot(rinv[b, hh], rhs)
        u = {p: sol[p][:, 0:DN_DV] for p in heads}
        wmat = {p: sol[p][:, DN_DV:DN_DV + DN_DK] for p in heads}
        qe = {(b, hh): q[b, hh // rep] * eg[b, hh] for b, hh in heads}
        s = {(b, hh): s_scr[b, hh] for b, hh in heads}
        v_new = {p: [] for p in heads}
        q_s = {p: [] for p in heads}
        for sc in range(n_sub):
            r0 = sc * chunk
            for b, hh in heads:
                p = (b, hh)
                ws = _dot(jnp.concatenate([wmat[p][r0:r0 + chunk], qe[p][r0:r0 + chunk]], axis=0), s[p])
                vn = u[p][r0:r0 + chunk] - ws[0:chunk]
                v_new[p].append(vn)
                q_s[p].append(ws[chunk:2 * chunk])
                g_last = gcum_tm[b][r0 + chunk - 1:r0 + chunk, 8 + hh:9 + hh]
                kd = k[b, hh // rep][r0:r0 + chunk] * jnp.exp(g_last - gc[p][r0:r0 + chunk])
                s[p] = s[p] * jnp.exp(g_last) + _dot_tn(kd, vn)
        for b, hh in heads:
            p = (b, hh)
            s_scr[b, hh] = s[p]
            sout_ref[b, hh] = s[p]
            vn = v_new[p][0] if n_sub == 1 else jnp.concatenate(v_new[p], axis=0)
            qs = q_s[p][0] if n_sub == 1 else jnp.concatenate(q_s[p], axis=0)
            o = qs + _dot(attn[p], vn)
            zh = z_ref[b, :, hh * DN_DV:(hh + 1) * DN_DV].astype(F32)
            n = o * lax.rsqrt(jnp.mean(o * o, -1, keepdims=True) + 1e-6) * ng_ref[...]
            o_ref[b, :, hh * DN_DV:(hh + 1) * DN_DV] = (n * _silu(zh)).astype(BF16)


def _deltanet(proj, small, small_t, conv_w, conv_init, prow, pcol, norm_g, s0, blk_len, chunk):
    nb, t = proj.shape[:2]
    nblk = t // blk_len
    nbb = max(DN_STREAMS, min(nb, DN_STREAM_ROWS // blk_len))
    grid = (nb // nbb, nblk)
    return pl.pallas_call(
        functools.partial(_dn_kernel, blk_len=blk_len, chunk=chunk),
        grid=grid,
        in_specs=[
            pl.BlockSpec((nbb, blk_len, 1024), lambda b, i: (b, i, 3072 // 1024)),
            pl.BlockSpec((nbb, blk_len, 1024), lambda b, i: (b, i, 4096 // 1024)),
            pl.BlockSpec((nbb, blk_len, 1024), lambda b, i: (b, i, 5120 // 1024)),
            pl.BlockSpec((nbb, blk_len, LANES), lambda b, i: (b, i, 0)),
            pl.BlockSpec((nbb, 16, blk_len), lambda b, i: (b, 0, i)),
            pl.BlockSpec((DN_CONV_W, DN_CONV_CH), lambda b, i: (0, 0)),
            pl.BlockSpec((nbb, 8, DN_CONV_CH), lambda b, i: (b, 0, 0)),
            pl.BlockSpec((8, LANES), lambda b, i: (0, 0)),
            pl.BlockSpec((16, LANES), lambda b, i: (0, 0)),
            pl.BlockSpec((1, DN_DV), lambda b, i: (0, 0)),
            pl.BlockSpec((nbb, DN_V_HEADS, DN_DK, DN_DV), lambda b, i: (b, 0, 0, 0)),
        ],
        out_specs=[
            pl.BlockSpec((nbb, blk_len, 1024), lambda b, i: (b, i, 0)),
            pl.BlockSpec((nbb, DN_V_HEADS, DN_DK, DN_DV), lambda b, i: (b, 0, 0, 0)),
            pl.BlockSpec((nbb, DN_CONV_W - 1, DN_CONV_CH), lambda b, i: (b, 0, 0)),
        ],
        out_shape=[
            jax.ShapeDtypeStruct((nb, t, 1024), BF16),
            jax.ShapeDtypeStruct((nb, DN_V_HEADS, DN_DK, DN_DV), F32),
            jax.ShapeDtypeStruct((nb, DN_CONV_W - 1, DN_CONV_CH), F32),
        ],
        scratch_shapes=[pltpu.VMEM((nbb, blk_len + 8, DN_CONV_CH), F32),
                        pltpu.VMEM((nbb, DN_V_HEADS, DN_DK, DN_DV), F32)],
        compiler_params=_params("parallel", "arbitrary"),
        name="deltanet",
    )(proj, proj, proj, small, small_t, conv_w, conv_init, prow, pcol, norm_g, s0)


def _memkv_kernel(m_ref, w_ref, o_ref):
    o_ref[...] = _dot(m_ref[...], w_ref[...])


def _memkv(mem, w_kv):
    n = mem.shape[0]
    tn = 1024
    return pl.pallas_call(
        _memkv_kernel,
        grid=(w_kv.shape[1] // tn,),
        in_specs=[pl.BlockSpec((n, D_MODEL), lambda j: (0, 0)),
                  pl.BlockSpec((D_MODEL, tn), lambda j: (0, j))],
        out_specs=pl.BlockSpec((n, tn), lambda j: (0, j)),
        out_shape=jax.ShapeDtypeStruct((n, w_kv.shape[1]), F32),
        compiler_params=_params("parallel"),
        name="memkv",
    )(mem, w_kv)


def _mix_kernel(h_ref, ret_ref, dn_ref, mk_ref, mv_ref, wxq_ref, wg_ref, wb_ref, wo_ref,
                g1_ref, b1_ref, h1_ref, mem_scr, *, n_batch, tiles_per_batch):
    @pl.when(pl.program_id(0) % tiles_per_batch == 0)
    def _():
        for b in range(n_batch):
            for hh in range(XA_HEADS):
                mem_scr[0, hh, b] = mk_ref[0, b, :, hh, :].astype(BF16)
                mem_scr[1, hh, b] = mv_ref[0, b, :, hh, :].astype(BF16)

    hb = h_ref[...].astype(BF16)
    xq = jnp.dot(hb, wxq_ref[...], preferred_element_type=F32)
    tb = xq.shape[0] // n_batch
    pairs = [(b, hh) for b in range(n_batch) for hh in range(XA_HEADS)]
    s = [_dot_nt(xq[b * tb:(b + 1) * tb, hh * XA_DH:(hh + 1) * XA_DH], mem_scr[0, hh, b]) * (XA_DH ** -0.5)
         for b, hh in pairs]
    p = [jnp.exp(si - jnp.max(si, -1, keepdims=True)) for si in s]
    p = [pi / jnp.sum(pi, -1, keepdims=True) for pi in p]
    xo = [_dot(pi, mem_scr[1, hh, b]) for pi, (b, hh) in zip(p, pairs)]
    xo = [jnp.concatenate(xo[b * XA_HEADS:(b + 1) * XA_HEADS], axis=1) for b in range(n_batch)]
    xo = xo[0] if n_batch == 1 else jnp.concatenate(xo, axis=0)
    mixed = None
    for n, br in enumerate((ret_ref[...], dn_ref[...], xo)):
        gate = _sigmoid(jnp.dot(hb, wg_ref[:, n * D_MODEL:(n + 1) * D_MODEL], preferred_element_type=F32))
        term = gate * _dot(br, wb_ref[n])
        mixed = term if mixed is None else mixed + term
    y = ALPHA * h_ref[...] + _dot(mixed, wo_ref[...])
    h1 = _layernorm(y, g1_ref[...], b1_ref[...])
    h1_ref[...] = h1


def _mix(h, ret_o, dn_o, mem_k, mem_v, w_xq, w_gate, w_branch, w_out, ln_g, ln_b, tm, t):
    n_tok = h.shape[0]
    const2 = lambda i: (0, 0)
    row = lambda i: (i, 0)
    single = pl.Buffered(1)
    if tm <= t:
        n_batch = 1
        tiles_per_batch = t // tm
    else:
        n_batch = tm // t
        tiles_per_batch = 1
    mem_spec = pl.BlockSpec((1, n_batch, N_MEM, XA_HEADS, XA_DH), lambda i: (0, i // tiles_per_batch, 0, 0, 0))
    return pl.pallas_call(
        functools.partial(_mix_kernel, n_batch=n_batch, tiles_per_batch=tiles_per_batch),
        grid=(n_tok // tm,),
        in_specs=[
            pl.BlockSpec((tm, D_MODEL), row),
            pl.BlockSpec((tm, D_MODEL), row),
            pl.BlockSpec((tm, D_MODEL), row),
            mem_spec,
            mem_spec,
            pl.BlockSpec((D_MODEL, D_MODEL), const2, pipeline_mode=single),
            pl.BlockSpec((D_MODEL, 3 * D_MODEL), const2, pipeline_mode=single),
            pl.BlockSpec((3, D_MODEL, D_MODEL), lambda i: (0, 0, 0), pipeline_mode=single),
            pl.BlockSpec((D_MODEL, D_MODEL), const2, pipeline_mode=single),
            pl.BlockSpec((1, D_MODEL), const2),
            pl.BlockSpec((1, D_MODEL), const2),
        ],
        out_specs=pl.BlockSpec((tm, D_MODEL), row),
        out_shape=jax.ShapeDtypeStruct((n_tok, D_MODEL), F32),
        scratch_shapes=[pltpu.VMEM((2, XA_HEADS, n_batch, N_MEM, XA_DH), BF16)],
        compiler_params=_params("arbitrary"),
        name="mix",
    )(h, ret_o, dn_o, mem_k, mem_v, w_xq, w_gate, w_branch, w_out, ln_g, ln_b)


def _route_t(x, wrt, brt):
    xh = x.astype(BF16)
    xl = (x - xh.astype(F32)).astype(BF16)
    wh = wrt.astype(BF16)
    wl = (wrt - wh.astype(F32)).astype(BF16)
    nt = lambda a, b: lax.dot_general(a, b, (((1,), (1,)), ((), ())), preferred_element_type=F32)
    logits = nt(wh, xh) + (nt(wl, xh) + nt(wh, xl)) + brt
    row_i = lax.broadcasted_iota(jnp.int32, logits.shape, 0)
    row = row_i.astype(F32)
    far = jnp.float32(LANES)
    neg = jnp.float32(-3.0e38)
    is_c = jnp.logical_and(row_i >= MOE_NE, row_i < MOE_NE + MOE_GROUPS)
    cl = jnp.where(is_c, logits, neg)
    cmax = jnp.max(cl, 0, keepdims=True)
    denom = jnp.sum(jnp.where(is_c, jnp.exp(jnp.where(is_c, logits - cmax, 0.0)), 0.0), 0, keepdims=True)
    p_grp = 1.0 / denom
    grp = jnp.min(jnp.where(jnp.logical_and(is_c, cl == cmax), row - MOE_NE, far), 0, keepdims=True)
    in_grp = jnp.logical_and(row_i < MOE_NE, _idiv(row_i, MOE_EXPERTS).astype(F32) == grp)
    fl = jnp.where(in_grp, logits, neg)
    v1 = jnp.max(fl, 0, keepdims=True)
    i1 = jnp.min(jnp.where(jnp.logical_and(in_grp, fl == v1), row, far), 0, keepdims=True)
    rest = jnp.logical_and(in_grp, row != i1)
    fl2 = jnp.where(rest, logits, neg)
    v2 = jnp.max(fl2, 0, keepdims=True)
    i2 = jnp.min(jnp.where(jnp.logical_and(rest, fl2 == v2), row, far), 0, keepdims=True)
    e2 = jnp.exp(v2 - v1)
    w1 = p_grp / (1.0 + e2)
    w2 = p_grp * e2 / (1.0 + e2)
    comb_t = jnp.where(row == i1, w1, 0.0) + jnp.where(row == i2, w2, 0.0)
    return jnp.where(row_i == LANES - 1, grp, comb_t)


def _route_kernel(h1_ref, wrt_ref, brt_ref, comb_ref, cnt_ref):
    comb = _route_t(h1_ref[...], wrt_ref[...], brt_ref[...]).T
    comb_ref[...] = comb
    lane = lax.broadcasted_iota(jnp.int32, comb.shape, 1).astype(F32)
    onehot = jnp.where(lane == comb[:, LANES - 1:LANES], 1.0, 0.0)
    cnt_ref[0] = jnp.broadcast_to(jnp.sum(onehot, 0, keepdims=True), (8, LANES))


def _route_call(h1, w_route, b_route, tm):
    n_tok = h1.shape[0]
    return pl.pallas_call(
        _route_kernel,
        grid=(n_tok // tm,),
        in_specs=[pl.BlockSpec((tm, D_MODEL), lambda i: (i, 0)),
                  pl.BlockSpec((LANES, D_MODEL), lambda i: (0, 0)),
                  pl.BlockSpec((LANES, 1), lambda i: (0, 0))],
        out_specs=[pl.BlockSpec((tm, LANES), lambda i: (i, 0)),
                   pl.BlockSpec((1, 8, LANES), lambda i: (i, 0, 0))],
        out_shape=[jax.ShapeDtypeStruct((n_tok, LANES), F32),
                   jax.ShapeDtypeStruct((n_tok // tm, 8, LANES), F32)],
        compiler_params=_params("parallel"),
        name="route",
    )(h1, w_route, b_route)


MOE_ALIGN = 16
MOE_EPS = 4


def _moe_sizes(tm):
    base = tm // MOE_GROUPS
    fine = {min(tm, base + MOE_ALIGN * i) for i in range(5)}
    return tuple(sorted(fine | {tm // 2, tm}))


def _moe_kernel(offs_ref, cls_ref, h1_ref, comb_ref, tri_ref, wg_ref, wu_ref, wd_ref,
                g2_ref, b2_ref, y_ref, pt_scr, xs_scr, cs_scr, ys_scr, *, tm, nr1, sizes):
    i = pl.program_id(0)
    e = pl.program_id(1)
    nrt = xs_scr.shape[0]

    @pl.when(e == 0)
    def _():
        comb = comb_ref[...]
        lane_i = lax.broadcasted_iota(jnp.int32, (tm, LANES), 1)
        grp = comb[:, LANES - 1:LANES]
        is_g = lane_i.astype(F32) == grp
        prefix = jnp.dot(tri_ref[...], jnp.where(is_g, 1.0, 0.0).astype(BF16), preferred_element_type=F32)
        rank = jnp.sum(jnp.where(is_g, prefix, 0.0), -1, keepdims=True)
        offv = jnp.zeros((tm, 1), F32)
        for g in range(MOE_GROUPS):
            offv = jnp.where(grp == g, offs_ref[i * MOE_GROUPS + g].astype(F32), offv)
        pos = offv + rank
        col = lax.broadcasted_iota(jnp.int32, (tm, nr1), 1).astype(F32)
        pt = jnp.where(col == pos, 1.0, 0.0).astype(BF16)
        pt_scr[...] = pt
        combw = jnp.where(lane_i < MOE_NE, comb, 0.0)
        c_hi = combw.astype(BF16)
        c_lo = (combw - c_hi.astype(F32)).astype(BF16)
        xs_scr[0:nr1, :] = _dot_tn(pt, h1_ref[...]).astype(BF16)
        c_pair = _dot_tn(pt, jnp.concatenate([c_hi, c_lo], axis=1))
        cs_scr[0:nr1, :] = c_pair[:, 0:LANES] + c_pair[:, LANES:2 * LANES]
        xs_scr[nr1:nrt, :] = jnp.zeros((nrt - nr1, D_MODEL), BF16)
        cs_scr[nr1:nrt, :] = jnp.zeros((nrt - nr1, LANES), F32)
        ys_scr[tm:nr1, :] = jnp.zeros((nr1 - tm, D_MODEL), F32)

    g = e // (MOE_EXPERTS // MOE_EPS)
    off = pl.multiple_of(offs_ref[i * MOE_GROUPS + g], MOE_ALIGN)
    cls = cls_ref[i * MOE_GROUPS + g]
    first = e % (MOE_EXPERTS // MOE_EPS) == 0
    for ci, m in enumerate(sizes):
        @pl.when(cls == ci)
        def _(m=m):
            rows = pl.ds(off, m)
            xb = xs_scr[rows, :]
            cb = cs_scr[rows, :]
            lane_i = lax.broadcasted_iota(jnp.int32, (m, LANES), 1)
            sub = range(MOE_EPS)
            colw = [jnp.sum(jnp.where(lane_i == e * MOE_EPS + j, cb, 0.0), -1, keepdims=True) for j in sub]
            gate = [jnp.dot(xb, wg_ref[j], preferred_element_type=F32) for j in sub]
            up = [jnp.dot(xb, wu_ref[j], preferred_element_type=F32) for j in sub]
            hg = [(_silu(gate[j]) * up[j] * colw[j]).astype(BF16) for j in sub]
            wd = wd_ref[...].reshape(MOE_EPS * MOE_DFF, D_MODEL)
            res = jnp.dot(jnp.concatenate(hg, axis=1), wd, preferred_element_type=F32)

            @pl.when(first)
            def _():
                ys_scr[rows, :] = res

            @pl.when(jnp.logical_not(first))
            def _():
                ys_scr[rows, :] += res

    @pl.when(e == MOE_NE // MOE_EPS - 1)
    def _():
        moe = jnp.dot(pt_scr[...], ys_scr[0:nr1, :].astype(BF16), preferred_element_type=F32)
        y_ref[...] = _layernorm(ALPHA * h1_ref[...] + moe, g2_ref[...], b2_ref[...])


def _moe(h1, comb, cnt, w_gate, w_up, w_down, ln_g, ln_b, tm):
    n_tok = h1.shape[0]
    sizes = _moe_sizes(tm)
    nr1 = -(-(tm + MOE_GROUPS * MOE_ALIGN) // LANES) * LANES
    nrt = nr1 + max(b - a for a, b in zip((0,) + sizes, sizes))
    c = cnt[:, 0, :MOE_GROUPS].astype(jnp.int32)
    seg = (c + MOE_ALIGN - 1) // MOE_ALIGN * MOE_ALIGN
    offs = (jnp.cumsum(seg, axis=1) - seg).reshape(-1)
    cls = sum((c > s).astype(jnp.int32) for s in sizes[:-1]).reshape(-1)
    tri =jnp.asarray(np.tril(np.ones((tm, tm), np.float32), -1), BF16)
    grid_spec = pltpu.PrefetchScalarGridSpec(
        num_scalar_prefetch=2,
        grid=(n_tok // tm, MOE_NE // MOE_EPS),
        in_specs=[
            pl.BlockSpec((tm, D_MODEL), lambda i, e, *_: (i, 0)),
            pl.BlockSpec((tm, LANES), lambda i, e, *_: (i, 0)),
            pl.BlockSpec((tm, tm), lambda i, e, *_: (0, 0), pipeline_mode=pl.Buffered(1)),
            pl.BlockSpec((MOE_EPS, D_MODEL, MOE_DFF), lambda i, e, *_: (e, 0, 0)),
            pl.BlockSpec((MOE_EPS, D_MODEL, MOE_DFF), lambda i, e, *_: (e, 0, 0)),
            pl.BlockSpec((MOE_EPS, MOE_DFF, D_MODEL), lambda i, e, *_: (e, 0, 0)),
            pl.BlockSpec((1, D_MODEL), lambda i, e, *_: (0, 0)),
            pl.BlockSpec((1, D_MODEL), lambda i, e, *_: (0, 0)),
        ],
        out_specs=pl.BlockSpec((tm, D_MODEL), lambda i, e, *_: (i, 0)),
        scratch_shapes=[pltpu.VMEM((tm, nr1), BF16), pltpu.VMEM((nrt, D_MODEL), BF16),
                        pltpu.VMEM((nrt, LANES), F32), pltpu.VMEM((nrt, D_MODEL), F32)],
    )
    return pl.pallas_call(
        functools.partial(_moe_kernel, tm=tm, nr1=nr1, sizes=sizes),
        grid_spec=grid_spec,
        out_shape=jax.ShapeDtypeStruct((n_tok, D_MODEL), F32),
        compiler_params=_params("parallel", "arbitrary"),
        name="moe",
    )(offs, cls, h1, comb, tri, w_gate, w_up, w_down, ln_g, ln_b)


def _rope_tables(start, t):
    half = RET_DK // 2
    inv = 1.0 / (ROPE_BASE ** (np.arange(half, dtype=np.float64) / half))
    ang = (start + np.arange(t, dtype=np.float64))[:, None] * inv[None, :]
    cos = np.cos(ang)
    sin = np.sin(ang)
    return (jnp.asarray(np.concatenate([cos, cos], -1), F32),
            jnp.asarray(np.concatenate([-sin, sin], -1), F32))


def _group(x, pos0, mem_k, mem_v, ret_s0, dn_s0, conv_buf, wts, *, tm, tm_mix, tm_moe, ret_chunk, dn_blk,
           dn_chunk):
    nb, t, d = x.shape
    n_tok = nb * t
    h, small, small_t, proj = _in_proj(x.reshape(n_tok, d), wts["ln_in_g"], wts["ln_in_b"],
                                           wts["w_in"], wts["w_small"], wts["w_small_t"], tm)
    proj3 = proj.reshape(nb, t, PROJ_W)
    cos2, sin2 = _rope_tables(pos0, t)
    ret_o, ret_s = _retention(proj3, cos2, sin2, ret_s0, wts["ret_gn_g"], wts["ret_gn_b"], ret_chunk)
    small_t3 = small_t.reshape(16, nb, t).transpose(1, 0, 2)
    conv_init = jnp.concatenate([jnp.zeros((nb, 8 - (DN_CONV_W - 1), DN_CONV_CH), F32), conv_buf], axis=1)
    dn_o, dn_s, conv_new = _deltanet(proj3, small.reshape(nb, t, LANES), small_t3, wts["dn_conv"], conv_init,
                                     wts["dn_prow"],
                                     wts["dn_pcol"], wts["dn_norm_g"], dn_s0, dn_blk, dn_chunk)
    h1 = _mix(h, ret_o.reshape(n_tok, d), dn_o.reshape(n_tok, d), mem_k, mem_v,
                   wts["w_xq"], wts["w_gate"], wts["w_branch"], wts["w_out"], wts["ln1_g"], wts["ln1_b"],
                   tm_mix, t)
    comb, cnt = _route_call(h1, wts["w_route"], wts["b_route"], tm_moe)
    y = _moe(h1, comb, cnt, wts["moe_w_gate"], wts["moe_w_up"], wts["moe_w_down"],
             wts["ln2_g"], wts["ln2_b"], tm_moe)
    return y.reshape(nb, t, d), ret_s, dn_s, conv_new


def kernel(x_prompt, x_sample, mem_prompt, state_ret, state_dn, state_dn_conv, cache_mem_k, cache_mem_v,
           ln_in_g, ln_in_b, w_in, ret_gn_g, ret_gn_b, dn_conv, dn_A_log, dn_dt_bias, dn_norm_g,
           w_mem_kv, w_branch, w_out, ln1_g, ln1_b, moe_w_coarse, moe_b_coarse, moe_w_fine, moe_b_fine,
           moe_w_gate, moe_w_up, moe_w_down, ln2_g, ln2_b):
    bp, tp, d = x_prompt.shape
    bs, ts, _ = x_sample.shape
    l = 0
    wi = w_in[l][:, 0:PROJ_W].astype(BF16)
    dba = w_in[l][:, PROJ_W:PROJ_W + 16].astype(BF16)
    xq, gates = w_in[l][:, 6160:7184].astype(BF16), w_in[l][:, 7184:10256].astype(BF16)
    wts = {
        "ln_in_g": ln_in_g.reshape(1, d), "ln_in_b": ln_in_b.reshape(1, d),
        "w_in": wi,
        "w_small": jnp.pad(dba, ((0, 0), (0, LANES - 16))),
        "w_small_t": dba.T,
        "ret_gn_g": ret_gn_g[l].reshape(1, -1), "ret_gn_b": ret_gn_b[l].reshape(1, -1),
        "dn_conv": dn_conv[l],
        "dn_prow": jnp.zeros((8, LANES), F32).at[0, 8:16].set(-jnp.exp(dn_A_log[l])).at[1, 8:16].set(dn_dt_bias[l]),
        "dn_pcol": jnp.zeros((16, LANES), F32).at[8:16, 0].set(-jnp.exp(dn_A_log[l])).at[8:16, 1].set(dn_dt_bias[l]),
        "dn_norm_g": dn_norm_g[l].reshape(1, -1),
        "w_xq": xq, "w_gate": gates,
        "w_branch": w_branch[l].astype(BF16), "w_out": w_out[l].astype(BF16),
        "ln1_g": ln1_g[l].reshape(1, d), "ln1_b": ln1_b[l].reshape(1, d),
        "w_route": jnp.pad(jnp.concatenate([moe_w_fine[l], moe_w_coarse[l]], axis=1).T,
                           ((0, LANES - MOE_NE - MOE_GROUPS), (0, 0))),
        "b_route": jnp.pad(jnp.concatenate([moe_b_fine[l], moe_b_coarse[l]]),
                           (0, LANES - MOE_NE - MOE_GROUPS)).reshape(LANES, 1),
        "moe_w_gate": moe_w_gate[l].reshape(MOE_NE, d, MOE_DFF).astype(BF16),
        "moe_w_up": moe_w_up[l].reshape(MOE_NE, d, MOE_DFF).astype(BF16),
        "moe_w_down": moe_w_down[l].reshape(MOE_NE, MOE_DFF, d).astype(BF16),
        "ln2_g": ln2_g[l].reshape(1, d), "ln2_b": ln2_b[l].reshape(1, d),
    }

    mkv = _memkv(mem_prompt.reshape(bp * N_MEM, d), w_mem_kv[l].astype(BF16))
    mk = mkv[:, :XA_HEADS * XA_DH].reshape(1, bp, N_MEM, XA_HEADS, XA_DH)
    mv = mkv[:, XA_HEADS * XA_DH:].reshape(1, bp, N_MEM, XA_HEADS, XA_DH)
    yp, rs_p, ds_p, cb_p = _group(
        x_prompt, 0, mk, mv,
        jnp.zeros((bp, RET_HEADS, RET_DK, RET_DV), F32),
        jnp.zeros((bp, DN_V_HEADS, DN_DK, DN_DV), F32),
        jnp.zeros((bp, DN_CONV_W - 1, DN_CONV_CH), F32),
        wts, tm=min(1024, tp), tm_mix=min(512, tp), tm_moe=min(1024, tp), ret_chunk=min(256, tp),
        dn_blk=min(128, tp),
        dn_chunk=min(DN_CHUNK, tp))
    ys, rs_s, ds_s, cb_s = _group(
        x_sample, PAST_LEN, cache_mem_k[l:l + 1], cache_mem_v[l:l + 1],
        state_ret[l], state_dn[l], state_dn_conv[l],
        wts, tm=bs * ts, tm_mix=bs * ts, tm_moe=bs * ts, ret_chunk=ts, dn_blk=ts, dn_chunk=min(DN_CHUNK, ts))
    return (yp, ys, rs_p[None], rs_s[None], ds_p[None], ds_s[None], cb_p[None], cb_s[None], mk, mv)
```

```python
import functools
import math

import numpy as np

import jax
import jax.numpy as jnp
from jax import lax
from jax.experimental import pallas as pl
from jax.experimental.pallas import tpu as pltpu

F32 = jnp.float32
BF16 = jnp.bfloat16

D_MODEL = 1024
PAST_LEN = 1024
RET_HEADS = 4
RET_DK = 128
RET_DV = 256
ROPE_BASE = 10000.0
DN_QK_HEADS = 4
DN_V_HEADS = 8
DN_DK = 128
DN_DV = 128
DN_QK = DN_QK_HEADS * DN_DK
DN_CONV_W = 4
DN_CONV_CH = 2048
DN_CHUNK = 64
DN_SUB = 16
DN_HEAD_GROUP = 8
DN_STREAMS = 2
DN_STREAM_ROWS = 256
XA_HEADS = 4
XA_DH = 256
N_MEM = 256
MOE_GROUPS = 4
MOE_EXPERTS = 8
MOE_NE = MOE_GROUPS * MOE_EXPERTS
MOE_DFF = 256
MOE_ROUTE_ROWS = 40
DEPTH = 1
ALPHA = (2.0 * DEPTH) ** 0.25
LANES = 128
PROJ_W = 6144
VMEM_LIMIT = 56 * 1024 * 1024


def _dot(a, b):
    return jnp.dot(a.astype(BF16), b.astype(BF16), preferred_element_type=F32)


def _dot_nt(a, b):
    return lax.dot_general(a.astype(BF16), b.astype(BF16), (((1,), (1,)), ((), ())),
                           preferred_element_type=F32)


def _dot_tn(a, b):
    return lax.dot_general(a.astype(BF16), b.astype(BF16), (((0,), (0,)), ((), ())),
                           preferred_element_type=F32)


def _dot_exact(a, b):
    return jnp.dot(a, b, preferred_element_type=F32, precision=lax.Precision.HIGHEST)


def _layernorm(x, g, b, eps=1e-5):
    mu = jnp.mean(x, -1, keepdims=True)
    xc = x - mu
    var = jnp.mean(xc * xc, -1, keepdims=True)
    return xc * lax.rsqrt(var + eps) * g + b


def _silu(x):
    return x * (1.0 / (1.0 + jnp.exp(-x)))


def _sigmoid(x):
    return 1.0 / (1.0 + jnp.exp(-x))


def _softplus(x):
    return jnp.maximum(x, 0.0) + jnp.log(1.0 + jnp.exp(-jnp.abs(x)))


def _idiv(x, n):
    return jnp.right_shift(x, int(math.log2(n)))


def _params(*sem):
    return pltpu.CompilerParams(dimension_semantics=sem, vmem_limit_bytes=VMEM_LIMIT)


def _in_proj_kernel(x_ref, g_ref, b_ref, w_ref, ws_ref, wst_ref,
                    h_ref, small_ref, smallt_ref, proj_ref, hs_ref):
    @pl.when(pl.program_id(1) == 0)
    def _():
        h = _layernorm(x_ref[...], g_ref[...], b_ref[...])
        h_ref[...] = h
        hb = h.astype(BF16)
        hs_ref[...] = hb
        small_ref[...] = jnp.dot(hb, ws_ref[...], preferred_element_type=F32)
        smallt_ref[...] = lax.dot_general(wst_ref[...], hb, (((1,), (1,)), ((), ())),
                                          preferred_element_type=F32)

    proj_ref[...] = jnp.dot(hs_ref[...], w_ref[...], preferred_element_type=F32).astype(BF16)


def _in_proj(x, ln_g, ln_b, w_main, w_small, w_small_t, tm, tn=2048):
    n_tok = x.shape[0]
    grid = (n_tok // tm, PROJ_W // tn)
    return pl.pallas_call(
        _in_proj_kernel,
        grid=grid,
        in_specs=[
            pl.BlockSpec((tm, D_MODEL), lambda i, n: (i, 0)),
            pl.BlockSpec((1, D_MODEL), lambda i, n: (0, 0)),
            pl.BlockSpec((1, D_MODEL), lambda i, n: (0, 0)),
            pl.BlockSpec((D_MODEL, tn), lambda i, n: (0, n)),
            pl.BlockSpec((D_MODEL, LANES), lambda i, n: (0, 0)),
            pl.BlockSpec((16, D_MODEL), lambda i, n: (0, 0)),
        ],
        out_specs=[
            pl.BlockSpec((tm, D_MODEL), lambda i, n: (i, 0)),
            pl.BlockSpec((tm, LANES), lambda i, n: (i, 0)),
            pl.BlockSpec((16, tm), lambda i, n: (0, i)),
            pl.BlockSpec((tm, tn), lambda i, n: (i, n)),
        ],
        out_shape=[
            jax.ShapeDtypeStruct((n_tok, D_MODEL), F32),
            jax.ShapeDtypeStruct((n_tok, LANES), F32),
            jax.ShapeDtypeStruct((16, n_tok), F32),
            jax.ShapeDtypeStruct((n_tok, PROJ_W), BF16),
        ],
        scratch_shapes=[pltpu.VMEM((tm, D_MODEL), BF16)],
        compiler_params=_params("parallel", "arbitrary"),
        name="in_proj",
    )(x, ln_g, ln_b, w_main, w_small, w_small_t)


def _ret_kernel(q_ref, k_ref, v_ref, rg_ref, cos_ref, sin_ref, s0_ref, gng_ref, gnb_ref,
                o_ref, sout_ref, s_scr, dec_scr, *, chunk):
    heads = range(RET_HEADS)
    lg = [math.log(1.0 - 2.0 ** (-5.0 - h)) for h in heads]

    @pl.when(pl.program_id(1) == 0)
    def _():
        s_scr[...] = s0_ref[0]
        ri = lax.broadcasted_iota(jnp.int32, (chunk, chunk), 0)
        ci = lax.broadcasted_iota(jnp.int32, (chunk, chunk), 1)
        causal = ri >= ci
        diff = jnp.where(causal, (ri - ci).astype(F32), 0.0)
        for h in heads:
            dec_scr[h] = jnp.where(causal, jnp.exp(diff * lg[h]), 0.0)

    cos = cos_ref[...]
    sin = sin_ref[...]
    idx = lax.broadcasted_iota(jnp.int32, (chunk, 1), 0).astype(F32)
    q = [q_ref[0, :, h * RET_DK:(h + 1) * RET_DK].astype(F32) for h in heads]
    k = [k_ref[0, :, h * RET_DK:(h + 1) * RET_DK].astype(F32) for h in heads]
    v = [v_ref[0, :, h * RET_DV:(h + 1) * RET_DV] for h in heads]
    qr = [q[h] * cos + pltpu.roll(q[h], RET_DK // 2, 1) * sin for h in heads]
    kr = [(k[h] * cos + pltpu.roll(k[h], RET_DK // 2, 1) * sin) * (RET_DK ** -0.5) for h in heads]
    scores = [_dot_nt(qr[h], kr[h]) * dec_scr[h] for h in heads]
    s = [s_scr[h] for h in heads]
    cross = [_dot(qr[h], s[h]) * jnp.exp((idx + 1.0) * lg[h]) for h in heads]
    o = [_dot(scores[h], v[h]) + cross[h] for h in heads]
    kv = [_dot_tn(kr[h] * jnp.exp((chunk - 1.0 - idx) * lg[h]), v[h]) for h in heads]
    for h in heads:
        s_new = s[h] * math.exp(chunk * lg[h]) + kv[h]
        s_scr[h] = s_new
        sout_ref[0, h] = s_new
        sl = slice(h * RET_DV, (h + 1) * RET_DV)
        gated = _layernorm(o[h], gng_ref[:, sl], gnb_ref[:, sl]) * _silu(rg_ref[0, :, sl].astype(F32))
        o_ref[0, :, sl] = gated.astype(BF16)


def _retention(proj, cos2, sin2, s0, gn_g, gn_b, chunk):
    nb, t = proj.shape[:2]
    grid = (nb, t // chunk)
    qk_w = RET_HEADS * RET_DK
    v_w = RET_HEADS * RET_DV
    return pl.pallas_call(
        functools.partial(_ret_kernel, chunk=chunk),
        grid=grid,
        in_specs=[
            pl.BlockSpec((1, chunk, qk_w), lambda b, c: (b, c, 0)),
            pl.BlockSpec((1, chunk, qk_w), lambda b, c: (b, c, 512 // qk_w)),
            pl.BlockSpec((1, chunk, v_w), lambda b, c: (b, c, 1024 // v_w)),
            pl.BlockSpec((1, chunk, v_w), lambda b, c: (b, c, 2048 // v_w)),
            pl.BlockSpec((chunk, RET_DK), lambda b, c: (c, 0)),
            pl.BlockSpec((chunk, RET_DK), lambda b, c: (c, 0)),
            pl.BlockSpec((1, RET_HEADS, RET_DK, RET_DV), lambda b, c: (b, 0, 0, 0)),
            pl.BlockSpec((1, v_w), lambda b, c: (0, 0)),
            pl.BlockSpec((1, v_w), lambda b, c: (0, 0)),
        ],
        out_specs=[
            pl.BlockSpec((1, chunk, v_w), lambda b, c: (b, c, 0)),
            pl.BlockSpec((1, RET_HEADS, RET_DK, RET_DV), lambda b, c: (b, 0, 0, 0)),
        ],
        out_shape=[
            jax.ShapeDtypeStruct((nb, t, v_w), BF16),
            jax.ShapeDtypeStruct((nb, RET_HEADS, RET_DK, RET_DV), F32),
        ],
        scratch_shapes=[pltpu.VMEM((RET_HEADS, RET_DK, RET_DV), F32),
                        pltpu.VMEM((RET_HEADS, chunk, chunk), F32)],
        compiler_params=_params("parallel", "arbitrary"),
        name="retention",
    )(proj, proj, proj, proj, cos2, sin2, s0, gn_g, gn_b)


def _dn_kernel(xqk_ref, xv_ref, z_ref, sm_ref, smt_ref, cw_ref, cinit_ref, prow_ref, pcol_ref, ng_ref, s0_ref,
               o_ref, sout_ref, cout_ref, xext, s_scr, *, blk_len, chunk):
    L = blk_len
    nbb = xqk_ref.shape[0]
    blk = pl.program_id(1)

    @pl.when(blk == 0)
    def _():
        xext[:, 0:8, :] = cinit_ref[...]
        s_scr[...] = s0_ref[...]

    ri = lax.broadcasted_iota(jnp.int32, (L, L), 0)
    ci = lax.broadcasted_iota(jnp.int32, (L, L), 1)
    same_chunk = _idiv(ri, chunk) == _idiv(ci, chunk)
    lower = jnp.logical_and(ri >= ci, same_chunk)
    strict = jnp.logical_and(ri > ci, same_chunk)
    same_sub = _idiv(ri, DN_SUB) == _idiv(ci, DN_SUB)
    tri = jnp.where(lower, 1.0, 0.0).astype(F32)
    tri_t = jnp.where(jnp.logical_and(ri <= ci, same_chunk), 1.0, 0.0).astype(F32)
    prow = prow_ref[...]
    pcol = pcol_ref[...]

    conv, beta_tm, gcum_tm, gcum_hm, egc_tm = [], [], [], [], []
    for b in range(nbb):
        x = jnp.concatenate([xqk_ref[b], xv_ref[b]], axis=1).astype(F32)
        xext[b, 8:8 + L, :] = x
        cv = x * cw_ref[3:4, :]
        for w in range(DN_CONV_W - 1):
            cv = cv + xext[b, 5 + w:5 + w + L, :] * cw_ref[w:w + 1, :]
        conv.append(_silu(cv))
        xext[b, 0:8, :] = x[L - 8:L, :]
        cout_ref[b] = x[L - (DN_CONV_W - 1):L, :]
        sm = sm_ref[b]
        beta_tm.append(_sigmoid(sm))
        g_tm = prow[0:1, :] * _softplus(sm + prow[1:2, :])
        g_hm = pcol[8:16, 0:1] * _softplus(smt_ref[b][8:16, :] + pcol[8:16, 1:2])
        gcum_tm.append(_dot_exact(tri, g_tm))
        gcum_hm.append(_dot_exact(g_hm, tri_t))
        egc_tm.append(jnp.exp(gcum_tm[b]))

    n_sub = L // chunk
    rep = DN_V_HEADS // DN_QK_HEADS
    bdot = lambda x, y: jnp.dot(x, y, preferred_element_type=F32)
    for h0 in range(0, DN_V_HEADS, DN_HEAD_GROUP):
        heads = [(b, hh) for b in range(nbb) for hh in range(h0, h0 + DN_HEAD_GROUP)]
        q, k, kk, qk = {}, {}, {}, {}
        for b in range(nbb):
            for j in range(h0 // rep, (h0 + DN_HEAD_GROUP) // rep):
                qj = conv[b][:, j * DN_DK:(j + 1) * DN_DK]
                kj = conv[b][:, DN_QK + j * DN_DK:DN_QK + (j + 1) * DN_DK]
                qj = qj * lax.rsqrt(jnp.sum(qj * qj, -1, keepdims=True) + 1e-6) * (DN_DK ** -0.5)
                kj = kj * lax.rsqrt(jnp.sum(kj * kj, -1, keepdims=True) + 1e-6)
                kq = _dot_nt(jnp.concatenate([kj, qj], axis=0), kj)
                q[b, j], k[b, j], kk[b, j], qk[b, j] = qj, kj, kq[0:L], kq[L:2 * L]
        gc = {(b, hh): gcum_tm[b][:, 8 + hh:9 + hh] for b, hh in heads}
        beta = {(b, hh): beta_tm[b][:, hh:hh + 1] for b, hh in heads}
        eg = {(b, hh): egc_tm[b][:, 8 + hh:9 + hh] for b, hh in heads}
        rel = {(b, hh): jnp.where(lower, jnp.exp(jnp.where(lower, gc[b, hh] - gcum_hm[b][hh:hh + 1, :], 0.0)),
                                  0.0) for b, hh in heads}
        a = {(b, hh): jnp.where(strict, beta[b, hh] * kk[b, hh // rep] * rel[b, hh], 0.0) for b, hh in heads}
        attn = {(b, hh): qk[b, hh // rep] * rel[b, hh] for b, hh in heads}
        pd = {hh: jnp.where(same_sub, -a[hh], 0.0) for hh in heads}
        e = {hh: a[hh] + pd[hh] for hh in heads}
        eb = {hh: e[hh].astype(BF16) for hh in heads}
        xacc = pd
        xaccb = {hh: pd[hh].astype(BF16) for hh in heads}
        pwb = xaccb
        for _ in range(3):
            pw = {hh: bdot(pwb[hh], pwb[hh]) for hh in heads}
            pwb = {hh: pw[hh].astype(BF16) for hh in heads}
            xacc = {hh: xacc[hh] + pw[hh] + bdot(xaccb[hh], pwb[hh]) for hh in heads}
            xaccb = {hh: xacc[hh].astype(BF16) for hh in heads}
        f = {hh: e[hh] + bdot(xaccb[hh], eb[hh]) for hh in heads}
        fb = {hh: f[hh].astype(BF16) for hh in heads}
        f2 = {hh: bdot(fb[hh], fb[hh]) for hh in heads}
        y = {hh: f2[hh] - f[hh] - bdot(fb[hh], f2[hh].astype(BF16)) for hh in heads}
        rinv = {hh: xacc[hh] + y[hh] + bdot(y[hh].astype(BF16), xaccb[hh]) for hh in heads}
        sol = {}
        for b, hh in heads:
            vh = conv[b][:, 2 * DN_QK + hh * DN_DV:2 * DN_QK + (hh + 1) * DN_DV]
            rhs = jnp.concatenate([vh * beta[b, hh], k[b, hh // rep] * (beta[b, hh] * eg[b, hh])], axis=1)
            sol[b, hh] = rhs + _dot(rinv[b, hh], rhs)
        u = {p: sol[p][:, 0:DN_DV] for p in heads}
        wmat = {p: sol[p][:, DN_DV:DN_DV + DN_DK] for p in heads}
        qe = {(b, hh): q[b, hh // rep] * eg[b, hh] for b, hh in heads}
        s = {(b, hh): s_scr[b, hh] for b, hh in heads}
        v_new = {p: [] for p in heads}
        q_s = {p: [] for p in heads}
        for sc in range(n_sub):
            r0 = sc * chunk
            for b, hh in heads:
                p = (b, hh)
                ws = _dot(jnp.concatenate([wmat[p][r0:r0 + chunk], qe[p][r0:r0 + chunk]], axis=0), s[p])
                vn = u[p][r0:r0 + chunk] - ws[0:chunk]
                v_new[p].append(vn)
                q_s[p].append(ws[chunk:2 * chunk])
                g_last = gcum_tm[b][r0 + chunk - 1:r0 + chunk, 8 + hh:9 + hh]
                kd = k[b, hh // rep][r0:r0 + chunk] * jnp.exp(g_last - gc[p][r0:r0 + chunk])
                s[p] = s[p] * jnp.exp(g_last) + _dot_tn(kd, vn)
        for b, hh in heads:
            p = (b, hh)
            s_scr[b, hh] = s[p]
            sout_ref[b, hh] = s[p]
            vn = v_new[p][0] if n_sub == 1 else jnp.concatenate(v_new[p], axis=0)
            qs = q_s[p][0] if n_sub == 1 else jnp.concatenate(q_s[p], axis=0)
            o = qs + _dot(attn[p], vn)
            zh = z_ref[b, :, hh * DN_DV:(hh + 1) * DN_DV].astype(F32)
            n = o * lax.rsqrt(jnp.mean(o * o, -1, keepdims=True) + 1e-6) * ng_ref[...]
            o_ref[b, :, hh * DN_DV:(hh + 1) * DN_DV] = (n * _silu(zh)).astype(BF16)


def _deltanet(proj, small, small_t, conv_w, conv_init, prow, pcol, norm_g, s0, blk_len, chunk):
    nb, t = proj.shape[:2]
    nblk = t // blk_len
    nbb = max(DN_STREAMS, min(nb, DN_STREAM_ROWS // blk_len))
    grid = (nb // nbb, nblk)
    return pl.pallas_call(
        functools.partial(_dn_kernel, blk_len=blk_len, chunk=chunk),
        grid=grid,
        in_specs=[
            pl.BlockSpec((nbb, blk_len, 1024), lambda b, i: (b, i, 3072 // 1024)),
            pl.BlockSpec((nbb, blk_len, 1024), lambda b, i: (b, i, 4096 // 1024)),
            pl.BlockSpec((nbb, blk_len, 1024), lambda b, i: (b, i, 5120 // 1024)),
            pl.BlockSpec((nbb, blk_len, LANES), lambda b, i: (b, i, 0)),
            pl.BlockSpec((nbb, 16, blk_len), lambda b, i: (b, 0, i)),
            pl.BlockSpec((DN_CONV_W, DN_CONV_CH), lambda b, i: (0, 0)),
            pl.BlockSpec((nbb, 8, DN_CONV_CH), lambda b, i: (b, 0, 0)),
            pl.BlockSpec((8, LANES), lambda b, i: (0, 0)),
            pl.BlockSpec((16, LANES), lambda b, i: (0, 0)),
            pl.BlockSpec((1, DN_DV), lambda b, i: (0, 0)),
            pl.BlockSpec((nbb, DN_V_HEADS, DN_DK, DN_DV), lambda b, i: (b, 0, 0, 0)),
        ],
        out_specs=[
            pl.BlockSpec((nbb, blk_len, 1024), lambda b, i: (b, i, 0)),
            pl.BlockSpec((nbb, DN_V_HEADS, DN_DK, DN_DV), lambda b, i: (b, 0, 0, 0)),
            pl.BlockSpec((nbb, DN_CONV_W - 1, DN_CONV_CH), lambda b, i: (b, 0, 0)),
        ],
        out_shape=[
            jax.ShapeDtypeStruct((nb, t, 1024), BF16),
            jax.ShapeDtypeStruct((nb, DN_V_HEADS, DN_DK, DN_DV), F32),
            jax.ShapeDtypeStruct((nb, DN_CONV_W - 1, DN_CONV_CH), F32),
        ],
        scratch_shapes=[pltpu.VMEM((nbb, blk_len + 8, DN_CONV_CH), F32),
                        pltpu.VMEM((nbb, DN_V_HEADS, DN_DK, DN_DV), F32)],
        compiler_params=_params("parallel", "arbitrary"),
        name="deltanet",
    )(proj, proj, proj, small, small_t, conv_w, conv_init, prow, pcol, norm_g, s0)


def _memkv_kernel(m_ref, w_ref, o_ref):
    o_ref[...] = _dot(m_ref[...], w_ref[...])


def _memkv(mem, w_kv):
    n = mem.shape[0]
    tn = 1024
    return pl.pallas_call(
        _memkv_kernel,
        grid=(w_kv.shape[1] // tn,),
        in_specs=[pl.BlockSpec((n, D_MODEL), lambda j: (0, 0)),
                  pl.BlockSpec((D_MODEL, tn), lambda j: (0, j))],
        out_specs=pl.BlockSpec((n, tn), lambda j: (0, j)),
        out_shape=jax.ShapeDtypeStruct((n, w_kv.shape[1]), F32),
        compiler_params=_params("parallel"),
        name="memkv",
    )(mem, w_kv)


def _mix_kernel(h_ref, ret_ref, dn_ref, mk_ref, mv_ref, wxq_ref, wg_ref, wb_ref, wo_ref,
                g1_ref, b1_ref, h1_ref, mem_scr, *, n_batch, tiles_per_batch):
    @pl.when(pl.program_id(0) % tiles_per_batch == 0)
    def _():
        for b in range(n_batch):
            for hh in range(XA_HEADS):
                mem_scr[0, hh, b] = mk_ref[0, b, :, hh, :].astype(BF16)
                mem_scr[1, hh, b] = mv_ref[0, b, :, hh, :].astype(BF16)

    hb = h_ref[...].astype(BF16)
    xq = jnp.dot(hb, wxq_ref[...], preferred_element_type=F32)
    tb = xq.shape[0] // n_batch
    pairs = [(b, hh) for b in range(n_batch) for hh in range(XA_HEADS)]
    s = [_dot_nt(xq[b * tb:(b + 1) * tb, hh * XA_DH:(hh + 1) * XA_DH], mem_scr[0, hh, b]) * (XA_DH ** -0.5)
         for b, hh in pairs]
    p = [jnp.exp(si - jnp.max(si, -1, keepdims=True)) for si in s]
    p = [pi / jnp.sum(pi, -1, keepdims=True) for pi in p]
    xo = [_dot(pi, mem_scr[1, hh, b]) for pi, (b, hh) in zip(p, pairs)]
    xo = [jnp.concatenate(xo[b * XA_HEADS:(b + 1) * XA_HEADS], axis=1) for b in range(n_batch)]
    xo = xo[0] if n_batch == 1 else jnp.concatenate(xo, axis=0)
    mixed = None
    for n, br in enumerate((ret_ref[...], dn_ref[...], xo)):
        gate = _sigmoid(jnp.dot(hb, wg_ref[:, n * D_MODEL:(n + 1) * D_MODEL], preferred_element_type=F32))
        term = gate * _dot(br, wb_ref[n])
        mixed = term if mixed is None else mixed + term
    y = ALPHA * h_ref[...] + _dot(mixed, wo_ref[...])
    h1 = _layernorm(y, g1_ref[...], b1_ref[...])
    h1_ref[...] = h1


def _mix(h, ret_o, dn_o, mem_k, mem_v, w_xq, w_gate, w_branch, w_out, ln_g, ln_b, tm, t):
    n_tok = h.shape[0]
    const2 = lambda i: (0, 0)
    row = lambda i: (i, 0)
    single = pl.Buffered(1)
    if tm <= t:
        n_batch = 1
        tiles_per_batch = t // tm
    else:
        n_batch = tm // t
        tiles_per_batch = 1
    mem_spec = pl.BlockSpec((1, n_batch, N_MEM, XA_HEADS, XA_DH), lambda i: (0, i // tiles_per_batch, 0, 0, 0))
    return pl.pallas_call(
        functools.partial(_mix_kernel, n_batch=n_batch, tiles_per_batch=tiles_per_batch),
        grid=(n_tok // tm,),
        in_specs=[
            pl.BlockSpec((tm, D_MODEL), row),
            pl.BlockSpec((tm, D_MODEL), row),
            pl.BlockSpec((tm, D_MODEL), row),
            mem_spec,
            mem_spec,
            pl.BlockSpec((D_MODEL, D_MODEL), const2, pipeline_mode=single),
            pl.BlockSpec((D_MODEL, 3 * D_MODEL), const2, pipeline_mode=single),
            pl.BlockSpec((3, D_MODEL, D_MODEL), lambda i: (0, 0, 0), pipeline_mode=single),
            pl.BlockSpec((D_MODEL, D_MODEL), const2, pipeline_mode=single),
            pl.BlockSpec((1, D_MODEL), const2),
            pl.BlockSpec((1, D_MODEL), const2),
        ],
        out_specs=pl.BlockSpec((tm, D_MODEL), row),
        out_shape=jax.ShapeDtypeStruct((n_tok, D_MODEL), F32),
        scratch_shapes=[pltpu.VMEM((2, XA_HEADS, n_batch, N_MEM, XA_DH), BF16)],
        compiler_params=_params("arbitrary"),
        name="mix",
    )(h, ret_o, dn_o, mem_k, mem_v, w_xq, w_gate, w_branch, w_out, ln_g, ln_b)


def _route_t(x, wrt, brt):
    xh = x.astype(BF16)
    xl = (x - xh.astype(F32)).astype(BF16)
    wh = wrt.astype(BF16)
    wl = (wrt - wh.astype(F32)).astype(BF16)
    nt = lambda a, b: lax.dot_general(a, b, (((1,), (1,)), ((), ())), preferred_element_type=F32)
    hi = nt(jnp.concatenate([wh, wl], axis=0), xh)
    logits = hi[0:MOE_ROUTE_ROWS] + (hi[MOE_ROUTE_ROWS:2 * MOE_ROUTE_ROWS] + nt(wh, xl)) + brt
    row_i = lax.broadcasted_iota(jnp.int32, logits.shape, 0)
    row = row_i.astype(F32)
    far = jnp.float32(LANES)
    neg = jnp.float32(-3.0e38)
    is_c = jnp.logical_and(row_i >= MOE_NE, row_i < MOE_NE + MOE_GROUPS)
    cl = jnp.where(is_c, logits, neg)
    cmax = jnp.max(cl, 0, keepdims=True)
    denom = jnp.sum(jnp.where(is_c, jnp.exp(jnp.where(is_c, logits - cmax, 0.0)), 0.0), 0, keepdims=True)
    p_grp = 1.0 / denom
    grp = jnp.min(jnp.where(jnp.logical_and(is_c, cl == cmax), row - MOE_NE, far), 0, keepdims=True)
    in_grp = jnp.logical_and(row_i < MOE_NE, _idiv(row_i, MOE_EXPERTS).astype(F32) == grp)
    fl = jnp.where(in_grp, logits, neg)
    v1 = jnp.max(fl, 0, keepdims=True)
    i1 = jnp.min(jnp.where(jnp.logical_and(in_grp, fl == v1), row, far), 0, keepdims=True)
    rest = jnp.logical_and(in_grp, row != i1)
    fl2 = jnp.where(rest, logits, neg)
    v2 = jnp.max(fl2, 0, keepdims=True)
    i2 = jnp.min(jnp.where(jnp.logical_and(rest, fl2 == v2), row, far), 0, keepdims=True)
    e2 = jnp.exp(v2 - v1)
    w1 = p_grp / (1.0 + e2)
    w2 = p_grp * e2 / (1.0 + e2)
    comb_t = jnp.where(row == i1, w1, 0.0) + jnp.where(row == i2, w2, 0.0)
    tm = x.shape[0]
    last = jnp.where(lax.broadcasted_iota(jnp.int32, (8, tm), 0) == 7, grp, 0.0)
    return jnp.concatenate([comb_t, jnp.zeros((LANES - MOE_ROUTE_ROWS - 8, tm), F32), last], axis=0)


def _route_kernel(h1_ref, wrt_ref, brt_ref, comb_ref, cnt_ref):
    comb = _route_t(h1_ref[...], wrt_ref[...], brt_ref[...]).T
    comb_ref[...] = comb
    lane = lax.broadcasted_iota(jnp.int32, comb.shape, 1).astype(F32)
    onehot = jnp.where(lane == comb[:, LANES - 1:LANES], 1.0, 0.0)
    cnt_ref[0] = jnp.broadcast_to(jnp.sum(onehot, 0, keepdims=True), (8, LANES))


def _route_call(h1, w_route, b_route, tm):
    n_tok = h1.shape[0]
    return pl.pallas_call(
        _route_kernel,
        grid=(n_tok // tm,),
        in_specs=[pl.BlockSpec((tm, D_MODEL), lambda i: (i, 0)),
                  pl.BlockSpec((MOE_ROUTE_ROWS, D_MODEL), lambda i: (0, 0)),
                  pl.BlockSpec((MOE_ROUTE_ROWS, 1), lambda i: (0, 0))],
        out_specs=[pl.BlockSpec((tm, LANES), lambda i: (i, 0)),
                   pl.BlockSpec((1, 8, LANES), lambda i: (i, 0, 0))],
        out_shape=[jax.ShapeDtypeStruct((n_tok, LANES), F32),
                   jax.ShapeDtypeStruct((n_tok // tm, 8, LANES), F32)],
        compiler_params=_params("parallel"),
        name="route",
    )(h1, w_route, b_route)


MOE_ALIGN = 16
MOE_EPS = 4


def _moe_sizes(tm):
    base = tm // MOE_GROUPS
    fine = {min(tm, base + MOE_ALIGN * i) for i in range(5)}
    return tuple(sorted(fine | {tm // 2, tm}))


def _moe_kernel(offs_ref, cls_ref, h1_ref, comb_ref, tri_ref, wg_ref, wu_ref, wd_ref,
                g2_ref, b2_ref, y_ref, pt_scr, xs_scr, cs_scr, ys_scr, *, tm, nr1, sizes):
    i = pl.program_id(0)
    e = pl.program_id(1)
    nrt = xs_scr.shape[0]

    @pl.when(e == 0)
    def _():
        comb = comb_ref[...]
        lane_i = lax.broadcasted_iota(jnp.int32, (tm, LANES), 1)
        grp = comb[:, LANES - 1:LANES]
        is_g = lane_i.astype(F32) == grp
        prefix = jnp.dot(tri_ref[...], jnp.where(is_g, 1.0, 0.0).astype(BF16), preferred_element_type=F32)
        rank = jnp.sum(jnp.where(is_g, prefix, 0.0), -1, keepdims=True)
        offv = jnp.zeros((tm, 1), F32)
        for g in range(MOE_GROUPS):
            offv = jnp.where(grp == g, offs_ref[i * MOE_GROUPS + g].astype(F32), offv)
        pos = offv + rank
        col = lax.broadcasted_iota(jnp.int32, (tm, nr1), 1).astype(F32)
        pt = jnp.where(col == pos, 1.0, 0.0).astype(BF16)
        pt_scr[...] = pt
        combw = jnp.where(lane_i < MOE_NE, comb, 0.0)
        c_hi = combw.astype(BF16)
        c_lo = (combw - c_hi.astype(F32)).astype(BF16)
        xs_scr[0:nr1, :] = _dot_tn(pt, h1_ref[...]).astype(BF16)
        c_pair = _dot_tn(pt, jnp.concatenate([c_hi, c_lo], axis=1))
        cs_scr[0:nr1, :] = c_pair[:, 0:LANES] + c_pair[:, LANES:2 * LANES]
        xs_scr[nr1:nrt, :] = jnp.zeros((nrt - nr1, D_MODEL), BF16)
        cs_scr[nr1:nrt, :] = jnp.zeros((nrt - nr1, LANES), F32)
        ys_scr[tm:nr1, :] = jnp.zeros((nr1 - tm, D_MODEL), F32)

    g = e // (MOE_EXPERTS // MOE_EPS)
    off = pl.multiple_of(offs_ref[i * MOE_GROUPS + g], MOE_ALIGN)
    cls = cls_ref[i * MOE_GROUPS + g]
    first = e % (MOE_EXPERTS // MOE_EPS) == 0
    for ci, m in enumerate(sizes):
        @pl.when(cls == ci)
        def _(m=m):
            rows = pl.ds(off, m)
            xb = xs_scr[rows, :]
            cb = cs_scr[rows, :]
            lane_i = lax.broadcasted_iota(jnp.int32, (m, LANES), 1)
            sub = range(MOE_EPS)
            colw = [jnp.sum(jnp.where(lane_i == e * MOE_EPS + j, cb, 0.0), -1, keepdims=True) for j in sub]
            gate = [jnp.dot(xb, wg_ref[j], preferred_element_type=F32) for j in sub]
            up = [jnp.dot(xb, wu_ref[j], preferred_element_type=F32) for j in sub]
            hg = [(_silu(gate[j]) * up[j] * colw[j]).astype(BF16) for j in sub]
            wd = wd_ref[...].reshape(MOE_EPS * MOE_DFF, D_MODEL)
            res = jnp.dot(jnp.concatenate(hg, axis=1), wd, preferred_element_type=F32)

            @pl.when(first)
            def _():
                ys_scr[rows, :] = res

            @pl.when(jnp.logical_not(first))
            def _():
                ys_scr[rows, :] += res

    @pl.when(e == MOE_NE // MOE_EPS - 1)
    def _():
        moe = jnp.dot(pt_scr[...], ys_scr[0:nr1, :].astype(BF16), preferred_element_type=F32)
        y_ref[...] = _layernorm(ALPHA * h1_ref[...] + moe, g2_ref[...], b2_ref[...])


def _moe(h1, comb, cnt, w_gate, w_up, w_down, ln_g, ln_b, tm):
    n_tok = h1.shape[0]
    sizes = _moe_sizes(tm)
    nr1 = -(-(tm + MOE_GROUPS * MOE_ALIGN) // LANES) * LANES
    nrt = nr1 + max(b - a for a, b in zip((0,) + sizes, sizes))
    c = cnt[:, 0, :MOE_GROUPS].astype(jnp.int32)
    seg = (c + MOE_ALIGN - 1) // MOE_ALIGN * MOE_ALIGN
    offs = (jnp.cumsum(seg, axis=1) - seg).reshape(-1)
    cls = sum((c > s).astype(jnp.int32) for s in sizes[:-1]).reshape(-1)
    tri =jnp.asarray(np.tril(np.ones((tm, tm), np.float32), -1), BF16)
    grid_spec = pltpu.PrefetchScalarGridSpec(
        num_scalar_prefetch=2,
        grid=(n_tok // tm, MOE_NE // MOE_EPS),
        in_specs=[
            pl.BlockSpec((tm, D_MODEL), lambda i, e, *_: (i, 0)),
            pl.BlockSpec((tm, LANES), lambda i, e, *_: (i, 0)),
            pl.BlockSpec((tm, tm), lambda i, e, *_: (0, 0), pipeline_mode=pl.Buffered(1)),
            pl.BlockSpec((MOE_EPS, D_MODEL, MOE_DFF), lambda i, e, *_: (e, 0, 0)),
            pl.BlockSpec((MOE_EPS, D_MODEL, MOE_DFF), lambda i, e, *_: (e, 0, 0)),
            pl.BlockSpec((MOE_EPS, MOE_DFF, D_MODEL), lambda i, e, *_: (e, 0, 0)),
            pl.BlockSpec((1, D_MODEL), lambda i, e, *_: (0, 0)),
            pl.BlockSpec((1, D_MODEL), lambda i, e, *_: (0, 0)),
        ],
        out_specs=pl.BlockSpec((tm, D_MODEL), lambda i, e, *_: (i, 0)),
        scratch_shapes=[pltpu.VMEM((tm, nr1), BF16), pltpu.VMEM((nrt, D_MODEL), BF16),
                        pltpu.VMEM((nrt, LANES), F32), pltpu.VMEM((nrt, D_MODEL), F32)],
    )
    return pl.pallas_call(
        functools.partial(_moe_kernel, tm=tm, nr1=nr1, sizes=sizes),
        grid_spec=grid_spec,
        out_shape=jax.ShapeDtypeStruct((n_tok, D_MODEL), F32),
        compiler_params=_params("parallel", "arbitrary"),
        name="moe",
    )(offs, cls, h1, comb, tri, w_gate, w_up, w_down, ln_g, ln_b)


def _rope_tables(start, t):
    half = RET_DK // 2
    inv = 1.0 / (ROPE_BASE ** (np.arange(half, dtype=np.float64) / half))
    ang = (start + np.arange(t, dtype=np.float64))[:, None] * inv[None, :]
    cos = np.cos(ang)
    sin = np.sin(ang)
    return (jnp.asarray(np.concatenate([cos, cos], -1), F32),
            jnp.asarray(np.concatenate([-sin, sin], -1), F32))


def _group(x, pos0, mem_k, mem_v, ret_s0, dn_s0, conv_buf, wts, *, tm, tm_mix, tm_moe, ret_chunk, dn_blk,
           dn_chunk):
    nb, t, d = x.shape
    n_tok = nb * t
    h, small, small_t, proj = _in_proj(x.reshape(n_tok, d), wts["ln_in_g"], wts["ln_in_b"],
                                           wts["w_in"], wts["w_small"], wts["w_small_t"], tm)
    proj3 = proj.reshape(nb, t, PROJ_W)
    cos2, sin2 = _rope_tables(pos0, t)
    ret_o, ret_s = _retention(proj3, cos2, sin2, ret_s0, wts["ret_gn_g"], wts["ret_gn_b"], ret_chunk)
    small_t3 = small_t.reshape(16, nb, t).transpose(1, 0, 2)
    conv_init = jnp.concatenate([jnp.zeros((nb, 8 - (DN_CONV_W - 1), DN_CONV_CH), F32), conv_buf], axis=1)
    dn_o, dn_s, conv_new = _deltanet(proj3, small.reshape(nb, t, LANES), small_t3, wts["dn_conv"], conv_init,
                                     wts["dn_prow"],
                                     wts["dn_pcol"], wts["dn_norm_g"], dn_s0, dn_blk, dn_chunk)
    h1 = _mix(h, ret_o.reshape(n_tok, d), dn_o.reshape(n_tok, d), mem_k, mem_v,
                   wts["w_xq"], wts["w_gate"], wts["w_branch"], wts["w_out"], wts["ln1_g"], wts["ln1_b"],
                   tm_mix, t)
    comb, cnt = _route_call(h1, wts["w_route"], wts["b_route"], tm_moe)
    y = _moe(h1, comb, cnt, wts["moe_w_gate"], wts["moe_w_up"], wts["moe_w_down"],
             wts["ln2_g"], wts["ln2_b"], tm_moe)
    return y.reshape(nb, t, d), ret_s, dn_s, conv_new


def kernel(x_prompt, x_sample, mem_prompt, state_ret, state_dn, state_dn_conv, cache_mem_k, cache_mem_v,
           ln_in_g, ln_in_b, w_in, ret_gn_g, ret_gn_b, dn_conv, dn_A_log, dn_dt_bias, dn_norm_g,
           w_mem_kv, w_branch, w_out, ln1_g, ln1_b, moe_w_coarse, moe_b_coarse, moe_w_fine, moe_b_fine,
           moe_w_gate, moe_w_up, moe_w_down, ln2_g, ln2_b):
    bp, tp, d = x_prompt.shape
    bs, ts, _ = x_sample.shape
    l = 0
    wi = w_in[l].astype(BF16)
    dba = wi[:, PROJ_W:PROJ_W + 16]
    xq, gates = wi[:, 6160:7184], wi[:, 7184:10256]
    wts = {
        "ln_in_g": ln_in_g.reshape(1, d), "ln_in_b": ln_in_b.reshape(1, d),
        "w_in": wi,
        "w_small": jnp.pad(dba, ((0, 0), (0, LANES - 16))),
        "w_small_t": dba.T,
        "ret_gn_g": ret_gn_g[l].reshape(1, -1), "ret_gn_b": ret_gn_b[l].reshape(1, -1),
        "dn_conv": dn_conv[l],
        "dn_prow": jnp.zeros((8, LANES), F32).at[0, 8:16].set(-jnp.exp(dn_A_log[l])).at[1, 8:16].set(dn_dt_bias[l]),
        "dn_pcol": jnp.zeros((16, LANES), F32).at[8:16, 0].set(-jnp.exp(dn_A_log[l])).at[8:16, 1].set(dn_dt_bias[l]),
        "dn_norm_g": dn_norm_g[l].reshape(1, -1),
        "w_xq": xq, "w_gate": gates,
        "w_branch": w_branch[l].astype(BF16), "w_out": w_out[l].astype(BF16),
        "ln1_g": ln1_g[l].reshape(1, d), "ln1_b": ln1_b[l].reshape(1, d),
        "w_route": jnp.pad(jnp.concatenate([moe_w_fine[l], moe_w_coarse[l]], axis=1).T,
                           ((0, MOE_ROUTE_ROWS - MOE_NE - MOE_GROUPS), (0, 0))),
        "b_route": jnp.pad(jnp.concatenate([moe_b_fine[l], moe_b_coarse[l]]),
                           (0, MOE_ROUTE_ROWS - MOE_NE - MOE_GROUPS)).reshape(MOE_ROUTE_ROWS, 1),
        "moe_w_gate": moe_w_gate[l].reshape(MOE_NE, d, MOE_DFF).astype(BF16),
        "moe_w_up": moe_w_up[l].reshape(MOE_NE, d, MOE_DFF).astype(BF16),
        "moe_w_down": moe_w_down[l].reshape(MOE_NE, MOE_DFF, d).astype(BF16),
        "ln2_g": ln2_g[l].reshape(1, d), "ln2_b": ln2_b[l].reshape(1, d),
    }

    mkv = _memkv(mem_prompt.reshape(bp * N_MEM, d), w_mem_kv[l].astype(BF16))
    mk = mkv[:, :XA_HEADS * XA_DH].reshape(1, bp, N_MEM, XA_HEADS, XA_DH)
    mv = mkv[:, XA_HEADS * XA_DH:].reshape(1, bp, N_MEM, XA_HEADS, XA_DH)
    yp, rs_p, ds_p, cb_p = _group(
        x_prompt, 0, mk, mv,
        jnp.zeros((bp, RET_HEADS, RET_DK, RET_DV), F32),
        jnp.zeros((bp, DN_V_HEADS, DN_DK, DN_DV), F32),
        jnp.zeros((bp, DN_CONV_W - 1, DN_CONV_CH), F32),
        wts, tm=min(1024, tp), tm_mix=min(512, tp), tm_moe=min(1024, tp), ret_chunk=min(256, tp),
        dn_blk=min(128, tp),
        dn_chunk=min(DN_CHUNK, tp))
    ys, rs_s, ds_s, cb_s = _group(
        x_sample, PAST_LEN, cache_mem_k[l:l + 1], cache_mem_v[l:l + 1],
        state_ret[l], state_dn[l], state_dn_conv[l],
        wts, tm=bs * ts, tm_mix=bs * ts, tm_moe=bs * ts, ret_chunk=ts, dn_blk=ts, dn_chunk=min(DN_CHUNK, ts))
    return (yp, ys, rs_p[None], rs_s[None], ds_p[None], ds_s[None], cb_p[None], cb_s[None], mk, mv)
```

```python
import functools
import math

import numpy as np

import jax
import jax.numpy as jnp
from jax import lax
from jax.experimental import pallas as pl
from jax.experimental.pallas import tpu as pltpu

F32 = jnp.float32
BF16 = jnp.bfloat16

D_MODEL = 1024
PAST_LEN = 1024
RET_HEADS = 4
RET_DK = 128
RET_DV = 256
ROPE_BASE = 10000.0
DN_QK_HEADS = 4
DN_V_HEADS = 8
DN_DK = 128
DN_DV = 128
DN_QK = DN_QK_HEADS * DN_DK
DN_CONV_W = 4
DN_CONV_CH = 2048
DN_CHUNK = 64
DN_SUB = 16
DN_HEAD_GROUP = 8
DN_STREAMS = 2
DN_STREAM_ROWS = 256
XA_HEADS = 4
XA_DH = 256
N_MEM = 256
MOE_GROUPS = 4
MOE_EXPERTS = 8
MOE_NE = MOE_GROUPS * MOE_EXPERTS
MOE_DFF = 256
MOE_ROUTE_ROWS = 40
DEPTH = 1
ALPHA = (2.0 * DEPTH) ** 0.25
LANES = 128
RET_QKW = RET_HEADS * RET_DK
RET_VW = RET_HEADS * RET_DV
DN_QKW = 2 * DN_QK
DN_VW = DN_V_HEADS * DN_DV
COL_RQ = 0
COL_RK = COL_RQ + RET_QKW
COL_RV = COL_RK + RET_QKW
COL_RG = COL_RV + RET_VW
COL_DQK = COL_RG + RET_VW
COL_DV = COL_DQK + DN_QKW
COL_DZ = COL_DV + DN_VW
PROJ_W = COL_DZ + DN_VW
DN_GATE_W = 2 * DN_V_HEADS
COL_XQ = PROJ_W + DN_GATE_W
COL_GATE = COL_XQ + XA_HEADS * XA_DH
N_IN = COL_GATE + 3 * D_MODEL
VMEM_LIMIT = 56 * 1024 * 1024


def _dot(a, b):
    return jnp.dot(a.astype(BF16), b.astype(BF16), preferred_element_type=F32)


def _dot_nt(a, b):
    return lax.dot_general(a.astype(BF16), b.astype(BF16), (((1,), (1,)), ((), ())),
                           preferred_element_type=F32)


def _dot_tn(a, b):
    return lax.dot_general(a.astype(BF16), b.astype(BF16), (((0,), (0,)), ((), ())),
                           preferred_element_type=F32)


def _dot_exact(a, b):
    return jnp.dot(a, b, preferred_element_type=F32, precision=lax.Precision.HIGHEST)


def _layernorm(x, g, b, eps=1e-5):
    mu = jnp.mean(x, -1, keepdims=True)
    xc = x - mu
    var = jnp.mean(xc * xc, -1, keepdims=True)
    return xc * lax.rsqrt(var + eps) * g + b


def _silu(x):
    return x * (1.0 / (1.0 + jnp.exp(-x)))


def _sigmoid(x):
    return 1.0 / (1.0 + jnp.exp(-x))


def _softplus(x):
    return jnp.maximum(x, 0.0) + jnp.log(1.0 + jnp.exp(-jnp.abs(x)))


def _idiv(x, n):
    return jnp.right_shift(x, int(math.log2(n)))


def _params(*sem):
    return pltpu.CompilerParams(dimension_semantics=sem, vmem_limit_bytes=VMEM_LIMIT)


def _in_proj_kernel(x_ref, g_ref, b_ref, w_ref, ws_ref, wst_ref,
                    h_ref, small_ref, smallt_ref, proj_ref, hs_ref):
    @pl.when(pl.program_id(1) == 0)
    def _():
        h = _layernorm(x_ref[...], g_ref[...], b_ref[...])
        h_ref[...] = h
        hb = h.astype(BF16)
        hs_ref[...] = hb
        small_ref[...] = jnp.dot(hb, ws_ref[...], preferred_element_type=F32)
        smallt_ref[...] = lax.dot_general(wst_ref[...], hb, (((1,), (1,)), ((), ())),
                                          preferred_element_type=F32)

    proj_ref[...] = jnp.dot(hs_ref[...], w_ref[...], preferred_element_type=F32).astype(BF16)


def _in_proj(x, ln_g, ln_b, w_main, w_small, w_small_t, tm, tn=2048):
    n_tok = x.shape[0]
    grid = (n_tok // tm, PROJ_W // tn)
    return pl.pallas_call(
        _in_proj_kernel,
        grid=grid,
        in_specs=[
            pl.BlockSpec((tm, D_MODEL), lambda i, n: (i, 0)),
            pl.BlockSpec((1, D_MODEL), lambda i, n: (0, 0)),
            pl.BlockSpec((1, D_MODEL), lambda i, n: (0, 0)),
            pl.BlockSpec((D_MODEL, tn), lambda i, n: (0, n)),
            pl.BlockSpec((D_MODEL, LANES), lambda i, n: (0, 0)),
            pl.BlockSpec((DN_GATE_W, D_MODEL), lambda i, n: (0, 0)),
        ],
        out_specs=[
            pl.BlockSpec((tm, D_MODEL), lambda i, n: (i, 0)),
            pl.BlockSpec((tm, LANES), lambda i, n: (i, 0)),
            pl.BlockSpec((DN_GATE_W, tm), lambda i, n: (0, i)),
            pl.BlockSpec((tm, tn), lambda i, n: (i, n)),
        ],
        out_shape=[
            jax.ShapeDtypeStruct((n_tok, D_MODEL), F32),
            jax.ShapeDtypeStruct((n_tok, LANES), F32),
            jax.ShapeDtypeStruct((DN_GATE_W, n_tok), F32),
            jax.ShapeDtypeStruct((n_tok, PROJ_W), BF16),
        ],
        scratch_shapes=[pltpu.VMEM((tm, D_MODEL), BF16)],
        compiler_params=_params("parallel", "arbitrary"),
        name="in_proj",
    )(x, ln_g, ln_b, w_main, w_small, w_small_t)


def _ret_kernel(q_ref, k_ref, v_ref, rg_ref, cos_ref, sin_ref, s0_ref, gng_ref, gnb_ref,
                o_ref, sout_ref, s_scr, dec_scr, *, chunk):
    heads = range(RET_HEADS)
    lg = [math.log(1.0 - 2.0 ** (-5.0 - h)) for h in heads]

    @pl.when(pl.program_id(1) == 0)
    def _():
        s_scr[...] = s0_ref[0]
        ri = lax.broadcasted_iota(jnp.int32, (chunk, chunk), 0)
        ci = lax.broadcasted_iota(jnp.int32, (chunk, chunk), 1)
        causal = ri >= ci
        diff = jnp.where(causal, (ri - ci).astype(F32), 0.0)
        for h in heads:
            dec_scr[h] = jnp.where(causal, jnp.exp(diff * lg[h]), 0.0)

    cos = cos_ref[...]
    sin = sin_ref[...]
    idx = lax.broadcasted_iota(jnp.int32, (chunk, 1), 0).astype(F32)
    q = [q_ref[0, :, h * RET_DK:(h + 1) * RET_DK].astype(F32) for h in heads]
    k = [k_ref[0, :, h * RET_DK:(h + 1) * RET_DK].astype(F32) for h in heads]
    v = [v_ref[0, :, h * RET_DV:(h + 1) * RET_DV] for h in heads]
    qr = [q[h] * cos + pltpu.roll(q[h], RET_DK // 2, 1) * sin for h in heads]
    kr = [(k[h] * cos + pltpu.roll(k[h], RET_DK // 2, 1) * sin) * (RET_DK ** -0.5) for h in heads]
    scores = [_dot_nt(qr[h], kr[h]) * dec_scr[h] for h in heads]
    s = [s_scr[h] for h in heads]
    cross = [_dot(qr[h], s[h]) * jnp.exp((idx + 1.0) * lg[h]) for h in heads]
    o = [_dot(scores[h], v[h]) + cross[h] for h in heads]
    kv = [_dot_tn(kr[h] * jnp.exp((chunk - 1.0 - idx) * lg[h]), v[h]) for h in heads]
    for h in heads:
        s_new = s[h] * math.exp(chunk * lg[h]) + kv[h]
        s_scr[h] = s_new
        sout_ref[0, h] = s_new
        sl = slice(h * RET_DV, (h + 1) * RET_DV)
        gated = _layernorm(o[h], gng_ref[:, sl], gnb_ref[:, sl]) * _silu(rg_ref[0, :, sl].astype(F32))
        o_ref[0, :, sl] = gated.astype(BF16)


def _retention(proj, cos2, sin2, s0, gn_g, gn_b, chunk):
    nb, t = proj.shape[:2]
    grid = (nb, t // chunk)
    qk_w = RET_QKW
    v_w = RET_VW
    return pl.pallas_call(
        functools.partial(_ret_kernel, chunk=chunk),
        grid=grid,
        in_specs=[
            pl.BlockSpec((1, chunk, qk_w), lambda b, c: (b, c, COL_RQ // qk_w)),
            pl.BlockSpec((1, chunk, qk_w), lambda b, c: (b, c, COL_RK // qk_w)),
            pl.BlockSpec((1, chunk, v_w), lambda b, c: (b, c, COL_RV // v_w)),
            pl.BlockSpec((1, chunk, v_w), lambda b, c: (b, c, COL_RG // v_w)),
            pl.BlockSpec((chunk, RET_DK), lambda b, c: (c, 0)),
            pl.BlockSpec((chunk, RET_DK), lambda b, c: (c, 0)),
            pl.BlockSpec((1, RET_HEADS, RET_DK, RET_DV), lambda b, c: (b, 0, 0, 0)),
            pl.BlockSpec((1, v_w), lambda b, c: (0, 0)),
            pl.BlockSpec((1, v_w), lambda b, c: (0, 0)),
        ],
        out_specs=[
            pl.BlockSpec((1, chunk, v_w), lambda b, c: (b, c, 0)),
            pl.BlockSpec((1, RET_HEADS, RET_DK, RET_DV), lambda b, c: (b, 0, 0, 0)),
        ],
        out_shape=[
            jax.ShapeDtypeStruct((nb, t, v_w), BF16),
            jax.ShapeDtypeStruct((nb, RET_HEADS, RET_DK, RET_DV), F32),
        ],
        scratch_shapes=[pltpu.VMEM((RET_HEADS, RET_DK, RET_DV), F32),
                        pltpu.VMEM((RET_HEADS, chunk, chunk), F32)],
        compiler_params=_params("parallel", "arbitrary"),
        name="retention",
    )(proj, proj, proj, proj, cos2, sin2, s0, gn_g, gn_b)


def _dn_kernel(xqk_ref, xv_ref, z_ref, sm_ref, smt_ref, cw_ref, cinit_ref, prow_ref, pcol_ref, ng_ref, s0_ref,
               o_ref, sout_ref, cout_ref, xext, s_scr, *, blk_len, chunk):
    L = blk_len
    nbb = xqk_ref.shape[0]
    blk = pl.program_id(1)

    @pl.when(blk == 0)
    def _():
        xext[:, 0:8, :] = cinit_ref[...]
        s_scr[...] = s0_ref[...]

    ri = lax.broadcasted_iota(jnp.int32, (L, L), 0)
    ci = lax.broadcasted_iota(jnp.int32, (L, L), 1)
    same_chunk = _idiv(ri, chunk) == _idiv(ci, chunk)
    lower = jnp.logical_and(ri >= ci, same_chunk)
    strict = jnp.logical_and(ri > ci, same_chunk)
    same_sub = _idiv(ri, DN_SUB) == _idiv(ci, DN_SUB)
    tri = jnp.where(lower, 1.0, 0.0).astype(F32)
    tri_t = jnp.where(jnp.logical_and(ri <= ci, same_chunk), 1.0, 0.0).astype(F32)
    prow = prow_ref[...]
    pcol = pcol_ref[...]

    conv, beta_tm, gcum_tm, gcum_hm, egc_tm = [], [], [], [], []
    for b in range(nbb):
        x = jnp.concatenate([xqk_ref[b], xv_ref[b]], axis=1).astype(F32)
        xext[b, 8:8 + L, :] = x
        cv = x * cw_ref[3:4, :]
        for w in range(DN_CONV_W - 1):
            cv = cv + xext[b, 5 + w:5 + w + L, :] * cw_ref[w:w + 1, :]
        conv.append(_silu(cv))
        xext[b, 0:8, :] = x[L - 8:L, :]
        cout_ref[b] = x[L - (DN_CONV_W - 1):L, :]
        sm = sm_ref[b]
        beta_tm.append(_sigmoid(sm))
        g_tm = prow[0:1, :] * _softplus(sm + prow[1:2, :])
        g_hm = pcol[8:16, 0:1] * _softplus(smt_ref[b][8:16, :] + pcol[8:16, 1:2])
        gcum_tm.append(_dot_exact(tri, g_tm))
        gcum_hm.append(_dot_exact(g_hm, tri_t))
        egc_tm.append(jnp.exp(gcum_tm[b]))

    n_sub = L // chunk
    rep = DN_V_HEADS // DN_QK_HEADS
    bdot = lambda x, y: jnp.dot(x, y, preferred_element_type=F32)
    for h0 in range(0, DN_V_HEADS, DN_HEAD_GROUP):
        heads = [(b, hh) for b in range(nbb) for hh in range(h0, h0 + DN_HEAD_GROUP)]
        q, k, kk, qk = {}, {}, {}, {}
        for b in range(nbb):
            for j in range(h0 // rep, (h0 + DN_HEAD_GROUP) // rep):
                qj = conv[b][:, j * DN_DK:(j + 1) * DN_DK]
                kj = conv[b][:, DN_QK + j * DN_DK:DN_QK + (j + 1) * DN_DK]
                qj = qj * lax.rsqrt(jnp.sum(qj * qj, -1, keepdims=True) + 1e-6) * (DN_DK ** -0.5)
                kj = kj * lax.rsqrt(jnp.sum(kj * kj, -1, keepdims=True) + 1e-6)
                kq = _dot_nt(jnp.concatenate([kj, qj], axis=0), kj)
                q[b, j], k[b, j], kk[b, j], qk[b, j] = qj, kj, kq[0:L], kq[L:2 * L]
        gc = {(b, hh): gcum_tm[b][:, 8 + hh:9 + hh] for b, hh in heads}
        beta = {(b, hh): beta_tm[b][:, hh:hh + 1] for b, hh in heads}
        eg = {(b, hh): egc_tm[b][:, 8 + hh:9 + hh] for b, hh in heads}
        rel = {(b, hh): jnp.where(lower, jnp.exp(jnp.where(lower, gc[b, hh] - gcum_hm[b][hh:hh + 1, :], 0.0)),
                                  0.0) for b, hh in heads}
        a = {(b, hh): jnp.where(strict, beta[b, hh] * kk[b, hh // rep] * rel[b, hh], 0.0) for b, hh in heads}
        attn = {(b, hh): qk[b, hh // rep] * rel[b, hh] for b, hh in heads}
        pd = {hh: jnp.where(same_sub, -a[hh], 0.0) for hh in heads}
        e = {hh: a[hh] + pd[hh] for hh in heads}
        eb = {hh: e[hh].astype(BF16) for hh in heads}
        xacc = pd
        xaccb = {hh: pd[hh].astype(BF16) for hh in heads}
        pwb = xaccb
        for _ in range(3):
            pw = {hh: bdot(pwb[hh], pwb[hh]) for hh in heads}
            pwb = {hh: pw[hh].astype(BF16) for hh in heads}
            xacc = {hh: xacc[hh] + pw[hh] + bdot(xaccb[hh], pwb[hh]) for hh in heads}
            xaccb = {hh: xacc[hh].astype(BF16) for hh in heads}
        f = {hh: e[hh] + bdot(xaccb[hh], eb[hh]) for hh in heads}
        fb = {hh: f[hh].astype(BF16) for hh in heads}
        f2 = {hh: bdot(fb[hh], fb[hh]) for hh in heads}
        y = {hh: f2[hh] - f[hh] - bdot(fb[hh], f2[hh].astype(BF16)) for hh in heads}
        rinv = {hh: xacc[hh] + y[hh] + bdot(y[hh].astype(BF16), xaccb[hh]) for hh in heads}
        sol = {}
        for b, hh in heads:
            vh = conv[b][:, 2 * DN_QK + hh * DN_DV:2 * DN_QK + (hh + 1) * DN_DV]
            rhs = jnp.concatenate([vh * beta[b, hh], k[b, hh // rep] * (beta[b, hh] * eg[b, hh])], axis=1)
            sol[b, hh] = rhs + _dot(rinv[b, hh], rhs)
        u = {p: sol[p][:, 0:DN_DV] for p in heads}
        wmat = {p: sol[p][:, DN_DV:DN_DV + DN_DK] for p in heads}
        qe = {(b, hh): q[b, hh // rep] * eg[b, hh] for b, hh in heads}
        s = {(b, hh): s_scr[b, hh] for b, hh in heads}
        v_new = {p: [] for p in heads}
        q_s = {p: [] for p in heads}
        for sc in range(n_sub):
            r0 = sc * chunk
            for b, hh in heads:
                p = (b, hh)
                ws = _dot(jnp.concatenate([wmat[p][r0:r0 + chunk], qe[p][r0:r0 + chunk]], axis=0), s[p])
                vn = u[p][r0:r0 + chunk] - ws[0:chunk]
                v_new[p].append(vn)
                q_s[p].append(ws[chunk:2 * chunk])
                g_last = gcum_tm[b][r0 + chunk - 1:r0 + chunk, 8 + hh:9 + hh]
                kd = k[b, hh // rep][r0:r0 + chunk] * jnp.exp(g_last - gc[p][r0:r0 + chunk])
                s[p] = s[p] * jnp.exp(g_last) + _dot_tn(kd, vn)
        for b, hh in heads:
            p = (b, hh)
            s_scr[b, hh] = s[p]
            sout_ref[b, hh] = s[p]
            vn = v_new[p][0] if n_sub == 1 else jnp.concatenate(v_new[p], axis=0)
            qs = q_s[p][0] if n_sub == 1 else jnp.concatenate(q_s[p], axis=0)
            o = qs + _dot(attn[p], vn)
            zh = z_ref[b, :, hh * DN_DV:(hh + 1) * DN_DV].astype(F32)
            n = o * lax.rsqrt(jnp.mean(o * o, -1, keepdims=True) + 1e-6) * ng_ref[...]
            o_ref[b, :, hh * DN_DV:(hh + 1) * DN_DV] = (n * _silu(zh)).astype(BF16)


def _deltanet(proj, small, small_t, conv_w, conv_init, prow, pcol, norm_g, s0, blk_len, chunk):
    nb, t = proj.shape[:2]
    nblk = t // blk_len
    nbb = max(DN_STREAMS, min(nb, DN_STREAM_ROWS // blk_len))
    grid = (nb // nbb, nblk)
    return pl.pallas_call(
        functools.partial(_dn_kernel, blk_len=blk_len, chunk=chunk),
        grid=grid,
        in_specs=[
            pl.BlockSpec((nbb, blk_len, DN_QKW), lambda b, i: (b, i, COL_DQK // DN_QKW)),
            pl.BlockSpec((nbb, blk_len, DN_VW), lambda b, i: (b, i, COL_DV // DN_VW)),
            pl.BlockSpec((nbb, blk_len, DN_VW), lambda b, i: (b, i, COL_DZ // DN_VW)),
            pl.BlockSpec((nbb, blk_len, LANES), lambda b, i: (b, i, 0)),
            pl.BlockSpec((nbb, DN_GATE_W, blk_len), lambda b, i: (b, 0, i)),
            pl.BlockSpec((DN_CONV_W, DN_CONV_CH), lambda b, i: (0, 0)),
            pl.BlockSpec((nbb, 8, DN_CONV_CH), lambda b, i: (b, 0, 0)),
            pl.BlockSpec((8, LANES), lambda b, i: (0, 0)),
            pl.BlockSpec((DN_GATE_W, LANES), lambda b, i: (0, 0)),
            pl.BlockSpec((1, DN_DV), lambda b, i: (0, 0)),
            pl.BlockSpec((nbb, DN_V_HEADS, DN_DK, DN_DV), lambda b, i: (b, 0, 0, 0)),
        ],
        out_specs=[
            pl.BlockSpec((nbb, blk_len, DN_VW), lambda b, i: (b, i, 0)),
            pl.BlockSpec((nbb, DN_V_HEADS, DN_DK, DN_DV), lambda b, i: (b, 0, 0, 0)),
            pl.BlockSpec((nbb, DN_CONV_W - 1, DN_CONV_CH), lambda b, i: (b, 0, 0)),
        ],
        out_shape=[
            jax.ShapeDtypeStruct((nb, t, DN_VW), BF16),
            jax.ShapeDtypeStruct((nb, DN_V_HEADS, DN_DK, DN_DV), F32),
            jax.ShapeDtypeStruct((nb, DN_CONV_W - 1, DN_CONV_CH), F32),
        ],
        scratch_shapes=[pltpu.VMEM((nbb, blk_len + 8, DN_CONV_CH), F32),
                        pltpu.VMEM((nbb, DN_V_HEADS, DN_DK, DN_DV), F32)],
        compiler_params=_params("parallel", "arbitrary"),
        name="deltanet",
    )(proj, proj, proj, small, small_t, conv_w, conv_init, prow, pcol, norm_g, s0)


def _memkv_kernel(m_ref, w_ref, o_ref):
    o_ref[...] = _dot(m_ref[...], w_ref[...])


def _memkv(mem, w_kv):
    n = mem.shape[0]
    tn = 1024
    return pl.pallas_call(
        _memkv_kernel,
        grid=(w_kv.shape[1] // tn,),
        in_specs=[pl.BlockSpec((n, D_MODEL), lambda j: (0, 0)),
                  pl.BlockSpec((D_MODEL, tn), lambda j: (0, j))],
        out_specs=pl.BlockSpec((n, tn), lambda j: (0, j)),
        out_shape=jax.ShapeDtypeStruct((n, w_kv.shape[1]), F32),
        compiler_params=_params("parallel"),
        name="memkv",
    )(mem, w_kv)


def _mix_kernel(h_ref, ret_ref, dn_ref, mk_ref, mv_ref, wxq_ref, wg_ref, wb_ref, wo_ref,
                g1_ref, b1_ref, h1_ref, mem_scr, *, n_batch, tiles_per_batch):
    @pl.when(pl.program_id(0) % tiles_per_batch == 0)
    def _():
        for b in range(n_batch):
            for hh in range(XA_HEADS):
                mem_scr[0, hh, b] = mk_ref[0, b, :, hh, :].astype(BF16)
                mem_scr[1, hh, b] = mv_ref[0, b, :, hh, :].astype(BF16)

    hb = h_ref[...].astype(BF16)
    xq = jnp.dot(hb, wxq_ref[...], preferred_element_type=F32)
    tb = xq.shape[0] // n_batch
    pairs = [(b, hh) for b in range(n_batch) for hh in range(XA_HEADS)]
    s = [_dot_nt(xq[b * tb:(b + 1) * tb, hh * XA_DH:(hh + 1) * XA_DH], mem_scr[0, hh, b]) * (XA_DH ** -0.5)
         for b, hh in pairs]
    p = [jnp.exp(si - jnp.max(si, -1, keepdims=True)) for si in s]
    p = [pi / jnp.sum(pi, -1, keepdims=True) for pi in p]
    xo = [_dot(pi, mem_scr[1, hh, b]) for pi, (b, hh) in zip(p, pairs)]
    xo = [jnp.concatenate(xo[b * XA_HEADS:(b + 1) * XA_HEADS], axis=1) for b in range(n_batch)]
    xo = xo[0] if n_batch == 1 else jnp.concatenate(xo, axis=0)
    mixed = None
    for n, br in enumerate((ret_ref[...], dn_ref[...], xo)):
        gate = _sigmoid(jnp.dot(hb, wg_ref[:, n * D_MODEL:(n + 1) * D_MODEL], preferred_element_type=F32))
        term = gate * _dot(br, wb_ref[n])
        mixed = term if mixed is None else mixed + term
    y = ALPHA * h_ref[...] + _dot(mixed, wo_ref[...])
    h1 = _layernorm(y, g1_ref[...], b1_ref[...])
    h1_ref[...] = h1


def _mix(h, ret_o, dn_o, mem_k, mem_v, w_xq, w_gate, w_branch, w_out, ln_g, ln_b, tm, t):
    n_tok = h.shape[0]
    const2 = lambda i: (0, 0)
    row = lambda i: (i, 0)
    single = pl.Buffered(1)
    if tm <= t:
        n_batch = 1
        tiles_per_batch = t // tm
    else:
        n_batch = tm // t
        tiles_per_batch = 1
    mem_spec = pl.BlockSpec((1, n_batch, N_MEM, XA_HEADS, XA_DH), lambda i: (0, i // tiles_per_batch, 0, 0, 0))
    return pl.pallas_call(
        functools.partial(_mix_kernel, n_batch=n_batch, tiles_per_batch=tiles_per_batch),
        grid=(n_tok // tm,),
        in_specs=[
            pl.BlockSpec((tm, D_MODEL), row),
            pl.BlockSpec((tm, D_MODEL), row),
            pl.BlockSpec((tm, D_MODEL), row),
            mem_spec,
            mem_spec,
            pl.BlockSpec((D_MODEL, D_MODEL), const2, pipeline_mode=single),
            pl.BlockSpec((D_MODEL, 3 * D_MODEL), const2, pipeline_mode=single),
            pl.BlockSpec((3, D_MODEL, D_MODEL), lambda i: (0, 0, 0), pipeline_mode=single),
            pl.BlockSpec((D_MODEL, D_MODEL), const2, pipeline_mode=single),
            pl.BlockSpec((1, D_MODEL), const2),
            pl.BlockSpec((1, D_MODEL), const2),
        ],
        out_specs=pl.BlockSpec((tm, D_MODEL), row),
        out_shape=jax.ShapeDtypeStruct((n_tok, D_MODEL), F32),
        scratch_shapes=[pltpu.VMEM((2, XA_HEADS, n_batch, N_MEM, XA_DH), BF16)],
        compiler_params=_params("arbitrary"),
        name="mix",
    )(h, ret_o, dn_o, mem_k, mem_v, w_xq, w_gate, w_branch, w_out, ln_g, ln_b)


def _route_t(x, wrt, brt):
    xh = x.astype(BF16)
    xl = (x - xh.astype(F32)).astype(BF16)
    wh = wrt.astype(BF16)
    wl = (wrt - wh.astype(F32)).astype(BF16)
    nt = lambda a, b: lax.dot_general(a, b, (((1,), (1,)), ((), ())), preferred_element_type=F32)
    hi = nt(jnp.concatenate([wh, wl], axis=0), xh)
    logits = hi[0:MOE_ROUTE_ROWS] + (hi[MOE_ROUTE_ROWS:2 * MOE_ROUTE_ROWS] + nt(wh, xl)) + brt
    row_i = lax.broadcasted_iota(jnp.int32, logits.shape, 0)
    row = row_i.astype(F32)
    far = jnp.float32(LANES)
    neg = jnp.float32(-3.0e38)
    is_c = jnp.logical_and(row_i >= MOE_NE, row_i < MOE_NE + MOE_GROUPS)
    cl = jnp.where(is_c, logits, neg)
    cmax = jnp.max(cl, 0, keepdims=True)
    denom = jnp.sum(jnp.where(is_c, jnp.exp(jnp.where(is_c, logits - cmax, 0.0)), 0.0), 0, keepdims=True)
    p_grp = 1.0 / denom
    grp = jnp.min(jnp.where(jnp.logical_and(is_c, cl == cmax), row - MOE_NE, far), 0, keepdims=True)
    in_grp = jnp.logical_and(row_i < MOE_NE, _idiv(row_i, MOE_EXPERTS).astype(F32) == grp)
    fl = jnp.where(in_grp, logits, neg)
    v1 = jnp.max(fl, 0, keepdims=True)
    i1 = jnp.min(jnp.where(jnp.logical_and(in_grp, fl == v1), row, far), 0, keepdims=True)
    rest = jnp.logical_and(in_grp, row != i1)
    fl2 = jnp.where(rest, logits, neg)
    v2 = jnp.max(fl2, 0, keepdims=True)
    i2 = jnp.min(jnp.where(jnp.logical_and(rest, fl2 == v2), row, far), 0, keepdims=True)
    e2 = jnp.exp(v2 - v1)
    w1 = p_grp / (1.0 + e2)
    w2 = p_grp * e2 / (1.0 + e2)
    comb_t = jnp.where(row == i1, w1, 0.0) + jnp.where(row == i2, w2, 0.0)
    tm = x.shape[0]
    last = jnp.where(lax.broadcasted_iota(jnp.int32, (8, tm), 0) == 7, grp, 0.0)
    return jnp.concatenate([comb_t, jnp.zeros((LANES - MOE_ROUTE_ROWS - 8, tm), F32), last], axis=0)


def _route_kernel(h1_ref, wrt_ref, brt_ref, comb_ref, cnt_ref):
    comb = _route_t(h1_ref[...], wrt_ref[...], brt_ref[...]).T
    comb_ref[...] = comb
    lane = lax.broadcasted_iota(jnp.int32, comb.shape, 1).astype(F32)
    onehot = jnp.where(lane == comb[:, LANES - 1:LANES], 1.0, 0.0)
    cnt_ref[0] = jnp.broadcast_to(jnp.sum(onehot, 0, keepdims=True), (8, LANES))


def _route_call(h1, w_route, b_route, tm):
    n_tok = h1.shape[0]
    return pl.pallas_call(
        _route_kernel,
        grid=(n_tok // tm,),
        in_specs=[pl.BlockSpec((tm, D_MODEL), lambda i: (i, 0)),
                  pl.BlockSpec((MOE_ROUTE_ROWS, D_MODEL), lambda i: (0, 0)),
                  pl.BlockSpec((MOE_ROUTE_ROWS, 1), lambda i: (0, 0))],
        out_specs=[pl.BlockSpec((tm, LANES), lambda i: (i, 0)),
                   pl.BlockSpec((1, 8, LANES), lambda i: (i, 0, 0))],
        out_shape=[jax.ShapeDtypeStruct((n_tok, LANES), F32),
                   jax.ShapeDtypeStruct((n_tok // tm, 8, LANES), F32)],
        compiler_params=_params("parallel"),
        name="route",
    )(h1, w_route, b_route)


MOE_ALIGN = 16
MOE_EPS = 4


def _moe_sizes(tm):
    base = tm // MOE_GROUPS
    fine = {min(tm, base + MOE_ALIGN * i) for i in range(5)}
    return tuple(sorted(fine | {tm // 2, tm}))


def _moe_kernel(offs_ref, cls_ref, h1_ref, comb_ref, tri_ref, wg_ref, wu_ref, wd_ref,
                g2_ref, b2_ref, y_ref, pt_scr, xs_scr, cs_scr, ys_scr, *, tm, nr1, sizes):
    i = pl.program_id(0)
    e = pl.program_id(1)
    nrt = xs_scr.shape[0]

    @pl.when(e == 0)
    def _():
        comb = comb_ref[...]
        lane_i = lax.broadcasted_iota(jnp.int32, (tm, LANES), 1)
        grp = comb[:, LANES - 1:LANES]
        is_g = lane_i.astype(F32) == grp
        prefix = jnp.dot(tri_ref[...], jnp.where(is_g, 1.0, 0.0).astype(BF16), preferred_element_type=F32)
        rank = jnp.sum(jnp.where(is_g, prefix, 0.0), -1, keepdims=True)
        offv = jnp.zeros((tm, 1), F32)
        for g in range(MOE_GROUPS):
            offv = jnp.where(grp == g, offs_ref[i * MOE_GROUPS + g].astype(F32), offv)
        pos = offv + rank
        col = lax.broadcasted_iota(jnp.int32, (tm, nr1), 1).astype(F32)
        pt = jnp.where(col == pos, 1.0, 0.0).astype(BF16)
        pt_scr[...] = pt
        combw = jnp.where(lane_i < MOE_NE, comb, 0.0)
        c_hi = combw.astype(BF16)
        c_lo = (combw - c_hi.astype(F32)).astype(BF16)
        xs_scr[0:nr1, :] = _dot_tn(pt, h1_ref[...]).astype(BF16)
        c_pair = _dot_tn(pt, jnp.concatenate([c_hi, c_lo], axis=1))
        cs_scr[0:nr1, :] = c_pair[:, 0:LANES] + c_pair[:, LANES:2 * LANES]
        xs_scr[nr1:nrt, :] = jnp.zeros((nrt - nr1, D_MODEL), BF16)
        cs_scr[nr1:nrt, :] = jnp.zeros((nrt - nr1, LANES), F32)
        ys_scr[tm:nr1, :] = jnp.zeros((nr1 - tm, D_MODEL), F32)

    g = e // (MOE_EXPERTS // MOE_EPS)
    off = pl.multiple_of(offs_ref[i * MOE_GROUPS + g], MOE_ALIGN)
    cls = cls_ref[i * MOE_GROUPS + g]
    first = e % (MOE_EXPERTS // MOE_EPS) == 0
    for ci, m in enumerate(sizes):
        @pl.when(cls == ci)
        def _(m=m):
            rows = pl.ds(off, m)
            xb = xs_scr[rows, :]
            cb = cs_scr[rows, :]
            lane_i = lax.broadcasted_iota(jnp.int32, (m, LANES), 1)
            sub = range(MOE_EPS)
            colw = [jnp.sum(jnp.where(lane_i == e * MOE_EPS + j, cb, 0.0), -1, keepdims=True) for j in sub]
            gate = [jnp.dot(xb, wg_ref[j], preferred_element_type=F32) for j in sub]
            up = [jnp.dot(xb, wu_ref[j], preferred_element_type=F32) for j in sub]
            hg = [(_silu(gate[j]) * up[j] * colw[j]).astype(BF16) for j in sub]
            wd = wd_ref[...].reshape(MOE_EPS * MOE_DFF, D_MODEL)
            res = jnp.dot(jnp.concatenate(hg, axis=1), wd, preferred_element_type=F32)

            @pl.when(first)
            def _():
                ys_scr[rows, :] = res

            @pl.when(jnp.logical_not(first))
            def _():
                ys_scr[rows, :] += res

    @pl.when(e == MOE_NE // MOE_EPS - 1)
    def _():
        moe = jnp.dot(pt_scr[...], ys_scr[0:nr1, :].astype(BF16), preferred_element_type=F32)
        y_ref[...] = _layernorm(ALPHA * h1_ref[...] + moe, g2_ref[...], b2_ref[...])


def _moe(h1, comb, cnt, w_gate, w_up, w_down, ln_g, ln_b, tm):
    n_tok = h1.shape[0]
    sizes = _moe_sizes(tm)
    nr1 = -(-(tm + MOE_GROUPS * MOE_ALIGN) // LANES) * LANES
    nrt = nr1 + max(b - a for a, b in zip((0,) + sizes, sizes))
    c = cnt[:, 0, :MOE_GROUPS].astype(jnp.int32)
    seg = (c + MOE_ALIGN - 1) // MOE_ALIGN * MOE_ALIGN
    offs = (jnp.cumsum(seg, axis=1) - seg).reshape(-1)
    cls = sum((c > s).astype(jnp.int32) for s in sizes[:-1]).reshape(-1)
    tri = jnp.asarray(np.tril(np.ones((tm, tm), np.float32), -1), BF16)
    grid_spec = pltpu.PrefetchScalarGridSpec(
        num_scalar_prefetch=2,
        grid=(n_tok // tm, MOE_NE // MOE_EPS),
        in_specs=[
            pl.BlockSpec((tm, D_MODEL), lambda i, e, *_: (i, 0)),
            pl.BlockSpec((tm, LANES), lambda i, e, *_: (i, 0)),
            pl.BlockSpec((tm, tm), lambda i, e, *_: (0, 0), pipeline_mode=pl.Buffered(1)),
            pl.BlockSpec((MOE_EPS, D_MODEL, MOE_DFF), lambda i, e, *_: (e, 0, 0)),
            pl.BlockSpec((MOE_EPS, D_MODEL, MOE_DFF), lambda i, e, *_: (e, 0, 0)),
            pl.BlockSpec((MOE_EPS, MOE_DFF, D_MODEL), lambda i, e, *_: (e, 0, 0)),
            pl.BlockSpec((1, D_MODEL), lambda i, e, *_: (0, 0)),
            pl.BlockSpec((1, D_MODEL), lambda i, e, *_: (0, 0)),
        ],
        out_specs=pl.BlockSpec((tm, D_MODEL), lambda i, e, *_: (i, 0)),
        scratch_shapes=[pltpu.VMEM((tm, nr1), BF16), pltpu.VMEM((nrt, D_MODEL), BF16),
                        pltpu.VMEM((nrt, LANES), F32), pltpu.VMEM((nrt, D_MODEL), F32)],
    )
    return pl.pallas_call(
        functools.partial(_moe_kernel, tm=tm, nr1=nr1, sizes=sizes),
        grid_spec=grid_spec,
        out_shape=jax.ShapeDtypeStruct((n_tok, D_MODEL), F32),
        compiler_params=_params("parallel", "arbitrary"),
        name="moe",
    )(offs, cls, h1, comb, tri, w_gate, w_up, w_down, ln_g, ln_b)


def _rope_tables(start, t):
    half = RET_DK // 2
    inv = 1.0 / (ROPE_BASE ** (np.arange(half, dtype=np.float64) / half))
    ang = (start + np.arange(t, dtype=np.float64))[:, None] * inv[None, :]
    cos = np.cos(ang)
    sin = np.sin(ang)
    return (jnp.asarray(np.concatenate([cos, cos], -1), F32),
            jnp.asarray(np.concatenate([-sin, sin], -1), F32))


def _group(x, pos0, mem_k, mem_v, ret_s0, dn_s0, conv_buf, wts, *, tm, tm_mix, tm_moe, ret_chunk, dn_blk,
           dn_chunk):
    nb, t, d = x.shape
    n_tok = nb * t
    h, small, small_t, proj = _in_proj(x.reshape(n_tok, d), wts["ln_in_g"], wts["ln_in_b"],
                                           wts["w_in"], wts["w_small"], wts["w_small_t"], tm)
    proj3 = proj.reshape(nb, t, PROJ_W)
    cos2, sin2 = _rope_tables(pos0, t)
    ret_o, ret_s = _retention(proj3, cos2, sin2, ret_s0, wts["ret_gn_g"], wts["ret_gn_b"], ret_chunk)
    small_t3 = small_t.reshape(DN_GATE_W, nb, t).transpose(1, 0, 2)
    conv_init = jnp.concatenate([jnp.zeros((nb, 8 - (DN_CONV_W - 1), DN_CONV_CH), F32), conv_buf], axis=1)
    dn_o, dn_s, conv_new = _deltanet(proj3, small.reshape(nb, t, LANES), small_t3, wts["dn_conv"], conv_init,
                                     wts["dn_prow"],
                                     wts["dn_pcol"], wts["dn_norm_g"], dn_s0, dn_blk, dn_chunk)
    h1 = _mix(h, ret_o.reshape(n_tok, d), dn_o.reshape(n_tok, d), mem_k, mem_v,
                   wts["w_xq"], wts["w_gate"], wts["w_branch"], wts["w_out"], wts["ln1_g"], wts["ln1_b"],
                   tm_mix, t)
    comb, cnt = _route_call(h1, wts["w_route"], wts["b_route"], tm_moe)
    y = _moe(h1, comb, cnt, wts["moe_w_gate"], wts["moe_w_up"], wts["moe_w_down"],
             wts["ln2_g"], wts["ln2_b"], tm_moe)
    return y.reshape(nb, t, d), ret_s, dn_s, conv_new


def kernel(x_prompt, x_sample, mem_prompt, state_ret, state_dn, state_dn_conv, cache_mem_k, cache_mem_v,
           ln_in_g, ln_in_b, w_in, ret_gn_g, ret_gn_b, dn_conv, dn_A_log, dn_dt_bias, dn_norm_g,
           w_mem_kv, w_branch, w_out, ln1_g, ln1_b, moe_w_coarse, moe_b_coarse, moe_w_fine, moe_b_fine,
           moe_w_gate, moe_w_up, moe_w_down, ln2_g, ln2_b):
    bp, tp, d = x_prompt.shape
    bs, ts, _ = x_sample.shape
    l = 0
    wi = w_in[l].astype(BF16)
    dba = wi[:, PROJ_W:COL_XQ]
    xq, gates = wi[:, COL_XQ:COL_GATE], wi[:, COL_GATE:N_IN]
    wts = {
        "ln_in_g": ln_in_g.reshape(1, d), "ln_in_b": ln_in_b.reshape(1, d),
        "w_in": wi,
        "w_small": jnp.pad(dba, ((0, 0), (0, LANES - DN_GATE_W))),
        "w_small_t": dba.T,
        "ret_gn_g": ret_gn_g[l].reshape(1, -1), "ret_gn_b": ret_gn_b[l].reshape(1, -1),
        "dn_conv": dn_conv[l],
        "dn_prow": jnp.zeros((8, LANES), F32).at[0, 8:16].set(-jnp.exp(dn_A_log[l])).at[1, 8:16].set(dn_dt_bias[l]),
        "dn_pcol": jnp.zeros((16, LANES), F32).at[8:16, 0].set(-jnp.exp(dn_A_log[l])).at[8:16, 1].set(dn_dt_bias[l]),
        "dn_norm_g": dn_norm_g[l].reshape(1, -1),
        "w_xq": xq, "w_gate": gates,
        "w_branch": w_branch[l].astype(BF16), "w_out": w_out[l].astype(BF16),
        "ln1_g": ln1_g[l].reshape(1, d), "ln1_b": ln1_b[l].reshape(1, d),
        "w_route": jnp.pad(jnp.concatenate([moe_w_fine[l], moe_w_coarse[l]], axis=1).T,
                           ((0, MOE_ROUTE_ROWS - MOE_NE - MOE_GROUPS), (0, 0))),
        "b_route": jnp.pad(jnp.concatenate([moe_b_fine[l], moe_b_coarse[l]]),
                           (0, MOE_ROUTE_ROWS - MOE_NE - MOE_GROUPS)).reshape(MOE_ROUTE_ROWS, 1),
        "moe_w_gate": moe_w_gate[l].reshape(MOE_NE, d, MOE_DFF).astype(BF16),
        "moe_w_up": moe_w_up[l].reshape(MOE_NE, d, MOE_DFF).astype(BF16),
        "moe_w_down": moe_w_down[l].reshape(MOE_NE, MOE_DFF, d).astype(BF16),
        "ln2_g": ln2_g[l].reshape(1, d), "ln2_b": ln2_b[l].reshape(1, d),
    }

    mkv = _memkv(mem_prompt.reshape(bp * N_MEM, d), w_mem_kv[l].astype(BF16))
    mk = mkv[:, :XA_HEADS * XA_DH].reshape(1, bp, N_MEM, XA_HEADS, XA_DH)
    mv = mkv[:, XA_HEADS * XA_DH:].reshape(1, bp, N_MEM, XA_HEADS, XA_DH)
    yp, rs_p, ds_p, cb_p = _group(
        x_prompt, 0, mk, mv,
        jnp.zeros((bp, RET_HEADS, RET_DK, RET_DV), F32),
        jnp.zeros((bp, DN_V_HEADS, DN_DK, DN_DV), F32),
        jnp.zeros((bp, DN_CONV_W - 1, DN_CONV_CH), F32),
        wts, tm=min(1024, tp), tm_mix=min(512, tp), tm_moe=min(1024, tp), ret_chunk=min(256, tp),
        dn_blk=min(128, tp),
        dn_chunk=min(DN_CHUNK, tp))
    ys, rs_s, ds_s, cb_s = _group(
        x_sample, PAST_LEN, cache_mem_k[l:l + 1], cache_mem_v[l:l + 1],
        state_ret[l], state_dn[l], state_dn_conv[l],
        wts, tm=bs * ts, tm_mix=bs * ts, tm_moe=bs * ts, ret_chunk=ts, dn_blk=ts, dn_chunk=min(DN_CHUNK, ts))
    return (yp, ys, rs_p[None], rs_s[None], ds_p[None], ds_s[None], cb_p[None], cb_s[None], mk, mv)
```

```python
import functools
import math

import numpy as np

import jax
import jax.numpy as jnp
from jax import lax
from jax.experimental import pallas as pl
from jax.experimental.pallas import tpu as pltpu

F32 = jnp.float32
BF16 = jnp.bfloat16

D_MODEL = 1024
PAST_LEN = 1024
RET_HEADS = 4
RET_DK = 128
RET_DV = 256
ROPE_BASE = 10000.0
DN_QK_HEADS = 4
DN_V_HEADS = 8
DN_DK = 128
DN_DV = 128
DN_QK = DN_QK_HEADS * DN_DK
DN_CONV_W = 4
DN_CONV_CH = 2048
DN_CHUNK = 64
DN_SUB = 16
DN_HEAD_GROUP = 8
DN_STREAMS = 2
DN_STREAM_ROWS = 256
XA_HEADS = 4
XA_DH = 256
N_MEM = 256
MOE_GROUPS = 4
MOE_EXPERTS = 8
MOE_NE = MOE_GROUPS * MOE_EXPERTS
MOE_DFF = 256
MOE_ROUTE_ROWS = 40
DEPTH = 1
ALPHA = (2.0 * DEPTH) ** 0.25
LANES = 128
RET_QKW = RET_HEADS * RET_DK
RET_VW = RET_HEADS * RET_DV
DN_QKW = 2 * DN_QK
DN_VW = DN_V_HEADS * DN_DV
COL_RQ = 0
COL_RK = COL_RQ + RET_QKW
COL_RV = COL_RK + RET_QKW
COL_RG = COL_RV + RET_VW
COL_DQK = COL_RG + RET_VW
COL_DV = COL_DQK + DN_QKW
COL_DZ = COL_DV + DN_VW
PROJ_W = COL_DZ + DN_VW
DN_GATE_W = 2 * DN_V_HEADS
COL_XQ = PROJ_W + DN_GATE_W
COL_GATE = COL_XQ + XA_HEADS * XA_DH
N_IN = COL_GATE + 3 * D_MODEL
VMEM_LIMIT = 56 * 1024 * 1024


def _dot(a, b):
    return jnp.dot(a.astype(BF16), b.astype(BF16), preferred_element_type=F32)


def _dot_nt(a, b):
    return lax.dot_general(a.astype(BF16), b.astype(BF16), (((1,), (1,)), ((), ())),
                           preferred_element_type=F32)


def _dot_tn(a, b):
    return lax.dot_general(a.astype(BF16), b.astype(BF16), (((0,), (0,)), ((), ())),
                           preferred_element_type=F32)


def _dot_exact(a, b):
    return jnp.dot(a, b, preferred_element_type=F32, precision=lax.Precision.HIGHEST)


def _layernorm(x, g, b, eps=1e-5):
    mu = jnp.mean(x, -1, keepdims=True)
    xc = x - mu
    var = jnp.mean(xc * xc, -1, keepdims=True)
    return xc * lax.rsqrt(var + eps) * g + b


def _silu(x):
    return x * (1.0 / (1.0 + jnp.exp(-x)))


def _sigmoid(x):
    return 1.0 / (1.0 + jnp.exp(-x))


def _softplus(x):
    return jnp.maximum(x, 0.0) + jnp.log(1.0 + jnp.exp(-jnp.abs(x)))


def _idiv(x, n):
    return jnp.right_shift(x, int(math.log2(n)))


def _params(*sem):
    return pltpu.CompilerParams(dimension_semantics=sem, vmem_limit_bytes=VMEM_LIMIT)


def _in_proj_kernel(x_ref, g_ref, b_ref, w_ref, ws_ref, wst_ref,
                    h_ref, small_ref, smallt_ref, proj_ref, hs_ref):
    @pl.when(pl.program_id(1) == 0)
    def _():
        h = _layernorm(x_ref[...], g_ref[...], b_ref[...])
        h_ref[...] = h
        hb = h.astype(BF16)
        hs_ref[...] = hb
        small_ref[...] = jnp.dot(hb, ws_ref[...], preferred_element_type=F32)
        smallt_ref[...] = lax.dot_general(wst_ref[...], hb, (((1,), (1,)), ((), ())),
                                          preferred_element_type=F32)

    proj_ref[...] = jnp.dot(hs_ref[...], w_ref[...], preferred_element_type=F32).astype(BF16)


def _in_proj(x, ln_g, ln_b, w_main, w_small, w_small_t, tm, tn=2048):
    n_tok = x.shape[0]
    n_col = PROJ_W // tn
    grid = (n_tok // tm, n_col)
    col = lambda i, n: jnp.where(i % 2 == 0, n, n_col - 1 - n)
    return pl.pallas_call(
        _in_proj_kernel,
        grid=grid,
        in_specs=[
            pl.BlockSpec((tm, D_MODEL), lambda i, n: (i, 0)),
            pl.BlockSpec((1, D_MODEL), lambda i, n: (0, 0)),
            pl.BlockSpec((1, D_MODEL), lambda i, n: (0, 0)),
            pl.BlockSpec((D_MODEL, tn), lambda i, n: (0, col(i, n))),
            pl.BlockSpec((D_MODEL, LANES), lambda i, n: (0, 0)),
            pl.BlockSpec((DN_GATE_W, D_MODEL), lambda i, n: (0, 0)),
        ],
        out_specs=[
            pl.BlockSpec((tm, D_MODEL), lambda i, n: (i, 0)),
            pl.BlockSpec((tm, LANES), lambda i, n: (i, 0)),
            pl.BlockSpec((DN_GATE_W, tm), lambda i, n: (0, i)),
            pl.BlockSpec((tm, tn), lambda i, n: (i, col(i, n))),
        ],
        out_shape=[
            jax.ShapeDtypeStruct((n_tok, D_MODEL), F32),
            jax.ShapeDtypeStruct((n_tok, LANES), F32),
            jax.ShapeDtypeStruct((DN_GATE_W, n_tok), F32),
            jax.ShapeDtypeStruct((n_tok, PROJ_W), BF16),
        ],
        scratch_shapes=[pltpu.VMEM((tm, D_MODEL), BF16)],
        compiler_params=_params("parallel", "arbitrary"),
        name="in_proj",
    )(x, ln_g, ln_b, w_main, w_small, w_small_t)


def _ret_kernel(q_ref, k_ref, v_ref, rg_ref, cos_ref, sin_ref, s0_ref, gng_ref, gnb_ref,
                o_ref, sout_ref, s_scr, dec_scr, *, chunk):
    heads = range(RET_HEADS)
    lg = [math.log(1.0 - 2.0 ** (-5.0 - h)) for h in heads]

    @pl.when(pl.program_id(1) == 0)
    def _():
        s_scr[...] = s0_ref[0]
        ri = lax.broadcasted_iota(jnp.int32, (chunk, chunk), 0)
        ci = lax.broadcasted_iota(jnp.int32, (chunk, chunk), 1)
        causal = ri >= ci
        diff = jnp.where(causal, (ri - ci).astype(F32), 0.0)
        for h in heads:
            dec_scr[h] = jnp.where(causal, jnp.exp(diff * lg[h]), 0.0)

    cos = cos_ref[...]
    sin = sin_ref[...]
    idx = lax.broadcasted_iota(jnp.int32, (chunk, 1), 0).astype(F32)
    q = [q_ref[0, :, h * RET_DK:(h + 1) * RET_DK].astype(F32) for h in heads]
    k = [k_ref[0, :, h * RET_DK:(h + 1) * RET_DK].astype(F32) for h in heads]
    v = [v_ref[0, :, h * RET_DV:(h + 1) * RET_DV] for h in heads]
    qr = [q[h] * cos + pltpu.roll(q[h], RET_DK // 2, 1) * sin for h in heads]
    kr = [(k[h] * cos + pltpu.roll(k[h], RET_DK // 2, 1) * sin) * (RET_DK ** -0.5) for h in heads]
    scores = [_dot_nt(qr[h], kr[h]) * dec_scr[h] for h in heads]
    s = [s_scr[h] for h in heads]
    cross = [_dot(qr[h], s[h]) * jnp.exp((idx + 1.0) * lg[h]) for h in heads]
    o = [_dot(scores[h], v[h]) + cross[h] for h in heads]
    kv = [_dot_tn(kr[h] * jnp.exp((chunk - 1.0 - idx) * lg[h]), v[h]) for h in heads]
    for h in heads:
        s_new = s[h] * math.exp(chunk * lg[h]) + kv[h]
        s_scr[h] = s_new
        sout_ref[0, h] = s_new
        sl = slice(h * RET_DV, (h + 1) * RET_DV)
        gated = _layernorm(o[h], gng_ref[:, sl], gnb_ref[:, sl]) * _silu(rg_ref[0, :, sl].astype(F32))
        o_ref[0, :, sl] = gated.astype(BF16)


def _retention(proj, cos2, sin2, s0, gn_g, gn_b, chunk):
    nb, t = proj.shape[:2]
    grid = (nb, t // chunk)
    qk_w = RET_QKW
    v_w = RET_VW
    return pl.pallas_call(
        functools.partial(_ret_kernel, chunk=chunk),
        grid=grid,
        in_specs=[
            pl.BlockSpec((1, chunk, qk_w), lambda b, c: (b, c, COL_RQ // qk_w)),
            pl.BlockSpec((1, chunk, qk_w), lambda b, c: (b, c, COL_RK // qk_w)),
            pl.BlockSpec((1, chunk, v_w), lambda b, c: (b, c, COL_RV // v_w)),
            pl.BlockSpec((1, chunk, v_w), lambda b, c: (b, c, COL_RG // v_w)),
            pl.BlockSpec((chunk, RET_DK), lambda b, c: (c, 0)),
            pl.BlockSpec((chunk, RET_DK), lambda b, c: (c, 0)),
            pl.BlockSpec((1, RET_HEADS, RET_DK, RET_DV), lambda b, c: (b, 0, 0, 0)),
            pl.BlockSpec((1, v_w), lambda b, c: (0, 0)),
            pl.BlockSpec((1, v_w), lambda b, c: (0, 0)),
        ],
        out_specs=[
            pl.BlockSpec((1, chunk, v_w), lambda b, c: (b, c, 0)),
            pl.BlockSpec((1, RET_HEADS, RET_DK, RET_DV), lambda b, c: (b, 0, 0, 0)),
        ],
        out_shape=[
            jax.ShapeDtypeStruct((nb, t, v_w), BF16),
            jax.ShapeDtypeStruct((nb, RET_HEADS, RET_DK, RET_DV), F32),
        ],
        scratch_shapes=[pltpu.VMEM((RET_HEADS, RET_DK, RET_DV), F32),
                        pltpu.VMEM((RET_HEADS, chunk, chunk), F32)],
        compiler_params=_params("parallel", "arbitrary"),
        name="retention",
    )(proj, proj, proj, proj, cos2, sin2, s0, gn_g, gn_b)


def _dn_kernel(xqk_ref, xv_ref, z_ref, sm_ref, smt_ref, cw_ref, cinit_ref, prow_ref, pcol_ref, ng_ref, s0_ref,
               o_ref, sout_ref, cout_ref, xext, s_scr, *, blk_len, chunk):
    L = blk_len
    nbb = xqk_ref.shape[0]
    blk = pl.program_id(1)

    @pl.when(blk == 0)
    def _():
        xext[:, 0:8, :] = cinit_ref[...]
        s_scr[...] = s0_ref[...]

    ri = lax.broadcasted_iota(jnp.int32, (L, L), 0)
    ci = lax.broadcasted_iota(jnp.int32, (L, L), 1)
    same_chunk = _idiv(ri, chunk) == _idiv(ci, chunk)
    lower = jnp.logical_and(ri >= ci, same_chunk)
    strict = jnp.logical_and(ri > ci, same_chunk)
    same_sub = _idiv(ri, DN_SUB) == _idiv(ci, DN_SUB)
    tri = jnp.where(lower, 1.0, 0.0).astype(F32)
    tri_t = jnp.where(jnp.logical_and(ri <= ci, same_chunk), 1.0, 0.0).astype(F32)
    prow = prow_ref[...]
    pcol = pcol_ref[...]

    conv, beta_tm, gcum_tm, gcum_hm, egc_tm = [], [], [], [], []
    for b in range(nbb):
        x = jnp.concatenate([xqk_ref[b], xv_ref[b]], axis=1).astype(F32)
        xext[b, 8:8 + L, :] = x
        cv = x * cw_ref[3:4, :]
        for w in range(DN_CONV_W - 1):
            cv = cv + xext[b, 5 + w:5 + w + L, :] * cw_ref[w:w + 1, :]
        conv.append(_silu(cv))
        xext[b, 0:8, :] = x[L - 8:L, :]
        cout_ref[b] = x[L - (DN_CONV_W - 1):L, :]
        sm = sm_ref[b]
        beta_tm.append(_sigmoid(sm))
        g_tm = prow[0:1, :] * _softplus(sm + prow[1:2, :])
        g_hm = pcol[8:16, 0:1] * _softplus(smt_ref[b][8:16, :] + pcol[8:16, 1:2])
        gcum_tm.append(_dot_exact(tri, g_tm))
        gcum_hm.append(_dot_exact(g_hm, tri_t))
        egc_tm.append(jnp.exp(gcum_tm[b]))

    n_sub = L // chunk
    rep = DN_V_HEADS // DN_QK_HEADS
    bdot = lambda x, y: jnp.dot(x, y, preferred_element_type=F32)
    for h0 in range(0, DN_V_HEADS, DN_HEAD_GROUP):
        heads = [(b, hh) for b in range(nbb) for hh in range(h0, h0 + DN_HEAD_GROUP)]
        q, k, kk, qk = {}, {}, {}, {}
        for b in range(nbb):
            for j in range(h0 // rep, (h0 + DN_HEAD_GROUP) // rep):
                qj = conv[b][:, j * DN_DK:(j + 1) * DN_DK]
                kj = conv[b][:, DN_QK + j * DN_DK:DN_QK + (j + 1) * DN_DK]
                qj = qj * lax.rsqrt(jnp.sum(qj * qj, -1, keepdims=True) + 1e-6) * (DN_DK ** -0.5)
                kj = kj * lax.rsqrt(jnp.sum(kj * kj, -1, keepdims=True) + 1e-6)
                kq = _dot_nt(jnp.concatenate([kj, qj], axis=0), kj)
                q[b, j], k[b, j], kk[b, j], qk[b, j] = qj, kj, kq[0:L], kq[L:2 * L]
        gc = {(b, hh): gcum_tm[b][:, 8 + hh:9 + hh] for b, hh in heads}
        beta = {(b, hh): beta_tm[b][:, hh:hh + 1] for b, hh in heads}
        eg = {(b, hh): egc_tm[b][:, 8 + hh:9 + hh] for b, hh in heads}
        rel = {(b, hh): jnp.where(lower, jnp.exp(jnp.where(lower, gc[b, hh] - gcum_hm[b][hh:hh + 1, :], 0.0)),
                                  0.0) for b, hh in heads}
        a = {(b, hh): jnp.where(strict, beta[b, hh] * kk[b, hh // rep] * rel[b, hh], 0.0) for b, hh in heads}
        attn = {(b, hh): qk[b, hh // rep] * rel[b, hh] for b, hh in heads}
        pd = {hh: jnp.where(same_sub, -a[hh], 0.0) for hh in heads}
        e = {hh: a[hh] + pd[hh] for hh in heads}
        eb = {hh: e[hh].astype(BF16) for hh in heads}
        xacc = pd
        xaccb = {hh: pd[hh].astype(BF16) for hh in heads}
        pwb = xaccb
        for _ in range(3):
            pw = {hh: bdot(pwb[hh], pwb[hh]) for hh in heads}
            pwb = {hh: pw[hh].astype(BF16) for hh in heads}
            xacc = {hh: xacc[hh] + pw[hh] + bdot(xaccb[hh], pwb[hh]) for hh in heads}
            xaccb = {hh: xacc[hh].astype(BF16) for hh in heads}
        f = {hh: e[hh] + bdot(xaccb[hh], eb[hh]) for hh in heads}
        fb = {hh: f[hh].astype(BF16) for hh in heads}
        f2 = {hh: bdot(fb[hh], fb[hh]) for hh in heads}
        y = {hh: f2[hh] - f[hh] - bdot(fb[hh], f2[hh].astype(BF16)) for hh in heads}
        rinv = {hh: xacc[hh] + y[hh] + bdot(y[hh].astype(BF16), xaccb[hh]) for hh in heads}
        sol = {}
        for b, hh in heads:
            vh = conv[b][:, 2 * DN_QK + hh * DN_DV:2 * DN_QK + (hh + 1) * DN_DV]
            rhs = jnp.concatenate([vh * beta[b, hh], k[b, hh // rep] * (beta[b, hh] * eg[b, hh])], axis=1)
            sol[b, hh] = rhs + _dot(rinv[b, hh], rhs)
        u = {p: sol[p][:, 0:DN_DV] for p in heads}
        wmat = {p: sol[p][:, DN_DV:DN_DV + DN_DK] for p in heads}
        qe = {(b, hh): q[b, hh // rep] * eg[b, hh] for b, hh in heads}
        s = {(b, hh): s_scr[b, hh] for b, hh in heads}
        v_new = {p: [] for p in heads}
        q_s = {p: [] for p in heads}
        for sc in range(n_sub):
            r0 = sc * chunk
            for b, hh in heads:
                p = (b, hh)
                ws = _dot(jnp.concatenate([wmat[p][r0:r0 + chunk], qe[p][r0:r0 + chunk]], axis=0), s[p])
                vn = u[p][r0:r0 + chunk] - ws[0:chunk]
                v_new[p].append(vn)
                q_s[p].append(ws[chunk:2 * chunk])
                g_last = gcum_tm[b][r0 + chunk - 1:r0 + chunk, 8 + hh:9 + hh]
                kd = k[b, hh // rep][r0:r0 + chunk] * jnp.exp(g_last - gc[p][r0:r0 + chunk])
                s[p] = s[p] * jnp.exp(g_last) + _dot_tn(kd, vn)
        for b, hh in heads:
            p = (b, hh)
            s_scr[b, hh] = s[p]
            sout_ref[b, hh] = s[p]
            vn = v_new[p][0] if n_sub == 1 else jnp.concatenate(v_new[p], axis=0)
            qs = q_s[p][0] if n_sub == 1 else jnp.concatenate(q_s[p], axis=0)
            o = qs + _dot(attn[p], vn)
            zh = z_ref[b, :, hh * DN_DV:(hh + 1) * DN_DV].astype(F32)
            n = o * lax.rsqrt(jnp.mean(o * o, -1, keepdims=True) + 1e-6) * ng_ref[...]
            o_ref[b, :, hh * DN_DV:(hh + 1) * DN_DV] = (n * _silu(zh)).astype(BF16)


def _deltanet(proj, small, small_t, conv_w, conv_init, prow, pcol, norm_g, s0, blk_len, chunk):
    nb, t = proj.shape[:2]
    nblk = t // blk_len
    nbb = max(DN_STREAMS, min(nb, DN_STREAM_ROWS // blk_len))
    grid = (nb // nbb, nblk)
    return pl.pallas_call(
        functools.partial(_dn_kernel, blk_len=blk_len, chunk=chunk),
        grid=grid,
        in_specs=[
            pl.BlockSpec((nbb, blk_len, DN_QKW), lambda b, i: (b, i, COL_DQK // DN_QKW)),
            pl.BlockSpec((nbb, blk_len, DN_VW), lambda b, i: (b, i, COL_DV // DN_VW)),
            pl.BlockSpec((nbb, blk_len, DN_VW), lambda b, i: (b, i, COL_DZ // DN_VW)),
            pl.BlockSpec((nbb, blk_len, LANES), lambda b, i: (b, i, 0)),
            pl.BlockSpec((nbb, DN_GATE_W, blk_len), lambda b, i: (b, 0, i)),
            pl.BlockSpec((DN_CONV_W, DN_CONV_CH), lambda b, i: (0, 0)),
            pl.BlockSpec((nbb, 8, DN_CONV_CH), lambda b, i: (b, 0, 0)),
            pl.BlockSpec((8, LANES), lambda b, i: (0, 0)),
            pl.BlockSpec((DN_GATE_W, LANES), lambda b, i: (0, 0)),
            pl.BlockSpec((1, DN_DV), lambda b, i: (0, 0)),
            pl.BlockSpec((nbb, DN_V_HEADS, DN_DK, DN_DV), lambda b, i: (b, 0, 0, 0)),
        ],
        out_specs=[
            pl.BlockSpec((nbb, blk_len, DN_VW), lambda b, i: (b, i, 0)),
            pl.BlockSpec((nbb, DN_V_HEADS, DN_DK, DN_DV), lambda b, i: (b, 0, 0, 0)),
            pl.BlockSpec((nbb, DN_CONV_W - 1, DN_CONV_CH), lambda b, i: (b, 0, 0)),
        ],
        out_shape=[
            jax.ShapeDtypeStruct((nb, t, DN_VW), BF16),
            jax.ShapeDtypeStruct((nb, DN_V_HEADS, DN_DK, DN_DV), F32),
            jax.ShapeDtypeStruct((nb, DN_CONV_W - 1, DN_CONV_CH), F32),
        ],
        scratch_shapes=[pltpu.VMEM((nbb, blk_len + 8, DN_CONV_CH), F32),
                        pltpu.VMEM((nbb, DN_V_HEADS, DN_DK, DN_DV), F32)],
        compiler_params=_params("parallel", "arbitrary"),
        name="deltanet",
    )(proj, proj, proj, small, small_t, conv_w, conv_init, prow, pcol, norm_g, s0)


def _memkv_kernel(m_ref, w_ref, o_ref):
    o_ref[...] = _dot(m_ref[...], w_ref[...])


def _memkv(mem, w_kv):
    n = mem.shape[0]
    tn = 1024
    return pl.pallas_call(
        _memkv_kernel,
        grid=(w_kv.shape[1] // tn,),
        in_specs=[pl.BlockSpec((n, D_MODEL), lambda j: (0, 0)),
                  pl.BlockSpec((D_MODEL, tn), lambda j: (0, j))],
        out_specs=pl.BlockSpec((n, tn), lambda j: (0, j)),
        out_shape=jax.ShapeDtypeStruct((n, w_kv.shape[1]), F32),
        compiler_params=_params("parallel"),
        name="memkv",
    )(mem, w_kv)


def _mix_kernel(h_ref, ret_ref, dn_ref, mk_ref, mv_ref, wxq_ref, wg_ref, wb_ref, wo_ref,
                g1_ref, b1_ref, h1_ref, mem_scr, *, n_batch, tiles_per_batch):
    @pl.when(pl.program_id(0) % tiles_per_batch == 0)
    def _():
        for b in range(n_batch):
            for hh in range(XA_HEADS):
                mem_scr[0, hh, b] = mk_ref[0, b, :, hh, :].astype(BF16)
                mem_scr[1, hh, b] = mv_ref[0, b, :, hh, :].astype(BF16)

    hb = h_ref[...].astype(BF16)
    xq = jnp.dot(hb, wxq_ref[...], preferred_element_type=F32)
    tb = xq.shape[0] // n_batch
    pairs = [(b, hh) for b in range(n_batch) for hh in range(XA_HEADS)]
    s = [_dot_nt(xq[b * tb:(b + 1) * tb, hh * XA_DH:(hh + 1) * XA_DH], mem_scr[0, hh, b]) * (XA_DH ** -0.5)
         for b, hh in pairs]
    p = [jnp.exp(si - jnp.max(si, -1, keepdims=True)) for si in s]
    p = [pi / jnp.sum(pi, -1, keepdims=True) for pi in p]
    xo = [_dot(pi, mem_scr[1, hh, b]) for pi, (b, hh) in zip(p, pairs)]
    xo = [jnp.concatenate(xo[b * XA_HEADS:(b + 1) * XA_HEADS], axis=1) for b in range(n_batch)]
    xo = xo[0] if n_batch == 1 else jnp.concatenate(xo, axis=0)
    mixed = None
    for n, br in enumerate((ret_ref[...], dn_ref[...], xo)):
        gate = _sigmoid(jnp.dot(hb, wg_ref[:, n * D_MODEL:(n + 1) * D_MODEL], preferred_element_type=F32))
        term = gate * _dot(br, wb_ref[n])
        mixed = term if mixed is None else mixed + term
    y = ALPHA * h_ref[...] + _dot(mixed, wo_ref[...])
    h1 = _layernorm(y, g1_ref[...], b1_ref[...])
    h1_ref[...] = h1


def _mix(h, ret_o, dn_o, mem_k, mem_v, w_xq, w_gate, w_branch, w_out, ln_g, ln_b, tm, t):
    n_tok = h.shape[0]
    const2 = lambda i: (0, 0)
    row = lambda i: (i, 0)
    single = pl.Buffered(1)
    if tm <= t:
        n_batch = 1
        tiles_per_batch = t // tm
    else:
        n_batch = tm // t
        tiles_per_batch = 1
    mem_spec = pl.BlockSpec((1, n_batch, N_MEM, XA_HEADS, XA_DH), lambda i: (0, i // tiles_per_batch, 0, 0, 0))
    return pl.pallas_call(
        functools.partial(_mix_kernel, n_batch=n_batch, tiles_per_batch=tiles_per_batch),
        grid=(n_tok // tm,),
        in_specs=[
            pl.BlockSpec((tm, D_MODEL), row),
            pl.BlockSpec((tm, D_MODEL), row),
            pl.BlockSpec((tm, D_MODEL), row),
            mem_spec,
            mem_spec,
            pl.BlockSpec((D_MODEL, D_MODEL), const2, pipeline_mode=single),
            pl.BlockSpec((D_MODEL, 3 * D_MODEL), const2, pipeline_mode=single),
            pl.BlockSpec((3, D_MODEL, D_MODEL), lambda i: (0, 0, 0), pipeline_mode=single),
            pl.BlockSpec((D_MODEL, D_MODEL), const2, pipeline_mode=single),
            pl.BlockSpec((1, D_MODEL), const2),
            pl.BlockSpec((1, D_MODEL), const2),
        ],
        out_specs=pl.BlockSpec((tm, D_MODEL), row),
        out_shape=jax.ShapeDtypeStruct((n_tok, D_MODEL), F32),
        scratch_shapes=[pltpu.VMEM((2, XA_HEADS, n_batch, N_MEM, XA_DH), BF16)],
        compiler_params=_params("arbitrary"),
        name="mix",
    )(h, ret_o, dn_o, mem_k, mem_v, w_xq, w_gate, w_branch, w_out, ln_g, ln_b)


def _route_t(x, wrt, brt):
    xh = x.astype(BF16)
    xl = (x - xh.astype(F32)).astype(BF16)
    wh = wrt.astype(BF16)
    wl = (wrt - wh.astype(F32)).astype(BF16)
    nt = lambda a, b: lax.dot_general(a, b, (((1,), (1,)), ((), ())), preferred_element_type=F32)
    hi = nt(jnp.concatenate([wh, wl], axis=0), xh)
    logits = hi[0:MOE_ROUTE_ROWS] + (hi[MOE_ROUTE_ROWS:2 * MOE_ROUTE_ROWS] + nt(wh, xl)) + brt
    row_i = lax.broadcasted_iota(jnp.int32, logits.shape, 0)
    row = row_i.astype(F32)
    far = jnp.float32(LANES)
    neg = jnp.float32(-3.0e38)
    is_c = jnp.logical_and(row_i >= MOE_NE, row_i < MOE_NE + MOE_GROUPS)
    cl = jnp.where(is_c, logits, neg)
    cmax = jnp.max(cl, 0, keepdims=True)
    denom = jnp.sum(jnp.where(is_c, jnp.exp(jnp.where(is_c, logits - cmax, 0.0)), 0.0), 0, keepdims=True)
    p_grp = 1.0 / denom
    grp = jnp.min(jnp.where(jnp.logical_and(is_c, cl == cmax), row - MOE_NE, far), 0, keepdims=True)
    in_grp = jnp.logical_and(row_i < MOE_NE, _idiv(row_i, MOE_EXPERTS).astype(F32) == grp)
    fl = jnp.where(in_grp, logits, neg)
    v1 = jnp.max(fl, 0, keepdims=True)
    i1 = jnp.min(jnp.where(jnp.logical_and(in_grp, fl == v1), row, far), 0, keepdims=True)
    rest = jnp.logical_and(in_grp, row != i1)
    fl2 = jnp.where(rest, logits, neg)
    v2 = jnp.max(fl2, 0, keepdims=True)
    i2 = jnp.min(jnp.where(jnp.logical_and(rest, fl2 == v2), row, far), 0, keepdims=True)
    e2 = jnp.exp(v2 - v1)
    w1 = p_grp / (1.0 + e2)
    w2 = p_grp * e2 / (1.0 + e2)
    comb_t = jnp.where(row == i1, w1, 0.0) + jnp.where(row == i2, w2, 0.0)
    tm = x.shape[0]
    last = jnp.where(lax.broadcasted_iota(jnp.int32, (8, tm), 0) == 7, grp, 0.0)
    return jnp.concatenate([comb_t, jnp.zeros((LANES - MOE_ROUTE_ROWS - 8, tm), F32), last], axis=0)


def _route_kernel(h1_ref, wrt_ref, brt_ref, comb_ref, cnt_ref):
    comb = _route_t(h1_ref[...], wrt_ref[...], brt_ref[...]).T
    comb_ref[...] = comb
    lane = lax.broadcasted_iota(jnp.int32, comb.shape, 1).astype(F32)
    onehot = jnp.where(lane == comb[:, LANES - 1:LANES], 1.0, 0.0)
    cnt_ref[0] = jnp.broadcast_to(jnp.sum(onehot, 0, keepdims=True), (8, LANES))


def _route_call(h1, w_route, b_route, tm):
    n_tok = h1.shape[0]
    return pl.pallas_call(
        _route_kernel,
        grid=(n_tok // tm,),
        in_specs=[pl.BlockSpec((tm, D_MODEL), lambda i: (i, 0)),
                  pl.BlockSpec((MOE_ROUTE_ROWS, D_MODEL), lambda i: (0, 0)),
                  pl.BlockSpec((MOE_ROUTE_ROWS, 1), lambda i: (0, 0))],
        out_specs=[pl.BlockSpec((tm, LANES), lambda i: (i, 0)),
                   pl.BlockSpec((1, 8, LANES), lambda i: (i, 0, 0))],
        out_shape=[jax.ShapeDtypeStruct((n_tok, LANES), F32),
                   jax.ShapeDtypeStruct((n_tok // tm, 8, LANES), F32)],
        compiler_params=_params("parallel"),
        name="route",
    )(h1, w_route, b_route)


MOE_ALIGN = 16
MOE_EPS = 4


def _moe_sizes(tm):
    base = tm // MOE_GROUPS
    fine = {min(tm, base + MOE_ALIGN * i) for i in range(5)}
    return tuple(sorted(fine | {tm // 2, tm}))


def _moe_kernel(offs_ref, cls_ref, h1_ref, comb_ref, tri_ref, wg_ref, wu_ref, wd_ref,
                g2_ref, b2_ref, y_ref, pt_scr, xs_scr, cs_scr, ys_scr, *, tm, nr1, sizes):
    i = pl.program_id(0)
    e = pl.program_id(1)
    nrt = xs_scr.shape[0]

    @pl.when(e == 0)
    def _():
        comb = comb_ref[...]
        lane_i = lax.broadcasted_iota(jnp.int32, (tm, LANES), 1)
        grp = comb[:, LANES - 1:LANES]
        is_g = lane_i.astype(F32) == grp
        prefix = jnp.dot(tri_ref[...], jnp.where(is_g, 1.0, 0.0).astype(BF16), preferred_element_type=F32)
        rank = jnp.sum(jnp.where(is_g, prefix, 0.0), -1, keepdims=True)
        offv = jnp.zeros((tm, 1), F32)
        for g in range(MOE_GROUPS):
            offv = jnp.where(grp == g, offs_ref[i * MOE_GROUPS + g].astype(F32), offv)
        pos = offv + rank
        col = lax.broadcasted_iota(jnp.int32, (tm, nr1), 1).astype(F32)
        pt = jnp.where(col == pos, 1.0, 0.0).astype(BF16)
        pt_scr[...] = pt
        combw = jnp.where(lane_i < MOE_NE, comb, 0.0)
        c_hi = combw.astype(BF16)
        c_lo = (combw - c_hi.astype(F32)).astype(BF16)
        xs_scr[0:nr1, :] = _dot_tn(pt, h1_ref[...]).astype(BF16)
        c_pair = _dot_tn(pt, jnp.concatenate([c_hi, c_lo], axis=1))
        cs_scr[0:nr1, :] = c_pair[:, 0:LANES] + c_pair[:, LANES:2 * LANES]
        xs_scr[nr1:nrt, :] = jnp.zeros((nrt - nr1, D_MODEL), BF16)
        cs_scr[nr1:nrt, :] = jnp.zeros((nrt - nr1, LANES), F32)
        ys_scr[tm:nr1, :] = jnp.zeros((nr1 - tm, D_MODEL), F32)

    g = e // (MOE_EXPERTS // MOE_EPS)
    off = pl.multiple_of(offs_ref[i * MOE_GROUPS + g], MOE_ALIGN)
    cls = cls_ref[i * MOE_GROUPS + g]
    first = e % (MOE_EXPERTS // MOE_EPS) == 0
    for ci, m in enumerate(sizes):
        @pl.when(cls == ci)
        def _(m=m):
            rows = pl.ds(off, m)
            xb = xs_scr[rows, :]
            cb = cs_scr[rows, :]
            lane_i = lax.broadcasted_iota(jnp.int32, (m, LANES), 1)
            sub = range(MOE_EPS)
            colw = [jnp.sum(jnp.where(lane_i == e * MOE_EPS + j, cb, 0.0), -1, keepdims=True) for j in sub]
            gate = [jnp.dot(xb, wg_ref[j], preferred_element_type=F32) for j in sub]
            up = [jnp.dot(xb, wu_ref[j], preferred_element_type=F32) for j in sub]
            hg = [(_silu(gate[j]) * up[j] * colw[j]).astype(BF16) for j in sub]
            wd = wd_ref[...].reshape(MOE_EPS * MOE_DFF, D_MODEL)
            res = jnp.dot(jnp.concatenate(hg, axis=1), wd, preferred_element_type=F32)

            @pl.when(first)
            def _():
                ys_scr[rows, :] = res

            @pl.when(jnp.logical_not(first))
            def _():
                ys_scr[rows, :] += res

    @pl.when(e == MOE_NE // MOE_EPS - 1)
    def _():
        moe = jnp.dot(pt_scr[...], ys_scr[0:nr1, :].astype(BF16), preferred_element_type=F32)
        y_ref[...] = _layernorm(ALPHA * h1_ref[...] + moe, g2_ref[...], b2_ref[...])


def _moe(h1, comb, cnt, w_gate, w_up, w_down, ln_g, ln_b, tm):
    n_tok = h1.shape[0]
    sizes = _moe_sizes(tm)
    nr1 = -(-(tm + MOE_GROUPS * MOE_ALIGN) // LANES) * LANES
    nrt = nr1 + max(b - a for a, b in zip((0,) + sizes, sizes))
    c = cnt[:, 0, :MOE_GROUPS].astype(jnp.int32)
    seg = (c + MOE_ALIGN - 1) // MOE_ALIGN * MOE_ALIGN
    offs = (jnp.cumsum(seg, axis=1) - seg).reshape(-1)
    cls = sum((c > s).astype(jnp.int32) for s in sizes[:-1]).reshape(-1)
    tri = jnp.asarray(np.tril(np.ones((tm, tm), np.float32), -1), BF16)
    grid_spec = pltpu.PrefetchScalarGridSpec(
        num_scalar_prefetch=2,
        grid=(n_tok // tm, MOE_NE // MOE_EPS),
        in_specs=[
            pl.BlockSpec((tm, D_MODEL), lambda i, e, *_: (i, 0)),
            pl.BlockSpec((tm, LANES), lambda i, e, *_: (i, 0)),
            pl.BlockSpec((tm, tm), lambda i, e, *_: (0, 0), pipeline_mode=pl.Buffered(1)),
            pl.BlockSpec((MOE_EPS, D_MODEL, MOE_DFF), lambda i, e, *_: (e, 0, 0)),
            pl.BlockSpec((MOE_EPS, D_MODEL, MOE_DFF), lambda i, e, *_: (e, 0, 0)),
            pl.BlockSpec((MOE_EPS, MOE_DFF, D_MODEL), lambda i, e, *_: (e, 0, 0)),
            pl.BlockSpec((1, D_MODEL), lambda i, e, *_: (0, 0)),
            pl.BlockSpec((1, D_MODEL), lambda i, e, *_: (0, 0)),
        ],
        out_specs=pl.BlockSpec((tm, D_MODEL), lambda i, e, *_: (i, 0)),
        scratch_shapes=[pltpu.VMEM((tm, nr1), BF16), pltpu.VMEM((nrt, D_MODEL), BF16),
                        pltpu.VMEM((nrt, LANES), F32), pltpu.VMEM((nrt, D_MODEL), F32)],
    )
    return pl.pallas_call(
        functools.partial(_moe_kernel, tm=tm, nr1=nr1, sizes=sizes),
        grid_spec=grid_spec,
        out_shape=jax.ShapeDtypeStruct((n_tok, D_MODEL), F32),
        compiler_params=_params("parallel", "arbitrary"),
        name="moe",
    )(offs, cls, h1, comb, tri, w_gate, w_up, w_down, ln_g, ln_b)


def _rope_tables(start, t):
    half = RET_DK // 2
    inv = 1.0 / (ROPE_BASE ** (np.arange(half, dtype=np.float64) / half))
    ang = (start + np.arange(t, dtype=np.float64))[:, None] * inv[None, :]
    cos = np.cos(ang)
    sin = np.sin(ang)
    return (jnp.asarray(np.concatenate([cos, cos], -1), F32),
            jnp.asarray(np.concatenate([-sin, sin], -1), F32))


def _group(x, pos0, mem_k, mem_v, ret_s0, dn_s0, conv_buf, wts, *, tm, tm_mix, tm_moe, ret_chunk, dn_blk,
           dn_chunk):
    nb, t, d = x.shape
    n_tok = nb * t
    h, small, small_t, proj = _in_proj(x.reshape(n_tok, d), wts["ln_in_g"], wts["ln_in_b"],
                                           wts["w_in"], wts["w_small"], wts["w_small_t"], tm)
    proj3 = proj.reshape(nb, t, PROJ_W)
    cos2, sin2 = _rope_tables(pos0, t)
    ret_o, ret_s = _retention(proj3, cos2, sin2, ret_s0, wts["ret_gn_g"], wts["ret_gn_b"], ret_chunk)
    small_t3 = small_t.reshape(DN_GATE_W, nb, t).transpose(1, 0, 2)
    conv_init = jnp.concatenate([jnp.zeros((nb, 8 - (DN_CONV_W - 1), DN_CONV_CH), F32), conv_buf], axis=1)
    dn_o, dn_s, conv_new = _deltanet(proj3, small.reshape(nb, t, LANES), small_t3, wts["dn_conv"], conv_init,
                                     wts["dn_prow"],
                                     wts["dn_pcol"], wts["dn_norm_g"], dn_s0, dn_blk, dn_chunk)
    h1 = _mix(h, ret_o.reshape(n_tok, d), dn_o.reshape(n_tok, d), mem_k, mem_v,
                   wts["w_xq"], wts["w_gate"], wts["w_branch"], wts["w_out"], wts["ln1_g"], wts["ln1_b"],
                   tm_mix, t)
    comb, cnt = _route_call(h1, wts["w_route"], wts["b_route"], tm_moe)
    y = _moe(h1, comb, cnt, wts["moe_w_gate"], wts["moe_w_up"], wts["moe_w_down"],
             wts["ln2_g"], wts["ln2_b"], tm_moe)
    return y.reshape(nb, t, d), ret_s, dn_s, conv_new


def kernel(x_prompt, x_sample, mem_prompt, state_ret, state_dn, state_dn_conv, cache_mem_k, cache_mem_v,
           ln_in_g, ln_in_b, w_in, ret_gn_g, ret_gn_b, dn_conv, dn_A_log, dn_dt_bias, dn_norm_g,
           w_mem_kv, w_branch, w_out, ln1_g, ln1_b, moe_w_coarse, moe_b_coarse, moe_w_fine, moe_b_fine,
           moe_w_gate, moe_w_up, moe_w_down, ln2_g, ln2_b):
    bp, tp, d = x_prompt.shape
    bs, ts, _ = x_sample.shape
    l = 0
    wi = w_in[l].astype(BF16)
    dba = wi[:, PROJ_W:COL_XQ]
    xq, gates = wi[:, COL_XQ:COL_GATE], wi[:, COL_GATE:N_IN]
    wts = {
        "ln_in_g": ln_in_g.reshape(1, d), "ln_in_b": ln_in_b.reshape(1, d),
        "w_in": wi,
        "w_small": jnp.pad(dba, ((0, 0), (0, LANES - DN_GATE_W))),
        "w_small_t": dba.T,
        "ret_gn_g": ret_gn_g[l].reshape(1, -1), "ret_gn_b": ret_gn_b[l].reshape(1, -1),
        "dn_conv": dn_conv[l],
        "dn_prow": jnp.zeros((8, LANES), F32).at[0, 8:16].set(-jnp.exp(dn_A_log[l])).at[1, 8:16].set(dn_dt_bias[l]),
        "dn_pcol": jnp.zeros((16, LANES), F32).at[8:16, 0].set(-jnp.exp(dn_A_log[l])).at[8:16, 1].set(dn_dt_bias[l]),
        "dn_norm_g": dn_norm_g[l].reshape(1, -1),
        "w_xq": xq, "w_gate": gates,
        "w_branch": w_branch[l].astype(BF16), "w_out": w_out[l].astype(BF16),
        "ln1_g": ln1_g[l].reshape(1, d), "ln1_b": ln1_b[l].reshape(1, d),
        "w_route": jnp.pad(jnp.concatenate([moe_w_fine[l], moe_w_coarse[l]], axis=1).T,
                           ((0, MOE_ROUTE_ROWS - MOE_NE - MOE_GROUPS), (0, 0))),
        "b_route": jnp.pad(jnp.concatenate([moe_b_fine[l], moe_b_coarse[l]]),
                           (0, MOE_ROUTE_ROWS - MOE_NE - MOE_GROUPS)).reshape(MOE_ROUTE_ROWS, 1),
        "moe_w_gate": moe_w_gate[l].reshape(MOE_NE, d, MOE_DFF).astype(BF16),
        "moe_w_up": moe_w_up[l].reshape(MOE_NE, d, MOE_DFF).astype(BF16),
        "moe_w_down": moe_w_down[l].reshape(MOE_NE, MOE_DFF, d).astype(BF16),
        "ln2_g": ln2_g[l].reshape(1, d), "ln2_b": ln2_b[l].reshape(1, d),
    }

    mkv = _memkv(mem_prompt.reshape(bp * N_MEM, d), w_mem_kv[l].astype(BF16))
    mk = mkv[:, :XA_HEADS * XA_DH].reshape(1, bp, N_MEM, XA_HEADS, XA_DH)
    mv = mkv[:, XA_HEADS * XA_DH:].reshape(1, bp, N_MEM, XA_HEADS, XA_DH)
    yp, rs_p, ds_p, cb_p = _group(
        x_prompt, 0, mk, mv,
        jnp.zeros((bp, RET_HEADS, RET_DK, RET_DV), F32),
        jnp.zeros((bp, DN_V_HEADS, DN_DK, DN_DV), F32),
        jnp.zeros((bp, DN_CONV_W - 1, DN_CONV_CH), F32),
        wts, tm=min(1024, tp), tm_mix=min(512, tp), tm_moe=min(1024, tp), ret_chunk=min(256, tp),
        dn_blk=min(128, tp),
        dn_chunk=min(DN_CHUNK, tp))
    ys, rs_s, ds_s, cb_s = _group(
        x_sample, PAST_LEN, cache_mem_k[l:l + 1], cache_mem_v[l:l + 1],
        state_ret[l], state_dn[l], state_dn_conv[l],
        wts, tm=bs * ts, tm_mix=bs * ts, tm_moe=bs * ts, ret_chunk=ts, dn_blk=ts, dn_chunk=min(DN_CHUNK, ts))
    return (yp, ys, rs_p[None], rs_s[None], ds_p[None], ds_s[None], cb_p[None], cb_s[None], mk, mv)
```

```python
import functools
import math

import numpy as np

import jax
import jax.numpy as jnp
from jax import lax
from jax.experimental import pallas as pl
from jax.experimental.pallas import tpu as pltpu

F32 = jnp.float32
BF16 = jnp.bfloat16

D_MODEL = 1024
PAST_LEN = 1024
RET_HEADS = 4
RET_DK = 128
RET_DV = 256
ROPE_BASE = 10000.0
DN_QK_HEADS = 4
DN_V_HEADS = 8
DN_DK = 128
DN_DV = 128
DN_QK = DN_QK_HEADS * DN_DK
DN_CONV_W = 4
DN_CONV_CH = 2048
DN_CHUNK = 64
DN_SUB = 16
DN_HEAD_GROUP = 8
DN_STREAMS = 2
DN_STREAM_ROWS = 256
XA_HEADS = 4
XA_DH = 256
N_MEM = 256
MOE_GROUPS = 4
MOE_EXPERTS = 8
MOE_NE = MOE_GROUPS * MOE_EXPERTS
MOE_DFF = 256
MOE_ROUTE_ROWS = 40
DEPTH = 1
ALPHA = (2.0 * DEPTH) ** 0.25
LANES = 128
RET_QKW = RET_HEADS * RET_DK
RET_VW = RET_HEADS * RET_DV
DN_QKW = 2 * DN_QK
DN_VW = DN_V_HEADS * DN_DV
COL_RQ = 0
COL_RK = COL_RQ + RET_QKW
COL_RV = COL_RK + RET_QKW
COL_RG = COL_RV + RET_VW
COL_DQK = COL_RG + RET_VW
COL_DV = COL_DQK + DN_QKW
COL_DZ = COL_DV + DN_VW
PROJ_W = COL_DZ + DN_VW
DN_GATE_W = 2 * DN_V_HEADS
COL_XQ = PROJ_W + DN_GATE_W
COL_GATE = COL_XQ + XA_HEADS * XA_DH
N_IN = COL_GATE + 3 * D_MODEL
VMEM_LIMIT = 56 * 1024 * 1024


def _dot(a, b):
    return jnp.dot(a.astype(BF16), b.astype(BF16), preferred_element_type=F32)


def _dot_nt(a, b):
    return lax.dot_general(a.astype(BF16), b.astype(BF16), (((1,), (1,)), ((), ())),
                           preferred_element_type=F32)


def _dot_tn(a, b):
    return lax.dot_general(a.astype(BF16), b.astype(BF16), (((0,), (0,)), ((), ())),
                           preferred_element_type=F32)


def _dot_exact(a, b):
    return jnp.dot(a, b, preferred_element_type=F32, precision=lax.Precision.HIGHEST)


def _layernorm(x, g, b, eps=1e-5):
    mu = jnp.mean(x, -1, keepdims=True)
    xc = x - mu
    var = jnp.mean(xc * xc, -1, keepdims=True)
    return xc * lax.rsqrt(var + eps) * g + b


def _silu(x):
    return x * (1.0 / (1.0 + jnp.exp(-x)))


def _sigmoid(x):
    return 1.0 / (1.0 + jnp.exp(-x))


def _softplus(x):
    return jnp.maximum(x, 0.0) + jnp.log(1.0 + jnp.exp(-jnp.abs(x)))


def _idiv(x, n):
    return jnp.right_shift(x, int(math.log2(n)))


def _params(*sem):
    return pltpu.CompilerParams(dimension_semantics=sem, vmem_limit_bytes=VMEM_LIMIT)


def _in_proj_kernel(x_ref, g_ref, b_ref, w_ref, ws_ref, wst_ref,
                    h_ref, small_ref, smallt_ref, proj_ref, hs_ref):
    @pl.when(pl.program_id(1) == 0)
    def _():
        h = _layernorm(x_ref[...], g_ref[...], b_ref[...])
        h_ref[...] = h
        hb = h.astype(BF16)
        hs_ref[...] = hb
        small_ref[...] = jnp.dot(hb, ws_ref[...], preferred_element_type=F32)
        smallt_ref[...] = lax.dot_general(wst_ref[...], hb, (((1,), (1,)), ((), ())),
                                          preferred_element_type=F32)

    proj_ref[...] = jnp.dot(hs_ref[...], w_ref[...], preferred_element_type=F32).astype(BF16)


def _in_proj(x, ln_g, ln_b, w_main, w_small, w_small_t, tm, tn=PROJ_W // 2):
    n_tok = x.shape[0]
    n_col = PROJ_W // tn
    grid = (n_tok // tm, n_col)
    col = lambda i, n: jnp.where(i % 2 == 0, n, n_col - 1 - n)
    return pl.pallas_call(
        _in_proj_kernel,
        grid=grid,
        in_specs=[
            pl.BlockSpec((tm, D_MODEL), lambda i, n: (i, 0)),
            pl.BlockSpec((1, D_MODEL), lambda i, n: (0, 0)),
            pl.BlockSpec((1, D_MODEL), lambda i, n: (0, 0)),
            pl.BlockSpec((D_MODEL, tn), lambda i, n: (0, col(i, n))),
            pl.BlockSpec((D_MODEL, LANES), lambda i, n: (0, 0)),
            pl.BlockSpec((DN_GATE_W, D_MODEL), lambda i, n: (0, 0)),
        ],
        out_specs=[
            pl.BlockSpec((tm, D_MODEL), lambda i, n: (i, 0)),
            pl.BlockSpec((tm, LANES), lambda i, n: (i, 0)),
            pl.BlockSpec((DN_GATE_W, tm), lambda i, n: (0, i)),
            pl.BlockSpec((tm, tn), lambda i, n: (i, col(i, n))),
        ],
        out_shape=[
            jax.ShapeDtypeStruct((n_tok, D_MODEL), F32),
            jax.ShapeDtypeStruct((n_tok, LANES), F32),
            jax.ShapeDtypeStruct((DN_GATE_W, n_tok), F32),
            jax.ShapeDtypeStruct((n_tok, PROJ_W), BF16),
        ],
        scratch_shapes=[pltpu.VMEM((tm, D_MODEL), BF16)],
        compiler_params=_params("parallel", "arbitrary"),
        name="in_proj",
    )(x, ln_g, ln_b, w_main, w_small, w_small_t)


def _ret_kernel(q_ref, k_ref, v_ref, rg_ref, cos_ref, sin_ref, s0_ref, gng_ref, gnb_ref,
                o_ref, sout_ref, s_scr, dec_scr, *, chunk):
    heads = range(RET_HEADS)
    lg = [math.log(1.0 - 2.0 ** (-5.0 - h)) for h in heads]

    @pl.when(pl.program_id(1) == 0)
    def _():
        s_scr[...] = s0_ref[0]
        ri = lax.broadcasted_iota(jnp.int32, (chunk, chunk), 0)
        ci = lax.broadcasted_iota(jnp.int32, (chunk, chunk), 1)
        causal = ri >= ci
        diff = jnp.where(causal, (ri - ci).astype(F32), 0.0)
        for h in heads:
            dec_scr[h] = jnp.where(causal, jnp.exp(diff * lg[h]), 0.0)

    cos = cos_ref[...]
    sin = sin_ref[...]
    idx = lax.broadcasted_iota(jnp.int32, (chunk, 1), 0).astype(F32)
    q = [q_ref[0, :, h * RET_DK:(h + 1) * RET_DK].astype(F32) for h in heads]
    k = [k_ref[0, :, h * RET_DK:(h + 1) * RET_DK].astype(F32) for h in heads]
    v = [v_ref[0, :, h * RET_DV:(h + 1) * RET_DV] for h in heads]
    qr = [q[h] * cos + pltpu.roll(q[h], RET_DK // 2, 1) * sin for h in heads]
    kr = [(k[h] * cos + pltpu.roll(k[h], RET_DK // 2, 1) * sin) * (RET_DK ** -0.5) for h in heads]
    scores = [_dot_nt(qr[h], kr[h]) * dec_scr[h] for h in heads]
    s = [s_scr[h] for h in heads]
    cross = [_dot(qr[h], s[h]) * jnp.exp((idx + 1.0) * lg[h]) for h in heads]
    o = [_dot(scores[h], v[h]) + cross[h] for h in heads]
    kv = [_dot_tn(kr[h] * jnp.exp((chunk - 1.0 - idx) * lg[h]), v[h]) for h in heads]
    for h in heads:
        s_new = s[h] * math.exp(chunk * lg[h]) + kv[h]
        s_scr[h] = s_new
        sout_ref[0, h] = s_new
        sl = slice(h * RET_DV, (h + 1) * RET_DV)
        gated = _layernorm(o[h], gng_ref[:, sl], gnb_ref[:, sl]) * _silu(rg_ref[0, :, sl].astype(F32))
        o_ref[0, :, sl] = gated.astype(BF16)


def _retention(proj, cos2, sin2, s0, gn_g, gn_b, chunk):
    nb, t = proj.shape[:2]
    grid = (nb, t // chunk)
    qk_w = RET_QKW
    v_w = RET_VW
    return pl.pallas_call(
        functools.partial(_ret_kernel, chunk=chunk),
        grid=grid,
        in_specs=[
            pl.BlockSpec((1, chunk, qk_w), lambda b, c: (b, c, COL_RQ // qk_w)),
            pl.BlockSpec((1, chunk, qk_w), lambda b, c: (b, c, COL_RK // qk_w)),
            pl.BlockSpec((1, chunk, v_w), lambda b, c: (b, c, COL_RV // v_w)),
            pl.BlockSpec((1, chunk, v_w), lambda b, c: (b, c, COL_RG // v_w)),
            pl.BlockSpec((chunk, RET_DK), lambda b, c: (c, 0)),
            pl.BlockSpec((chunk, RET_DK), lambda b, c: (c, 0)),
            pl.BlockSpec((1, RET_HEADS, RET_DK, RET_DV), lambda b, c: (b, 0, 0, 0)),
            pl.BlockSpec((1, v_w), lambda b, c: (0, 0)),
            pl.BlockSpec((1, v_w), lambda b, c: (0, 0)),
        ],
        out_specs=[
            pl.BlockSpec((1, chunk, v_w), lambda b, c: (b, c, 0)),
            pl.BlockSpec((1, RET_HEADS, RET_DK, RET_DV), lambda b, c: (b, 0, 0, 0)),
        ],
        out_shape=[
            jax.ShapeDtypeStruct((nb, t, v_w), BF16),
            jax.ShapeDtypeStruct((nb, RET_HEADS, RET_DK, RET_DV), F32),
        ],
        scratch_shapes=[pltpu.VMEM((RET_HEADS, RET_DK, RET_DV), F32),
                        pltpu.VMEM((RET_HEADS, chunk, chunk), F32)],
        compiler_params=_params("parallel", "arbitrary"),
        name="retention",
    )(proj, proj, proj, proj, cos2, sin2, s0, gn_g, gn_b)


def _dn_kernel(xqk_ref, xv_ref, z_ref, sm_ref, smt_ref, cw_ref, cinit_ref, prow_ref, pcol_ref, ng_ref, s0_ref,
               o_ref, sout_ref, cout_ref, xext, s_scr, *, blk_len, chunk):
    L = blk_len
    nbb = xqk_ref.shape[0]
    blk = pl.program_id(1)

    @pl.when(blk == 0)
    def _():
        xext[:, 0:8, :] = cinit_ref[...]
        s_scr[...] = s0_ref[...]

    ri = lax.broadcasted_iota(jnp.int32, (L, L), 0)
    ci = lax.broadcasted_iota(jnp.int32, (L, L), 1)
    same_chunk = _idiv(ri, chunk) == _idiv(ci, chunk)
    lower = jnp.logical_and(ri >= ci, same_chunk)
    strict = jnp.logical_and(ri > ci, same_chunk)
    same_sub = _idiv(ri, DN_SUB) == _idiv(ci, DN_SUB)
    tri = jnp.where(lower, 1.0, 0.0).astype(F32)
    tri_t = jnp.where(jnp.logical_and(ri <= ci, same_chunk), 1.0, 0.0).astype(F32)
    prow = prow_ref[...]
    pcol = pcol_ref[...]

    conv, beta_tm, gcum_tm, gcum_hm, egc_tm = [], [], [], [], []
    for b in range(nbb):
        x = jnp.concatenate([xqk_ref[b], xv_ref[b]], axis=1).astype(F32)
        xext[b, 8:8 + L, :] = x
        cv = x * cw_ref[3:4, :]
        for w in range(DN_CONV_W - 1):
            cv = cv + xext[b, 5 + w:5 + w + L, :] * cw_ref[w:w + 1, :]
        conv.append(_silu(cv))
        xext[b, 0:8, :] = x[L - 8:L, :]
        cout_ref[b] = x[L - (DN_CONV_W - 1):L, :]
        sm = sm_ref[b]
        beta_tm.append(_sigmoid(sm))
        g_tm = prow[0:1, :] * _softplus(sm + prow[1:2, :])
        g_hm = pcol[8:16, 0:1] * _softplus(smt_ref[b][8:16, :] + pcol[8:16, 1:2])
        gcum_tm.append(_dot_exact(tri, g_tm))
        gcum_hm.append(_dot_exact(g_hm, tri_t))
        egc_tm.append(jnp.exp(gcum_tm[b]))

    n_sub = L // chunk
    rep = DN_V_HEADS // DN_QK_HEADS
    bdot = lambda x, y: jnp.dot(x, y, preferred_element_type=F32)
    for h0 in range(0, DN_V_HEADS, DN_HEAD_GROUP):
        heads = [(b, hh) for b in range(nbb) for hh in range(h0, h0 + DN_HEAD_GROUP)]
        q, k, kk, qk = {}, {}, {}, {}
        for b in range(nbb):
            for j in range(h0 // rep, (h0 + DN_HEAD_GROUP) // rep):
                qj = conv[b][:, j * DN_DK:(j + 1) * DN_DK]
                kj = conv[b][:, DN_QK + j * DN_DK:DN_QK + (j + 1) * DN_DK]
                qj = qj * lax.rsqrt(jnp.sum(qj * qj, -1, keepdims=True) + 1e-6) * (DN_DK ** -0.5)
                kj = kj * lax.rsqrt(jnp.sum(kj * kj, -1, keepdims=True) + 1e-6)
                kq = _dot_nt(jnp.concatenate([kj, qj], axis=0), kj)
                q[b, j], k[b, j], kk[b, j], qk[b, j] = qj, kj, kq[0:L], kq[L:2 * L]
        gc = {(b, hh): gcum_tm[b][:, 8 + hh:9 + hh] for b, hh in heads}
        beta = {(b, hh): beta_tm[b][:, hh:hh + 1] for b, hh in heads}
        eg = {(b, hh): egc_tm[b][:, 8 + hh:9 + hh] for b, hh in heads}
        rel = {(b, hh): jnp.where(lower, jnp.exp(jnp.where(lower, gc[b, hh] - gcum_hm[b][hh:hh + 1, :], 0.0)),
                                  0.0) for b, hh in heads}
        a = {(b, hh): jnp.where(strict, beta[b, hh] * kk[b, hh // rep] * rel[b, hh], 0.0) for b, hh in heads}
        attn = {(b, hh): qk[b, hh // rep] * rel[b, hh] for b, hh in heads}
        pd = {hh: jnp.where(same_sub, -a[hh], 0.0) for hh in heads}
        e = {hh: a[hh] + pd[hh] for hh in heads}
        eb = {hh: e[hh].astype(BF16) for hh in heads}
        xacc = pd
        xaccb = {hh: pd[hh].astype(BF16) for hh in heads}
        pwb = xaccb
        for _ in range(3):
            pw = {hh: bdot(pwb[hh], pwb[hh]) for hh in heads}
            pwb = {hh: pw[hh].astype(BF16) for hh in heads}
            xacc = {hh: xacc[hh] + pw[hh] + bdot(xaccb[hh], pwb[hh]) for hh in heads}
            xaccb = {hh: xacc[hh].astype(BF16) for hh in heads}
        f = {hh: e[hh] + bdot(xaccb[hh], eb[hh]) for hh in heads}
        fb = {hh: f[hh].astype(BF16) for hh in heads}
        f2 = {hh: bdot(fb[hh], fb[hh]) for hh in heads}
        y = {hh: f2[hh] - f[hh] - bdot(fb[hh], f2[hh].astype(BF16)) for hh in heads}
        rinv = {hh: xacc[hh] + y[hh] + bdot(y[hh].astype(BF16), xaccb[hh]) for hh in heads}
        sol = {}
        for b, hh in heads:
            vh = conv[b][:, 2 * DN_QK + hh * DN_DV:2 * DN_QK + (hh + 1) * DN_DV]
            rhs = jnp.concatenate([vh * beta[b, hh], k[b, hh // rep] * (beta[b, hh] * eg[b, hh])], axis=1)
            sol[b, hh] = rhs + _dot(rinv[b, hh], rhs)
        u = {p: sol[p][:, 0:DN_DV] for p in heads}
        wmat = {p: sol[p][:, DN_DV:DN_DV + DN_DK] for p in heads}
        qe = {(b, hh): q[b, hh // rep] * eg[b, hh] for b, hh in heads}
        s = {(b, hh): s_scr[b, hh] for b, hh in heads}
        v_new = {p: [] for p in heads}
        q_s = {p: [] for p in heads}
        for sc in range(n_sub):
            r0 = sc * chunk
            for b, hh in heads:
                p = (b, hh)
                ws = _dot(jnp.concatenate([wmat[p][r0:r0 + chunk], qe[p][r0:r0 + chunk]], axis=0), s[p])
                vn = u[p][r0:r0 + chunk] - ws[0:chunk]
                v_new[p].append(vn)
                q_s[p].append(ws[chunk:2 * chunk])
                g_last = gcum_tm[b][r0 + chunk - 1:r0 + chunk, 8 + hh:9 + hh]
                kd = k[b, hh // rep][r0:r0 + chunk] * jnp.exp(g_last - gc[p][r0:r0 + chunk])
                s[p] = s[p] * jnp.exp(g_last) + _dot_tn(kd, vn)
        for b, hh in heads:
            p = (b, hh)
            s_scr[b, hh] = s[p]
            sout_ref[b, hh] = s[p]
            vn = v_new[p][0] if n_sub == 1 else jnp.concatenate(v_new[p], axis=0)
            qs = q_s[p][0] if n_sub == 1 else jnp.concatenate(q_s[p], axis=0)
            o = qs + _dot(attn[p], vn)
            zh = z_ref[b, :, hh * DN_DV:(hh + 1) * DN_DV].astype(F32)
            n = o * lax.rsqrt(jnp.mean(o * o, -1, keepdims=True) + 1e-6) * ng_ref[...]
            o_ref[b, :, hh * DN_DV:(hh + 1) * DN_DV] = (n * _silu(zh)).astype(BF16)


def _deltanet(proj, small, small_t, conv_w, conv_init, prow, pcol, norm_g, s0, blk_len, chunk):
    nb, t = proj.shape[:2]
    nblk = t // blk_len
    nbb = max(DN_STREAMS, min(nb, DN_STREAM_ROWS // blk_len))
    grid = (nb // nbb, nblk)
    return pl.pallas_call(
        functools.partial(_dn_kernel, blk_len=blk_len, chunk=chunk),
        grid=grid,
        in_specs=[
            pl.BlockSpec((nbb, blk_len, DN_QKW), lambda b, i: (b, i, COL_DQK // DN_QKW)),
            pl.BlockSpec((nbb, blk_len, DN_VW), lambda b, i: (b, i, COL_DV // DN_VW)),
            pl.BlockSpec((nbb, blk_len, DN_VW), lambda b, i: (b, i, COL_DZ // DN_VW)),
            pl.BlockSpec((nbb, blk_len, LANES), lambda b, i: (b, i, 0)),
            pl.BlockSpec((nbb, DN_GATE_W, blk_len), lambda b, i: (b, 0, i)),
            pl.BlockSpec((DN_CONV_W, DN_CONV_CH), lambda b, i: (0, 0)),
            pl.BlockSpec((nbb, 8, DN_CONV_CH), lambda b, i: (b, 0, 0)),
            pl.BlockSpec((8, LANES), lambda b, i: (0, 0)),
            pl.BlockSpec((DN_GATE_W, LANES), lambda b, i: (0, 0)),
            pl.BlockSpec((1, DN_DV), lambda b, i: (0, 0)),
            pl.BlockSpec((nbb, DN_V_HEADS, DN_DK, DN_DV), lambda b, i: (b, 0, 0, 0)),
        ],
        out_specs=[
            pl.BlockSpec((nbb, blk_len, DN_VW), lambda b, i: (b, i, 0)),
            pl.BlockSpec((nbb, DN_V_HEADS, DN_DK, DN_DV), lambda b, i: (b, 0, 0, 0)),
            pl.BlockSpec((nbb, DN_CONV_W - 1, DN_CONV_CH), lambda b, i: (b, 0, 0)),
        ],
        out_shape=[
            jax.ShapeDtypeStruct((nb, t, DN_VW), BF16),
            jax.ShapeDtypeStruct((nb, DN_V_HEADS, DN_DK, DN_DV), F32),
            jax.ShapeDtypeStruct((nb, DN_CONV_W - 1, DN_CONV_CH), F32),
        ],
        scratch_shapes=[pltpu.VMEM((nbb, blk_len + 8, DN_CONV_CH), F32),
                        pltpu.VMEM((nbb, DN_V_HEADS, DN_DK, DN_DV), F32)],
        compiler_params=_params("parallel", "arbitrary"),
        name="deltanet",
    )(proj, proj, proj, small, small_t, conv_w, conv_init, prow, pcol, norm_g, s0)


def _memkv_kernel(m_ref, w_ref, o_ref):
    o_ref[...] = _dot(m_ref[...], w_ref[...])


def _memkv(mem, w_kv):
    n = mem.shape[0]
    tn = 1024
    return pl.pallas_call(
        _memkv_kernel,
        grid=(w_kv.shape[1] // tn,),
        in_specs=[pl.BlockSpec((n, D_MODEL), lambda j: (0, 0)),
                  pl.BlockSpec((D_MODEL, tn), lambda j: (0, j))],
        out_specs=pl.BlockSpec((n, tn), lambda j: (0, j)),
        out_shape=jax.ShapeDtypeStruct((n, w_kv.shape[1]), F32),
        compiler_params=_params("parallel"),
        name="memkv",
    )(mem, w_kv)


def _mix_kernel(h_ref, ret_ref, dn_ref, mk_ref, mv_ref, wxq_ref, wg_ref, wb_ref, wo_ref,
                g1_ref, b1_ref, h1_ref, mem_scr, *, n_batch, tiles_per_batch):
    @pl.when(pl.program_id(0) % tiles_per_batch == 0)
    def _():
        for b in range(n_batch):
            for hh in range(XA_HEADS):
                mem_scr[0, hh, b] = mk_ref[0, b, :, hh, :].astype(BF16)
                mem_scr[1, hh, b] = mv_ref[0, b, :, hh, :].astype(BF16)

    hb = h_ref[...].astype(BF16)
    xq = jnp.dot(hb, wxq_ref[...], preferred_element_type=F32)
    tb = xq.shape[0] // n_batch
    pairs = [(b, hh) for b in range(n_batch) for hh in range(XA_HEADS)]
    s = [_dot_nt(xq[b * tb:(b + 1) * tb, hh * XA_DH:(hh + 1) * XA_DH], mem_scr[0, hh, b]) * (XA_DH ** -0.5)
         for b, hh in pairs]
    p = [jnp.exp(si - jnp.max(si, -1, keepdims=True)) for si in s]
    p = [pi / jnp.sum(pi, -1, keepdims=True) for pi in p]
    xo = [_dot(pi, mem_scr[1, hh, b]) for pi, (b, hh) in zip(p, pairs)]
    xo = [jnp.concatenate(xo[b * XA_HEADS:(b + 1) * XA_HEADS], axis=1) for b in range(n_batch)]
    xo = xo[0] if n_batch == 1 else jnp.concatenate(xo, axis=0)
    mixed = None
    for n, br in enumerate((ret_ref[...], dn_ref[...], xo)):
        gate = _sigmoid(jnp.dot(hb, wg_ref[:, n * D_MODEL:(n + 1) * D_MODEL], preferred_element_type=F32))
        term = gate * _dot(br, wb_ref[n])
        mixed = term if mixed is None else mixed + term
    y = ALPHA * h_ref[...] + _dot(mixed, wo_ref[...])
    h1 = _layernorm(y, g1_ref[...], b1_ref[...])
    h1_ref[...] = h1


def _mix(h, ret_o, dn_o, mem_k, mem_v, w_xq, w_gate, w_branch, w_out, ln_g, ln_b, tm, t):
    n_tok = h.shape[0]
    const2 = lambda i: (0, 0)
    row = lambda i: (i, 0)
    single = pl.Buffered(1)
    if tm <= t:
        n_batch = 1
        tiles_per_batch = t // tm
    else:
        n_batch = tm // t
        tiles_per_batch = 1
    mem_spec = pl.BlockSpec((1, n_batch, N_MEM, XA_HEADS, XA_DH), lambda i: (0, i // tiles_per_batch, 0, 0, 0))
    return pl.pallas_call(
        functools.partial(_mix_kernel, n_batch=n_batch, tiles_per_batch=tiles_per_batch),
        grid=(n_tok // tm,),
        in_specs=[
            pl.BlockSpec((tm, D_MODEL), row),
            pl.BlockSpec((tm, D_MODEL), row),
            pl.BlockSpec((tm, D_MODEL), row),
            mem_spec,
            mem_spec,
            pl.BlockSpec((D_MODEL, D_MODEL), const2, pipeline_mode=single),
            pl.BlockSpec((D_MODEL, 3 * D_MODEL), const2, pipeline_mode=single),
            pl.BlockSpec((3, D_MODEL, D_MODEL), lambda i: (0, 0, 0), pipeline_mode=single),
            pl.BlockSpec((D_MODEL, D_MODEL), const2, pipeline_mode=single),
            pl.BlockSpec((1, D_MODEL), const2),
            pl.BlockSpec((1, D_MODEL), const2),
        ],
        out_specs=pl.BlockSpec((tm, D_MODEL), row),
        out_shape=jax.ShapeDtypeStruct((n_tok, D_MODEL), F32),
        scratch_shapes=[pltpu.VMEM((2, XA_HEADS, n_batch, N_MEM, XA_DH), BF16)],
        compiler_params=_params("arbitrary"),
        name="mix",
    )(h, ret_o, dn_o, mem_k, mem_v, w_xq, w_gate, w_branch, w_out, ln_g, ln_b)


def _route_t(x, wrt, brt):
    xh = x.astype(BF16)
    xl = (x - xh.astype(F32)).astype(BF16)
    wh = wrt.astype(BF16)
    wl = (wrt - wh.astype(F32)).astype(BF16)
    nt = lambda a, b: lax.dot_general(a, b, (((1,), (1,)), ((), ())), preferred_element_type=F32)
    hi = nt(jnp.concatenate([wh, wl], axis=0), xh)
    logits = hi[0:MOE_ROUTE_ROWS] + (hi[MOE_ROUTE_ROWS:2 * MOE_ROUTE_ROWS] + nt(wh, xl)) + brt
    row_i = lax.broadcasted_iota(jnp.int32, logits.shape, 0)
    row = row_i.astype(F32)
    far = jnp.float32(LANES)
    neg = jnp.float32(-3.0e38)
    is_c = jnp.logical_and(row_i >= MOE_NE, row_i < MOE_NE + MOE_GROUPS)
    cl = jnp.where(is_c, logits, neg)
    cmax = jnp.max(cl, 0, keepdims=True)
    denom = jnp.sum(jnp.where(is_c, jnp.exp(jnp.where(is_c, logits - cmax, 0.0)), 0.0), 0, keepdims=True)
    p_grp = 1.0 / denom
    grp = jnp.min(jnp.where(jnp.logical_and(is_c, cl == cmax), row - MOE_NE, far), 0, keepdims=True)
    in_grp = jnp.logical_and(row_i < MOE_NE, _idiv(row_i, MOE_EXPERTS).astype(F32) == grp)
    fl = jnp.where(in_grp, logits, neg)
    v1 = jnp.max(fl, 0, keepdims=True)
    i1 = jnp.min(jnp.where(jnp.logical_and(in_grp, fl == v1), row, far), 0, keepdims=True)
    rest = jnp.logical_and(in_grp, row != i1)
    fl2 = jnp.where(rest, logits, neg)
    v2 = jnp.max(fl2, 0, keepdims=True)
    i2 = jnp.min(jnp.where(jnp.logical_and(rest, fl2 == v2), row, far), 0, keepdims=True)
    e2 = jnp.exp(v2 - v1)
    w1 = p_grp / (1.0 + e2)
    w2 = p_grp * e2 / (1.0 + e2)
    comb_t = jnp.where(row == i1, w1, 0.0) + jnp.where(row == i2, w2, 0.0)
    tm = x.shape[0]
    last = jnp.where(lax.broadcasted_iota(jnp.int32, (8, tm), 0) == 7, grp, 0.0)
    return jnp.concatenate([comb_t, jnp.zeros((LANES - MOE_ROUTE_ROWS - 8, tm), F32), last], axis=0)


def _route_kernel(h1_ref, wrt_ref, brt_ref, comb_ref, cnt_ref):
    comb = _route_t(h1_ref[...], wrt_ref[...], brt_ref[...]).T
    comb_ref[...] = comb
    lane = lax.broadcasted_iota(jnp.int32, comb.shape, 1).astype(F32)
    onehot = jnp.where(lane == comb[:, LANES - 1:LANES], 1.0, 0.0)
    cnt_ref[0] = jnp.broadcast_to(jnp.sum(onehot, 0, keepdims=True), (8, LANES))


def _route_call(h1, w_route, b_route, tm):
    n_tok = h1.shape[0]
    return pl.pallas_call(
        _route_kernel,
        grid=(n_tok // tm,),
        in_specs=[pl.BlockSpec((tm, D_MODEL), lambda i: (i, 0)),
                  pl.BlockSpec((MOE_ROUTE_ROWS, D_MODEL), lambda i: (0, 0)),
                  pl.BlockSpec((MOE_ROUTE_ROWS, 1), lambda i: (0, 0))],
        out_specs=[pl.BlockSpec((tm, LANES), lambda i: (i, 0)),
                   pl.BlockSpec((1, 8, LANES), lambda i: (i, 0, 0))],
        out_shape=[jax.ShapeDtypeStruct((n_tok, LANES), F32),
                   jax.ShapeDtypeStruct((n_tok // tm, 8, LANES), F32)],
        compiler_params=_params("parallel"),
        name="route",
    )(h1, w_route, b_route)


MOE_ALIGN = 16
MOE_EPS = 4


def _moe_sizes(tm):
    base = tm // MOE_GROUPS
    fine = {min(tm, base + MOE_ALIGN * i) for i in range(5)}
    return tuple(sorted(fine | {tm // 2, tm}))


def _moe_kernel(offs_ref, cls_ref, h1_ref, comb_ref, tri_ref, wg_ref, wu_ref, wd_ref,
                g2_ref, b2_ref, y_ref, pt_scr, xs_scr, cs_scr, ys_scr, *, tm, nr1, sizes):
    i = pl.program_id(0)
    e = pl.program_id(1)
    nrt = xs_scr.shape[0]

    @pl.when(e == 0)
    def _():
        comb = comb_ref[...]
        lane_i = lax.broadcasted_iota(jnp.int32, (tm, LANES), 1)
        grp = comb[:, LANES - 1:LANES]
        is_g = lane_i.astype(F32) == grp
        prefix = jnp.dot(tri_ref[...], jnp.where(is_g, 1.0, 0.0).astype(BF16), preferred_element_type=F32)
        rank = jnp.sum(jnp.where(is_g, prefix, 0.0), -1, keepdims=True)
        offv = jnp.zeros((tm, 1), F32)
        for g in range(MOE_GROUPS):
            offv = jnp.where(grp == g, offs_ref[i * MOE_GROUPS + g].astype(F32), offv)
        pos = offv + rank
        col = lax.broadcasted_iota(jnp.int32, (tm, nr1), 1).astype(F32)
        pt = jnp.where(col == pos, 1.0, 0.0).astype(BF16)
        pt_scr[...] = pt
        combw = jnp.where(lane_i < MOE_NE, comb, 0.0)
        c_hi = combw.astype(BF16)
        c_lo = (combw - c_hi.astype(F32)).astype(BF16)
        xs_scr[0:nr1, :] = _dot_tn(pt, h1_ref[...]).astype(BF16)
        c_pair = _dot_tn(pt, jnp.concatenate([c_hi, c_lo], axis=1))
        cs_scr[0:nr1, :] = c_pair[:, 0:LANES] + c_pair[:, LANES:2 * LANES]
        xs_scr[nr1:nrt, :] = jnp.zeros((nrt - nr1, D_MODEL), BF16)
        cs_scr[nr1:nrt, :] = jnp.zeros((nrt - nr1, LANES), F32)
        ys_scr[tm:nr1, :] = jnp.zeros((nr1 - tm, D_MODEL), F32)

    g = e // (MOE_EXPERTS // MOE_EPS)
    off = pl.multiple_of(offs_ref[i * MOE_GROUPS + g], MOE_ALIGN)
    cls = cls_ref[i * MOE_GROUPS + g]
    first = e % (MOE_EXPERTS // MOE_EPS) == 0
    for ci, m in enumerate(sizes):
        @pl.when(cls == ci)
        def _(m=m):
            rows = pl.ds(off, m)
            xb = xs_scr[rows, :]
            cb = cs_scr[rows, :]
            lane_i = lax.broadcasted_iota(jnp.int32, (m, LANES), 1)
            sub = range(MOE_EPS)
            colw = [jnp.sum(jnp.where(lane_i == e * MOE_EPS + j, cb, 0.0), -1, keepdims=True) for j in sub]
            gate = [jnp.dot(xb, wg_ref[j], preferred_element_type=F32) for j in sub]
            up = [jnp.dot(xb, wu_ref[j], preferred_element_type=F32) for j in sub]
            hg = [(_silu(gate[j]) * up[j] * colw[j]).astype(BF16) for j in sub]
            wd = wd_ref[...].reshape(MOE_EPS * MOE_DFF, D_MODEL)
            res = jnp.dot(jnp.concatenate(hg, axis=1), wd, preferred_element_type=F32)

            @pl.when(first)
            def _():
                ys_scr[rows, :] = res

            @pl.when(jnp.logical_not(first))
            def _():
                ys_scr[rows, :] += res

    @pl.when(e == MOE_NE // MOE_EPS - 1)
    def _():
        moe = jnp.dot(pt_scr[...], ys_scr[0:nr1, :].astype(BF16), preferred_element_type=F32)
        y_ref[...] = _layernorm(ALPHA * h1_ref[...] + moe, g2_ref[...], b2_ref[...])


def _moe(h1, comb, cnt, w_gate, w_up, w_down, ln_g, ln_b, tm):
    n_tok = h1.shape[0]
    sizes = _moe_sizes(tm)
    nr1 = -(-(tm + MOE_GROUPS * MOE_ALIGN) // LANES) * LANES
    nrt = nr1 + max(b - a for a, b in zip((0,) + sizes, sizes))
    c = cnt[:, 0, :MOE_GROUPS].astype(jnp.int32)
    seg = (c + MOE_ALIGN - 1) // MOE_ALIGN * MOE_ALIGN
    offs = (jnp.cumsum(seg, axis=1) - seg).reshape(-1)
    cls = sum((c > s).astype(jnp.int32) for s in sizes[:-1]).reshape(-1)
    tri = jnp.asarray(np.tril(np.ones((tm, tm), np.float32), -1), BF16)
    grid_spec = pltpu.PrefetchScalarGridSpec(
        num_scalar_prefetch=2,
        grid=(n_tok // tm, MOE_NE // MOE_EPS),
        in_specs=[
            pl.BlockSpec((tm, D_MODEL), lambda i, e, *_: (i, 0)),
            pl.BlockSpec((tm, LANES), lambda i, e, *_: (i, 0)),
            pl.BlockSpec((tm, tm), lambda i, e, *_: (0, 0), pipeline_mode=pl.Buffered(1)),
            pl.BlockSpec((MOE_EPS, D_MODEL, MOE_DFF), lambda i, e, *_: (e, 0, 0)),
            pl.BlockSpec((MOE_EPS, D_MODEL, MOE_DFF), lambda i, e, *_: (e, 0, 0)),
            pl.BlockSpec((MOE_EPS, MOE_DFF, D_MODEL), lambda i, e, *_: (e, 0, 0)),
            pl.BlockSpec((1, D_MODEL), lambda i, e, *_: (0, 0)),
            pl.BlockSpec((1, D_MODEL), lambda i, e, *_: (0, 0)),
        ],
        out_specs=pl.BlockSpec((tm, D_MODEL), lambda i, e, *_: (i, 0)),
        scratch_shapes=[pltpu.VMEM((tm, nr1), BF16), pltpu.VMEM((nrt, D_MODEL), BF16),
                        pltpu.VMEM((nrt, LANES), F32), pltpu.VMEM((nrt, D_MODEL), F32)],
    )
    return pl.pallas_call(
        functools.partial(_moe_kernel, tm=tm, nr1=nr1, sizes=sizes),
        grid_spec=grid_spec,
        out_shape=jax.ShapeDtypeStruct((n_tok, D_MODEL), F32),
        compiler_params=_params("parallel", "arbitrary"),
        name="moe",
    )(offs, cls, h1, comb, tri, w_gate, w_up, w_down, ln_g, ln_b)


def _rope_tables(start, t):
    half = RET_DK // 2
    inv = 1.0 / (ROPE_BASE ** (np.arange(half, dtype=np.float64) / half))
    ang = (start + np.arange(t, dtype=np.float64))[:, None] * inv[None, :]
    cos = np.cos(ang)
    sin = np.sin(ang)
    return (jnp.asarray(np.concatenate([cos, cos], -1), F32),
            jnp.asarray(np.concatenate([-sin, sin], -1), F32))


def _group(x, pos0, mem_k, mem_v, ret_s0, dn_s0, conv_buf, wts, *, tm, tm_mix, tm_moe, ret_chunk, dn_blk,
           dn_chunk):
    nb, t, d = x.shape
    n_tok = nb * t
    h, small, small_t, proj = _in_proj(x.reshape(n_tok, d), wts["ln_in_g"], wts["ln_in_b"],
                                           wts["w_in"], wts["w_small"], wts["w_small_t"], tm)
    proj3 = proj.reshape(nb, t, PROJ_W)
    cos2, sin2 = _rope_tables(pos0, t)
    ret_o, ret_s = _retention(proj3, cos2, sin2, ret_s0, wts["ret_gn_g"], wts["ret_gn_b"], ret_chunk)
    small_t3 = small_t.reshape(DN_GATE_W, nb, t).transpose(1, 0, 2)
    conv_init = jnp.concatenate([jnp.zeros((nb, 8 - (DN_CONV_W - 1), DN_CONV_CH), F32), conv_buf], axis=1)
    dn_o, dn_s, conv_new = _deltanet(proj3, small.reshape(nb, t, LANES), small_t3, wts["dn_conv"], conv_init,
                                     wts["dn_prow"],
                                     wts["dn_pcol"], wts["dn_norm_g"], dn_s0, dn_blk, dn_chunk)
    h1 = _mix(h, ret_o.reshape(n_tok, d), dn_o.reshape(n_tok, d), mem_k, mem_v,
                   wts["w_xq"], wts["w_gate"], wts["w_branch"], wts["w_out"], wts["ln1_g"], wts["ln1_b"],
                   tm_mix, t)
    comb, cnt = _route_call(h1, wts["w_route"], wts["b_route"], tm_moe)
    y = _moe(h1, comb, cnt, wts["moe_w_gate"], wts["moe_w_up"], wts["moe_w_down"],
             wts["ln2_g"], wts["ln2_b"], tm_moe)
    return y.reshape(nb, t, d), ret_s, dn_s, conv_new


def kernel(x_prompt, x_sample, mem_prompt, state_ret, state_dn, state_dn_conv, cache_mem_k, cache_mem_v,
           ln_in_g, ln_in_b, w_in, ret_gn_g, ret_gn_b, dn_conv, dn_A_log, dn_dt_bias, dn_norm_g,
           w_mem_kv, w_branch, w_out, ln1_g, ln1_b, moe_w_coarse, moe_b_coarse, moe_w_fine, moe_b_fine,
           moe_w_gate, moe_w_up, moe_w_down, ln2_g, ln2_b):
    bp, tp, d = x_prompt.shape
    bs, ts, _ = x_sample.shape
    l = 0
    wi = w_in[l].astype(BF16)
    dba = wi[:, PROJ_W:COL_XQ]
    xq, gates = wi[:, COL_XQ:COL_GATE], wi[:, COL_GATE:N_IN]
    wts = {
        "ln_in_g": ln_in_g.reshape(1, d), "ln_in_b": ln_in_b.reshape(1, d),
        "w_in": wi,
        "w_small": jnp.pad(dba, ((0, 0), (0, LANES - DN_GATE_W))),
        "w_small_t": dba.T,
        "ret_gn_g": ret_gn_g[l].reshape(1, -1), "ret_gn_b": ret_gn_b[l].reshape(1, -1),
        "dn_conv": dn_conv[l],
        "dn_prow": jnp.zeros((8, LANES), F32).at[0, 8:16].set(-jnp.exp(dn_A_log[l])).at[1, 8:16].set(dn_dt_bias[l]),
        "dn_pcol": jnp.zeros((16, LANES), F32).at[8:16, 0].set(-jnp.exp(dn_A_log[l])).at[8:16, 1].set(dn_dt_bias[l]),
        "dn_norm_g": dn_norm_g[l].reshape(1, -1),
        "w_xq": xq, "w_gate": gates,
        "w_branch": w_branch[l].astype(BF16), "w_out": w_out[l].astype(BF16),
        "ln1_g": ln1_g[l].reshape(1, d), "ln1_b": ln1_b[l].reshape(1, d),
        "w_route": jnp.pad(jnp.concatenate([moe_w_fine[l], moe_w_coarse[l]], axis=1).T,
                           ((0, MOE_ROUTE_ROWS - MOE_NE - MOE_GROUPS), (0, 0))),
        "b_route": jnp.pad(jnp.concatenate([moe_b_fine[l], moe_b_coarse[l]]),
                           (0, MOE_ROUTE_ROWS - MOE_NE - MOE_GROUPS)).reshape(MOE_ROUTE_ROWS, 1),
        "moe_w_gate": moe_w_gate[l].reshape(MOE_NE, d, MOE_DFF).astype(BF16),
        "moe_w_up": moe_w_up[l].reshape(MOE_NE, d, MOE_DFF).astype(BF16),
        "moe_w_down": moe_w_down[l].reshape(MOE_NE, MOE_DFF, d).astype(BF16),
        "ln2_g": ln2_g[l].reshape(1, d), "ln2_b": ln2_b[l].reshape(1, d),
    }

    mkv = _memkv(mem_prompt.reshape(bp * N_MEM, d), w_mem_kv[l].astype(BF16))
    mk = mkv[:, :XA_HEADS * XA_DH].reshape(1, bp, N_MEM, XA_HEADS, XA_DH)
    mv = mkv[:, XA_HEADS * XA_DH:].reshape(1, bp, N_MEM, XA_HEADS, XA_DH)
    yp, rs_p, ds_p, cb_p = _group(
        x_prompt, 0, mk, mv,
        jnp.zeros((bp, RET_HEADS, RET_DK, RET_DV), F32),
        jnp.zeros((bp, DN_V_HEADS, DN_DK, DN_DV), F32),
        jnp.zeros((bp, DN_CONV_W - 1, DN_CONV_CH), F32),
        wts, tm=min(1024, tp), tm_mix=min(512, tp), tm_moe=min(1024, tp), ret_chunk=min(256, tp),
        dn_blk=min(128, tp),
        dn_chunk=min(DN_CHUNK, tp))
    ys, rs_s, ds_s, cb_s = _group(
        x_sample, PAST_LEN, cache_mem_k[l:l + 1], cache_mem_v[l:l + 1],
        state_ret[l], state_dn[l], state_dn_conv[l],
        wts, tm=bs * ts, tm_mix=bs * ts, tm_moe=bs * ts, ret_chunk=ts, dn_blk=ts, dn_chunk=min(DN_CHUNK, ts))
    return (yp, ys, rs_p[None], rs_s[None], ds_p[None], ds_s[None], cb_p[None], cb_s[None], mk, mv)
```

```python
import functools
import math

import numpy as np

import jax
import jax.numpy as jnp
from jax import lax
from jax.experimental import pallas as pl
from jax.experimental.pallas import tpu as pltpu

F32 = jnp.float32
BF16 = jnp.bfloat16

D_MODEL = 1024
PAST_LEN = 1024
RET_HEADS = 4
RET_DK = 128
RET_DV = 256
ROPE_BASE = 10000.0
DN_QK_HEADS = 4
DN_V_HEADS = 8
DN_DK = 128
DN_DV = 128
DN_QK = DN_QK_HEADS * DN_DK
DN_CONV_W = 4
DN_CONV_CH = 2048
DN_CHUNK = 64
DN_SUB = 16
DN_HEAD_GROUP = 8
DN_STREAMS = 2
DN_STREAM_ROWS = 256
XA_HEADS = 4
XA_DH = 256
N_MEM = 256
MOE_GROUPS = 4
MOE_EXPERTS = 8
MOE_NE = MOE_GROUPS * MOE_EXPERTS
MOE_DFF = 256
MOE_ROUTE_ROWS = 40
DEPTH = 1
ALPHA = (2.0 * DEPTH) ** 0.25
LANES = 128
RET_QKW = RET_HEADS * RET_DK
RET_VW = RET_HEADS * RET_DV
DN_QKW = 2 * DN_QK
DN_VW = DN_V_HEADS * DN_DV
COL_RQ = 0
COL_RK = COL_RQ + RET_QKW
COL_RV = COL_RK + RET_QKW
COL_RG = COL_RV + RET_VW
COL_DQK = COL_RG + RET_VW
COL_DV = COL_DQK + DN_QKW
COL_DZ = COL_DV + DN_VW
PROJ_W = COL_DZ + DN_VW
DN_GATE_W = 2 * DN_V_HEADS
COL_XQ = PROJ_W + DN_GATE_W
COL_GATE = COL_XQ + XA_HEADS * XA_DH
N_IN = COL_GATE + 3 * D_MODEL
VMEM_LIMIT = 56 * 1024 * 1024


def _dot(a, b):
    return jnp.dot(a.astype(BF16), b.astype(BF16), preferred_element_type=F32)


def _dot_nt(a, b):
    return lax.dot_general(a.astype(BF16), b.astype(BF16), (((1,), (1,)), ((), ())),
                           preferred_element_type=F32)


def _dot_tn(a, b):
    return lax.dot_general(a.astype(BF16), b.astype(BF16), (((0,), (0,)), ((), ())),
                           preferred_element_type=F32)


def _dot_exact(a, b):
    return jnp.dot(a, b, preferred_element_type=F32, precision=lax.Precision.HIGHEST)


def _layernorm(x, g, b, eps=1e-5):
    mu = jnp.mean(x, -1, keepdims=True)
    xc = x - mu
    var = jnp.mean(xc * xc, -1, keepdims=True)
    return xc * lax.rsqrt(var + eps) * g + b


def _silu(x):
    return x * (1.0 / (1.0 + jnp.exp(-x)))


def _sigmoid(x):
    return 1.0 / (1.0 + jnp.exp(-x))


def _softplus(x):
    return jnp.maximum(x, 0.0) + jnp.log(1.0 + jnp.exp(-jnp.abs(x)))


def _idiv(x, n):
    return jnp.right_shift(x, int(math.log2(n)))


def _params(*sem):
    return pltpu.CompilerParams(dimension_semantics=sem, vmem_limit_bytes=VMEM_LIMIT)


def _in_proj_kernel(x_ref, g_ref, b_ref, w_ref, ws_ref, wst_ref,
                    h_ref, small_ref, smallt_ref, proj_ref, hs_ref):
    @pl.when(pl.program_id(1) == 0)
    def _():
        h = _layernorm(x_ref[...], g_ref[...], b_ref[...])
        h_ref[...] = h
        hb = h.astype(BF16)
        hs_ref[...] = hb
        small_ref[...] = jnp.dot(hb, ws_ref[...], preferred_element_type=F32)
        smallt_ref[...] = lax.dot_general(wst_ref[...], hb, (((1,), (1,)), ((), ())),
                                          preferred_element_type=F32)

    proj_ref[...] = jnp.dot(hs_ref[...], w_ref[...], preferred_element_type=F32).astype(BF16)


def _in_proj(x, ln_g, ln_b, w_main, w_small, w_small_t, tm, tn=PROJ_W // 2):
    n_tok = x.shape[0]
    n_col = PROJ_W // tn
    grid = (n_tok // tm, n_col)
    col = lambda i, n: jnp.where(i % 2 == 0, n, n_col - 1 - n)
    return pl.pallas_call(
        _in_proj_kernel,
        grid=grid,
        in_specs=[
            pl.BlockSpec((tm, D_MODEL), lambda i, n: (i, 0)),
            pl.BlockSpec((1, D_MODEL), lambda i, n: (0, 0)),
            pl.BlockSpec((1, D_MODEL), lambda i, n: (0, 0)),
            pl.BlockSpec((D_MODEL, tn), lambda i, n: (0, col(i, n))),
            pl.BlockSpec((D_MODEL, LANES), lambda i, n: (0, 0)),
            pl.BlockSpec((DN_GATE_W, D_MODEL), lambda i, n: (0, 0)),
        ],
        out_specs=[
            pl.BlockSpec((tm, D_MODEL), lambda i, n: (i, 0)),
            pl.BlockSpec((tm, LANES), lambda i, n: (i, 0)),
            pl.BlockSpec((DN_GATE_W, tm), lambda i, n: (0, i)),
            pl.BlockSpec((tm, tn), lambda i, n: (i, col(i, n))),
        ],
        out_shape=[
            jax.ShapeDtypeStruct((n_tok, D_MODEL), F32),
            jax.ShapeDtypeStruct((n_tok, LANES), F32),
            jax.ShapeDtypeStruct((DN_GATE_W, n_tok), F32),
            jax.ShapeDtypeStruct((n_tok, PROJ_W), BF16),
        ],
        scratch_shapes=[pltpu.VMEM((tm, D_MODEL), BF16)],
        compiler_params=_params("parallel", "arbitrary"),
        name="in_proj",
    )(x, ln_g, ln_b, w_main, w_small, w_small_t)


def _ret_kernel(q_ref, k_ref, v_ref, rg_ref, cos_ref, sin_ref, s0_ref, gng_ref, gnb_ref,
                o_ref, sout_ref, s_scr, dec_scr, *, chunk):
    heads = range(RET_HEADS)
    lg = [math.log(1.0 - 2.0 ** (-5.0 - h)) for h in heads]

    @pl.when(pl.program_id(1) == 0)
    def _():
        s_scr[...] = s0_ref[0]
        ri = lax.broadcasted_iota(jnp.int32, (chunk, chunk), 0)
        ci = lax.broadcasted_iota(jnp.int32, (chunk, chunk), 1)
        causal = ri >= ci
        diff = jnp.where(causal, (ri - ci).astype(F32), 0.0)
        for h in heads:
            dec_scr[h] = jnp.where(causal, jnp.exp(diff * lg[h]), 0.0)

    cos = cos_ref[...]
    sin = sin_ref[...]
    idx = lax.broadcasted_iota(jnp.int32, (chunk, 1), 0).astype(F32)
    q = [q_ref[0, :, h * RET_DK:(h + 1) * RET_DK].astype(F32) for h in heads]
    k = [k_ref[0, :, h * RET_DK:(h + 1) * RET_DK].astype(F32) for h in heads]
    v = [v_ref[0, :, h * RET_DV:(h + 1) * RET_DV] for h in heads]
    qr = [q[h] * cos + pltpu.roll(q[h], RET_DK // 2, 1) * sin for h in heads]
    kr = [(k[h] * cos + pltpu.roll(k[h], RET_DK // 2, 1) * sin) * (RET_DK ** -0.5) for h in heads]
    scores = [_dot_nt(qr[h], kr[h]) * dec_scr[h] for h in heads]
    s = [s_scr[h] for h in heads]
    cross = [_dot(qr[h], s[h]) * jnp.exp((idx + 1.0) * lg[h]) for h in heads]
    o = [_dot(scores[h], v[h]) + cross[h] for h in heads]
    kv = [_dot_tn(kr[h] * jnp.exp((chunk - 1.0 - idx) * lg[h]), v[h]) for h in heads]
    for h in heads:
        s_new = s[h] * math.exp(chunk * lg[h]) + kv[h]
        s_scr[h] = s_new
        sout_ref[0, h] = s_new
        sl = slice(h * RET_DV, (h + 1) * RET_DV)
        gated = _layernorm(o[h], gng_ref[:, sl], gnb_ref[:, sl]) * _silu(rg_ref[0, :, sl].astype(F32))
        o_ref[0, :, sl] = gated.astype(BF16)


def _retention(proj, cos2, sin2, s0, gn_g, gn_b, chunk):
    nb, t = proj.shape[:2]
    grid = (nb, t // chunk)
    qk_w = RET_QKW
    v_w = RET_VW
    return pl.pallas_call(
        functools.partial(_ret_kernel, chunk=chunk),
        grid=grid,
        in_specs=[
            pl.BlockSpec((1, chunk, qk_w), lambda b, c: (b, c, COL_RQ // qk_w)),
            pl.BlockSpec((1, chunk, qk_w), lambda b, c: (b, c, COL_RK // qk_w)),
            pl.BlockSpec((1, chunk, v_w), lambda b, c: (b, c, COL_RV // v_w)),
            pl.BlockSpec((1, chunk, v_w), lambda b, c: (b, c, COL_RG // v_w)),
            pl.BlockSpec((chunk, RET_DK), lambda b, c: (c, 0)),
            pl.BlockSpec((chunk, RET_DK), lambda b, c: (c, 0)),
            pl.BlockSpec((1, RET_HEADS, RET_DK, RET_DV), lambda b, c: (b, 0, 0, 0)),
            pl.BlockSpec((1, v_w), lambda b, c: (0, 0)),
            pl.BlockSpec((1, v_w), lambda b, c: (0, 0)),
        ],
        out_specs=[
            pl.BlockSpec((1, chunk, v_w), lambda b, c: (b, c, 0)),
            pl.BlockSpec((1, RET_HEADS, RET_DK, RET_DV), lambda b, c: (b, 0, 0, 0)),
        ],
        out_shape=[
            jax.ShapeDtypeStruct((nb, t, v_w), BF16),
            jax.ShapeDtypeStruct((nb, RET_HEADS, RET_DK, RET_DV), F32),
        ],
        scratch_shapes=[pltpu.VMEM((RET_HEADS, RET_DK, RET_DV), F32),
                        pltpu.VMEM((RET_HEADS, chunk, chunk), F32)],
        compiler_params=_params("parallel", "arbitrary"),
        name="retention",
    )(proj, proj, proj, proj, cos2, sin2, s0, gn_g, gn_b)


def _dn_kernel(xqk_ref, xv_ref, z_ref, sm_ref, smt_ref, cw_ref, cinit_ref, prow_ref, pcol_ref, ng_ref, s0_ref,
               o_ref, sout_ref, cout_ref, xext, s_scr, *, blk_len, chunk):
    L = blk_len
    nbb = xqk_ref.shape[0]
    blk = pl.program_id(1)

    @pl.when(blk == 0)
    def _():
        xext[:, 0:8, :] = cinit_ref[...]
        s_scr[...] = s0_ref[...]

    ri = lax.broadcasted_iota(jnp.int32, (L, L), 0)
    ci = lax.broadcasted_iota(jnp.int32, (L, L), 1)
    same_chunk = _idiv(ri, chunk) == _idiv(ci, chunk)
    lower = jnp.logical_and(ri >= ci, same_chunk)
    strict = jnp.logical_and(ri > ci, same_chunk)
    same_sub = _idiv(ri, DN_SUB) == _idiv(ci, DN_SUB)
    tri = jnp.where(lower, 1.0, 0.0).astype(F32)
    tri_t = jnp.where(jnp.logical_and(ri <= ci, same_chunk), 1.0, 0.0).astype(F32)
    prow = prow_ref[...]
    pcol = pcol_ref[...]

    conv, beta_tm, gcum_tm, gcum_hm, egc_tm = [], [], [], [], []
    for b in range(nbb):
        x = jnp.concatenate([xqk_ref[b], xv_ref[b]], axis=1).astype(F32)
        xext[b, 8:8 + L, :] = x
        cv = x * cw_ref[3:4, :]
        for w in range(DN_CONV_W - 1):
            cv = cv + xext[b, 5 + w:5 + w + L, :] * cw_ref[w:w + 1, :]
        conv.append(_silu(cv))
        xext[b, 0:8, :] = x[L - 8:L, :]
        cout_ref[b] = x[L - (DN_CONV_W - 1):L, :]
        sm = sm_ref[b]
        beta_tm.append(_sigmoid(sm))
        g_tm = prow[0:1, :] * _softplus(sm + prow[1:2, :])
        g_hm = pcol[8:16, 0:1] * _softplus(smt_ref[b][8:16, :] + pcol[8:16, 1:2])
        gcum_tm.append(_dot_exact(tri, g_tm))
        gcum_hm.append(_dot_exact(g_hm, tri_t))
        egc_tm.append(jnp.exp(gcum_tm[b]))

    n_sub = L // chunk
    rep = DN_V_HEADS // DN_QK_HEADS
    bdot = lambda x, y: jnp.dot(x, y, preferred_element_type=F32)
    for h0 in range(0, DN_V_HEADS, DN_HEAD_GROUP):
        heads = [(b, hh) for b in range(nbb) for hh in range(h0, h0 + DN_HEAD_GROUP)]
        q, k, kk, qk = {}, {}, {}, {}
        for b in range(nbb):
            for j in range(h0 // rep, (h0 + DN_HEAD_GROUP) // rep):
                qj = conv[b][:, j * DN_DK:(j + 1) * DN_DK]
                kj = conv[b][:, DN_QK + j * DN_DK:DN_QK + (j + 1) * DN_DK]
                qj = qj * lax.rsqrt(jnp.sum(qj * qj, -1, keepdims=True) + 1e-6) * (DN_DK ** -0.5)
                kj = kj * lax.rsqrt(jnp.sum(kj * kj, -1, keepdims=True) + 1e-6)
                kq = _dot_nt(jnp.concatenate([kj, qj], axis=0), kj)
                q[b, j], k[b, j], kk[b, j], qk[b, j] = qj, kj, kq[0:L], kq[L:2 * L]
        gc = {(b, hh): gcum_tm[b][:, 8 + hh:9 + hh] for b, hh in heads}
        beta = {(b, hh): beta_tm[b][:, hh:hh + 1] for b, hh in heads}
        eg = {(b, hh): egc_tm[b][:, 8 + hh:9 + hh] for b, hh in heads}
        rel = {(b, hh): jnp.where(lower, jnp.exp(jnp.where(lower, gc[b, hh] - gcum_hm[b][hh:hh + 1, :], 0.0)),
                                  0.0) for b, hh in heads}
        a = {(b, hh): jnp.where(strict, beta[b, hh] * kk[b, hh // rep] * rel[b, hh], 0.0) for b, hh in heads}
        attn = {(b, hh): qk[b, hh // rep] * rel[b, hh] for b, hh in heads}
        pd = {hh: jnp.where(same_sub, -a[hh], 0.0) for hh in heads}
        e = {hh: a[hh] + pd[hh] for hh in heads}
        eb = {hh: e[hh].astype(BF16) for hh in heads}
        xacc = pd
        xaccb = {hh: pd[hh].astype(BF16) for hh in heads}
        pwb = xaccb
        for _ in range(3):
            pw = {hh: bdot(pwb[hh], pwb[hh]) for hh in heads}
            pwb = {hh: pw[hh].astype(BF16) for hh in heads}
            xacc = {hh: xacc[hh] + pw[hh] + bdot(xaccb[hh], pwb[hh]) for hh in heads}
            xaccb = {hh: xacc[hh].astype(BF16) for hh in heads}
        f = {hh: e[hh] + bdot(xaccb[hh], eb[hh]) for hh in heads}
        fb = {hh: f[hh].astype(BF16) for hh in heads}
        f2 = {hh: bdot(fb[hh], fb[hh]) for hh in heads}
        y = {hh: f2[hh] - f[hh] - bdot(fb[hh], f2[hh].astype(BF16)) for hh in heads}
        rinv = {hh: xacc[hh] + y[hh] + bdot(y[hh].astype(BF16), xaccb[hh]) for hh in heads}
        sol = {}
        for b, hh in heads:
            vh = conv[b][:, 2 * DN_QK + hh * DN_DV:2 * DN_QK + (hh + 1) * DN_DV]
            rhs = jnp.concatenate([vh * beta[b, hh], k[b, hh // rep] * (beta[b, hh] * eg[b, hh])], axis=1)
            sol[b, hh] = rhs + _dot(rinv[b, hh], rhs)
        u = {p: sol[p][:, 0:DN_DV] for p in heads}
        wmat = {p: sol[p][:, DN_DV:DN_DV + DN_DK] for p in heads}
        qe = {(b, hh): q[b, hh // rep] * eg[b, hh] for b, hh in heads}
        s = {(b, hh): s_scr[b, hh] for b, hh in heads}
        v_new = {p: [] for p in heads}
        q_s = {p: [] for p in heads}
        for sc in range(n_sub):
            r0 = sc * chunk
            for b, hh in heads:
                p = (b, hh)
                ws = _dot(jnp.concatenate([wmat[p][r0:r0 + chunk], qe[p][r0:r0 + chunk]], axis=0), s[p])
                vn = u[p][r0:r0 + chunk] - ws[0:chunk]
                v_new[p].append(vn)
                q_s[p].append(ws[chunk:2 * chunk])
                g_last = gcum_tm[b][r0 + chunk - 1:r0 + chunk, 8 + hh:9 + hh]
                kd = k[b, hh // rep][r0:r0 + chunk] * jnp.exp(g_last - gc[p][r0:r0 + chunk])
                s[p] = s[p] * jnp.exp(g_last) + _dot_tn(kd, vn)
        for b, hh in heads:
            p = (b, hh)
            s_scr[b, hh] = s[p]
            sout_ref[b, hh] = s[p]
            vn = v_new[p][0] if n_sub == 1 else jnp.concatenate(v_new[p], axis=0)
            qs = q_s[p][0] if n_sub == 1 else jnp.concatenate(q_s[p], axis=0)
            o = qs + _dot(attn[p], vn)
            zh = z_ref[b, :, hh * DN_DV:(hh + 1) * DN_DV].astype(F32)
            n = o * lax.rsqrt(jnp.mean(o * o, -1, keepdims=True) + 1e-6) * ng_ref[...]
            o_ref[b, :, hh * DN_DV:(hh + 1) * DN_DV] = (n * _silu(zh)).astype(BF16)


def _deltanet(proj, small, small_t, conv_w, conv_init, prow, pcol, norm_g, s0, blk_len, chunk):
    nb, t = proj.shape[:2]
    nblk = t // blk_len
    nbb = max(DN_STREAMS, min(nb, DN_STREAM_ROWS // blk_len))
    grid = (nb // nbb, nblk)
    return pl.pallas_call(
        functools.partial(_dn_kernel, blk_len=blk_len, chunk=chunk),
        grid=grid,
        in_specs=[
            pl.BlockSpec((nbb, blk_len, DN_QKW), lambda b, i: (b, i, COL_DQK // DN_QKW)),
            pl.BlockSpec((nbb, blk_len, DN_VW), lambda b, i: (b, i, COL_DV // DN_VW)),
            pl.BlockSpec((nbb, blk_len, DN_VW), lambda b, i: (b, i, COL_DZ // DN_VW)),
            pl.BlockSpec((nbb, blk_len, LANES), lambda b, i: (b, i, 0)),
            pl.BlockSpec((nbb, DN_GATE_W, blk_len), lambda b, i: (b, 0, i)),
            pl.BlockSpec((DN_CONV_W, DN_CONV_CH), lambda b, i: (0, 0)),
            pl.BlockSpec((nbb, 8, DN_CONV_CH), lambda b, i: (b, 0, 0)),
            pl.BlockSpec((8, LANES), lambda b, i: (0, 0)),
            pl.BlockSpec((DN_GATE_W, LANES), lambda b, i: (0, 0)),
            pl.BlockSpec((1, DN_DV), lambda b, i: (0, 0)),
            pl.BlockSpec((nbb, DN_V_HEADS, DN_DK, DN_DV), lambda b, i: (b, 0, 0, 0)),
        ],
        out_specs=[
            pl.BlockSpec((nbb, blk_len, DN_VW), lambda b, i: (b, i, 0)),
            pl.BlockSpec((nbb, DN_V_HEADS, DN_DK, DN_DV), lambda b, i: (b, 0, 0, 0)),
            pl.BlockSpec((nbb, DN_CONV_W - 1, DN_CONV_CH), lambda b, i: (b, 0, 0)),
        ],
        out_shape=[
            jax.ShapeDtypeStruct((nb, t, DN_VW), BF16),
            jax.ShapeDtypeStruct((nb, DN_V_HEADS, DN_DK, DN_DV), F32),
            jax.ShapeDtypeStruct((nb, DN_CONV_W - 1, DN_CONV_CH), F32),
        ],
        scratch_shapes=[pltpu.VMEM((nbb, blk_len + 8, DN_CONV_CH), F32),
                        pltpu.VMEM((nbb, DN_V_HEADS, DN_DK, DN_DV), F32)],
        compiler_params=_params("parallel", "arbitrary"),
        name="deltanet",
    )(proj, proj, proj, small, small_t, conv_w, conv_init, prow, pcol, norm_g, s0)


def _memkv_kernel(m_ref, w_ref, o_ref):
    o_ref[...] = _dot(m_ref[...], w_ref[...])


def _memkv(mem, w_kv):
    n = mem.shape[0]
    tn = 1024
    return pl.pallas_call(
        _memkv_kernel,
        grid=(w_kv.shape[1] // tn,),
        in_specs=[pl.BlockSpec((n, D_MODEL), lambda j: (0, 0)),
                  pl.BlockSpec((D_MODEL, tn), lambda j: (0, j))],
        out_specs=pl.BlockSpec((n, tn), lambda j: (0, j)),
        out_shape=jax.ShapeDtypeStruct((n, w_kv.shape[1]), F32),
        compiler_params=_params("parallel"),
        name="memkv",
    )(mem, w_kv)


def _mix_kernel(h_ref, ret_ref, dn_ref, mk_ref, mv_ref, wxq_ref, wg_ref, wb_ref, wo_ref,
                g1_ref, b1_ref, wrt_ref, brt_ref, h1_ref, comb_ref, cnt_ref, mem_scr, *, n_batch, tiles_per_batch):
    @pl.when(pl.program_id(0) % tiles_per_batch == 0)
    def _():
        for b in range(n_batch):
            for hh in range(XA_HEADS):
                mem_scr[0, hh, b] = mk_ref[0, b, :, hh, :].astype(BF16)
                mem_scr[1, hh, b] = mv_ref[0, b, :, hh, :].astype(BF16)

    hb = h_ref[...].astype(BF16)
    xq = jnp.dot(hb, wxq_ref[...], preferred_element_type=F32)
    tb = xq.shape[0] // n_batch
    pairs = [(b, hh) for b in range(n_batch) for hh in range(XA_HEADS)]
    s = [_dot_nt(xq[b * tb:(b + 1) * tb, hh * XA_DH:(hh + 1) * XA_DH], mem_scr[0, hh, b]) * (XA_DH ** -0.5)
         for b, hh in pairs]
    p = [jnp.exp(si - jnp.max(si, -1, keepdims=True)) for si in s]
    p = [pi / jnp.sum(pi, -1, keepdims=True) for pi in p]
    xo = [_dot(pi, mem_scr[1, hh, b]) for pi, (b, hh) in zip(p, pairs)]
    xo = [jnp.concatenate(xo[b * XA_HEADS:(b + 1) * XA_HEADS], axis=1) for b in range(n_batch)]
    xo = xo[0] if n_batch == 1 else jnp.concatenate(xo, axis=0)
    mixed = None
    for n, br in enumerate((ret_ref[...], dn_ref[...], xo)):
        gate = _sigmoid(jnp.dot(hb, wg_ref[:, n * D_MODEL:(n + 1) * D_MODEL], preferred_element_type=F32))
        term = gate * _dot(br, wb_ref[n])
        mixed = term if mixed is None else mixed + term
    y = ALPHA * h_ref[...] + _dot(mixed, wo_ref[...])
    h1 = _layernorm(y, g1_ref[...], b1_ref[...])
    h1_ref[...] = h1
    comb = _route_t(h1, wrt_ref[...], brt_ref[...]).T
    comb_ref[...] = comb
    lane = lax.broadcasted_iota(jnp.int32, comb.shape, 1).astype(F32)
    onehot = jnp.where(lane == comb[:, LANES - 1:LANES], 1.0, 0.0)
    cnt_ref[0] = jnp.broadcast_to(jnp.sum(onehot, 0, keepdims=True), (8, LANES))


def _mix(h, ret_o, dn_o, mem_k, mem_v, w_xq, w_gate, w_branch, w_out, ln_g, ln_b, w_route, b_route, tm, t):
    n_tok = h.shape[0]
    const2 = lambda i: (0, 0)
    row = lambda i: (i, 0)
    single = pl.Buffered(1)
    if tm <= t:
        n_batch = 1
        tiles_per_batch = t // tm
    else:
        n_batch = tm // t
        tiles_per_batch = 1
    mem_spec = pl.BlockSpec((1, n_batch, N_MEM, XA_HEADS, XA_DH), lambda i: (0, i // tiles_per_batch, 0, 0, 0))
    return pl.pallas_call(
        functools.partial(_mix_kernel, n_batch=n_batch, tiles_per_batch=tiles_per_batch),
        grid=(n_tok // tm,),
        in_specs=[
            pl.BlockSpec((tm, D_MODEL), row),
            pl.BlockSpec((tm, D_MODEL), row),
            pl.BlockSpec((tm, D_MODEL), row),
            mem_spec,
            mem_spec,
            pl.BlockSpec((D_MODEL, D_MODEL), const2, pipeline_mode=single),
            pl.BlockSpec((D_MODEL, 3 * D_MODEL), const2, pipeline_mode=single),
            pl.BlockSpec((3, D_MODEL, D_MODEL), lambda i: (0, 0, 0), pipeline_mode=single),
            pl.BlockSpec((D_MODEL, D_MODEL), const2, pipeline_mode=single),
            pl.BlockSpec((1, D_MODEL), const2),
            pl.BlockSpec((1, D_MODEL), const2),
            pl.BlockSpec((MOE_ROUTE_ROWS, D_MODEL), const2),
            pl.BlockSpec((MOE_ROUTE_ROWS, 1), const2),
        ],
        out_specs=[pl.BlockSpec((tm, D_MODEL), row),
                   pl.BlockSpec((tm, LANES), row),
                   pl.BlockSpec((1, 8, LANES), lambda i: (i, 0, 0))],
        out_shape=[jax.ShapeDtypeStruct((n_tok, D_MODEL), F32),
                   jax.ShapeDtypeStruct((n_tok, LANES), F32),
                   jax.ShapeDtypeStruct((n_tok // tm, 8, LANES), F32)],
        scratch_shapes=[pltpu.VMEM((2, XA_HEADS, n_batch, N_MEM, XA_DH), BF16)],
        compiler_params=_params("arbitrary"),
        name="mix",
    )(h, ret_o, dn_o, mem_k, mem_v, w_xq, w_gate, w_branch, w_out, ln_g, ln_b, w_route, b_route)


def _route_t(x, wrt, brt):
    xh = x.astype(BF16)
    xl = (x - xh.astype(F32)).astype(BF16)
    wh = wrt.astype(BF16)
    wl = (wrt - wh.astype(F32)).astype(BF16)
    nt = lambda a, b: lax.dot_general(a, b, (((1,), (1,)), ((), ())), preferred_element_type=F32)
    hi = nt(jnp.concatenate([wh, wl], axis=0), xh)
    logits = hi[0:MOE_ROUTE_ROWS] + (hi[MOE_ROUTE_ROWS:2 * MOE_ROUTE_ROWS] + nt(wh, xl)) + brt
    row_i = lax.broadcasted_iota(jnp.int32, logits.shape, 0)
    row = row_i.astype(F32)
    far = jnp.float32(LANES)
    neg = jnp.float32(-3.0e38)
    is_c = jnp.logical_and(row_i >= MOE_NE, row_i < MOE_NE + MOE_GROUPS)
    cl = jnp.where(is_c, logits, neg)
    cmax = jnp.max(cl, 0, keepdims=True)
    denom = jnp.sum(jnp.where(is_c, jnp.exp(jnp.where(is_c, logits - cmax, 0.0)), 0.0), 0, keepdims=True)
    p_grp = 1.0 / denom
    grp = jnp.min(jnp.where(jnp.logical_and(is_c, cl == cmax), row - MOE_NE, far), 0, keepdims=True)
    in_grp = jnp.logical_and(row_i < MOE_NE, _idiv(row_i, MOE_EXPERTS).astype(F32) == grp)
    fl = jnp.where(in_grp, logits, neg)
    v1 = jnp.max(fl, 0, keepdims=True)
    i1 = jnp.min(jnp.where(jnp.logical_and(in_grp, fl == v1), row, far), 0, keepdims=True)
    rest = jnp.logical_and(in_grp, row != i1)
    fl2 = jnp.where(rest, logits, neg)
    v2 = jnp.max(fl2, 0, keepdims=True)
    i2 = jnp.min(jnp.where(jnp.logical_and(rest, fl2 == v2), row, far), 0, keepdims=True)
    e2 = jnp.exp(v2 - v1)
    w1 = p_grp / (1.0 + e2)
    w2 = p_grp * e2 / (1.0 + e2)
    comb_t = jnp.where(row == i1, w1, 0.0) + jnp.where(row == i2, w2, 0.0)
    tm = x.shape[0]
    last = jnp.where(lax.broadcasted_iota(jnp.int32, (8, tm), 0) == 7, grp, 0.0)
    return jnp.concatenate([comb_t, jnp.zeros((LANES - MOE_ROUTE_ROWS - 8, tm), F32), last], axis=0)


MOE_ALIGN = 16
MOE_EPS = 4


def _moe_sizes(tm):
    base = tm // MOE_GROUPS
    fine = {min(tm, base + MOE_ALIGN * i) for i in range(5)}
    return tuple(sorted(fine | {tm // 2, tm}))


def _moe_kernel(offs_ref, cls_ref, h1_ref, comb_ref, tri_ref, wg_ref, wu_ref, wd_ref,
                g2_ref, b2_ref, y_ref, pt_scr, xs_scr, cs_scr, ys_scr, *, tm, nr1, sizes):
    i = pl.program_id(0)
    e = pl.program_id(1)
    nrt = xs_scr.shape[0]

    @pl.when(e == 0)
    def _():
        comb = comb_ref[...]
        lane_i = lax.broadcasted_iota(jnp.int32, (tm, LANES), 1)
        grp = comb[:, LANES - 1:LANES]
        is_g = lane_i.astype(F32) == grp
        prefix = jnp.dot(tri_ref[...], jnp.where(is_g, 1.0, 0.0).astype(BF16), preferred_element_type=F32)
        rank = jnp.sum(jnp.where(is_g, prefix, 0.0), -1, keepdims=True)
        offv = jnp.zeros((tm, 1), F32)
        for g in range(MOE_GROUPS):
            offv = jnp.where(grp == g, offs_ref[i * MOE_GROUPS + g].astype(F32), offv)
        pos = offv + rank
        col = lax.broadcasted_iota(jnp.int32, (tm, nr1), 1).astype(F32)
        pt = jnp.where(col == pos, 1.0, 0.0).astype(BF16)
        pt_scr[...] = pt
        combw = jnp.where(lane_i < MOE_NE, comb, 0.0)
        c_hi = combw.astype(BF16)
        c_lo = (combw - c_hi.astype(F32)).astype(BF16)
        xs_scr[0:nr1, :] = _dot_tn(pt, h1_ref[...]).astype(BF16)
        c_pair = _dot_tn(pt, jnp.concatenate([c_hi, c_lo], axis=1))
        cs_scr[0:nr1, :] = c_pair[:, 0:LANES] + c_pair[:, LANES:2 * LANES]
        xs_scr[nr1:nrt, :] = jnp.zeros((nrt - nr1, D_MODEL), BF16)
        cs_scr[nr1:nrt, :] = jnp.zeros((nrt - nr1, LANES), F32)
        ys_scr[tm:nr1, :] = jnp.zeros((nr1 - tm, D_MODEL), F32)

    g = e // (MOE_EXPERTS // MOE_EPS)
    off = pl.multiple_of(offs_ref[i * MOE_GROUPS + g], MOE_ALIGN)
    cls = cls_ref[i * MOE_GROUPS + g]
    first = e % (MOE_EXPERTS // MOE_EPS) == 0
    for ci, m in enumerate(sizes):
        @pl.when(cls == ci)
        def _(m=m):
            rows = pl.ds(off, m)
            xb = xs_scr[rows, :]
            cb = cs_scr[rows, :]
            lane_i = lax.broadcasted_iota(jnp.int32, (m, LANES), 1)
            sub = range(MOE_EPS)
            colw = [jnp.sum(jnp.where(lane_i == e * MOE_EPS + j, cb, 0.0), -1, keepdims=True) for j in sub]
            gate = [jnp.dot(xb, wg_ref[j], preferred_element_type=F32) for j in sub]
            up = [jnp.dot(xb, wu_ref[j], preferred_element_type=F32) for j in sub]
            hg = [(_silu(gate[j]) * up[j] * colw[j]).astype(BF16) for j in sub]
            wd = wd_ref[...].reshape(MOE_EPS * MOE_DFF, D_MODEL)
            res = jnp.dot(jnp.concatenate(hg, axis=1), wd, preferred_element_type=F32)

            @pl.when(first)
            def _():
                ys_scr[rows, :] = res

            @pl.when(jnp.logical_not(first))
            def _():
                ys_scr[rows, :] += res

    @pl.when(e == MOE_NE // MOE_EPS - 1)
    def _():
        moe = jnp.dot(pt_scr[...], ys_scr[0:nr1, :].astype(BF16), preferred_element_type=F32)
        y_ref[...] = _layernorm(ALPHA * h1_ref[...] + moe, g2_ref[...], b2_ref[...])


def _moe(h1, comb, cnt, w_gate, w_up, w_down, ln_g, ln_b, tm):
    n_tok = h1.shape[0]
    sizes = _moe_sizes(tm)
    nr1 = -(-(tm + MOE_GROUPS * MOE_ALIGN) // LANES) * LANES
    nrt = nr1 + max(b - a for a, b in zip((0,) + sizes, sizes))
    c = cnt[:, 0, :MOE_GROUPS].astype(jnp.int32)
    seg = (c + MOE_ALIGN - 1) // MOE_ALIGN * MOE_ALIGN
    offs = (jnp.cumsum(seg, axis=1) - seg).reshape(-1)
    cls = sum((c > s).astype(jnp.int32) for s in sizes[:-1]).reshape(-1)
    tri = jnp.asarray(np.tril(np.ones((tm, tm), np.float32), -1), BF16)
    grid_spec = pltpu.PrefetchScalarGridSpec(
        num_scalar_prefetch=2,
        grid=(n_tok // tm, MOE_NE // MOE_EPS),
        in_specs=[
            pl.BlockSpec((tm, D_MODEL), lambda i, e, *_: (i, 0)),
            pl.BlockSpec((tm, LANES), lambda i, e, *_: (i, 0)),
            pl.BlockSpec((tm, tm), lambda i, e, *_: (0, 0), pipeline_mode=pl.Buffered(1)),
            pl.BlockSpec((MOE_EPS, D_MODEL, MOE_DFF), lambda i, e, *_: (e, 0, 0)),
            pl.BlockSpec((MOE_EPS, D_MODEL, MOE_DFF), lambda i, e, *_: (e, 0, 0)),
            pl.BlockSpec((MOE_EPS, MOE_DFF, D_MODEL), lambda i, e, *_: (e, 0, 0)),
            pl.BlockSpec((1, D_MODEL), lambda i, e, *_: (0, 0)),
            pl.BlockSpec((1, D_MODEL), lambda i, e, *_: (0, 0)),
        ],
        out_specs=pl.BlockSpec((tm, D_MODEL), lambda i, e, *_: (i, 0)),
        scratch_shapes=[pltpu.VMEM((tm, nr1), BF16), pltpu.VMEM((nrt, D_MODEL), BF16),
                        pltpu.VMEM((nrt, LANES), F32), pltpu.VMEM((nrt, D_MODEL), F32)],
    )
    return pl.pallas_call(
        functools.partial(_moe_kernel, tm=tm, nr1=nr1, sizes=sizes),
        grid_spec=grid_spec,
        out_shape=jax.ShapeDtypeStruct((n_tok, D_MODEL), F32),
        compiler_params=_params("parallel", "arbitrary"),
        name="moe",
    )(offs, cls, h1, comb, tri, w_gate, w_up, w_down, ln_g, ln_b)


def _rope_tables(start, t):
    half = RET_DK // 2
    inv = 1.0 / (ROPE_BASE ** (np.arange(half, dtype=np.float64) / half))
    ang = (start + np.arange(t, dtype=np.float64))[:, None] * inv[None, :]
    cos = np.cos(ang)
    sin = np.sin(ang)
    return (jnp.asarray(np.concatenate([cos, cos], -1), F32),
            jnp.asarray(np.concatenate([-sin, sin], -1), F32))


def _group(x, pos0, mem_k, mem_v, ret_s0, dn_s0, conv_buf, wts, *, tm, tm_mix, tm_moe, ret_chunk, dn_blk,
           dn_chunk):
    nb, t, d = x.shape
    n_tok = nb * t
    h, small, small_t, proj = _in_proj(x.reshape(n_tok, d), wts["ln_in_g"], wts["ln_in_b"],
                                           wts["w_in"], wts["w_small"], wts["w_small_t"], tm)
    proj3 = proj.reshape(nb, t, PROJ_W)
    cos2, sin2 = _rope_tables(pos0, t)
    ret_o, ret_s = _retention(proj3, cos2, sin2, ret_s0, wts["ret_gn_g"], wts["ret_gn_b"], ret_chunk)
    small_t3 = small_t.reshape(DN_GATE_W, nb, t).transpose(1, 0, 2)
    conv_init = jnp.concatenate([jnp.zeros((nb, 8 - (DN_CONV_W - 1), DN_CONV_CH), F32), conv_buf], axis=1)
    dn_o, dn_s, conv_new = _deltanet(proj3, small.reshape(nb, t, LANES), small_t3, wts["dn_conv"], conv_init,
                                     wts["dn_prow"],
                                     wts["dn_pcol"], wts["dn_norm_g"], dn_s0, dn_blk, dn_chunk)
    h1, comb, cnt = _mix(h, ret_o.reshape(n_tok, d), dn_o.reshape(n_tok, d), mem_k, mem_v,
                         wts["w_xq"], wts["w_gate"], wts["w_branch"], wts["w_out"], wts["ln1_g"], wts["ln1_b"],
                         wts["w_route"], wts["b_route"], tm_mix, t)
    cnt = cnt.reshape(n_tok // tm_moe, tm_moe // tm_mix, 8, LANES).sum(axis=1)
    y = _moe(h1, comb, cnt, wts["moe_w_gate"], wts["moe_w_up"], wts["moe_w_down"],
             wts["ln2_g"], wts["ln2_b"], tm_moe)
    return y.reshape(nb, t, d), ret_s, dn_s, conv_new


def kernel(x_prompt, x_sample, mem_prompt, state_ret, state_dn, state_dn_conv, cache_mem_k, cache_mem_v,
           ln_in_g, ln_in_b, w_in, ret_gn_g, ret_gn_b, dn_conv, dn_A_log, dn_dt_bias, dn_norm_g,
           w_mem_kv, w_branch, w_out, ln1_g, ln1_b, moe_w_coarse, moe_b_coarse, moe_w_fine, moe_b_fine,
           moe_w_gate, moe_w_up, moe_w_down, ln2_g, ln2_b):
    bp, tp, d = x_prompt.shape
    bs, ts, _ = x_sample.shape
    l = 0
    wi = w_in[l].astype(BF16)
    dba = wi[:, PROJ_W:COL_XQ]
    xq, gates = wi[:, COL_XQ:COL_GATE], wi[:, COL_GATE:N_IN]
    wts = {
        "ln_in_g": ln_in_g.reshape(1, d), "ln_in_b": ln_in_b.reshape(1, d),
        "w_in": wi,
        "w_small": jnp.pad(dba, ((0, 0), (0, LANES - DN_GATE_W))),
        "w_small_t": dba.T,
        "ret_gn_g": ret_gn_g[l].reshape(1, -1), "ret_gn_b": ret_gn_b[l].reshape(1, -1),
        "dn_conv": dn_conv[l],
        "dn_prow": jnp.zeros((8, LANES), F32).at[0, 8:16].set(-jnp.exp(dn_A_log[l])).at[1, 8:16].set(dn_dt_bias[l]),
        "dn_pcol": jnp.zeros((16, LANES), F32).at[8:16, 0].set(-jnp.exp(dn_A_log[l])).at[8:16, 1].set(dn_dt_bias[l]),
        "dn_norm_g": dn_norm_g[l].reshape(1, -1),
        "w_xq": xq, "w_gate": gates,
        "w_branch": w_branch[l].astype(BF16), "w_out": w_out[l].astype(BF16),
        "ln1_g": ln1_g[l].reshape(1, d), "ln1_b": ln1_b[l].reshape(1, d),
        "w_route": jnp.pad(jnp.concatenate([moe_w_fine[l], moe_w_coarse[l]], axis=1).T,
                           ((0, MOE_ROUTE_ROWS - MOE_NE - MOE_GROUPS), (0, 0))),
        "b_route": jnp.pad(jnp.concatenate([moe_b_fine[l], moe_b_coarse[l]]),
                           (0, MOE_ROUTE_ROWS - MOE_NE - MOE_GROUPS)).reshape(MOE_ROUTE_ROWS, 1),
        "moe_w_gate": moe_w_gate[l].reshape(MOE_NE, d, MOE_DFF).astype(BF16),
        "moe_w_up": moe_w_up[l].reshape(MOE_NE, d, MOE_DFF).astype(BF16),
        "moe_w_down": moe_w_down[l].reshape(MOE_NE, MOE_DFF, d).astype(BF16),
        "ln2_g": ln2_g[l].reshape(1, d), "ln2_b": ln2_b[l].reshape(1, d),
    }

    mkv = _memkv(mem_prompt.reshape(bp * N_MEM, d), w_mem_kv[l].astype(BF16))
    mk = mkv[:, :XA_HEADS * XA_DH].reshape(1, bp, N_MEM, XA_HEADS, XA_DH)
    mv = mkv[:, XA_HEADS * XA_DH:].reshape(1, bp, N_MEM, XA_HEADS, XA_DH)
    yp, rs_p, ds_p, cb_p = _group(
        x_prompt, 0, mk, mv,
        jnp.zeros((bp, RET_HEADS, RET_DK, RET_DV), F32),
        jnp.zeros((bp, DN_V_HEADS, DN_DK, DN_DV), F32),
        jnp.zeros((bp, DN_CONV_W - 1, DN_CONV_CH), F32),
        wts, tm=min(1024, tp), tm_mix=min(512, tp), tm_moe=min(1024, tp), ret_chunk=min(256, tp),
        dn_blk=min(128, tp),
        dn_chunk=min(DN_CHUNK, tp))
    ys, rs_s, ds_s, cb_s = _group(
        x_sample, PAST_LEN, cache_mem_k[l:l + 1], cache_mem_v[l:l + 1],
        state_ret[l], state_dn[l], state_dn_conv[l],
        wts, tm=bs * ts, tm_mix=bs * ts, tm_moe=bs * ts, ret_chunk=ts, dn_blk=ts, dn_chunk=min(DN_CHUNK, ts))
    return (yp, ys, rs_p[None], rs_s[None], ds_p[None], ds_s[None], cb_p[None], cb_s[None], mk, mv)
```
